```python
import jax, jax.numpy as jnp
from jax import lax
import numpy as np

D_MODEL = 1024
BATCH = 2
SEQ = 8192
DEPTH = 1

D_MIX = D_MODEL
N_HEADS_A = 8
HEAD_DIM_A = 64
D_A = N_HEADS_A * HEAD_DIM_A
N_IDX_HEADS = 8
IDX_DIM = 32
TOPK_MAX = 256
Q_BLOCK = 128
N_HEADS_M = 4
HEAD_DIM_M = 128
D_M = N_HEADS_M * HEAD_DIM_M
CONV_K = 4
CHUNK = 64
N_GROUPS = 4
EXPERTS_PER_GROUP = 8
N_EXPERTS = N_GROUPS * EXPERTS_PER_GROUP
TOP_K_EXPERTS = 2
D_EXPERT = 256
D_PLE = 256
LN_EPS = 1e-5
NEG_BIG = -1e30
DEEPNORM_ALPHA = (2.0 * DEPTH) ** 0.25
DEEPNORM_BETA = (8.0 * DEPTH) ** -0.25
IN_SPLITS = (D_A, D_A, D_A, N_IDX_HEADS * IDX_DIM, IDX_DIM, N_IDX_HEADS, D_M, D_M, D_M, D_M, N_HEADS_M, N_HEADS_M)
D_IN_PROJ = 3 * D_A + N_IDX_HEADS * IDX_DIM + IDX_DIM + N_IDX_HEADS + 4 * D_M + 2 * N_HEADS_M

kernel_name = 'hymba_dsa_mlstm_hiermoe_deepnorm'


def _layer_norm(x, g, b):
    xf = x.astype(jnp.float32)
    mu = jnp.mean(xf, axis=-1, keepdims=True)
    var = jnp.mean(jnp.square(xf - mu), axis=-1, keepdims=True)
    y = (xf - mu) * lax.rsqrt(var + LN_EPS)
    return (y * g.astype(jnp.float32) + b.astype(jnp.float32)).astype(x.dtype)


def _causal_depthwise_conv(x, w, b):
    S = x.shape[1]
    xp = jnp.pad(x, ((0, 0), (CONV_K - 1, 0), (0, 0)))
    out = b
    for j in range(CONV_K):
        out = out + w[j] * xp[:, j:j + S]
    return out


def _dsa_attention(q, k, v, q_idx, k_idx, w_idx):
    B, S = q.shape[0], q.shape[1]
    topk = min(TOPK_MAX, S // 4)
    n_blocks = S // Q_BLOCK
    slopes = jnp.exp2(-8.0 * jnp.arange(1, N_HEADS_A + 1, dtype=jnp.float32) / N_HEADS_A)
    k_idx_f = k_idx.astype(jnp.float32)
    key_pos = jnp.arange(S, dtype=jnp.int32)

    def to_blocks(a):
        return jnp.swapaxes(a.reshape((B, n_blocks, Q_BLOCK) + a.shape[2:]), 0, 1)

    def attend_block(args):
        qb, qib, wb, blk = args
        t = blk * Q_BLOCK + jnp.arange(Q_BLOCK, dtype=jnp.int32)
        logits = jnp.einsum('bqhd,bsd->bqhs', qib.astype(jnp.float32), k_idx_f) * IDX_DIM ** -0.5
        w_heads = wb.astype(jnp.float32) * N_IDX_HEADS ** -0.5
        score = jnp.einsum('bqh,bqhs->bqs', w_heads, jax.nn.relu(logits))
        score = jnp.where(key_pos[None, None, :] <= t[None, :, None], score, NEG_BIG)
        _, idx = lax.top_k(score, topk)
        k_sel = jax.vmap(lambda kb, ib: kb[ib])(k, idx)
        v_sel = jax.vmap(lambda vb, ib: vb[ib])(v, idx)
        s = jnp.einsum('bqhd,bqkhd->bhqk', qb, k_sel).astype(jnp.float32) * HEAD_DIM_A ** -0.5
        dist = (t[None, :, None] - idx).astype(jnp.float32)
        s = s - slopes[None, :, None, None] * dist[:, None]
        s = jnp.where((idx <= t[None, :, None])[:, None], s, -jnp.inf)
        prob = jax.nn.softmax(s, axis=-1).astype(v.dtype)
        return jnp.einsum('bhqk,bqkhd->bqhd', prob, v_sel)

    out = lax.map(attend_block, (to_blocks(q), to_blocks(q_idx), to_blocks(w_idx),
                                 jnp.arange(n_blocks, dtype=jnp.int32)))
    return jnp.swapaxes(out, 0, 1).reshape(B, S, D_A)


def _mlstm(q, k, v, i_pre, f_pre, o_pre, gn_w):
    B, S = q.shape[0], q.shape[1]
    nc = S // CHUNK
    f32 = jnp.float32

    def chunks(a):
        return a.astype(f32).reshape(B, nc, CHUNK, N_HEADS_M, HEAD_DIM_M).transpose(1, 0, 3, 2, 4)

    def gate_chunks(a):
        return a.reshape(B, nc, CHUNK, N_HEADS_M).transpose(1, 0, 3, 2)

    qc = chunks(q) * HEAD_DIM_M ** -0.5
    kc, vc = chunks(k), chunks(v)
    ic = gate_chunks(i_pre.astype(f32))
    fc = gate_chunks(jax.nn.log_sigmoid(f_pre.astype(f32)))
    causal = jnp.tril(jnp.ones((CHUNK, CHUNK), dtype=bool))

    def step(carry, inp):
        C, n, m = carry
        qb, kb, vb, ib, fb = inp
        b_cum = jnp.cumsum(fb, axis=-1)
        d_log = jnp.where(causal, b_cum[..., :, None] - b_cum[..., None, :] + ib[..., None, :], -jnp.inf)
        inter = b_cum + m[..., None]
        m_t = jnp.maximum(inter, jnp.max(d_log, axis=-1))
        w_intra = jnp.einsum('bhtd,bhsd->bhts', qb, kb) * jnp.exp(d_log - m_t[..., None])
        w_inter = jnp.exp(inter - m_t)
        num = jnp.einsum('bhts,bhsv->bhtv', w_intra, vb) + w_inter[..., None] * jnp.einsum('bhtd,bhdv->bhtv', qb, C)
        den = jnp.sum(w_intra, axis=-1) + w_inter * jnp.einsum('bhtd,bhd->bht', qb, n)
        h = num / jnp.maximum(jnp.abs(den), jnp.exp(-m_t))[..., None]
        b_last = b_cum[..., -1]
        w_log = b_last[..., None] - b_cum + ib
        m_new = jnp.maximum(b_last + m, jnp.max(w_log, axis=-1))
        decay = jnp.exp(b_last + m - m_new)
        w_in = jnp.exp(w_log - m_new[..., None])
        C_new = decay[..., None, None] * C + jnp.einsum('bhs,bhsd,bhsv->bhdv', w_in, kb, vb)
        n_new = decay[..., None] * n + jnp.einsum('bhs,bhsd->bhd', w_in, kb)
        return (C_new, n_new, m_new), h

    init = (jnp.zeros((B, N_HEADS_M, HEAD_DIM_M, HEAD_DIM_M), f32),
            jnp.zeros((B, N_HEADS_M, HEAD_DIM_M), f32),
            jnp.zeros((B, N_HEADS_M), f32))
    _, hc = lax.scan(step, init, (qc, kc, vc, ic, fc))
    h = hc.transpose(1, 0, 3, 2, 4).reshape(B, S, N_HEADS_M, HEAD_DIM_M)
    mu = jnp.mean(h, axis=-1, keepdims=True)
    var = jnp.mean(jnp.square(h - mu), axis=-1, keepdims=True)
    h = ((h - mu) * lax.rsqrt(var + LN_EPS)).reshape(B, S, D_M) * gn_w.astype(f32)
    return (jax.nn.sigmoid(o_pre.astype(f32)) * h).astype(q.dtype)


def _hierarchical_moe(x, rg_w, rg_b, re_w, re_b, e_gate, e_up, e_down):
    B, S, D = x.shape
    xt = x.reshape(B * S, D)
    g_prob = jax.nn.softmax((xt @ rg_w + rg_b).astype(jnp.float32), axis=-1)
    g_sel = jnp.argmax(g_prob, axis=-1)
    g_p = jnp.take_along_axis(g_prob, g_sel[:, None], axis=-1)[:, 0]
    e_logit = (xt @ re_w + re_b).astype(jnp.float32).reshape(-1, N_GROUPS, EXPERTS_PER_GROUP)
    e_logit = jnp.take_along_axis(e_logit, g_sel[:, None, None], axis=1)[:, 0]
    e_prob = jax.nn.softmax(e_logit, axis=-1)
    top_w, top_i = lax.top_k(e_prob, TOP_K_EXPERTS)
    top_w = top_w / jnp.sum(top_w, axis=-1, keepdims=True)
    w_local = jnp.einsum('nk,nke->ne', top_w, jax.nn.one_hot(top_i, EXPERTS_PER_GROUP, dtype=jnp.float32))
    comb = (jax.nn.one_hot(g_sel, N_GROUPS, dtype=jnp.float32)[:, :, None]
            * w_local[:, None, :] * g_p[:, None, None]).astype(x.dtype)
    y = jnp.zeros_like(xt)
    for g in range(N_GROUPS):
        sl = slice(g * EXPERTS_PER_GROUP, (g + 1) * EXPERTS_PER_GROUP)
        h = jax.nn.silu(jnp.einsum('nd,edf->nef', xt, e_gate[sl])) * jnp.einsum('nd,edf->nef', xt, e_up[sl])
        h = h * comb[:, g, :, None]
        y = y + jnp.einsum('nef,efd->nd', h, e_down[sl])
    return y.reshape(B, S, D)


def _hybrid_layer(x, p_i, w_in, b_igate, b_fgate, conv_w, conv_b, gn_w, w_out, ln1_g, ln1_b,
                  rg_w, rg_b, re_w, re_b, e_gate, e_up, e_down, ln2_g, ln2_b, ple_gate_w, ple_proj_w):
    B, S, _ = x.shape
    z = x @ w_in
    split_points = [int(c) for c in np.cumsum(IN_SPLITS)[:-1]]
    q_a, k_a, v_a, q_idx, k_idx, w_idx, q_m, k_m, v_m, o_m, i_m, f_m = jnp.split(z, split_points, axis=-1)
    shape_a = (B, S, N_HEADS_A, HEAD_DIM_A)
    y_a = _dsa_attention(q_a.reshape(shape_a), k_a.reshape(shape_a), v_a.reshape(shape_a),
                         q_idx.reshape(B, S, N_IDX_HEADS, IDX_DIM), k_idx, w_idx)
    qk_m = jax.nn.silu(_causal_depthwise_conv(jnp.concatenate([q_m, k_m], axis=-1), conv_w, conv_b))
    q_m, k_m = jnp.split(qk_m, 2, axis=-1)
    shape_m = (B, S, N_HEADS_M, HEAD_DIM_M)
    y_m = _mlstm(q_m.reshape(shape_m), k_m.reshape(shape_m), v_m.reshape(shape_m),
                 i_m + b_igate, f_m + b_fgate, o_m, gn_w)
    mix = jnp.concatenate([y_a, y_m], axis=-1) @ w_out
    x = _layer_norm(DEEPNORM_ALPHA * x + mix, ln1_g, ln1_b)
    ffn = _hierarchical_moe(x, rg_w, rg_b, re_w, re_b, e_gate, e_up, e_down)
    x = _layer_norm(DEEPNORM_ALPHA * x + ffn, ln2_g, ln2_b)
    return x + jax.nn.sigmoid(x @ ple_gate_w) * (p_i @ ple_proj_w)


def setup_inputs(seed: int = 0) -> dict:
    key = jax.random.key(seed)
    ks = jax.random.split(key, 24)

    def nrm(k, shape, scale):
        return scale * jax.random.normal(k, shape, jnp.float32)

    col_scale = jnp.concatenate([jnp.full((n,), DEEPNORM_BETA if j in (2, 8) else 1.0, jnp.float32)
                                 for j, n in enumerate(IN_SPLITS)])
    return {
        'x': nrm(ks[0], (BATCH, SEQ, D_MODEL), 1.0),
        'p': nrm(ks[1], (DEPTH, BATCH, SEQ, D_PLE), 1.0),
        'w_in': nrm(ks[2], (DEPTH, D_MODEL, D_IN_PROJ), D_MODEL ** -0.5) * col_scale,
        'b_igate': nrm(ks[3], (DEPTH, N_HEADS_M), 0.1),
        'b_fgate': jnp.linspace(3.0, 6.0, N_HEADS_M, dtype=jnp.float32)[None] + nrm(ks[4], (DEPTH, N_HEADS_M), 0.01),
        'conv_w': nrm(ks[5], (DEPTH, CONV_K, 2 * D_M), CONV_K ** -0.5),
        'conv_b': nrm(ks[6], (DEPTH, 2 * D_M), 0.01),
        'gn_w': 1.0 + nrm(ks[7], (DEPTH, D_M), 0.01),
        'w_out': nrm(ks[8], (DEPTH, D_MIX, D_MODEL), D_MIX ** -0.5 * DEEPNORM_BETA),
        'ln1_g': 1.0 + nrm(ks[9], (DEPTH, D_MODEL), 0.01),
        'ln1_b': nrm(ks[10], (DEPTH, D_MODEL), 0.01),
        'router_group_w': nrm(ks[11], (DEPTH, D_MODEL, N_GROUPS), D_MODEL ** -0.5),
        'router_group_b': nrm(ks[12], (DEPTH, N_GROUPS), 0.01),
        'router_expert_w': nrm(ks[13], (DEPTH, D_MODEL, N_EXPERTS), D_MODEL ** -0.5),
        'router_expert_b': nrm(ks[14], (DEPTH, N_EXPERTS), 0.01),
        'expert_w_gate': nrm(ks[15], (DEPTH, N_EXPERTS, D_MODEL, D_EXPERT), D_MODEL ** -0.5),
        'expert_w_up': nrm(ks[16], (DEPTH, N_EXPERTS, D_MODEL, D_EXPERT), D_MODEL ** -0.5),
        'expert_w_down': nrm(ks[17], (DEPTH, N_EXPERTS, D_EXPERT, D_MODEL), D_EXPERT ** -0.5 * DEEPNORM_BETA),
        'ln2_g': 1.0 + nrm(ks[18], (DEPTH, D_MODEL), 0.01),
        'ln2_b': nrm(ks[19], (DEPTH, D_MODEL), 0.01),
        'ple_gate_w': nrm(ks[20], (DEPTH, D_MODEL, D_MODEL), D_MODEL ** -0.5),
        'ple_proj_w': nrm(ks[21], (DEPTH, D_PLE, D_MODEL), D_PLE ** -0.5),
    }


def reference(x, p, w_in, b_igate, b_fgate, conv_w, conv_b, gn_w, w_out, ln1_g, ln1_b,
              router_group_w, router_group_b, router_expert_w, router_expert_b,
              expert_w_gate, expert_w_up, expert_w_down, ln2_g, ln2_b, ple_gate_w, ple_proj_w):
    for i in range(DEPTH):
        x = _hybrid_layer(x, p[i], w_in[i], b_igate[i], b_fgate[i], conv_w[i], conv_b[i], gn_w[i],
                          w_out[i], ln1_g[i], ln1_b[i], router_group_w[i], router_group_b[i],
                          router_expert_w[i], router_expert_b[i], expert_w_gate[i], expert_w_up[i],
                          expert_w_down[i], ln2_g[i], ln2_b[i], ple_gate_w[i], ple_proj_w[i])
    return x
```

```python
import functools

import numpy as np
import jax
import jax.numpy as jnp
from jax import lax
from jax.experimental import pallas as pl
from jax.experimental.pallas import tpu as pltpu

F32 = jnp.float32
BF16 = jnp.bfloat16
I32 = jnp.int32

N_HEADS_A = 8
HEAD_DIM_A = 64
D_A = N_HEADS_A * HEAD_DIM_A
N_IDX_HEADS = 8
IDX_DIM = 32
TOPK_MAX = 256
N_HEADS_M = 4
HEAD_DIM_M = 128
D_M = N_HEADS_M * HEAD_DIM_M
CONV_K = 4
N_GROUPS = 4
EXPERTS_PER_GROUP = 8
N_EXPERTS = N_GROUPS * EXPERTS_PER_GROUP
D_EXPERT = 256
LN_EPS = 1e-5
NEG_BIG = -1e30
IN_SPLITS = (D_A, D_A, D_A, N_IDX_HEADS * IDX_DIM, IDX_DIM, N_IDX_HEADS, D_M, D_M, D_M, D_M, N_HEADS_M, N_HEADS_M)

LANES = 128
SUBLANES = 8
MIB = 1024 * 1024

C_QKV = 0
C_QIDX = C_QKV + 3 * D_A
C_K4 = C_QIDX + N_IDX_HEADS * IDX_DIM
C_GATE = C_K4 + LANES
C_QKM = C_GATE + LANES
C_VM = C_QKM + 2 * D_M
C_OM = C_VM + D_M
C_END = C_OM + D_M
G_W = 0
G_I = N_IDX_HEADS
G_F = G_I + N_HEADS_M

_NEG_BITS = int(np.float32(NEG_BIG).view(np.int32))
NEG_KEY = _NEG_BITS ^ ((_NEG_BITS >> 31) & 0x7FFFFFFF)
INT_MIN = -2 ** 31

NT_DIMS = (((1,), (1,)), ((), ()))


def _cparams(semantics, vmem_mib):
    return pltpu.CompilerParams(dimension_semantics=semantics, vmem_limit_bytes=int(vmem_mib * MIB))


def _layer_norm(r, g, b):
    mu = jnp.mean(r, axis=-1, keepdims=True)
    d = r - mu
    var = jnp.mean(d * d, axis=-1, keepdims=True)
    return d * lax.rsqrt(var + LN_EPS) * g + b


def _pack_w_in(w_in):
    d = w_in.shape[0]
    cuts = [int(c) for c in np.cumsum(IN_SPLITS)[:-1]]
    q_a, k_a, v_a, q_idx, k_idx, w_idx, q_m, k_m, v_m, o_m, i_m, f_m = jnp.split(w_in, cuts, axis=1)
    k4 = jnp.tile(k_idx, (1, LANES // IDX_DIM))
    gate = jnp.concatenate([w_idx, i_m, f_m, jnp.zeros((d, LANES - G_F - N_HEADS_M), w_in.dtype)], axis=1)
    return jnp.concatenate([q_a, k_a, v_a, q_idx, k4, gate, q_m, k_m, v_m, o_m], axis=1).astype(BF16)


def _in_proj_body(x_ref, w_ref, qkv_ref, qi_ref, k4_ref, g_ref, qkm_ref, vm_ref, om_ref):
    xb = x_ref[...].astype(BF16)

    def proj(lo, hi):
        return jnp.dot(xb, w_ref[:, lo:hi], preferred_element_type=F32)

    qkv_ref[...] = proj(C_QKV, C_QIDX).astype(BF16)
    qi_ref[...] = proj(C_QIDX, C_K4).astype(BF16)
    k4_ref[...] = proj(C_K4, C_GATE).astype(BF16)
    g_ref[...] = proj(C_GATE, C_QKM)
    qkm_ref[...] = proj(C_QKM, C_VM)
    vm_ref[...] = proj(C_VM, C_OM).astype(BF16)
    om_ref[...] = proj(C_OM, C_END)


def _in_proj(x2d, w_packed, tm):
    n, d = x2d.shape
    widths = [(C_QIDX - C_QKV, BF16), (C_K4 - C_QIDX, BF16), (C_GATE - C_K4, BF16), (C_QKM - C_GATE, F32),
              (C_VM - C_QKM, F32), (C_OM - C_VM, BF16), (C_END - C_OM, F32)]
    return pl.pallas_call(
        _in_proj_body,
        grid=(n // tm,),
        in_specs=[pl.BlockSpec((tm, d), lambda i: (i, 0)),
                  pl.BlockSpec((d, C_END), lambda i: (0, 0))],
        out_specs=[pl.BlockSpec((tm, w), lambda i: (i, 0)) for w, _ in widths],
        out_shape=[jax.ShapeDtypeStruct((n, w), dt) for w, dt in widths],
        compiler_params=_cparams(("arbitrary",), 48),
        name="in_proj",
    )(x2d, w_packed)


def _sortable_key(score):
    bits = lax.bitcast_convert_type(score, I32)
    key = bits ^ ((bits >> 31) & jnp.int32(0x7FFFFFFF))
    return jnp.where(key == -1, 0, key)


def _dsa_body(qa_ref, ka_ref, va_ref, qi_ref, k4_ref, g_ref, o_ref,
              keys_ref, qm_ref, wb_ref, qh_ref, acc_ref, m_ref, l_ref, *, seq, topk, tq, tk):
    t0 = pl.program_id(1) * tq
    n_kb = (t0 + tq + tk - 1) // tk
    lane = lax.broadcasted_iota(I32, (tq, LANES), 1)
    lane_tiles = tk // LANES

    for h in range(N_IDX_HEADS):
        per_slab = LANES // IDX_DIM
        slab = qi_ref[:, (h // per_slab) * LANES:(h // per_slab + 1) * LANES]
        lo = (h % per_slab) * IDX_DIM
        qm_ref[h * tq:(h + 1) * tq, :] = jnp.where((lane >= lo) & (lane < lo + IDX_DIM), slab, jnp.zeros_like(slab))
        w_col = g_ref[:, G_W + h:G_W + h + 1] * ((IDX_DIM * N_IDX_HEADS) ** -0.5)
        wb_ref[h] = jnp.broadcast_to(w_col, (tq, tk))
    for h in range(N_HEADS_A):
        per_slab = LANES // HEAD_DIM_A
        slab = qa_ref[:, (h // per_slab) * LANES:(h // per_slab + 1) * LANES].astype(F32) * (HEAD_DIM_A ** -0.5)
        lo = (h % per_slab) * HEAD_DIM_A
        qh_ref[h] = jnp.where((lane >= lo) & (lane < lo + HEAD_DIM_A), slab, 0.0).astype(BF16)

    row_t = t0 + lax.broadcasted_iota(I32, (tq, tk), 0)
    col = lax.broadcasted_iota(I32, (tq, tk), 1)

    def score_block(j, carry):
        kb = k4_ref[pl.ds(pl.multiple_of(j * tk, tk), tk), :]
        logits = lax.dot_general(qm_ref[...], kb, NT_DIMS, preferred_element_type=F32)
        score = jnp.zeros((tq, tk), F32)
        for h in range(N_IDX_HEADS):
            score = score + wb_ref[h] * jnp.maximum(logits[h * tq:(h + 1) * tq], 0.0)
        score = jnp.where(j * tk + col <= row_t, score, NEG_BIG)
        keys_ref[j] = _sortable_key(score)
        return carry

    lax.fori_loop(0, n_kb, score_block, 0)

    n_tail = (seq - n_kb * tk).astype(F32)

    def count(thr, strict):
        def body(j, acc):
            k = keys_ref[j]
            hit = ((k > thr) if strict else (k >= thr)).astype(F32)
            part = hit[:, 0:LANES]
            for c in range(1, lane_tiles):
                part = part + hit[:, c * LANES:(c + 1) * LANES]
            return acc + part

        acc = lax.fori_loop(0, n_kb, body, jnp.zeros((tq, LANES), F32))
        tail_hit = (NEG_KEY > thr) if strict else (NEG_KEY >= thr)
        return jnp.sum(acc, axis=1, keepdims=True) + jnp.where(tail_hit, n_tail, 0.0)

    def bit_step(i, carry):
        prefix, cnt_prefix = carry
        cand = prefix | lax.shift_left(jnp.int32(1), (31 - i).astype(I32))
        cnt = count(cand ^ jnp.int32(INT_MIN), strict=False)
        take = cnt >= topk
        return jnp.where(take, cand, prefix), jnp.where(take, cnt, cnt_prefix)

    prefix, _ = lax.fori_loop(0, 32, bit_step,
                              (jnp.zeros((tq, 1), I32), jnp.full((tq, 1), float(seq), F32)))
    thr = prefix ^ jnp.int32(INT_MIN)

    need = topk - count(thr, strict=True)
    tri_r = lax.broadcasted_iota(I32, (tk, tk), 0)
    tri_c = lax.broadcasted_iota(I32, (tk, tk), 1)
    before = (tri_r < tri_c).astype(BF16)

    def drop_late_ties(j, seen):
        k = keys_ref[j]
        eq = k == thr
        eq_f = eq.astype(F32)
        rank = seen + jnp.dot(eq_f.astype(BF16), before, preferred_element_type=F32)
        keys_ref[j] = jnp.where(eq & (rank >= need), jnp.int32(INT_MIN), k)
        return seen + jnp.sum(eq_f, axis=1, keepdims=True)

    lax.fori_loop(0, n_kb, drop_late_ties, jnp.zeros((tq, 1), F32))

    m_ref[...] = jnp.full(m_ref.shape, NEG_BIG, F32)
    l_ref[...] = jnp.zeros(l_ref.shape, F32)
    acc_ref[...] = jnp.zeros(acc_ref.shape, F32)
    heads_per_slab = LANES // HEAD_DIM_A

    def attend(j, carry):
        k0 = pl.multiple_of(j * tk, tk)
        kidx = j * tk + col
        sel = (keys_ref[j] >= thr) & (kidx <= row_t)
        dist = (row_t - kidx).astype(F32)
        for p in range(N_HEADS_A // heads_per_slab):
            k2 = ka_ref[pl.ds(k0, tk), p * LANES:(p + 1) * LANES]
            v2 = va_ref[pl.ds(k0, tk), p * LANES:(p + 1) * LANES]
            acc_old = acc_ref[p]
            acc_new = acc_old
            for hh in range(heads_per_slab):
                h = p * heads_per_slab + hh
                slope = 2.0 ** (-8.0 * (h + 1) / N_HEADS_A)
                s = lax.dot_general(qh_ref[h], k2, NT_DIMS, preferred_element_type=F32)
                s = jnp.where(sel, s - slope * dist, -jnp.inf)
                parts = [s[:, c * LANES:(c + 1) * LANES] for c in range(lane_tiles)]
                mx = parts[0]
                for c in range(1, lane_tiles):
                    mx = jnp.maximum(mx, parts[c])
                m_old = m_ref[h]
                m_new = jnp.maximum(m_old, jnp.max(mx, axis=1, keepdims=True))
                alpha = jnp.exp(m_old - m_new)
                probs = [jnp.exp(pt - m_new) for pt in parts]
                l_new = alpha * l_ref[h]
                for pt in probs:
                    l_new = l_new + pt
                m_ref[h] = m_new
                l_ref[h] = l_new
                pv = jnp.dot(jnp.concatenate(probs, axis=1).astype(BF16), v2, preferred_element_type=F32)
                in_head = (lane >= hh * HEAD_DIM_A) & (lane < (hh + 1) * HEAD_DIM_A)
                acc_new = jnp.where(in_head, alpha * acc_old + pv, acc_new)
            acc_ref[p] = acc_new
        return carry

    lax.fori_loop(0, n_kb, attend, 0)

    for p in range(N_HEADS_A // heads_per_slab):
        denom = jnp.zeros((tq, LANES), F32)
        for hh in range(heads_per_slab):
            in_head = (lane >= hh * HEAD_DIM_A) & (lane < (hh + 1) * HEAD_DIM_A)
            l_h = jnp.sum(l_ref[p * heads_per_slab + hh], axis=1, keepdims=True)
            denom = jnp.where(in_head, l_h, denom)
        o_ref[:, p * LANES:(p + 1) * LANES] = (acc_ref[p] / denom).astype(BF16)


def _dsa_attention(qkv, q_idx, k4, gates, tq, tk):
    b, s, _ = qkv.shape
    topk = min(TOPK_MAX, s // 4)
    once = pl.Buffered(1)
    body = functools.partial(_dsa_body, seq=s, topk=topk, tq=tq, tk=tk)
    return pl.pallas_call(
        body,
        grid=(b, s // tq),
        in_specs=[pl.BlockSpec((None, tq, D_A), lambda bi, qi: (bi, qi, 0)),
                  pl.BlockSpec((None, s, D_A), lambda bi, qi: (bi, 0, 1), pipeline_mode=once),
                  pl.BlockSpec((None, s, D_A), lambda bi, qi: (bi, 0, 2), pipeline_mode=once),
                  pl.BlockSpec((None, tq, N_IDX_HEADS * IDX_DIM), lambda bi, qi: (bi, qi, 0)),
                  pl.BlockSpec((None, s, LANES), lambda bi, qi: (bi, 0, 0), pipeline_mode=once),
                  pl.BlockSpec((None, tq, LANES), lambda bi, qi: (bi, qi, 0))],
        out_specs=pl.BlockSpec((None, tq, D_A), lambda bi, qi: (bi, qi, 0)),
        out_shape=jax.ShapeDtypeStruct((b, s, D_A), BF16),
        scratch_shapes=[pltpu.VMEM((s // tk, tq, tk), I32),
                        pltpu.VMEM((N_IDX_HEADS * tq, LANES), BF16),
                        pltpu.VMEM((N_IDX_HEADS, tq, tk), F32),
                        pltpu.VMEM((N_HEADS_A, tq, LANES), BF16),
                        pltpu.VMEM((N_HEADS_A * HEAD_DIM_A // LANES, tq, LANES), F32),
                        pltpu.VMEM((N_HEADS_A, tq, LANES), F32),
                        pltpu.VMEM((N_HEADS_A, tq, LANES), F32)],
        compiler_params=_cparams(("arbitrary", "arbitrary"), 48),
        name="dsa_attention",
    )(qkv, qkv, qkv, q_idx, k4, gates)


def _log_sigmoid(x):
    return jnp.minimum(x, 0.0) - jnp.log(1.0 + jnp.exp(-jnp.abs(x)))


def _split3(a):
    hi = a.astype(BF16)
    r1 = a - hi.astype(F32)
    mid = r1.astype(BF16)
    lo = (r1 - mid.astype(F32)).astype(BF16)
    return hi, mid, lo


def _mlstm_body(qk_ref, v_ref, o_ref, g_ref, gt_ref, cw_ref, cb_ref, bg_ref, bgt_ref, gn_ref, y_ref,
                xe_ref, c_ref, m_ref, *, chunk):
    @pl.when(pl.program_id(1) == 0)
    def _():
        xe_ref[0:SUBLANES, :] = jnp.zeros((SUBLANES, 2 * D_M), F32)
        c_ref[...] = jnp.zeros(c_ref.shape, F32)
        m_ref[...] = jnp.zeros(m_ref.shape, F32)

    xe_ref[SUBLANES:SUBLANES + chunk, :] = qk_ref[...]
    conv = cb_ref[...]
    for j in range(CONV_K):
        start = SUBLANES - (CONV_K - 1) + j
        conv = conv + cw_ref[j:j + 1, :] * xe_ref[start:start + chunk, :]
    qk = conv * jax.nn.sigmoid(conv)
    xe_ref[0:SUBLANES, :] = xe_ref[chunk:chunk + SUBLANES, :]

    gb = g_ref[...] + bg_ref[...]
    gbt = gt_ref[...] + bgt_ref[...]
    r = lax.broadcasted_iota(I32, (chunk, chunk), 0)
    c = lax.broadcasted_iota(I32, (chunk, chunk), 1)
    causal = r >= c
    incl = causal.astype(BF16)
    incl_t = (r <= c).astype(BF16)
    cum = jnp.zeros((chunk, LANES), F32)
    for term in _split3(_log_sigmoid(gb)):
        cum = cum + jnp.dot(incl, term, preferred_element_type=F32)
    cum_t = jnp.zeros((2 * N_HEADS_M, chunk), F32)
    for term in _split3(_log_sigmoid(gbt)):
        cum_t = cum_t + jnp.dot(term, incl_t, preferred_element_type=F32)

    lane = lax.broadcasted_iota(I32, (chunk, LANES), 1)
    ones_col = jnp.where(lane == 0, 1.0, 0.0).astype(BF16)

    for h in range(N_HEADS_M):
        sl = slice(h * HEAD_DIM_M, (h + 1) * HEAD_DIM_M)
        q_h = (qk[:, sl] * (HEAD_DIM_M ** -0.5)).astype(BF16)
        k_f = qk[:, D_M + h * HEAD_DIM_M:D_M + (h + 1) * HEAD_DIM_M]
        v_aug = jnp.concatenate([v_ref[:, sl], ones_col], axis=1)
        b_col = cum[:, G_F + h:G_F + h + 1]
        i_col = gb[:, G_I + h:G_I + h + 1]
        b_row = cum_t[N_HEADS_M + h:N_HEADS_M + h + 1, :]
        i_row = gbt[h:h + 1, :]
        m_prev = m_ref[h:h + 1, 0:1]
        c_aug = c_ref[h]

        d_log = jnp.where(causal, b_col - b_row + i_row, -jnp.inf)
        inter = b_col + m_prev
        m_t = jnp.maximum(inter, jnp.max(d_log, axis=1, keepdims=True))
        qk_t = lax.dot_general(q_h, k_f.astype(BF16), NT_DIMS, preferred_element_type=F32)
        w_intra = qk_t * jnp.exp(d_log - m_t)
        w_inter = jnp.exp(inter - m_t)
        intra = jnp.dot(w_intra.astype(BF16), v_aug[:, 0:HEAD_DIM_M], preferred_element_type=F32)
        carried = jnp.dot(q_h, c_aug.astype(BF16), preferred_element_type=F32)
        num = intra + w_inter * carried[:, 0:HEAD_DIM_M]
        den = jnp.sum(w_intra, axis=1, keepdims=True) + w_inter * carried[:, HEAD_DIM_M:HEAD_DIM_M + 1]
        hid = num / jnp.maximum(jnp.abs(den), jnp.exp(-m_t))

        mu = jnp.mean(hid, axis=1, keepdims=True)
        dev = hid - mu
        var = jnp.mean(dev * dev, axis=1, keepdims=True)
        normed = dev * lax.rsqrt(var + LN_EPS) * gn_ref[:, sl]
        y_ref[:, sl] = (jax.nn.sigmoid(o_ref[:, sl]) * normed).astype(BF16)

        b_last = b_col[chunk - 1:chunk, :]
        w_log = b_last - b_col + i_col
        m_new = jnp.maximum(b_last + m_prev, jnp.max(w_log, axis=0, keepdims=True))
        decay = jnp.exp(b_last + m_prev - m_new)
        kw_t = (k_f * jnp.exp(w_log - m_new)).T.astype(BF16)
        c_ref[h] = decay * c_aug + jnp.dot(kw_t, v_aug, preferred_element_type=F32)
        m_ref[h:h + 1, :] = jnp.broadcast_to(m_new, (1, LANES))


def _mlstm(qk_m, v_m, o_m, gates, gates_t, conv_w, conv_b, bias_slab, bias_col, gn_w, chunk):
    b, s, _ = qk_m.shape
    body = functools.partial(_mlstm_body, chunk=chunk)
    const = lambda bi, ci: (0, 0)
    return pl.pallas_call(
        body,
        grid=(b, s // chunk),
        in_specs=[pl.BlockSpec((None, chunk, 2 * D_M), lambda bi, ci: (bi, ci, 0)),
                  pl.BlockSpec((None, chunk, D_M), lambda bi, ci: (bi, ci, 0)),
                  pl.BlockSpec((None, chunk, D_M), lambda bi, ci: (bi, ci, 0)),
                  pl.BlockSpec((None, chunk, LANES), lambda bi, ci: (bi, ci, 0)),
                  pl.BlockSpec((None, 2 * N_HEADS_M, chunk), lambda bi, ci: (bi, 0, ci)),
                  pl.BlockSpec((CONV_K, 2 * D_M), const),
                  pl.BlockSpec((1, 2 * D_M), const),
                  pl.BlockSpec((1, LANES), const),
                  pl.BlockSpec((2 * N_HEADS_M, 1), const),
                  pl.BlockSpec((1, D_M), const)],
        out_specs=pl.BlockSpec((None, chunk, D_M), lambda bi, ci: (bi, ci, 0)),
        out_shape=jax.ShapeDtypeStruct((b, s, D_M), BF16),
        scratch_shapes=[pltpu.VMEM((chunk + SUBLANES, 2 * D_M), F32),
                        pltpu.VMEM((N_HEADS_M, HEAD_DIM_M, 2 * LANES), F32),
                        pltpu.VMEM((SUBLANES, LANES), F32)],
        compiler_params=_cparams(("arbitrary", "arbitrary"), 32),
        name="mlstm",
    )(qk_m, v_m, o_m, gates, gates_t, conv_w, conv_b, bias_slab, bias_col, gn_w)


def _route(logits):
    lane = lax.broadcasted_iota(I32, logits.shape, 1).astype(F32)
    g_mask = (lane >= N_EXPERTS) & (lane < N_EXPERTS + N_GROUPS)
    g_logit = jnp.where(g_mask, logits, -jnp.inf)
    g_exp = jnp.exp(g_logit - jnp.max(g_logit, axis=1, keepdims=True))
    g_prob = g_exp / jnp.sum(g_exp, axis=1, keepdims=True)
    g_p = jnp.max(g_prob, axis=1, keepdims=True)
    g_sel = jnp.min(jnp.where(g_mask & (g_prob == g_p), lane, float(LANES)), axis=1, keepdims=True) - N_EXPERTS
    e_mask = (lane >= g_sel * EXPERTS_PER_GROUP) & (lane < (g_sel + 1.0) * EXPERTS_PER_GROUP)
    e_logit = jnp.where(e_mask, logits, -jnp.inf)
    e_exp = jnp.exp(e_logit - jnp.max(e_logit, axis=1, keepdims=True))
    e_prob = e_exp / jnp.sum(e_exp, axis=1, keepdims=True)
    p1 = jnp.max(jnp.where(e_mask, e_prob, -1.0), axis=1, keepdims=True)
    i1 = jnp.min(jnp.where(e_mask & (e_prob == p1), lane, float(LANES)), axis=1, keepdims=True)
    rest = e_mask & (lane != i1)
    p2 = jnp.max(jnp.where(rest, e_prob, -1.0), axis=1, keepdims=True)
    i2 = jnp.min(jnp.where(rest & (e_prob == p2), lane, float(LANES)), axis=1, keepdims=True)
    total = p1 + p2
    return (jnp.where(lane == i1, p1 / total, 0.0) + jnp.where(lane == i2, p2 / total, 0.0)) * g_p


def _mix_body(ya_ref, ym_ref, x_ref, wo_ref, g_ref, b_ref, wr_ref, br_ref, x1_ref, comb_ref, *, alpha):
    mix = (jnp.dot(ya_ref[...], wo_ref[0:D_A, :], preferred_element_type=F32)
           + jnp.dot(ym_ref[...], wo_ref[D_A:D_A + D_M, :], preferred_element_type=F32))
    x1 = _layer_norm(alpha * x_ref[...] + mix, g_ref[...], b_ref[...])
    x1_ref[...] = x1
    logits = jnp.dot(x1, wr_ref[...], preferred_element_type=F32, precision=lax.Precision.HIGHEST) + br_ref[...]
    comb_ref[...] = _route(logits)


def _mix_ln_router(y_a, y_m, x2d, w_out, ln_g, ln_b, w_router, b_router, alpha, tm):
    n, d = x2d.shape
    const = lambda i: (0, 0)
    return pl.pallas_call(
        functools.partial(_mix_body, alpha=alpha),
        grid=(n // tm,),
        in_specs=[pl.BlockSpec((tm, D_A), lambda i: (i, 0)),
                  pl.BlockSpec((tm, D_M), lambda i: (i, 0)),
                  pl.BlockSpec((tm, d), lambda i: (i, 0)),
                  pl.BlockSpec((D_A + D_M, d), const),
                  pl.BlockSpec((1, d), const),
                  pl.BlockSpec((1, d), const),
                  pl.BlockSpec((d, LANES), const),
                  pl.BlockSpec((1, LANES), const)],
        out_specs=[pl.BlockSpec((tm, d), lambda i: (i, 0)),
                   pl.BlockSpec((tm, LANES), lambda i: (i, 0))],
        out_shape=[jax.ShapeDtypeStruct((n, d), F32), jax.ShapeDtypeStruct((n, LANES), F32)],
        compiler_params=_cparams(("arbitrary",), 32),
        name="mix_ln_router",
    )(y_a, y_m, x2d, w_out, ln_g, ln_b, w_router, b_router)


def _moe_body(x1_ref, comb_ref, wg_ref, wu_ref, wd_ref, g_ref, b_ref, y_ref, xb_ref, *, alpha):
    e = pl.program_id(1)

    @pl.when(e == 0)
    def _():
        xb_ref[...] = x1_ref[...].astype(BF16)
        y_ref[...] = jnp.zeros(y_ref.shape, F32)

    xb = xb_ref[...]
    gate = jnp.dot(xb, wg_ref[...], preferred_element_type=F32)
    up = jnp.dot(xb, wu_ref[...], preferred_element_type=F32)
    lane = lax.broadcasted_iota(I32, comb_ref.shape, 1)
    weight = jnp.sum(jnp.where(lane == e, comb_ref[...], 0.0), axis=1, keepdims=True)
    hidden = gate * jax.nn.sigmoid(gate) * up * weight
    y_ref[...] += jnp.dot(hidden.astype(BF16), wd_ref[...], preferred_element_type=F32)

    @pl.when(e == N_EXPERTS - 1)
    def _():
        y_ref[...] = _layer_norm(alpha * x1_ref[...] + y_ref[...], g_ref[...], b_ref[...])


def _moe_ln(x1, comb, w_gate, w_up, w_down, ln_g, ln_b, alpha, tm):
    n, d = x1.shape
    const = lambda i, e: (0, 0)
    return pl.pallas_call(
        functools.partial(_moe_body, alpha=alpha),
        grid=(n // tm, N_EXPERTS),
        in_specs=[pl.BlockSpec((tm, d), lambda i, e: (i, 0)),
                  pl.BlockSpec((tm, LANES), lambda i, e: (i, 0)),
                  pl.BlockSpec((None, d, D_EXPERT), lambda i, e: (e, 0, 0)),
                  pl.BlockSpec((None, d, D_EXPERT), lambda i, e: (e, 0, 0)),
                  pl.BlockSpec((None, D_EXPERT, d), lambda i, e: (e, 0, 0)),
                  pl.BlockSpec((1, d), const),
                  pl.BlockSpec((1, d), const)],
        out_specs=pl.BlockSpec((tm, d), lambda i, e: (i, 0)),
        out_shape=jax.ShapeDtypeStruct((n, d), F32),
        scratch_shapes=[pltpu.VMEM((tm, d), BF16)],
        compiler_params=_cparams(("arbitrary", "arbitrary"), 48),
        name="moe_ln",
    )(x1, comb, w_gate, w_up, w_down, ln_g, ln_b)


def _ple_body(x_ref, p_ref, wg_ref, wp_ref, o_ref):
    x = x_ref[...]
    gate = jax.nn.sigmoid(jnp.dot(x.astype(BF16), wg_ref[...], preferred_element_type=F32))
    proj = jnp.dot(p_ref[...].astype(BF16), wp_ref[...], preferred_element_type=F32)
    o_ref[...] = x + gate * proj


def _ple(x2, p2d, w_gate, w_proj, tm):
    n, d = x2.shape
    dp = p2d.shape[1]
    const = lambda i: (0, 0)
    return pl.pallas_call(
        _ple_body,
        grid=(n // tm,),
        in_specs=[pl.BlockSpec((tm, d), lambda i: (i, 0)),
                  pl.BlockSpec((tm, dp), lambda i: (i, 0)),
                  pl.BlockSpec((d, d), const),
                  pl.BlockSpec((dp, d), const)],
        out_specs=pl.BlockSpec((tm, d), lambda i: (i, 0)),
        out_shape=jax.ShapeDtypeStruct((n, d), F32),
        compiler_params=_cparams(("arbitrary",), 32),
        name="ple_gate",
    )(x2, p2d, w_gate, w_proj)


def _tile(n, want):
    t = min(want, n)
    assert n % t == 0
    return t


def _layer(x, p_i, w_in, b_igate, b_fgate, conv_w, conv_b, gn_w, w_out, ln1_g, ln1_b,
           rg_w, rg_b, re_w, re_b, e_gate, e_up, e_down, ln2_g, ln2_b, ple_gate_w, ple_proj_w, alpha):
    b, s, d = x.shape
    n = b * s
    x2d = x.reshape(n, d)

    qkv, q_idx, k4, gates, qk_m, v_m, o_m = _in_proj(x2d, _pack_w_in(w_in), _tile(n, 512))

    gates3 = gates.reshape(b, s, LANES)
    y_a = _dsa_attention(qkv.reshape(b, s, 3 * D_A), q_idx.reshape(b, s, N_IDX_HEADS * IDX_DIM),
                         k4.reshape(b, s, LANES), gates3, _tile(s, 128), _tile(s, 256))

    gate_bias = jnp.concatenate([b_igate, b_fgate]).astype(F32)
    bias_slab = jnp.zeros((1, LANES), F32).at[0, G_I:G_I + 2 * N_HEADS_M].set(gate_bias)
    gates_t = jnp.swapaxes(gates3[:, :, G_I:G_I + 2 * N_HEADS_M], 1, 2)
    y_m = _mlstm(qk_m.reshape(b, s, 2 * D_M), v_m.reshape(b, s, D_M), o_m.reshape(b, s, D_M), gates3, gates_t,
                 conv_w, conv_b.reshape(1, -1), bias_slab, gate_bias.reshape(-1, 1), gn_w.reshape(1, -1),
                 _tile(s, 256))

    w_router = jnp.concatenate([re_w, rg_w, jnp.zeros((d, LANES - N_EXPERTS - N_GROUPS), F32)], axis=1)
    b_router = jnp.concatenate([re_b, rg_b, jnp.zeros((LANES - N_EXPERTS - N_GROUPS,), F32)]).reshape(1, LANES)
    x1, comb = _mix_ln_router(y_a.reshape(n, D_A), y_m.reshape(n, D_M), x2d, w_out.astype(BF16),
                              ln1_g.reshape(1, d), ln1_b.reshape(1, d), w_router, b_router, alpha, _tile(n, 512))

    x2 = _moe_ln(x1, comb, e_gate.astype(BF16), e_up.astype(BF16), e_down.astype(BF16),
                 ln2_g.reshape(1, d), ln2_b.reshape(1, d), alpha, _tile(n, 1024))

    out = _ple(x2, p_i.reshape(n, -1), ple_gate_w.astype(BF16), ple_proj_w.astype(BF16), _tile(n, 512))
    return out.reshape(b, s, d)


def kernel(x, p, w_in, b_igate, b_fgate, conv_w, conv_b, gn_w, w_out, ln1_g, ln1_b, router_group_w, router_group_b,
           router_expert_w, router_expert_b, expert_w_gate, expert_w_up, expert_w_down, ln2_g, ln2_b, ple_gate_w,
           ple_proj_w):
    depth = w_in.shape[0]
    alpha = (2.0 * depth) ** 0.25
    for i in range(depth):
        x = _layer(x, p[i], w_in[i], b_igate[i], b_fgate[i], conv_w[i], conv_b[i], gn_w[i], w_out[i], ln1_g[i],
                   ln1_b[i], router_group_w[i], router_group_b[i], router_expert_w[i], router_expert_b[i],
                   expert_w_gate[i], expert_w_up[i], expert_w_down[i], ln2_g[i], ln2_b[i], ple_gate_w[i],
                   ple_proj_w[i], alpha)
    return x
```

```python
import functools

import numpy as np
import jax
import jax.numpy as jnp
from jax import lax
from jax.experimental import pallas as pl
from jax.experimental.pallas import tpu as pltpu

F32 = jnp.float32
BF16 = jnp.bfloat16
I32 = jnp.int32

N_HEADS_A = 8
HEAD_DIM_A = 64
D_A = N_HEADS_A * HEAD_DIM_A
N_IDX_HEADS = 8
IDX_DIM = 32
TOPK_MAX = 256
N_HEADS_M = 4
HEAD_DIM_M = 128
D_M = N_HEADS_M * HEAD_DIM_M
CONV_K = 4
N_GROUPS = 4
EXPERTS_PER_GROUP = 8
N_EXPERTS = N_GROUPS * EXPERTS_PER_GROUP
D_EXPERT = 256
LN_EPS = 1e-5
NEG_BIG = -1e30
IN_SPLITS = (D_A, D_A, D_A, N_IDX_HEADS * IDX_DIM, IDX_DIM, N_IDX_HEADS, D_M, D_M, D_M, D_M, N_HEADS_M, N_HEADS_M)

LANES = 128
SUBLANES = 8
MIB = 1024 * 1024

C_QKV = 0
C_QIDX = C_QKV + 3 * D_A
C_K4 = C_QIDX + N_IDX_HEADS * IDX_DIM
C_GATE = C_K4 + LANES
C_QKM = C_GATE + LANES
C_VM = C_QKM + 2 * D_M
C_OM = C_VM + D_M
C_END = C_OM + D_M
N_QKV_SLABS = 3 * D_A // LANES
SLABS_PER_TENSOR = D_A // LANES
G_W = 0
G_I = N_IDX_HEADS
G_F = G_I + N_HEADS_M

INT_MIN = -2 ** 31

TM_IN_PROJ = 512
TQ_DSA = 128
TK_DSA = 512
CHUNK_MLSTM = 256
TM_MIX = 512
TM_MOE = 1024
TM_PLE = 512

NT_DIMS = (((1,), (1,)), ((), ()))


def _cparams(semantics, vmem_mib):
    return pltpu.CompilerParams(dimension_semantics=semantics, vmem_limit_bytes=int(vmem_mib * MIB))


def _layer_norm(r, g, b):
    mu = jnp.mean(r, axis=-1, keepdims=True)
    d = r - mu
    var = jnp.mean(d * d, axis=-1, keepdims=True)
    return d * lax.rsqrt(var + LN_EPS) * g + b


def _pack_w_in(w_in):
    d = w_in.shape[0]
    cuts = [int(c) for c in np.cumsum(IN_SPLITS)[:-1]]
    q_a, k_a, v_a, q_idx, k_idx, w_idx, q_m, k_m, v_m, o_m, i_m, f_m = jnp.split(w_in, cuts, axis=1)
    k4 = jnp.tile(k_idx, (1, LANES // IDX_DIM))
    gate = jnp.concatenate([w_idx, i_m, f_m, jnp.zeros((d, LANES - G_F - N_HEADS_M), w_in.dtype)], axis=1)
    return jnp.concatenate([q_a, k_a, v_a, q_idx, k4, gate, q_m, k_m, v_m, o_m], axis=1).astype(BF16)


def _in_proj_body(x_ref, w_ref, qkv_ref, qi_ref, k4_ref, g_ref, qkm_ref, vm_ref, om_ref):
    xb = x_ref[...].astype(BF16)

    def proj(lo, hi):
        return jnp.dot(xb, w_ref[:, lo:hi], preferred_element_type=F32)

    qkv = proj(C_QKV, C_QIDX).astype(BF16)
    for c in range(N_QKV_SLABS):
        qkv_ref[c] = qkv[:, c * LANES:(c + 1) * LANES]
    qi_ref[...] = proj(C_QIDX, C_K4).astype(BF16)
    k4_ref[...] = proj(C_K4, C_GATE).astype(BF16)
    g_ref[...] = proj(C_GATE, C_QKM)
    qkm_ref[...] = proj(C_QKM, C_VM)
    vm_ref[...] = proj(C_VM, C_OM).astype(BF16)
    om_ref[...] = proj(C_OM, C_END)


def _in_proj(x2d, w_packed, tm):
    n, d = x2d.shape
    widths = [(C_K4 - C_QIDX, BF16), (C_GATE - C_K4, BF16), (C_QKM - C_GATE, F32),
              (C_VM - C_QKM, F32), (C_OM - C_VM, BF16), (C_END - C_OM, F32)]
    return pl.pallas_call(
        _in_proj_body,
        grid=(n // tm,),
        in_specs=[pl.BlockSpec((tm, d), lambda i: (i, 0)),
                  pl.BlockSpec((d, C_END), lambda i: (0, 0))],
        out_specs=([pl.BlockSpec((N_QKV_SLABS, tm, LANES), lambda i: (0, i, 0))]
                   + [pl.BlockSpec((tm, w), lambda i: (i, 0)) for w, _ in widths]),
        out_shape=([jax.ShapeDtypeStruct((N_QKV_SLABS, n, LANES), BF16)]
                   + [jax.ShapeDtypeStruct((n, w), dt) for w, dt in widths]),
        compiler_params=_cparams(("arbitrary",), 48),
        name="in_proj",
    )(x2d, w_packed)


def _float_to_ordered_int(value):
    bits = lax.bitcast_convert_type(value, I32)
    return bits ^ ((bits >> 31) & jnp.int32(0x7FFFFFFF))


def _ordered_int_to_float(key):
    bits = key ^ ((key >> 31) & jnp.int32(0x7FFFFFFF))
    return lax.bitcast_convert_type(bits, F32)


SEARCH_FREE_PASSES = 16
SEARCH_MAX_PASSES = SEARCH_FREE_PASSES + 33
ZERO_TIE_MARGIN = 128


def _dsa_body(slope_ref, qa_ref, ka_ref, va_ref, qi_ref, k4_ref, g_ref, o_ref,
              sc_ref, qm_ref, wb_ref, qh_ref, mask_ref, acc_ref, m_ref, l_ref, *, seq, topk, tq, tk):
    t0 = pl.program_id(1) * tq
    n_kb = (t0 + tq + tk - 1) // tk
    lane = lax.broadcasted_iota(I32, (tq, LANES), 1)
    lane_tiles = tk // LANES

    for h in range(N_IDX_HEADS):
        per_slab = LANES // IDX_DIM
        slab = qi_ref[:, (h // per_slab) * LANES:(h // per_slab + 1) * LANES]
        lo = (h % per_slab) * IDX_DIM
        qm_ref[h * tq:(h + 1) * tq, :] = jnp.where((lane >= lo) & (lane < lo + IDX_DIM), slab, jnp.zeros_like(slab))
        w_col = g_ref[:, G_W + h:G_W + h + 1] * ((IDX_DIM * N_IDX_HEADS) ** -0.5)
        wb_ref[h] = jnp.broadcast_to(w_col, (tq, LANES))
    heads_per_slab = LANES // HEAD_DIM_A
    for h in range(N_HEADS_A):
        p, hh = divmod(h, heads_per_slab)
        slab = qa_ref[p].astype(F32) * (HEAD_DIM_A ** -0.5)
        in_head = (lane >= hh * HEAD_DIM_A) & (lane < (hh + 1) * HEAD_DIM_A)
        qh_ref[p, hh * tq:(hh + 1) * tq, :] = jnp.where(in_head, slab, 0.0).astype(BF16)

    row_t = t0 + lax.broadcasted_iota(I32, (tq, tk), 0)
    col = lax.broadcasted_iota(I32, (tq, tk), 1)

    def score_block(j, carry):
        hi_part, lo_part = carry
        kb = k4_ref[pl.ds(pl.multiple_of(j * tk, tk), tk), :]
        logits = lax.dot_general(qm_ref[...], kb, NT_DIMS, preferred_element_type=F32)
        tiles = []
        for c in range(lane_tiles):
            acc = jnp.zeros((tq, LANES), F32)
            for h in range(N_IDX_HEADS):
                acc = acc + wb_ref[h] * jnp.maximum(logits[h * tq:(h + 1) * tq, c * LANES:(c + 1) * LANES], 0.0)
            tiles.append(acc)
        score = jnp.concatenate(tiles, axis=1)
        causal = j * tk + col <= row_t
        kept = jnp.where(causal, score, NEG_BIG)
        sc_ref[j] = kept
        floor = jnp.where(causal, score, jnp.inf)
        for c in range(lane_tiles):
            hi_part = jnp.maximum(hi_part, kept[:, c * LANES:(c + 1) * LANES])
            lo_part = jnp.minimum(lo_part, floor[:, c * LANES:(c + 1) * LANES])
        return hi_part, lo_part

    hi_part, lo_part = lax.fori_loop(0, n_kb, score_block,
                                     (jnp.full((tq, LANES), -jnp.inf, F32), jnp.full((tq, LANES), jnp.inf, F32)))
    row_max = jnp.max(hi_part, axis=1, keepdims=True)
    row_min = jnp.min(lo_part, axis=1, keepdims=True)

    n_tail = (seq - n_kb * tk).astype(F32)

    def count(thr, strict):
        thr_b = jnp.broadcast_to(thr, (tq, LANES))

        def body(j, acc):
            for c in range(lane_tiles):
                s = sc_ref[j, :, c * LANES:(c + 1) * LANES]
                acc = acc + jnp.where((s > thr_b) if strict else (s >= thr_b), 1.0, 0.0)
            return acc

        acc = lax.fori_loop(0, n_kb, body, jnp.zeros((tq, LANES), F32))
        tail_hit = (NEG_BIG > thr) if strict else (NEG_BIG >= thr)
        return jnp.sum(acc, axis=1, keepdims=True) + jnp.where(tail_hit, n_tail, 0.0)

    n_causal = (t0 + 1 + lax.broadcasted_iota(I32, (tq, 1), 0)).astype(F32)
    few = n_causal < topk
    lo0 = jnp.where(few, _float_to_ordered_int(jnp.full((tq, 1), NEG_BIG, F32)), _float_to_ordered_int(row_min))
    cnt_lo0 = jnp.where(few, float(seq), n_causal)
    hi0 = _float_to_ordered_int(row_max) + 1
    log_k = float(np.log(topk))

    def log_excess(cnt):
        return jnp.log(jnp.maximum(cnt, 0.5)) - log_k

    def search_step(state):
        it, _, lo, hi, cnt_lo, f_lo, f_hi, last, done_i = state
        done = done_i != 0
        it_v = jnp.zeros((tq, 1), I32) + it
        lo_v = _ordered_int_to_float(lo)
        hi_v = _ordered_int_to_float(hi)
        guess = _float_to_ordered_int(lo_v + (f_lo / (f_lo - f_hi)) * (hi_v - lo_v))
        guess = jnp.where(it_v == 0, 0, guess)
        guess = jnp.where((it_v == 1) & (lo == 0) & (cnt_lo < topk + ZERO_TIE_MARGIN), 1, guess)
        middle = lo + lax.shift_right_logical(hi - lo, jnp.ones((tq, 1), I32))
        usable = (guess > lo) & (guess < hi) & (it_v < SEARCH_FREE_PASSES)
        cand = jnp.where(usable, guess, middle)
        cnt = count(_ordered_int_to_float(cand), strict=False)
        f_c = log_excess(cnt)
        take = (cnt >= topk) & ~done
        drop = (cnt < topk) & ~done
        f_hi = jnp.where(take, jnp.where(last == 1, 0.5 * f_hi, f_hi), jnp.where(drop, f_c, f_hi))
        f_lo = jnp.where(drop, jnp.where(last == -1, 0.5 * f_lo, f_lo), jnp.where(take, f_c, f_lo))
        last = jnp.where(take, 1, jnp.where(drop, -1, last))
        lo = jnp.where(take, cand, lo)
        cnt_lo = jnp.where(take, cnt, cnt_lo)
        hi = jnp.where(drop, cand, hi)
        done = done | (cnt_lo == topk) | ((hi - lo) == 1)
        active = jnp.sum(jnp.where(done, 0.0, 1.0))
        return it + 1, active, lo, hi, cnt_lo, f_lo, f_hi, last, done.astype(I32)

    def search_on(state):
        return (state[0] < SEARCH_MAX_PASSES) & (state[1] > 0.0)

    state = lax.while_loop(search_on, search_step,
                           (jnp.int32(0), jnp.float32(1.0), lo0, hi0, cnt_lo0, log_excess(cnt_lo0),
                            log_excess(jnp.zeros((tq, 1), F32)), jnp.zeros((tq, 1), I32), few.astype(I32)))
    thr = _ordered_int_to_float(state[2])
    cnt_thr = state[4]

    @pl.when(jnp.max(cnt_thr) > topk)
    def _():
        need = topk - count(thr, strict=True)
        tri_r = lax.broadcasted_iota(I32, (tk, tk), 0)
        tri_c = lax.broadcasted_iota(I32, (tk, tk), 1)
        before = (tri_r < tri_c).astype(BF16)

        def drop_late_ties(j, seen):
            s = sc_ref[j]
            eq = s == thr
            eq_f = eq.astype(F32)
            rank = seen + jnp.dot(eq_f.astype(BF16), before, preferred_element_type=F32)
            sc_ref[j] = jnp.where(eq & (rank >= need), -jnp.inf, s)
            return seen + jnp.sum(eq_f, axis=1, keepdims=True)

        lax.fori_loop(0, n_kb, drop_late_ties, jnp.zeros((tq, 1), F32))

    m_ref[...] = jnp.full(m_ref.shape, NEG_BIG, F32)
    l_ref[...] = jnp.zeros(l_ref.shape, F32)
    acc_ref[...] = jnp.zeros(acc_ref.shape, F32)
    def attend(j, carry):
        k0 = pl.multiple_of(j * tk, tk)
        mask_ref[...] = jnp.where((sc_ref[j] >= thr) & (j * tk + col <= row_t), 0.0, -jnp.inf)
        key_pos = (j * tk + lax.broadcasted_iota(I32, (1, tk), 1)).astype(F32)

        def slab_step(p, inner):
            k2 = ka_ref[p, pl.ds(k0, tk), :]
            v2 = va_ref[p, pl.ds(k0, tk), :]
            s2 = lax.dot_general(qh_ref[p], k2, NT_DIMS, preferred_element_type=F32)
            acc_old = acc_ref[p]
            probs2, updated = [], []
            for hh in range(heads_per_slab):
                slope = slope_ref[p * heads_per_slab + hh]
                s = (s2[hh * tq:(hh + 1) * tq] + slope * key_pos) + mask_ref[...]
                parts = [s[:, c * LANES:(c + 1) * LANES] for c in range(lane_tiles)]
                mx = parts[0]
                for c in range(1, lane_tiles):
                    mx = jnp.maximum(mx, parts[c])
                m_old = m_ref[p, hh]
                m_new = jnp.maximum(m_old, jnp.max(mx, axis=1, keepdims=True))
                alpha = jnp.exp(m_old - m_new)
                probs = [jnp.exp(pt - m_new) for pt in parts]
                l_new = alpha * l_ref[p, hh]
                for pt in probs:
                    l_new = l_new + pt
                m_ref[p, hh] = m_new
                l_ref[p, hh] = l_new
                probs2.append(jnp.concatenate(probs, axis=1).astype(BF16))
                updated.append(alpha * acc_old)
            pv2 = jnp.dot(jnp.concatenate(probs2, axis=0), v2, preferred_element_type=F32)
            acc_new = acc_old
            for hh in range(heads_per_slab):
                in_head = (lane >= hh * HEAD_DIM_A) & (lane < (hh + 1) * HEAD_DIM_A)
                acc_new = jnp.where(in_head, updated[hh] + pv2[hh * tq:(hh + 1) * tq], acc_new)
            acc_ref[p] = acc_new
            return inner

        lax.fori_loop(0, SLABS_PER_TENSOR, slab_step, 0)
        return carry

    lax.fori_loop(0, n_kb, attend, 0)

    for p in range(SLABS_PER_TENSOR):
        denom = jnp.zeros((tq, LANES), F32)
        for hh in range(heads_per_slab):
            in_head = (lane >= hh * HEAD_DIM_A) & (lane < (hh + 1) * HEAD_DIM_A)
            l_h = jnp.sum(l_ref[p, hh], axis=1, keepdims=True)
            denom = jnp.where(in_head, l_h, denom)
        o_ref[:, p * LANES:(p + 1) * LANES] = (acc_ref[p] / denom).astype(BF16)


def _dsa_attention(qkv, q_idx, k4, gates, tq, tk):
    _, b, s, _ = qkv.shape
    topk = min(TOPK_MAX, s // 4)
    once = pl.Buffered(1)
    heads_per_slab = LANES // HEAD_DIM_A
    slopes = jnp.asarray([2.0 ** (-8.0 * (h + 1) / N_HEADS_A) for h in range(N_HEADS_A)], F32)
    body = functools.partial(_dsa_body, seq=s, topk=topk, tq=tq, tk=tk)
    return pl.pallas_call(
        body,
        grid=(b, s // tq),
        in_specs=[pl.BlockSpec(memory_space=pltpu.SMEM),
                  pl.BlockSpec((SLABS_PER_TENSOR, None, tq, LANES), lambda bi, qi: (0, bi, qi, 0)),
                  pl.BlockSpec((SLABS_PER_TENSOR, None, s, LANES), lambda bi, qi: (1, bi, 0, 0), pipeline_mode=once),
                  pl.BlockSpec((SLABS_PER_TENSOR, None, s, LANES), lambda bi, qi: (2, bi, 0, 0), pipeline_mode=once),
                  pl.BlockSpec((None, tq, N_IDX_HEADS * IDX_DIM), lambda bi, qi: (bi, qi, 0)),
                  pl.BlockSpec((None, s, LANES), lambda bi, qi: (bi, 0, 0), pipeline_mode=once),
                  pl.BlockSpec((None, tq, LANES), lambda bi, qi: (bi, qi, 0))],
        out_specs=pl.BlockSpec((None, tq, D_A), lambda bi, qi: (bi, qi, 0)),
        out_shape=jax.ShapeDtypeStruct((b, s, D_A), BF16),
        scratch_shapes=[pltpu.VMEM((s // tk, tq, tk), F32),
                        pltpu.VMEM((N_IDX_HEADS * tq, LANES), BF16),
                        pltpu.VMEM((N_IDX_HEADS, tq, LANES), F32),
                        pltpu.VMEM((SLABS_PER_TENSOR, heads_per_slab * tq, LANES), BF16),
                        pltpu.VMEM((tq, tk), F32),
                        pltpu.VMEM((SLABS_PER_TENSOR, tq, LANES), F32),
                        pltpu.VMEM((SLABS_PER_TENSOR, heads_per_slab, tq, LANES), F32),
                        pltpu.VMEM((SLABS_PER_TENSOR, heads_per_slab, tq, LANES), F32)],
        compiler_params=_cparams(("arbitrary", "arbitrary"), 48),
        name="dsa_attention",
    )(slopes, qkv, qkv, qkv, q_idx, k4, gates)


def _log_sigmoid(x):
    return jnp.minimum(x, 0.0) - jnp.log(1.0 + jnp.exp(-jnp.abs(x)))


def _split3(a):
    hi = a.astype(BF16)
    r1 = a - hi.astype(F32)
    mid = r1.astype(BF16)
    lo = (r1 - mid.astype(F32)).astype(BF16)
    return hi, mid, lo


def _mlstm_body(qk_ref, v_ref, o_ref, g_ref, gt_ref, cw_ref, cb_ref, bg_ref, bgt_ref, gn_ref, y_ref,
                xe_ref, c_ref, m_ref, *, chunk):
    @pl.when(pl.program_id(1) == 0)
    def _():
        xe_ref[0:SUBLANES, :] = jnp.zeros((SUBLANES, 2 * D_M), F32)
        c_ref[...] = jnp.zeros(c_ref.shape, F32)
        m_ref[...] = jnp.zeros(m_ref.shape, F32)

    xe_ref[SUBLANES:SUBLANES + chunk, :] = qk_ref[...]
    conv = cb_ref[...]
    for j in range(CONV_K):
        start = SUBLANES - (CONV_K - 1) + j
        conv = conv + cw_ref[j:j + 1, :] * xe_ref[start:start + chunk, :]
    qk = conv * jax.nn.sigmoid(conv)
    xe_ref[0:SUBLANES, :] = xe_ref[chunk:chunk + SUBLANES, :]

    gb = g_ref[...] + bg_ref[...]
    gbt = gt_ref[...] + bgt_ref[...]
    r = lax.broadcasted_iota(I32, (chunk, chunk), 0)
    c = lax.broadcasted_iota(I32, (chunk, chunk), 1)
    causal = r >= c
    incl = causal.astype(BF16)
    incl_t = (r <= c).astype(BF16)
    cum = jnp.zeros((chunk, LANES), F32)
    for term in _split3(_log_sigmoid(gb)):
        cum = cum + jnp.dot(incl, term, preferred_element_type=F32)
    cum_t = jnp.zeros((2 * N_HEADS_M, chunk), F32)
    for term in _split3(_log_sigmoid(gbt)):
        cum_t = cum_t + jnp.dot(term, incl_t, preferred_element_type=F32)

    lane = lax.broadcasted_iota(I32, (chunk, LANES), 1)
    ones_col = jnp.where(lane == 0, 1.0, 0.0).astype(BF16)

    for h in range(N_HEADS_M):
        sl = slice(h * HEAD_DIM_M, (h + 1) * HEAD_DIM_M)
        q_h = (qk[:, sl] * (HEAD_DIM_M ** -0.5)).astype(BF16)
        k_f = qk[:, D_M + h * HEAD_DIM_M:D_M + (h + 1) * HEAD_DIM_M]
        v_aug = jnp.concatenate([v_ref[:, sl], ones_col], axis=1)
        b_col = cum[:, G_F + h:G_F + h + 1]
        i_col = gb[:, G_I + h:G_I + h + 1]
        b_row = cum_t[N_HEADS_M + h:N_HEADS_M + h + 1, :]
        i_row = gbt[h:h + 1, :]
        m_prev = m_ref[h:h + 1, 0:1]
        c_aug = c_ref[h]

        d_log = jnp.where(causal, b_col - b_row + i_row, -jnp.inf)
        inter = b_col + m_prev
        m_t = jnp.maximum(inter, jnp.max(d_log, axis=1, keepdims=True))
        qk_t = lax.dot_general(q_h, k_f.astype(BF16), NT_DIMS, preferred_element_type=F32)
        w_intra = qk_t * jnp.exp(d_log - m_t)
        w_inter = jnp.exp(inter - m_t)
        intra = jnp.dot(w_intra.astype(BF16), v_aug[:, 0:HEAD_DIM_M], preferred_element_type=F32)
        carried = jnp.dot(q_h, c_aug.astype(BF16), preferred_element_type=F32)
        num = intra + w_inter * carried[:, 0:HEAD_DIM_M]
        den = jnp.sum(w_intra, axis=1, keepdims=True) + w_inter * carried[:, HEAD_DIM_M:HEAD_DIM_M + 1]
        hid = num / jnp.maximum(jnp.abs(den), jnp.exp(-m_t))

        mu = jnp.mean(hid, axis=1, keepdims=True)
        dev = hid - mu
        var = jnp.mean(dev * dev, axis=1, keepdims=True)
        normed = dev * lax.rsqrt(var + LN_EPS) * gn_ref[:, sl]
        y_ref[:, sl] = (jax.nn.sigmoid(o_ref[:, sl]) * normed).astype(BF16)

        b_last = b_col[chunk - 1:chunk, :]
        w_log = b_last - b_col + i_col
        m_new = jnp.maximum(b_last + m_prev, jnp.max(w_log, axis=0, keepdims=True))
        decay = jnp.exp(b_last + m_prev - m_new)
        kw_t = (k_f * jnp.exp(w_log - m_new)).T.astype(BF16)
        c_ref[h] = decay * c_aug + jnp.dot(kw_t, v_aug, preferred_element_type=F32)
        m_ref[h:h + 1, :] = jnp.broadcast_to(m_new, (1, LANES))


def _mlstm(qk_m, v_m, o_m, gates, gates_t, conv_w, conv_b, bias_slab, bias_col, gn_w, chunk):
    b, s, _ = qk_m.shape
    body = functools.partial(_mlstm_body, chunk=chunk)
    const = lambda bi, ci: (0, 0)
    return pl.pallas_call(
        body,
        grid=(b, s // chunk),
        in_specs=[pl.BlockSpec((None, chunk, 2 * D_M), lambda bi, ci: (bi, ci, 0)),
                  pl.BlockSpec((None, chunk, D_M), lambda bi, ci: (bi, ci, 0)),
                  pl.BlockSpec((None, chunk, D_M), lambda bi, ci: (bi, ci, 0)),
                  pl.BlockSpec((None, chunk, LANES), lambda bi, ci: (bi, ci, 0)),
                  pl.BlockSpec((None, 2 * N_HEADS_M, chunk), lambda bi, ci: (bi, 0, ci)),
                  pl.BlockSpec((CONV_K, 2 * D_M), const),
                  pl.BlockSpec((1, 2 * D_M), const),
                  pl.BlockSpec((1, LANES), const),
                  pl.BlockSpec((2 * N_HEADS_M, 1), const),
                  pl.BlockSpec((1, D_M), const)],
        out_specs=pl.BlockSpec((None, chunk, D_M), lambda bi, ci: (bi, ci, 0)),
        out_shape=jax.ShapeDtypeStruct((b, s, D_M), BF16),
        scratch_shapes=[pltpu.VMEM((chunk + SUBLANES, 2 * D_M), F32),
                        pltpu.VMEM((N_HEADS_M, HEAD_DIM_M, 2 * LANES), F32),
                        pltpu.VMEM((SUBLANES, LANES), F32)],
        compiler_params=_cparams(("arbitrary", "arbitrary"), 32),
        name="mlstm",
    )(qk_m, v_m, o_m, gates, gates_t, conv_w, conv_b, bias_slab, bias_col, gn_w)


def _route(logits):
    lane = lax.broadcasted_iota(I32, logits.shape, 1).astype(F32)
    g_mask = (lane >= N_EXPERTS) & (lane < N_EXPERTS + N_GROUPS)
    g_logit = jnp.where(g_mask, logits, -jnp.inf)
    g_exp = jnp.exp(g_logit - jnp.max(g_logit, axis=1, keepdims=True))
    g_prob = g_exp / jnp.sum(g_exp, axis=1, keepdims=True)
    g_p = jnp.max(g_prob, axis=1, keepdims=True)
    g_sel = jnp.min(jnp.where(g_mask & (g_prob == g_p), lane, float(LANES)), axis=1, keepdims=True) - N_EXPERTS
    e_mask = (lane >= g_sel * EXPERTS_PER_GROUP) & (lane < (g_sel + 1.0) * EXPERTS_PER_GROUP)
    e_logit = jnp.where(e_mask, logits, -jnp.inf)
    e_exp = jnp.exp(e_logit - jnp.max(e_logit, axis=1, keepdims=True))
    e_prob = e_exp / jnp.sum(e_exp, axis=1, keepdims=True)
    p1 = jnp.max(jnp.where(e_mask, e_prob, -1.0), axis=1, keepdims=True)
    i1 = jnp.min(jnp.where(e_mask & (e_prob == p1), lane, float(LANES)), axis=1, keepdims=True)
    rest = e_mask & (lane != i1)
    p2 = jnp.max(jnp.where(rest, e_prob, -1.0), axis=1, keepdims=True)
    i2 = jnp.min(jnp.where(rest & (e_prob == p2), lane, float(LANES)), axis=1, keepdims=True)
    total = p1 + p2
    return (jnp.where(lane == i1, p1 / total, 0.0) + jnp.where(lane == i2, p2 / total, 0.0)) * g_p


def _mix_body(ya_ref, ym_ref, x_ref, wo_ref, g_ref, b_ref, wr_ref, br_ref, x1_ref, comb_ref, *, alpha):
    mix = (jnp.dot(ya_ref[...], wo_ref[0:D_A, :], preferred_element_type=F32)
           + jnp.dot(ym_ref[...], wo_ref[D_A:D_A + D_M, :], preferred_element_type=F32))
    x1 = _layer_norm(alpha * x_ref[...] + mix, g_ref[...], b_ref[...])
    x1_ref[...] = x1
    logits = jnp.dot(x1, wr_ref[...], preferred_element_type=F32, precision=lax.Precision.HIGHEST) + br_ref[...]
    comb_ref[...] = _route(logits)


def _mix_ln_router(y_a, y_m, x2d, w_out, ln_g, ln_b, w_router, b_router, alpha, tm):
    n, d = x2d.shape
    const = lambda i: (0, 0)
    return pl.pallas_call(
        functools.partial(_mix_body, alpha=alpha),
        grid=(n // tm,),
        in_specs=[pl.BlockSpec((tm, D_A), lambda i: (i, 0)),
                  pl.BlockSpec((tm, D_M), lambda i: (i, 0)),
                  pl.BlockSpec((tm, d), lambda i: (i, 0)),
                  pl.BlockSpec((D_A + D_M, d), const),
                  pl.BlockSpec((1, d), const),
                  pl.BlockSpec((1, d), const),
                  pl.BlockSpec((d, LANES), const),
                  pl.BlockSpec((1, LANES), const)],
        out_specs=[pl.BlockSpec((tm, d), lambda i: (i, 0)),
                   pl.BlockSpec((tm, LANES), lambda i: (i, 0))],
        out_shape=[jax.ShapeDtypeStruct((n, d), F32), jax.ShapeDtypeStruct((n, LANES), F32)],
        compiler_params=_cparams(("arbitrary",), 32),
        name="mix_ln_router",
    )(y_a, y_m, x2d, w_out, ln_g, ln_b, w_router, b_router)


def _moe_body(x1_ref, comb_ref, wg_ref, wu_ref, wd_ref, g_ref, b_ref, y_ref, xb_ref, *, alpha):
    e = pl.program_id(1)

    @pl.when(e == 0)
    def _():
        xb_ref[...] = x1_ref[...].astype(BF16)
        y_ref[...] = jnp.zeros(y_ref.shape, F32)

    xb = xb_ref[...]
    gate = jnp.dot(xb, wg_ref[...], preferred_element_type=F32)
    up = jnp.dot(xb, wu_ref[...], preferred_element_type=F32)
    lane = lax.broadcasted_iota(I32, comb_ref.shape, 1)
    weight = jnp.sum(jnp.where(lane == e, comb_ref[...], 0.0), axis=1, keepdims=True)
    hidden = gate * jax.nn.sigmoid(gate) * up * weight
    y_ref[...] += jnp.dot(hidden.astype(BF16), wd_ref[...], preferred_element_type=F32)

    @pl.when(e == N_EXPERTS - 1)
    def _():
        y_ref[...] = _layer_norm(alpha * x1_ref[...] + y_ref[...], g_ref[...], b_ref[...])


def _moe_ln(x1, comb, w_gate, w_up, w_down, ln_g, ln_b, alpha, tm):
    n, d = x1.shape
    const = lambda i, e: (0, 0)
    return pl.pallas_call(
        functools.partial(_moe_body, alpha=alpha),
        grid=(n // tm, N_EXPERTS),
        in_specs=[pl.BlockSpec((tm, d), lambda i, e: (i, 0)),
                  pl.BlockSpec((tm, LANES), lambda i, e: (i, 0)),
                  pl.BlockSpec((None, d, D_EXPERT), lambda i, e: (e, 0, 0)),
                  pl.BlockSpec((None, d, D_EXPERT), lambda i, e: (e, 0, 0)),
                  pl.BlockSpec((None, D_EXPERT, d), lambda i, e: (e, 0, 0)),
                  pl.BlockSpec((1, d), const),
                  pl.BlockSpec((1, d), const)],
        out_specs=pl.BlockSpec((tm, d), lambda i, e: (i, 0)),
        out_shape=jax.ShapeDtypeStruct((n, d), F32),
        scratch_shapes=[pltpu.VMEM((tm, d), BF16)],
        compiler_params=_cparams(("arbitrary", "arbitrary"), 48),
        name="moe_ln",
    )(x1, comb, w_gate, w_up, w_down, ln_g, ln_b)


def _ple_body(x_ref, p_ref, wg_ref, wp_ref, o_ref):
    x = x_ref[...]
    gate = jax.nn.sigmoid(jnp.dot(x.astype(BF16), wg_ref[...], preferred_element_type=F32))
    proj = jnp.dot(p_ref[...].astype(BF16), wp_ref[...], preferred_element_type=F32)
    o_ref[...] = x + gate * proj


def _ple(x2, p2d, w_gate, w_proj, tm):
    n, d = x2.shape
    dp = p2d.shape[1]
    const = lambda i: (0, 0)
    return pl.pallas_call(
        _ple_body,
        grid=(n // tm,),
        in_specs=[pl.BlockSpec((tm, d), lambda i: (i, 0)),
                  pl.BlockSpec((tm, dp), lambda i: (i, 0)),
                  pl.BlockSpec((d, d), const),
                  pl.BlockSpec((dp, d), const)],
        out_specs=pl.BlockSpec((tm, d), lambda i: (i, 0)),
        out_shape=jax.ShapeDtypeStruct((n, d), F32),
        compiler_params=_cparams(("arbitrary",), 32),
        name="ple_gate",
    )(x2, p2d, w_gate, w_proj)


def _tile(n, want):
    t = min(want, n)
    assert n % t == 0
    return t


def _layer(x, p_i, w_in, b_igate, b_fgate, conv_w, conv_b, gn_w, w_out, ln1_g, ln1_b,
           rg_w, rg_b, re_w, re_b, e_gate, e_up, e_down, ln2_g, ln2_b, ple_gate_w, ple_proj_w, alpha):
    b, s, d = x.shape
    n = b * s
    x2d = x.reshape(n, d)

    qkv, q_idx, k4, gates, qk_m, v_m, o_m = _in_proj(x2d, _pack_w_in(w_in), _tile(n, TM_IN_PROJ))

    gates3 = gates.reshape(b, s, LANES)
    y_a = _dsa_attention(qkv.reshape(N_QKV_SLABS, b, s, LANES), q_idx.reshape(b, s, N_IDX_HEADS * IDX_DIM),
                         k4.reshape(b, s, LANES), gates3, _tile(s, TQ_DSA), _tile(s, TK_DSA))

    gate_bias = jnp.concatenate([b_igate, b_fgate]).astype(F32)
    bias_slab = jnp.zeros((1, LANES), F32).at[0, G_I:G_I + 2 * N_HEADS_M].set(gate_bias)
    gates_t = jnp.swapaxes(gates3[:, :, G_I:G_I + 2 * N_HEADS_M], 1, 2)
    y_m = _mlstm(qk_m.reshape(b, s, 2 * D_M), v_m.reshape(b, s, D_M), o_m.reshape(b, s, D_M), gates3, gates_t,
                 conv_w, conv_b.reshape(1, -1), bias_slab, gate_bias.reshape(-1, 1), gn_w.reshape(1, -1),
                 _tile(s, CHUNK_MLSTM))

    w_router = jnp.concatenate([re_w, rg_w, jnp.zeros((d, LANES - N_EXPERTS - N_GROUPS), F32)], axis=1)
    b_router = jnp.concatenate([re_b, rg_b, jnp.zeros((LANES - N_EXPERTS - N_GROUPS,), F32)]).reshape(1, LANES)
    x1, comb = _mix_ln_router(y_a.reshape(n, D_A), y_m.reshape(n, D_M), x2d, w_out.astype(BF16),
                              ln1_g.reshape(1, d), ln1_b.reshape(1, d), w_router, b_router, alpha, _tile(n, TM_MIX))

    x2 = _moe_ln(x1, comb, e_gate.astype(BF16), e_up.astype(BF16), e_down.astype(BF16),
                 ln2_g.reshape(1, d), ln2_b.reshape(1, d), alpha, _tile(n, TM_MOE))

    out = _ple(x2, p_i.reshape(n, -1), ple_gate_w.astype(BF16), ple_proj_w.astype(BF16), _tile(n, TM_PLE))
    return out.reshape(b, s, d)


def kernel(x, p, w_in, b_igate, b_fgate, conv_w, conv_b, gn_w, w_out, ln1_g, ln1_b, router_group_w, router_group_b,
           router_expert_w, router_expert_b, expert_w_gate, expert_w_up, expert_w_down, ln2_g, ln2_b, ple_gate_w,
           ple_proj_w):
    depth = w_in.shape[0]
    alpha = (2.0 * depth) ** 0.25
    for i in range(depth):
        x = _layer(x, p[i], w_in[i], b_igate[i], b_fgate[i], conv_w[i], conv_b[i], gn_w[i], w_out[i], ln1_g[i],
                   ln1_b[i], router_group_w[i], router_group_b[i], router_expert_w[i], router_expert_b[i],
                   expert_w_gate[i], expert_w_up[i], expert_w_down[i], ln2_g[i], ln2_b[i], ple_gate_w[i],
                   ple_proj_w[i], alpha)
    return x
```

```python
import functools

import numpy as np
import jax
import jax.numpy as jnp
from jax import lax
from jax.experimental import pallas as pl
from jax.experimental.pallas import tpu as pltpu

F32 = jnp.float32
BF16 = jnp.bfloat16
I32 = jnp.int32

N_HEADS_A = 8
HEAD_DIM_A = 64
D_A = N_HEADS_A * HEAD_DIM_A
N_IDX_HEADS = 8
IDX_DIM = 32
TOPK_MAX = 256
N_HEADS_M = 4
HEAD_DIM_M = 128
D_M = N_HEADS_M * HEAD_DIM_M
CONV_K = 4
N_GROUPS = 4
EXPERTS_PER_GROUP = 8
N_EXPERTS = N_GROUPS * EXPERTS_PER_GROUP
D_EXPERT = 256
LN_EPS = 1e-5
NEG_BIG = -1e30
IN_SPLITS = (D_A, D_A, D_A, N_IDX_HEADS * IDX_DIM, IDX_DIM, N_IDX_HEADS, D_M, D_M, D_M, D_M, N_HEADS_M, N_HEADS_M)

LANES = 128
SUBLANES = 8
MIB = 1024 * 1024

C_QK = 0
C_QIDX = C_QK + 2 * D_A
C_K4 = C_QIDX + N_IDX_HEADS * IDX_DIM
C_GATE = C_K4 + LANES
C_QKM = C_GATE + LANES
C_VM = C_QKM + 2 * D_M
C_OM = C_VM + D_M
C_END = C_OM + D_M
SLABS_PER_TENSOR = D_A // LANES
HEADS_PER_SLAB = LANES // HEAD_DIM_A
G_W = 0
G_I = N_IDX_HEADS
G_F = G_I + N_HEADS_M

INT_MIN = -2 ** 31

TQ_DSA = LANES
TK_DSA = 512
TM_IN_PROJ = TK_DSA
CHUNK_MLSTM = 256
TM_MIX = 512
TM_MOE = 1024
TM_PLE = 512

NT_DIMS = (((1,), (1,)), ((), ()))


def _cparams(semantics, vmem_mib):
    return pltpu.CompilerParams(dimension_semantics=semantics, vmem_limit_bytes=int(vmem_mib * MIB))


def _layer_norm(r, g, b):
    mu = jnp.mean(r, axis=-1, keepdims=True)
    d = r - mu
    var = jnp.mean(d * d, axis=-1, keepdims=True)
    return d * lax.rsqrt(var + LN_EPS) * g + b


def _pack_w_in(w_in):
    d = w_in.shape[0]
    cuts = [int(c) for c in np.cumsum(IN_SPLITS)[:-1]]
    q_a, k_a, v_a, q_idx, k_idx, w_idx, q_m, k_m, v_m, o_m, i_m, f_m = jnp.split(w_in, cuts, axis=1)
    k4 = jnp.tile(k_idx, (1, LANES // IDX_DIM))
    gate = jnp.concatenate([w_idx, i_m, f_m, jnp.zeros((d, LANES - G_F - N_HEADS_M), w_in.dtype)], axis=1)
    packed = jnp.concatenate([q_a, k_a, q_idx, k4, gate, q_m, k_m, v_m, o_m], axis=1).astype(BF16)
    return packed, v_a.T.astype(BF16)


def _in_proj_body(x_ref, w_ref, wvt_ref, qk_ref, vt_ref, qi_ref, k4_ref, g_ref, qkm_ref, vm_ref, om_ref):
    xb = x_ref[...].astype(BF16)

    def proj(lo, hi):
        return jnp.dot(xb, w_ref[:, lo:hi], preferred_element_type=F32)

    qk = proj(C_QK, C_QIDX).astype(BF16)
    for c in range(2 * SLABS_PER_TENSOR):
        qk_ref[c] = qk[:, c * LANES:(c + 1) * LANES]
    v_t = lax.dot_general(wvt_ref[...], xb, NT_DIMS, preferred_element_type=F32).astype(BF16)
    for c in range(SLABS_PER_TENSOR):
        vt_ref[c, 0] = v_t[c * LANES:(c + 1) * LANES, :]
    qi_ref[...] = proj(C_QIDX, C_K4).astype(BF16)
    k4_ref[...] = proj(C_K4, C_GATE).astype(BF16)
    g_ref[...] = proj(C_GATE, C_QKM)
    qkm_ref[...] = proj(C_QKM, C_VM)
    vm_ref[...] = proj(C_VM, C_OM).astype(BF16)
    om_ref[...] = proj(C_OM, C_END)


def _in_proj(x2d, w_packed, w_vt, tm):
    n, d = x2d.shape
    widths = [(C_K4 - C_QIDX, BF16), (C_GATE - C_K4, BF16), (C_QKM - C_GATE, F32),
              (C_VM - C_QKM, F32), (C_OM - C_VM, BF16), (C_END - C_OM, F32)]
    return pl.pallas_call(
        _in_proj_body,
        grid=(n // tm,),
        in_specs=[pl.BlockSpec((tm, d), lambda i: (i, 0)),
                  pl.BlockSpec((d, C_END), lambda i: (0, 0)),
                  pl.BlockSpec((D_A, d), lambda i: (0, 0))],
        out_specs=([pl.BlockSpec((2 * SLABS_PER_TENSOR, tm, LANES), lambda i: (0, i, 0)),
                    pl.BlockSpec((SLABS_PER_TENSOR, 1, LANES, tm), lambda i: (0, i, 0, 0))]
                   + [pl.BlockSpec((tm, w), lambda i: (i, 0)) for w, _ in widths]),
        out_shape=([jax.ShapeDtypeStruct((2 * SLABS_PER_TENSOR, n, LANES), BF16),
                    jax.ShapeDtypeStruct((SLABS_PER_TENSOR, n // tm, LANES, tm), BF16)]
                   + [jax.ShapeDtypeStruct((n, w), dt) for w, dt in widths]),
        compiler_params=_cparams(("arbitrary",), 48),
        name="in_proj",
    )(x2d, w_packed, w_vt)


def _float_to_ordered_int(value):
    bits = lax.bitcast_convert_type(value, I32)
    return bits ^ ((bits >> 31) & jnp.int32(0x7FFFFFFF))


def _ordered_int_to_float(key):
    bits = key ^ ((key >> 31) & jnp.int32(0x7FFFFFFF))
    return lax.bitcast_convert_type(bits, F32)


SEARCH_FREE_PASSES = 16
SEARCH_MAX_PASSES = SEARCH_FREE_PASSES + 33
ZERO_TIE_MARGIN = 128
CHAINS = 4
POS_RADIX = 64


def _dsa_body(qa_ref, ka_ref, vt_ref, qi_ref, k4_ref, g_ref, o_ref,
              sc_ref, qm_ref, qh_ref, mask_ref, acc_ref, m_ref, l_ref, *, seq, topk, tq, tk):
    t0 = pl.program_id(1) * tq
    n_kb = (t0 + tq + tk - 1) // tk
    groups = tk // (CHAINS * SUBLANES)
    shape4 = (groups, CHAINS, SUBLANES, tq)

    def chunks(x):
        return x.reshape(shape4)

    def fold(x):
        out = x[0]
        for c in range(1, CHAINS):
            out = out + x[c]
        return out

    def rows8(row):
        return jnp.broadcast_to(row, (SUBLANES, tq))

    key_off = ((lax.broadcasted_iota(I32, shape4, 0) * CHAINS + lax.broadcasted_iota(I32, shape4, 1)) * SUBLANES
               + lax.broadcasted_iota(I32, shape4, 2))
    q_pos = t0 + lax.broadcasted_iota(I32, shape4, 3)
    lane = lax.broadcasted_iota(I32, (tq, LANES), 1)

    for h in range(N_IDX_HEADS):
        per_slab = LANES // IDX_DIM
        slab = qi_ref[:, (h // per_slab) * LANES:(h // per_slab + 1) * LANES]
        lo = (h % per_slab) * IDX_DIM
        qm_ref[h * tq:(h + 1) * tq, :] = jnp.where((lane >= lo) & (lane < lo + IDX_DIM), slab, jnp.zeros_like(slab))
    for h in range(N_HEADS_A):
        p, hh = divmod(h, HEADS_PER_SLAB)
        slope = 2.0 ** (-8.0 * (h + 1) / N_HEADS_A)
        slab = qa_ref[p].astype(F32) * (HEAD_DIM_A ** -0.5)
        in_head = (lane >= hh * HEAD_DIM_A) & (lane < (hh + 1) * HEAD_DIM_A)
        qh_ref[p, hh * tq:(hh + 1) * tq, 0:LANES] = jnp.where(in_head, slab, 0.0).astype(BF16)
        feat = jnp.where(lane == 0, slope * POS_RADIX, jnp.where(lane == 1, slope, 0.0))
        qh_ref[p, hh * tq:(hh + 1) * tq, LANES:2 * LANES] = feat.astype(BF16)
    w_rows = g_ref[...].T[G_W:G_W + N_IDX_HEADS, :] * ((IDX_DIM * N_IDX_HEADS) ** -0.5)

    def score_block(j, carry):
        hi_part, lo_part = carry
        k0 = pl.multiple_of(j * tk, tk)
        logits = lax.dot_general(k4_ref[pl.ds(k0, tk), :], qm_ref[...], NT_DIMS,
                                 preferred_element_type=F32)
        score = jnp.zeros(shape4, F32)
        for h in range(N_IDX_HEADS):
            score = score + rows8(w_rows[h:h + 1, :]) * jnp.maximum(chunks(logits[:, h * tq:(h + 1) * tq]), 0.0)
        causal = k0 + key_off <= q_pos
        kept = jnp.where(causal, score, NEG_BIG)
        sc_ref[j] = kept
        hi_part = jnp.maximum(hi_part, jnp.max(kept, axis=0))
        lo_part = jnp.minimum(lo_part, jnp.min(jnp.where(causal, score, jnp.inf), axis=0))
        return hi_part, lo_part

    part_shape = (CHAINS, SUBLANES, tq)
    hi_part, lo_part = lax.fori_loop(0, n_kb, score_block,
                                     (jnp.full(part_shape, -jnp.inf, F32), jnp.full(part_shape, jnp.inf, F32)))
    row_max = jnp.max(jnp.max(hi_part, axis=0), axis=0, keepdims=True)
    row_min = jnp.min(jnp.min(lo_part, axis=0), axis=0, keepdims=True)

    n_tail = (seq - n_kb * tk).astype(F32)

    def count(thr, strict):
        thr8 = rows8(thr)

        def body(j, acc):
            s = sc_ref[j]
            return acc + jnp.sum(jnp.where((s > thr8) if strict else (s >= thr8), 1.0, 0.0), axis=0)

        acc = lax.fori_loop(0, n_kb, body, jnp.zeros(part_shape, F32))
        tail_hit = (NEG_BIG > thr) if strict else (NEG_BIG >= thr)
        return jnp.sum(fold(acc), axis=0, keepdims=True) + jnp.where(tail_hit, n_tail, 0.0)

    n_causal = (t0 + 1 + lax.broadcasted_iota(I32, (1, tq), 1)).astype(F32)
    few = n_causal < topk
    lo0 = jnp.where(few, _float_to_ordered_int(jnp.full((1, tq), NEG_BIG, F32)), _float_to_ordered_int(row_min))
    cnt_lo0 = jnp.where(few, float(seq), n_causal)
    hi0 = _float_to_ordered_int(row_max) + 1
    log_k = float(np.log(topk))

    def log_excess(cnt):
        return jnp.log(jnp.maximum(cnt, 0.5)) - log_k

    def search_step(state):
        it, _, lo, hi, cnt_lo, f_lo, f_hi, last, done_i = state
        done = done_i != 0
        it_v = jnp.zeros((1, tq), I32) + it
        lo_v = _ordered_int_to_float(lo)
        hi_v = _ordered_int_to_float(hi)
        guess = _float_to_ordered_int(lo_v + (f_lo / (f_lo - f_hi)) * (hi_v - lo_v))
        guess = jnp.where(it_v == 0, 0, guess)
        guess = jnp.where((it_v == 1) & (lo == 0) & (cnt_lo < topk + ZERO_TIE_MARGIN), 1, guess)
        middle = lo + lax.shift_right_logical(hi - lo, jnp.ones((1, tq), I32))
        usable = (guess > lo) & (guess < hi) & (it_v < SEARCH_FREE_PASSES)
        cand = jnp.where(usable, guess, middle)
        cnt = count(_ordered_int_to_float(cand), strict=False)
        f_c = log_excess(cnt)
        take = (cnt >= topk) & ~done
        drop = (cnt < topk) & ~done
        f_hi = jnp.where(take, jnp.where(last == 1, 0.5 * f_hi, f_hi), jnp.where(drop, f_c, f_hi))
        f_lo = jnp.where(drop, jnp.where(last == -1, 0.5 * f_lo, f_lo), jnp.where(take, f_c, f_lo))
        last = jnp.where(take, 1, jnp.where(drop, -1, last))
        lo = jnp.where(take, cand, lo)
        cnt_lo = jnp.where(take, cnt, cnt_lo)
        hi = jnp.where(drop, cand, hi)
        done = done | (cnt_lo == topk) | ((hi - lo) == 1)
        active = jnp.sum(jnp.where(done, 0.0, 1.0))
        return it + 1, active, lo, hi, cnt_lo, f_lo, f_hi, last, done.astype(I32)

    def search_on(state):
        return (state[0] < SEARCH_MAX_PASSES) & (state[1] > 0.0)

    state = lax.while_loop(search_on, search_step,
                           (jnp.int32(0), jnp.float32(1.0), lo0, hi0, cnt_lo0, log_excess(cnt_lo0),
                            log_excess(jnp.zeros((1, tq), F32)), jnp.zeros((1, tq), I32), few.astype(I32)))
    thr = _ordered_int_to_float(state[2])
    cnt_thr = state[4]
    thr8 = rows8(thr)

    @pl.when(jnp.max(cnt_thr) > topk)
    def _():
        need = topk - count(thr, strict=True)
        tri_r = lax.broadcasted_iota(I32, (tk, tk), 0)
        tri_c = lax.broadcasted_iota(I32, (tk, tk), 1)
        earlier = (tri_c < tri_r).astype(BF16)

        def drop_late_ties(j, seen):
            s = sc_ref[j]
            eq = s == thr8
            eq_f = jnp.where(eq, 1.0, 0.0)
            rank = seen + jnp.dot(earlier, eq_f.reshape(tk, tq).astype(BF16), preferred_element_type=F32)
            sc_ref[j] = jnp.where(eq & (chunks(rank) >= rows8(need)), -jnp.inf, s)
            return seen + jnp.sum(fold(jnp.sum(eq_f, axis=0)), axis=0, keepdims=True)

        lax.fori_loop(0, n_kb, drop_late_ties, jnp.zeros((1, tq), F32))

    m_ref[...] = jnp.full(m_ref.shape, NEG_BIG, F32)
    l_ref[...] = jnp.zeros(l_ref.shape, F32)
    acc_ref[...] = jnp.zeros(acc_ref.shape, F32)
    pos_lane = lax.broadcasted_iota(I32, (tk, LANES), 1)
    pos_row = lax.broadcasted_iota(I32, (tk, LANES), 0)
    feat0 = jnp.where(pos_lane == 0, pos_row // POS_RADIX, jnp.where(pos_lane == 1, pos_row % POS_RADIX, 0)).astype(F32)
    hi_lane = jnp.where(pos_lane == 0, 1.0, 0.0)

    def attend(j, carry):
        k0 = pl.multiple_of(j * tk, tk)
        mask_ref[...] = jnp.where((sc_ref[j] >= thr8) & (k0 + key_off <= q_pos), 0.0, -jnp.inf)
        pos_feat = (feat0 + hi_lane * (j * (tk // POS_RADIX)).astype(F32)).astype(BF16)

        scores = []
        for p in range(SLABS_PER_TENSOR):
            keys = jnp.concatenate([ka_ref[p, pl.ds(k0, tk), :], pos_feat], axis=1)
            scores.append(lax.dot_general(keys, qh_ref[p], NT_DIMS, preferred_element_type=F32))
        weights = []
        for p in range(SLABS_PER_TENSOR):
            for hh in range(HEADS_PER_SLAB):
                s = chunks(scores[p][:, hh * tq:(hh + 1) * tq]) + mask_ref[...]
                m_old = m_ref[p, hh]
                m_blk = jnp.max(fold_max(jnp.max(s, axis=0)), axis=0, keepdims=True)
                m_new = jnp.maximum(m_old, rows8(m_blk))
                alpha = jnp.exp(m_old - m_new)
                probs = jnp.exp(s - m_new)
                l_ref[p, hh] = alpha * l_ref[p, hh] + jnp.sum(probs, axis=0)
                m_ref[p, hh] = m_new
                weights.append((alpha, probs.reshape(tk, tq).astype(BF16)))
        for p in range(SLABS_PER_TENSOR):
            for hh in range(HEADS_PER_SLAB):
                rows = slice(hh * HEAD_DIM_A, (hh + 1) * HEAD_DIM_A)
                alpha, probs = weights[p * HEADS_PER_SLAB + hh]
                pv = jnp.dot(vt_ref[p, j, rows, :], probs, preferred_element_type=F32)
                old = acc_ref[p, rows, :].reshape(HEAD_DIM_A // SUBLANES, SUBLANES, tq)
                acc_ref[p, rows, :] = (alpha * old).reshape(HEAD_DIM_A, tq) + pv
        return carry

    def fold_max(x):
        out = x[0]
        for c in range(1, CHAINS):
            out = jnp.maximum(out, x[c])
        return out

    lax.fori_loop(0, n_kb, attend, 0)

    for p in range(SLABS_PER_TENSOR):
        halves = []
        for hh in range(HEADS_PER_SLAB):
            rows = slice(hh * HEAD_DIM_A, (hh + 1) * HEAD_DIM_A)
            l_row = jnp.sum(fold(l_ref[p, hh]), axis=0, keepdims=True)
            acc = acc_ref[p, rows, :].reshape(HEAD_DIM_A // SUBLANES, SUBLANES, tq)
            halves.append((acc / rows8(l_row)).reshape(HEAD_DIM_A, tq))
        o_ref[:, p * LANES:(p + 1) * LANES] = jnp.concatenate(halves, axis=0).T.astype(BF16)


def _dsa_attention(qk, v_t, q_idx, k4, gates, tq, tk):
    _, b, s, _ = qk.shape
    assert tq == LANES and s // POS_RADIX <= 256
    topk = min(TOPK_MAX, s // 4)
    once = pl.Buffered(1)
    groups = tk // (CHAINS * SUBLANES)
    body = functools.partial(_dsa_body, seq=s, topk=topk, tq=tq, tk=tk)
    return pl.pallas_call(
        body,
        grid=(b, s // tq),
        in_specs=[pl.BlockSpec((SLABS_PER_TENSOR, None, tq, LANES), lambda bi, qi: (0, bi, qi, 0)),
                  pl.BlockSpec((SLABS_PER_TENSOR, None, s, LANES), lambda bi, qi: (1, bi, 0, 0), pipeline_mode=once),
                  pl.BlockSpec((SLABS_PER_TENSOR, None, s // tk, LANES, tk), lambda bi, qi: (0, bi, 0, 0, 0),
                               pipeline_mode=once),
                  pl.BlockSpec((None, tq, N_IDX_HEADS * IDX_DIM), lambda bi, qi: (bi, qi, 0)),
                  pl.BlockSpec((None, s, LANES), lambda bi, qi: (bi, 0, 0), pipeline_mode=once),
                  pl.BlockSpec((None, tq, LANES), lambda bi, qi: (bi, qi, 0))],
        out_specs=pl.BlockSpec((None, tq, D_A), lambda bi, qi: (bi, qi, 0)),
        out_shape=jax.ShapeDtypeStruct((b, s, D_A), BF16),
        scratch_shapes=[pltpu.VMEM((s // tk, groups, CHAINS, SUBLANES, tq), F32),
                        pltpu.VMEM((N_IDX_HEADS * tq, LANES), BF16),
                        pltpu.VMEM((SLABS_PER_TENSOR, HEADS_PER_SLAB * tq, 2 * LANES), BF16),
                        pltpu.VMEM((groups, CHAINS, SUBLANES, tq), F32),
                        pltpu.VMEM((SLABS_PER_TENSOR, LANES, tq), F32),
                        pltpu.VMEM((SLABS_PER_TENSOR, HEADS_PER_SLAB, SUBLANES, tq), F32),
                        pltpu.VMEM((SLABS_PER_TENSOR, HEADS_PER_SLAB, CHAINS, SUBLANES, tq), F32)],
        compiler_params=_cparams(("arbitrary", "arbitrary"), 48),
        name="dsa_attention",
    )(qk, qk, v_t, q_idx, k4, gates)


def _log_sigmoid(x):
    return jnp.minimum(x, 0.0) - jnp.log(1.0 + jnp.exp(-jnp.abs(x)))


def _split3(a):
    hi = a.astype(BF16)
    r1 = a - hi.astype(F32)
    mid = r1.astype(BF16)
    lo = (r1 - mid.astype(F32)).astype(BF16)
    return hi, mid, lo


def _mlstm_body(qk_ref, v_ref, o_ref, g_ref, gt_ref, cw_ref, cb_ref, bg_ref, bgt_ref, gn_ref, y_ref,
                xe_ref, c_ref, m_ref, *, chunk):
    @pl.when(pl.program_id(1) == 0)
    def _():
        xe_ref[0:SUBLANES, :] = jnp.zeros((SUBLANES, 2 * D_M), F32)
        c_ref[...] = jnp.zeros(c_ref.shape, F32)
        m_ref[...] = jnp.zeros(m_ref.shape, F32)

    xe_ref[SUBLANES:SUBLANES + chunk, :] = qk_ref[...]
    conv = cb_ref[...]
    for j in range(CONV_K):
        start = SUBLANES - (CONV_K - 1) + j
        conv = conv + cw_ref[j:j + 1, :] * xe_ref[start:start + chunk, :]
    qk = conv * jax.nn.sigmoid(conv)
    xe_ref[0:SUBLANES, :] = xe_ref[chunk:chunk + SUBLANES, :]

    gb = g_ref[...] + bg_ref[...]
    gbt = gt_ref[...] + bgt_ref[...]
    r = lax.broadcasted_iota(I32, (chunk, chunk), 0)
    c = lax.broadcasted_iota(I32, (chunk, chunk), 1)
    causal = r >= c
    incl = causal.astype(BF16)
    incl_t = (r <= c).astype(BF16)
    cum = jnp.zeros((chunk, LANES), F32)
    for term in _split3(_log_sigmoid(gb)):
        cum = cum + jnp.dot(incl, term, preferred_element_type=F32)
    cum_t = jnp.zeros((2 * N_HEADS_M, chunk), F32)
    for term in _split3(_log_sigmoid(gbt)):
        cum_t = cum_t + jnp.dot(term, incl_t, preferred_element_type=F32)

    lane = lax.broadcasted_iota(I32, (chunk, LANES), 1)
    ones_col = jnp.where(lane == 0, 1.0, 0.0).astype(BF16)

    for h in range(N_HEADS_M):
        sl = slice(h * HEAD_DIM_M, (h + 1) * HEAD_DIM_M)
        q_h = (qk[:, sl] * (HEAD_DIM_M ** -0.5)).astype(BF16)
        k_f = qk[:, D_M + h * HEAD_DIM_M:D_M + (h + 1) * HEAD_DIM_M]
        v_aug = jnp.concatenate([v_ref[:, sl], ones_col], axis=1)
        b_col = cum[:, G_F + h:G_F + h + 1]
        i_col = gb[:, G_I + h:G_I + h + 1]
        b_row = cum_t[N_HEADS_M + h:N_HEADS_M + h + 1, :]
        i_row = gbt[h:h + 1, :]
        m_prev = m_ref[h:h + 1, 0:1]
        c_aug = c_ref[h]

        d_log = jnp.where(causal, b_col - b_row + i_row, -jnp.inf)
        inter = b_col + m_prev
        m_t = jnp.maximum(inter, jnp.max(d_log, axis=1, keepdims=True))
        qk_t = lax.dot_general(q_h, k_f.astype(BF16), NT_DIMS, preferred_element_type=F32)
        w_intra = qk_t * jnp.exp(d_log - m_t)
        w_inter = jnp.exp(inter - m_t)
        intra = jnp.dot(w_intra.astype(BF16), v_aug[:, 0:HEAD_DIM_M], preferred_element_type=F32)
        carried = jnp.dot(q_h, c_aug.astype(BF16), preferred_element_type=F32)
        num = intra + w_inter * carried[:, 0:HEAD_DIM_M]
        den = jnp.sum(w_intra, axis=1, keepdims=True) + w_inter * carried[:, HEAD_DIM_M:HEAD_DIM_M + 1]
        hid = num / jnp.maximum(jnp.abs(den), jnp.exp(-m_t))

        mu = jnp.mean(hid, axis=1, keepdims=True)
        dev = hid - mu
        var = jnp.mean(dev * dev, axis=1, keepdims=True)
        normed = dev * lax.rsqrt(var + LN_EPS) * gn_ref[:, sl]
        y_ref[:, sl] = (jax.nn.sigmoid(o_ref[:, sl]) * normed).astype(BF16)

        b_last = b_col[chunk - 1:chunk, :]
        w_log = b_last - b_col + i_col
        m_new = jnp.maximum(b_last + m_prev, jnp.max(w_log, axis=0, keepdims=True))
        decay = jnp.exp(b_last + m_prev - m_new)
        kw_t = (k_f * jnp.exp(w_log - m_new)).T.astype(BF16)
        c_ref[h] = decay * c_aug + jnp.dot(kw_t, v_aug, preferred_element_type=F32)
        m_ref[h:h + 1, :] = jnp.broadcast_to(m_new, (1, LANES))


def _mlstm(qk_m, v_m, o_m, gates, gates_t, conv_w, conv_b, bias_slab, bias_col, gn_w, chunk):
    b, s, _ = qk_m.shape
    body = functools.partial(_mlstm_body, chunk=chunk)
    const = lambda bi, ci: (0, 0)
    return pl.pallas_call(
        body,
        grid=(b, s // chunk),
        in_specs=[pl.BlockSpec((None, chunk, 2 * D_M), lambda bi, ci: (bi, ci, 0)),
                  pl.BlockSpec((None, chunk, D_M), lambda bi, ci: (bi, ci, 0)),
                  pl.BlockSpec((None, chunk, D_M), lambda bi, ci: (bi, ci, 0)),
                  pl.BlockSpec((None, chunk, LANES), lambda bi, ci: (bi, ci, 0)),
                  pl.BlockSpec((None, 2 * N_HEADS_M, chunk), lambda bi, ci: (bi, 0, ci)),
                  pl.BlockSpec((CONV_K, 2 * D_M), const),
                  pl.BlockSpec((1, 2 * D_M), const),
                  pl.BlockSpec((1, LANES), const),
                  pl.BlockSpec((2 * N_HEADS_M, 1), const),
                  pl.BlockSpec((1, D_M), const)],
        out_specs=pl.BlockSpec((None, chunk, D_M), lambda bi, ci: (bi, ci, 0)),
        out_shape=jax.ShapeDtypeStruct((b, s, D_M), BF16),
        scratch_shapes=[pltpu.VMEM((chunk + SUBLANES, 2 * D_M), F32),
                        pltpu.VMEM((N_HEADS_M, HEAD_DIM_M, 2 * LANES), F32),
                        pltpu.VMEM((SUBLANES, LANES), F32)],
        compiler_params=_cparams(("arbitrary", "arbitrary"), 32),
        name="mlstm",
    )(qk_m, v_m, o_m, gates, gates_t, conv_w, conv_b, bias_slab, bias_col, gn_w)


def _route(logits):
    lane = lax.broadcasted_iota(I32, logits.shape, 1).astype(F32)
    g_mask = (lane >= N_EXPERTS) & (lane < N_EXPERTS + N_GROUPS)
    g_logit = jnp.where(g_mask, logits, -jnp.inf)
    g_exp = jnp.exp(g_logit - jnp.max(g_logit, axis=1, keepdims=True))
    g_prob = g_exp / jnp.sum(g_exp, axis=1, keepdims=True)
    g_p = jnp.max(g_prob, axis=1, keepdims=True)
    g_sel = jnp.min(jnp.where(g_mask & (g_prob == g_p), lane, float(LANES)), axis=1, keepdims=True) - N_EXPERTS
    e_mask = (lane >= g_sel * EXPERTS_PER_GROUP) & (lane < (g_sel + 1.0) * EXPERTS_PER_GROUP)
    e_logit = jnp.where(e_mask, logits, -jnp.inf)
    e_exp = jnp.exp(e_logit - jnp.max(e_logit, axis=1, keepdims=True))
    e_prob = e_exp / jnp.sum(e_exp, axis=1, keepdims=True)
    p1 = jnp.max(jnp.where(e_mask, e_prob, -1.0), axis=1, keepdims=True)
    i1 = jnp.min(jnp.where(e_mask & (e_prob == p1), lane, float(LANES)), axis=1, keepdims=True)
    rest = e_mask & (lane != i1)
    p2 = jnp.max(jnp.where(rest, e_prob, -1.0), axis=1, keepdims=True)
    i2 = jnp.min(jnp.where(rest & (e_prob == p2), lane, float(LANES)), axis=1, keepdims=True)
    total = p1 + p2
    return (jnp.where(lane == i1, p1 / total, 0.0) + jnp.where(lane == i2, p2 / total, 0.0)) * g_p


def _mix_body(ya_ref, ym_ref, x_ref, wo_ref, g_ref, b_ref, wr_ref, br_ref, x1_ref, comb_ref, *, alpha):
    mix = (jnp.dot(ya_ref[...], wo_ref[0:D_A, :], preferred_element_type=F32)
           + jnp.dot(ym_ref[...], wo_ref[D_A:D_A + D_M, :], preferred_element_type=F32))
    x1 = _layer_norm(alpha * x_ref[...] + mix, g_ref[...], b_ref[...])
    x1_ref[...] = x1
    logits = jnp.dot(x1, wr_ref[...], preferred_element_type=F32, precision=lax.Precision.HIGHEST) + br_ref[...]
    comb_ref[...] = _route(logits)


def _mix_ln_router(y_a, y_m, x2d, w_out, ln_g, ln_b, w_router, b_router, alpha, tm):
    n, d = x2d.shape
    const = lambda i: (0, 0)
    return pl.pallas_call(
        functools.partial(_mix_body, alpha=alpha),
        grid=(n // tm,),
        in_specs=[pl.BlockSpec((tm, D_A), lambda i: (i, 0)),
                  pl.BlockSpec((tm, D_M), lambda i: (i, 0)),
                  pl.BlockSpec((tm, d), lambda i: (i, 0)),
                  pl.BlockSpec((D_A + D_M, d), const),
                  pl.BlockSpec((1, d), const),
                  pl.BlockSpec((1, d), const),
                  pl.BlockSpec((d, LANES), const),
                  pl.BlockSpec((1, LANES), const)],
        out_specs=[pl.BlockSpec((tm, d), lambda i: (i, 0)),
                   pl.BlockSpec((tm, LANES), lambda i: (i, 0))],
        out_shape=[jax.ShapeDtypeStruct((n, d), F32), jax.ShapeDtypeStruct((n, LANES), F32)],
        compiler_params=_cparams(("arbitrary",), 32),
        name="mix_ln_router",
    )(y_a, y_m, x2d, w_out, ln_g, ln_b, w_router, b_router)


def _moe_body(x1_ref, comb_ref, wg_ref, wu_ref, wd_ref, g_ref, b_ref, y_ref, xb_ref, *, alpha):
    e = pl.program_id(1)

    @pl.when(e == 0)
    def _():
        xb_ref[...] = x1_ref[...].astype(BF16)
        y_ref[...] = jnp.zeros(y_ref.shape, F32)

    xb = xb_ref[...]
    gate = jnp.dot(xb, wg_ref[...], preferred_element_type=F32)
    up = jnp.dot(xb, wu_ref[...], preferred_element_type=F32)
    lane = lax.broadcasted_iota(I32, comb_ref.shape, 1)
    weight = jnp.sum(jnp.where(lane == e, comb_ref[...], 0.0), axis=1, keepdims=True)
    hidden = gate * jax.nn.sigmoid(gate) * up * weight
    y_ref[...] += jnp.dot(hidden.astype(BF16), wd_ref[...], preferred_element_type=F32)

    @pl.when(e == N_EXPERTS - 1)
    def _():
        y_ref[...] = _layer_norm(alpha * x1_ref[...] + y_ref[...], g_ref[...], b_ref[...])


def _moe_ln(x1, comb, w_gate, w_up, w_down, ln_g, ln_b, alpha, tm):
    n, d = x1.shape
    const = lambda i, e: (0, 0)
    return pl.pallas_call(
        functools.partial(_moe_body, alpha=alpha),
        grid=(n // tm, N_EXPERTS),
        in_specs=[pl.BlockSpec((tm, d), lambda i, e: (i, 0)),
                  pl.BlockSpec((tm, LANES), lambda i, e: (i, 0)),
                  pl.BlockSpec((None, d, D_EXPERT), lambda i, e: (e, 0, 0)),
                  pl.BlockSpec((None, d, D_EXPERT), lambda i, e: (e, 0, 0)),
                  pl.BlockSpec((None, D_EXPERT, d), lambda i, e: (e, 0, 0)),
                  pl.BlockSpec((1, d), const),
                  pl.BlockSpec((1, d), const)],
        out_specs=pl.BlockSpec((tm, d), lambda i, e: (i, 0)),
        out_shape=jax.ShapeDtypeStruct((n, d), F32),
        scratch_shapes=[pltpu.VMEM((tm, d), BF16)],
        compiler_params=_cparams(("arbitrary", "arbitrary"), 48),
        name="moe_ln",
    )(x1, comb, w_gate, w_up, w_down, ln_g, ln_b)


def _ple_body(x_ref, p_ref, wg_ref, wp_ref, o_ref):
    x = x_ref[...]
    gate = jax.nn.sigmoid(jnp.dot(x.astype(BF16), wg_ref[...], preferred_element_type=F32))
    proj = jnp.dot(p_ref[...].astype(BF16), wp_ref[...], preferred_element_type=F32)
    o_ref[...] = x + gate * proj


def _ple(x2, p2d, w_gate, w_proj, tm):
    n, d = x2.shape
    dp = p2d.shape[1]
    const = lambda i: (0, 0)
    return pl.pallas_call(
        _ple_body,
        grid=(n // tm,),
        in_specs=[pl.BlockSpec((tm, d), lambda i: (i, 0)),
                  pl.BlockSpec((tm, dp), lambda i: (i, 0)),
                  pl.BlockSpec((d, d), const),
                  pl.BlockSpec((dp, d), const)],
        out_specs=pl.BlockSpec((tm, d), lambda i: (i, 0)),
        out_shape=jax.ShapeDtypeStruct((n, d), F32),
        compiler_params=_cparams(("arbitrary",), 32),
        name="ple_gate",
    )(x2, p2d, w_gate, w_proj)


def _tile(n, want):
    t = min(want, n)
    assert n % t == 0
    return t


def _layer(x, p_i, w_in, b_igate, b_fgate, conv_w, conv_b, gn_w, w_out, ln1_g, ln1_b,
           rg_w, rg_b, re_w, re_b, e_gate, e_up, e_down, ln2_g, ln2_b, ple_gate_w, ple_proj_w, alpha):
    b, s, d = x.shape
    n = b * s
    x2d = x.reshape(n, d)

    tk = _tile(s, TK_DSA)
    w_packed, w_vt = _pack_w_in(w_in)
    qk, v_t, q_idx, k4, gates, qk_m, v_m, o_m = _in_proj(x2d, w_packed, w_vt, tk)

    gates3 = gates.reshape(b, s, LANES)
    y_a = _dsa_attention(qk.reshape(2 * SLABS_PER_TENSOR, b, s, LANES),
                         v_t.reshape(SLABS_PER_TENSOR, b, s // tk, LANES, tk),
                         q_idx.reshape(b, s, N_IDX_HEADS * IDX_DIM), k4.reshape(b, s, LANES), gates3,
                         _tile(s, TQ_DSA), tk)

    gate_bias = jnp.concatenate([b_igate, b_fgate]).astype(F32)
    bias_slab = jnp.zeros((1, LANES), F32).at[0, G_I:G_I + 2 * N_HEADS_M].set(gate_bias)
    gates_t = jnp.swapaxes(gates3[:, :, G_I:G_I + 2 * N_HEADS_M], 1, 2)
    y_m = _mlstm(qk_m.reshape(b, s, 2 * D_M), v_m.reshape(b, s, D_M), o_m.reshape(b, s, D_M), gates3, gates_t,
                 conv_w, conv_b.reshape(1, -1), bias_slab, gate_bias.reshape(-1, 1), gn_w.reshape(1, -1),
                 _tile(s, CHUNK_MLSTM))

    w_router = jnp.concatenate([re_w, rg_w, jnp.zeros((d, LANES - N_EXPERTS - N_GROUPS), F32)], axis=1)
    b_router = jnp.concatenate([re_b, rg_b, jnp.zeros((LANES - N_EXPERTS - N_GROUPS,), F32)]).reshape(1, LANES)
    x1, comb = _mix_ln_router(y_a.reshape(n, D_A), y_m.reshape(n, D_M), x2d, w_out.astype(BF16),
                              ln1_g.reshape(1, d), ln1_b.reshape(1, d), w_router, b_router, alpha, _tile(n, TM_MIX))

    x2 = _moe_ln(x1, comb, e_gate.astype(BF16), e_up.astype(BF16), e_down.astype(BF16),
                 ln2_g.reshape(1, d), ln2_b.reshape(1, d), alpha, _tile(n, TM_MOE))

    out = _ple(x2, p_i.reshape(n, -1), ple_gate_w.astype(BF16), ple_proj_w.astype(BF16), _tile(n, TM_PLE))
    return out.reshape(b, s, d)


def kernel(x, p, w_in, b_igate, b_fgate, conv_w, conv_b, gn_w, w_out, ln1_g, ln1_b, router_group_w, router_group_b,
           router_expert_w, router_expert_b, expert_w_gate, expert_w_up, expert_w_down, ln2_g, ln2_b, ple_gate_w,
           ple_proj_w):
    depth = w_in.shape[0]
    alpha = (2.0 * depth) ** 0.25
    for i in range(depth):
        x = _layer(x, p[i], w_in[i], b_igate[i], b_fgate[i], conv_w[i], conv_b[i], gn_w[i], w_out[i], ln1_g[i],
                   ln1_b[i], router_group_w[i], router_group_b[i], router_expert_w[i], router_expert_b[i],
                   expert_w_gate[i], expert_w_up[i], expert_w_down[i], ln2_g[i], ln2_b[i], ple_gate_w[i],
                   ple_proj_w[i], alpha)
    return x
```

```python
import functools

import numpy as np
import jax
import jax.numpy as jnp
from jax import lax
from jax.experimental import pallas as pl
from jax.experimental.pallas import tpu as pltpu

F32 = jnp.float32
BF16 = jnp.bfloat16
I32 = jnp.int32

N_HEADS_A = 8
HEAD_DIM_A = 64
D_A = N_HEADS_A * HEAD_DIM_A
N_IDX_HEADS = 8
IDX_DIM = 32
TOPK_MAX = 256
N_HEADS_M = 4
HEAD_DIM_M = 128
D_M = N_HEADS_M * HEAD_DIM_M
CONV_K = 4
N_GROUPS = 4
EXPERTS_PER_GROUP = 8
N_EXPERTS = N_GROUPS * EXPERTS_PER_GROUP
D_EXPERT = 256
LN_EPS = 1e-5
NEG_BIG = -1e30
IN_SPLITS = (D_A, D_A, D_A, N_IDX_HEADS * IDX_DIM, IDX_DIM, N_IDX_HEADS, D_M, D_M, D_M, D_M, N_HEADS_M, N_HEADS_M)

LANES = 128
SUBLANES = 8
MIB = 1024 * 1024

C_QK = 0
C_QIDX = C_QK + 2 * D_A
C_K4 = C_QIDX + N_IDX_HEADS * IDX_DIM
C_GATE = C_K4 + LANES
C_QKM = C_GATE + LANES
C_VM = C_QKM + 2 * D_M
C_OM = C_VM + D_M
C_END = C_OM + D_M
SLABS_PER_TENSOR = D_A // LANES
HEADS_PER_SLAB = LANES // HEAD_DIM_A
G_W = 0
G_I = N_IDX_HEADS
G_F = G_I + N_HEADS_M

INT_MIN = -2 ** 31

TQ_DSA = LANES
TK_DSA = 1024
TM_IN_PROJ = TK_DSA
CHUNK_MLSTM = 256
TM_MIX = 512
TM_MOE = 1024

NT_DIMS = (((1,), (1,)), ((), ()))


def _cparams(semantics, vmem_mib):
    return pltpu.CompilerParams(dimension_semantics=semantics, vmem_limit_bytes=int(vmem_mib * MIB))


def _layer_norm(r, g, b):
    mu = jnp.mean(r, axis=-1, keepdims=True)
    d = r - mu
    var = jnp.mean(d * d, axis=-1, keepdims=True)
    return d * lax.rsqrt(var + LN_EPS) * g + b


def _pack_w_in(w_in):
    d = w_in.shape[0]
    cuts = [int(c) for c in np.cumsum(IN_SPLITS)[:-1]]
    q_a, k_a, v_a, q_idx, k_idx, w_idx, q_m, k_m, v_m, o_m, i_m, f_m = jnp.split(w_in, cuts, axis=1)
    k4 = jnp.tile(k_idx, (1, LANES // IDX_DIM))
    gate = jnp.concatenate([w_idx, i_m, f_m, jnp.zeros((d, LANES - G_F - N_HEADS_M), w_in.dtype)], axis=1)
    packed = jnp.concatenate([q_a, k_a, q_idx, k4, gate, q_m, k_m, v_m, o_m], axis=1).astype(BF16)
    return packed, v_a.T.astype(BF16)


def _in_proj_body(x_ref, w_ref, wvt_ref, qk_ref, vt_ref, qi_ref, k4_ref, g_ref, qkm_ref, vm_ref, om_ref):
    xb = x_ref[...].astype(BF16)

    def proj(lo, hi):
        return jnp.dot(xb, w_ref[:, lo:hi], preferred_element_type=F32)

    qk = proj(C_QK, C_QIDX).astype(BF16)
    for c in range(2 * SLABS_PER_TENSOR):
        qk_ref[c] = qk[:, c * LANES:(c + 1) * LANES]
    v_t = lax.dot_general(wvt_ref[...], xb, NT_DIMS, preferred_element_type=F32).astype(BF16)
    for c in range(SLABS_PER_TENSOR):
        vt_ref[c, 0] = v_t[c * LANES:(c + 1) * LANES, :]
    qi_ref[...] = proj(C_QIDX, C_K4).astype(BF16)
    k4_ref[...] = proj(C_K4, C_GATE).astype(BF16)
    g_ref[...] = proj(C_GATE, C_QKM)
    qkm_ref[...] = proj(C_QKM, C_VM)
    vm_ref[...] = proj(C_VM, C_OM).astype(BF16)
    om_ref[...] = proj(C_OM, C_END)


def _in_proj(x2d, w_packed, w_vt, tm):
    n, d = x2d.shape
    widths = [(C_K4 - C_QIDX, BF16), (C_GATE - C_K4, BF16), (C_QKM - C_GATE, F32),
              (C_VM - C_QKM, F32), (C_OM - C_VM, BF16), (C_END - C_OM, F32)]
    return pl.pallas_call(
        _in_proj_body,
        grid=(n // tm,),
        in_specs=[pl.BlockSpec((tm, d), lambda i: (i, 0)),
                  pl.BlockSpec((d, C_END), lambda i: (0, 0)),
                  pl.BlockSpec((D_A, d), lambda i: (0, 0))],
        out_specs=([pl.BlockSpec((2 * SLABS_PER_TENSOR, tm, LANES), lambda i: (0, i, 0)),
                    pl.BlockSpec((SLABS_PER_TENSOR, 1, LANES, tm), lambda i: (0, i, 0, 0))]
                   + [pl.BlockSpec((tm, w), lambda i: (i, 0)) for w, _ in widths]),
        out_shape=([jax.ShapeDtypeStruct((2 * SLABS_PER_TENSOR, n, LANES), BF16),
                    jax.ShapeDtypeStruct((SLABS_PER_TENSOR, n // tm, LANES, tm), BF16)]
                   + [jax.ShapeDtypeStruct((n, w), dt) for w, dt in widths]),
        compiler_params=_cparams(("arbitrary",), 48),
        name="in_proj",
    )(x2d, w_packed, w_vt)


def _float_to_ordered_int(value):
    bits = lax.bitcast_convert_type(value, I32)
    return bits ^ ((bits >> 31) & jnp.int32(0x7FFFFFFF))


def _ordered_int_to_float(key):
    bits = key ^ ((key >> 31) & jnp.int32(0x7FFFFFFF))
    return lax.bitcast_convert_type(bits, F32)


SEARCH_FREE_PASSES = 16
SEARCH_MAX_PASSES = SEARCH_FREE_PASSES + 33
ZERO_TIE_MARGIN = 128
CHAINS = 4
POS_RADIX = 64


def _dsa_body(qa_ref, ka_ref, vt_ref, qi_ref, k4_ref, g_ref, o_ref,
              sc_ref, qm_ref, qh_ref, mask_ref, acc_ref, m_ref, l_ref, *, seq, topk, tq, tk):
    t0 = pl.program_id(1) * tq
    n_kb = (t0 + tq + tk - 1) // tk
    groups = tk // (CHAINS * SUBLANES)
    shape4 = (groups, CHAINS, SUBLANES, tq)

    def chunks(x):
        return x.reshape(shape4)

    def fold(x):
        out = x[0]
        for c in range(1, CHAINS):
            out = out + x[c]
        return out

    def rows8(row):
        return jnp.broadcast_to(row, (SUBLANES, tq))

    key_off = ((lax.broadcasted_iota(I32, shape4, 0) * CHAINS + lax.broadcasted_iota(I32, shape4, 1)) * SUBLANES
               + lax.broadcasted_iota(I32, shape4, 2))
    q_pos = t0 + lax.broadcasted_iota(I32, shape4, 3)
    lane = lax.broadcasted_iota(I32, (tq, LANES), 1)

    for h in range(N_IDX_HEADS):
        per_slab = LANES // IDX_DIM
        slab = qi_ref[:, (h // per_slab) * LANES:(h // per_slab + 1) * LANES]
        lo = (h % per_slab) * IDX_DIM
        qm_ref[h * tq:(h + 1) * tq, :] = jnp.where((lane >= lo) & (lane < lo + IDX_DIM), slab, jnp.zeros_like(slab))
    for h in range(N_HEADS_A):
        p, hh = divmod(h, HEADS_PER_SLAB)
        slope = 2.0 ** (-8.0 * (h + 1) / N_HEADS_A)
        slab = qa_ref[p].astype(F32) * (HEAD_DIM_A ** -0.5)
        in_head = (lane >= hh * HEAD_DIM_A) & (lane < (hh + 1) * HEAD_DIM_A)
        qh_ref[p, hh * tq:(hh + 1) * tq, 0:LANES] = jnp.where(in_head, slab, 0.0).astype(BF16)
        feat = jnp.where(lane == 0, slope * POS_RADIX, jnp.where(lane == 1, slope, 0.0))
        qh_ref[p, hh * tq:(hh + 1) * tq, LANES:2 * LANES] = feat.astype(BF16)
    w_rows = g_ref[...].T[G_W:G_W + N_IDX_HEADS, :] * ((IDX_DIM * N_IDX_HEADS) ** -0.5)

    def score_block(j, carry):
        hi_part, lo_part = carry
        k0 = pl.multiple_of(j * tk, tk)
        logits = lax.dot_general(k4_ref[pl.ds(k0, tk), :], qm_ref[...], NT_DIMS,
                                 preferred_element_type=F32)
        score = jnp.zeros(shape4, F32)
        for h in range(N_IDX_HEADS):
            score = score + rows8(w_rows[h:h + 1, :]) * jnp.maximum(chunks(logits[:, h * tq:(h + 1) * tq]), 0.0)
        causal = k0 + key_off <= q_pos
        kept = jnp.where(causal, score, NEG_BIG)
        sc_ref[j] = kept
        hi_part = jnp.maximum(hi_part, jnp.max(kept, axis=0))
        lo_part = jnp.minimum(lo_part, jnp.min(jnp.where(causal, score, jnp.inf), axis=0))
        return hi_part, lo_part

    part_shape = (CHAINS, SUBLANES, tq)
    hi_part, lo_part = lax.fori_loop(0, n_kb, score_block,
                                     (jnp.full(part_shape, -jnp.inf, F32), jnp.full(part_shape, jnp.inf, F32)))
    row_max = jnp.max(jnp.max(hi_part, axis=0), axis=0, keepdims=True)
    row_min = jnp.min(jnp.min(lo_part, axis=0), axis=0, keepdims=True)

    n_tail = (seq - n_kb * tk).astype(F32)

    def count(thr, strict):
        thr8 = rows8(thr)

        def body(j, acc):
            s = sc_ref[j]
            return acc + jnp.sum(jnp.where((s > thr8) if strict else (s >= thr8), 1.0, 0.0), axis=0)

        acc = lax.fori_loop(0, n_kb, body, jnp.zeros(part_shape, F32))
        tail_hit = (NEG_BIG > thr) if strict else (NEG_BIG >= thr)
        return jnp.sum(fold(acc), axis=0, keepdims=True) + jnp.where(tail_hit, n_tail, 0.0)

    n_causal = (t0 + 1 + lax.broadcasted_iota(I32, (1, tq), 1)).astype(F32)
    few = n_causal < topk
    lo0 = jnp.where(few, _float_to_ordered_int(jnp.full((1, tq), NEG_BIG, F32)), _float_to_ordered_int(row_min))
    cnt_lo0 = jnp.where(few, float(seq), n_causal)
    hi0 = _float_to_ordered_int(row_max) + 1
    log_k = float(np.log(topk))

    def log_excess(cnt):
        return jnp.log(jnp.maximum(cnt, 0.5)) - log_k

    def search_step(state):
        it, _, lo, hi, cnt_lo, f_lo, f_hi, last, done_i = state
        done = done_i != 0
        it_v = jnp.zeros((1, tq), I32) + it
        lo_v = _ordered_int_to_float(lo)
        hi_v = _ordered_int_to_float(hi)
        guess = _float_to_ordered_int(lo_v + (f_lo / (f_lo - f_hi)) * (hi_v - lo_v))
        guess = jnp.where(it_v == 0, 0, guess)
        guess = jnp.where((it_v == 1) & (lo == 0) & (cnt_lo < topk + ZERO_TIE_MARGIN), 1, guess)
        middle = lo + lax.shift_right_logical(hi - lo, jnp.ones((1, tq), I32))
        usable = (guess > lo) & (guess < hi) & (it_v < SEARCH_FREE_PASSES)
        cand = jnp.where(usable, guess, middle)
        cnt = count(_ordered_int_to_float(cand), strict=False)
        f_c = log_excess(cnt)
        take = (cnt >= topk) & ~done
        drop = (cnt < topk) & ~done
        f_hi = jnp.where(take, jnp.where(last == 1, 0.5 * f_hi, f_hi), jnp.where(drop, f_c, f_hi))
        f_lo = jnp.where(drop, jnp.where(last == -1, 0.5 * f_lo, f_lo), jnp.where(take, f_c, f_lo))
        last = jnp.where(take, 1, jnp.where(drop, -1, last))
        lo = jnp.where(take, cand, lo)
        cnt_lo = jnp.where(take, cnt, cnt_lo)
        hi = jnp.where(drop, cand, hi)
        done = done | (cnt_lo == topk) | ((hi - lo) == 1)
        active = jnp.sum(jnp.where(done, 0.0, 1.0))
        return it + 1, active, lo, hi, cnt_lo, f_lo, f_hi, last, done.astype(I32)

    def search_on(state):
        return (state[0] < SEARCH_MAX_PASSES) & (state[1] > 0.0)

    state = lax.while_loop(search_on, search_step,
                           (jnp.int32(0), jnp.float32(1.0), lo0, hi0, cnt_lo0, log_excess(cnt_lo0),
                            log_excess(jnp.zeros((1, tq), F32)), jnp.zeros((1, tq), I32), few.astype(I32)))
    thr = _ordered_int_to_float(state[2])
    cnt_thr = state[4]
    thr8 = rows8(thr)

    @pl.when(jnp.max(cnt_thr) > topk)
    def _():
        need = topk - count(thr, strict=True)
        tri_r = lax.broadcasted_iota(I32, (tk, tk), 0)
        tri_c = lax.broadcasted_iota(I32, (tk, tk), 1)
        earlier = (tri_c < tri_r).astype(BF16)

        def drop_late_ties(j, seen):
            s = sc_ref[j]
            eq = s == thr8
            eq_f = jnp.where(eq, 1.0, 0.0)
            rank = seen + jnp.dot(earlier, eq_f.reshape(tk, tq).astype(BF16), preferred_element_type=F32)
            sc_ref[j] = jnp.where(eq & (chunks(rank) >= rows8(need)), -jnp.inf, s)
            return seen + jnp.sum(fold(jnp.sum(eq_f, axis=0)), axis=0, keepdims=True)

        lax.fori_loop(0, n_kb, drop_late_ties, jnp.zeros((1, tq), F32))

    m_ref[...] = jnp.full(m_ref.shape, NEG_BIG, F32)
    l_ref[...] = jnp.zeros(l_ref.shape, F32)
    acc_ref[...] = jnp.zeros(acc_ref.shape, F32)
    pos_lane = lax.broadcasted_iota(I32, (tk, LANES), 1)
    pos_row = lax.broadcasted_iota(I32, (tk, LANES), 0)
    feat0 = jnp.where(pos_lane == 0, pos_row // POS_RADIX, jnp.where(pos_lane == 1, pos_row % POS_RADIX, 0)).astype(F32)
    hi_lane = jnp.where(pos_lane == 0, 1.0, 0.0)

    def attend(j, carry):
        k0 = pl.multiple_of(j * tk, tk)
        mask_ref[...] = jnp.where((sc_ref[j] >= thr8) & (k0 + key_off <= q_pos), 0.0, -jnp.inf)
        pos_feat = (feat0 + hi_lane * (j * (tk // POS_RADIX)).astype(F32)).astype(BF16)

        scores = []
        for p in range(SLABS_PER_TENSOR):
            keys = jnp.concatenate([ka_ref[p, pl.ds(k0, tk), :], pos_feat], axis=1)
            scores.append(lax.dot_general(keys, qh_ref[p], NT_DIMS, preferred_element_type=F32))
        weights = []
        for p in range(SLABS_PER_TENSOR):
            for hh in range(HEADS_PER_SLAB):
                s = chunks(scores[p][:, hh * tq:(hh + 1) * tq]) + mask_ref[...]
                m_old = m_ref[p, hh]
                m_blk = jnp.max(fold_max(jnp.max(s, axis=0)), axis=0, keepdims=True)
                m_new = jnp.maximum(m_old, rows8(m_blk))
                alpha = jnp.exp(m_old - m_new)
                m_ref[p, hh] = m_new
                weights.append((alpha, jnp.exp(s - m_new).reshape(tk, tq).astype(BF16)))
        ones_rows = jnp.ones((SUBLANES, tk), BF16)
        for p in range(SLABS_PER_TENSOR):
            for hh in range(HEADS_PER_SLAB):
                rows = slice(hh * HEAD_DIM_A, (hh + 1) * HEAD_DIM_A)
                alpha, probs = weights[p * HEADS_PER_SLAB + hh]
                values = jnp.concatenate([vt_ref[p, j, rows, :], ones_rows], axis=0)
                pv = jnp.dot(values, probs, preferred_element_type=F32)
                old = acc_ref[p, rows, :].reshape(HEAD_DIM_A // SUBLANES, SUBLANES, tq)
                acc_ref[p, rows, :] = (alpha * old).reshape(HEAD_DIM_A, tq) + pv[0:HEAD_DIM_A]
                l_ref[p, hh] = alpha * l_ref[p, hh] + pv[HEAD_DIM_A:HEAD_DIM_A + SUBLANES]
        return carry

    def fold_max(x):
        out = x[0]
        for c in range(1, CHAINS):
            out = jnp.maximum(out, x[c])
        return out

    lax.fori_loop(0, n_kb, attend, 0)

    for p in range(SLABS_PER_TENSOR):
        halves = []
        for hh in range(HEADS_PER_SLAB):
            rows = slice(hh * HEAD_DIM_A, (hh + 1) * HEAD_DIM_A)
            acc = acc_ref[p, rows, :].reshape(HEAD_DIM_A // SUBLANES, SUBLANES, tq)
            halves.append((acc / l_ref[p, hh]).reshape(HEAD_DIM_A, tq))
        o_ref[:, p * LANES:(p + 1) * LANES] = jnp.concatenate(halves, axis=0).T.astype(BF16)


def _dsa_attention(qk, v_t, q_idx, k4, gates, tq, tk):
    _, b, s, _ = qk.shape
    assert tq == LANES and s // POS_RADIX <= 256
    topk = min(TOPK_MAX, s // 4)
    once = pl.Buffered(1)
    groups = tk // (CHAINS * SUBLANES)
    body = functools.partial(_dsa_body, seq=s, topk=topk, tq=tq, tk=tk)
    return pl.pallas_call(
        body,
        grid=(b, s // tq),
        in_specs=[pl.BlockSpec((SLABS_PER_TENSOR, None, tq, LANES), lambda bi, qi: (0, bi, qi, 0)),
                  pl.BlockSpec((SLABS_PER_TENSOR, None, s, LANES), lambda bi, qi: (1, bi, 0, 0), pipeline_mode=once),
                  pl.BlockSpec((SLABS_PER_TENSOR, None, s // tk, LANES, tk), lambda bi, qi: (0, bi, 0, 0, 0),
                               pipeline_mode=once),
                  pl.BlockSpec((None, tq, N_IDX_HEADS * IDX_DIM), lambda bi, qi: (bi, qi, 0)),
                  pl.BlockSpec((None, s, LANES), lambda bi, qi: (bi, 0, 0), pipeline_mode=once),
                  pl.BlockSpec((None, tq, LANES), lambda bi, qi: (bi, qi, 0))],
        out_specs=pl.BlockSpec((None, tq, D_A), lambda bi, qi: (bi, qi, 0)),
        out_shape=jax.ShapeDtypeStruct((b, s, D_A), BF16),
        scratch_shapes=[pltpu.VMEM((s // tk, groups, CHAINS, SUBLANES, tq), F32),
                        pltpu.VMEM((N_IDX_HEADS * tq, LANES), BF16),
                        pltpu.VMEM((SLABS_PER_TENSOR, HEADS_PER_SLAB * tq, 2 * LANES), BF16),
                        pltpu.VMEM((groups, CHAINS, SUBLANES, tq), F32),
                        pltpu.VMEM((SLABS_PER_TENSOR, LANES, tq), F32),
                        pltpu.VMEM((SLABS_PER_TENSOR, HEADS_PER_SLAB, SUBLANES, tq), F32),
                        pltpu.VMEM((SLABS_PER_TENSOR, HEADS_PER_SLAB, SUBLANES, tq), F32)],
        compiler_params=_cparams(("arbitrary", "arbitrary"), 48),
        name="dsa_attention",
    )(qk, qk, v_t, q_idx, k4, gates)


def _log_sigmoid(x):
    return jnp.minimum(x, 0.0) - jnp.log(1.0 + jnp.exp(-jnp.abs(x)))


def _split3(a):
    hi = a.astype(BF16)
    r1 = a - hi.astype(F32)
    mid = r1.astype(BF16)
    lo = (r1 - mid.astype(F32)).astype(BF16)
    return hi, mid, lo


def _mlstm_body(qk_ref, v_ref, o_ref, g_ref, gt_ref, cw_ref, cb_ref, bg_ref, bgt_ref, gn_ref, y_ref,
                xe_ref, c_ref, m_ref, *, chunk, batch):
    @pl.when(pl.program_id(0) == 0)
    def _():
        xe_ref[:, 0:SUBLANES, :] = jnp.zeros((batch, SUBLANES, 2 * D_M), F32)
        c_ref[...] = jnp.zeros(c_ref.shape, F32)
        m_ref[...] = jnp.zeros(m_ref.shape, F32)

    gates = [_mlstm_gates(qk_ref.at[bi], g_ref.at[bi], gt_ref.at[bi], cw_ref, cb_ref, bg_ref, bgt_ref,
                          xe_ref.at[bi], chunk) for bi in range(batch)]
    for h in range(N_HEADS_M):
        for bi in range(batch):
            _mlstm_head(h, gates[bi], v_ref.at[bi], o_ref.at[bi], gn_ref, y_ref.at[bi], c_ref.at[bi], m_ref.at[bi],
                        chunk)


def _mlstm_gates(qk_ref, g_ref, gt_ref, cw_ref, cb_ref, bg_ref, bgt_ref, xe_ref, chunk):
    xe_ref[SUBLANES:SUBLANES + chunk, :] = qk_ref[...]
    conv = cb_ref[...]
    for j in range(CONV_K):
        start = SUBLANES - (CONV_K - 1) + j
        conv = conv + cw_ref[j:j + 1, :] * xe_ref[start:start + chunk, :]
    qk = conv * jax.nn.sigmoid(conv)
    xe_ref[0:SUBLANES, :] = xe_ref[chunk:chunk + SUBLANES, :]

    gb = g_ref[...] + bg_ref[...]
    gbt = gt_ref[...] + bgt_ref[...]
    r = lax.broadcasted_iota(I32, (chunk, chunk), 0)
    c = lax.broadcasted_iota(I32, (chunk, chunk), 1)
    causal = r >= c
    incl = causal.astype(BF16)
    incl_t = (r <= c).astype(BF16)
    cum = jnp.zeros((chunk, LANES), F32)
    for term in _split3(_log_sigmoid(gb)):
        cum = cum + jnp.dot(incl, term, preferred_element_type=F32)
    cum_t = jnp.zeros((2 * N_HEADS_M, chunk), F32)
    for term in _split3(_log_sigmoid(gbt)):
        cum_t = cum_t + jnp.dot(term, incl_t, preferred_element_type=F32)

    return qk, gb, gbt, cum, cum_t, causal


def _mlstm_head(h, gates, v_ref, o_ref, gn_ref, y_ref, c_ref, m_ref, chunk):
    qk, gb, gbt, cum, cum_t, causal = gates
    lane = lax.broadcasted_iota(I32, (chunk, LANES), 1)
    ones_col = jnp.where(lane == 0, 1.0, 0.0).astype(BF16)
    sl = slice(h * HEAD_DIM_M, (h + 1) * HEAD_DIM_M)
    q_h = (qk[:, sl] * (HEAD_DIM_M ** -0.5)).astype(BF16)
    k_f = qk[:, D_M + h * HEAD_DIM_M:D_M + (h + 1) * HEAD_DIM_M]
    v_aug = jnp.concatenate([v_ref[:, sl], ones_col], axis=1)
    b_col = cum[:, G_F + h:G_F + h + 1]
    i_col = gb[:, G_I + h:G_I + h + 1]
    b_row = cum_t[N_HEADS_M + h:N_HEADS_M + h + 1, :]
    i_row = gbt[h:h + 1, :]
    m_prev = m_ref[h:h + 1, 0:1]
    c_aug = c_ref[h]

    d_log = jnp.where(causal, b_col - b_row + i_row, -jnp.inf)
    inter = b_col + m_prev
    m_t = jnp.maximum(inter, jnp.max(d_log, axis=1, keepdims=True))
    qk_t = lax.dot_general(q_h, k_f.astype(BF16), NT_DIMS, preferred_element_type=F32)
    w_intra = qk_t * jnp.exp(d_log - m_t)
    w_inter = jnp.exp(inter - m_t)
    intra = jnp.dot(w_intra.astype(BF16), v_aug[:, 0:HEAD_DIM_M], preferred_element_type=F32)
    carried = jnp.dot(q_h, c_aug.astype(BF16), preferred_element_type=F32)
    num = intra + w_inter * carried[:, 0:HEAD_DIM_M]
    den = jnp.sum(w_intra, axis=1, keepdims=True) + w_inter * carried[:, HEAD_DIM_M:HEAD_DIM_M + 1]
    hid = num / jnp.maximum(jnp.abs(den), jnp.exp(-m_t))

    mu = jnp.mean(hid, axis=1, keepdims=True)
    dev = hid - mu
    var = jnp.mean(dev * dev, axis=1, keepdims=True)
    normed = dev * lax.rsqrt(var + LN_EPS) * gn_ref[:, sl]
    y_ref[:, sl] = (jax.nn.sigmoid(o_ref[:, sl]) * normed).astype(BF16)

    b_last = b_col[chunk - 1:chunk, :]
    w_log = b_last - b_col + i_col
    m_new = jnp.maximum(b_last + m_prev, jnp.max(w_log, axis=0, keepdims=True))
    decay = jnp.exp(b_last + m_prev - m_new)
    kw_t = (k_f * jnp.exp(w_log - m_new)).T.astype(BF16)
    c_ref[h] = decay * c_aug + jnp.dot(kw_t, v_aug, preferred_element_type=F32)
    m_ref[h:h + 1, :] = jnp.broadcast_to(m_new, (1, LANES))


def _mlstm(qk_m, v_m, o_m, gates, gates_t, conv_w, conv_b, bias_slab, bias_col, gn_w, chunk):
    b, s, _ = qk_m.shape
    body = functools.partial(_mlstm_body, chunk=chunk, batch=b)
    const = lambda ci: (0, 0)
    return pl.pallas_call(
        body,
        grid=(s // chunk,),
        in_specs=[pl.BlockSpec((b, chunk, 2 * D_M), lambda ci: (0, ci, 0)),
                  pl.BlockSpec((b, chunk, D_M), lambda ci: (0, ci, 0)),
                  pl.BlockSpec((b, chunk, D_M), lambda ci: (0, ci, 0)),
                  pl.BlockSpec((b, chunk, LANES), lambda ci: (0, ci, 0)),
                  pl.BlockSpec((b, 2 * N_HEADS_M, chunk), lambda ci: (0, 0, ci)),
                  pl.BlockSpec((CONV_K, 2 * D_M), const),
                  pl.BlockSpec((1, 2 * D_M), const),
                  pl.BlockSpec((1, LANES), const),
                  pl.BlockSpec((2 * N_HEADS_M, 1), const),
                  pl.BlockSpec((1, D_M), const)],
        out_specs=pl.BlockSpec((b, chunk, D_M), lambda ci: (0, ci, 0)),
        out_shape=jax.ShapeDtypeStruct((b, s, D_M), BF16),
        scratch_shapes=[pltpu.VMEM((b, chunk + SUBLANES, 2 * D_M), F32),
                        pltpu.VMEM((b, N_HEADS_M, HEAD_DIM_M, 2 * LANES), F32),
                        pltpu.VMEM((b, SUBLANES, LANES), F32)],
        compiler_params=_cparams(("arbitrary",), 32),
        name="mlstm",
    )(qk_m, v_m, o_m, gates, gates_t, conv_w, conv_b, bias_slab, bias_col, gn_w)


def _route(logits):
    lane = lax.broadcasted_iota(I32, logits.shape, 1).astype(F32)
    g_mask = (lane >= N_EXPERTS) & (lane < N_EXPERTS + N_GROUPS)
    g_logit = jnp.where(g_mask, logits, -jnp.inf)
    g_exp = jnp.exp(g_logit - jnp.max(g_logit, axis=1, keepdims=True))
    g_prob = g_exp / jnp.sum(g_exp, axis=1, keepdims=True)
    g_p = jnp.max(g_prob, axis=1, keepdims=True)
    g_sel = jnp.min(jnp.where(g_mask & (g_prob == g_p), lane, float(LANES)), axis=1, keepdims=True) - N_EXPERTS
    e_mask = (lane >= g_sel * EXPERTS_PER_GROUP) & (lane < (g_sel + 1.0) * EXPERTS_PER_GROUP)
    e_logit = jnp.where(e_mask, logits, -jnp.inf)
    e_exp = jnp.exp(e_logit - jnp.max(e_logit, axis=1, keepdims=True))
    e_prob = e_exp / jnp.sum(e_exp, axis=1, keepdims=True)
    p1 = jnp.max(jnp.where(e_mask, e_prob, -1.0), axis=1, keepdims=True)
    i1 = jnp.min(jnp.where(e_mask & (e_prob == p1), lane, float(LANES)), axis=1, keepdims=True)
    rest = e_mask & (lane != i1)
    p2 = jnp.max(jnp.where(rest, e_prob, -1.0), axis=1, keepdims=True)
    i2 = jnp.min(jnp.where(rest & (e_prob == p2), lane, float(LANES)), axis=1, keepdims=True)
    total = p1 + p2
    return (jnp.where(lane == i1, p1 / total, 0.0) + jnp.where(lane == i2, p2 / total, 0.0)) * g_p


def _mix_body(ya_ref, ym_ref, x_ref, wo_ref, g_ref, b_ref, wr_ref, br_ref, x1_ref, comb_ref, *, alpha):
    mix = (jnp.dot(ya_ref[...], wo_ref[0:D_A, :], preferred_element_type=F32)
           + jnp.dot(ym_ref[...], wo_ref[D_A:D_A + D_M, :], preferred_element_type=F32))
    x1 = _layer_norm(alpha * x_ref[...] + mix, g_ref[...], b_ref[...])
    x1_ref[...] = x1
    x_hi = x1.astype(BF16)
    x_lo = (x1 - x_hi.astype(F32)).astype(BF16)
    w_hi = wr_ref[0]
    w_lo = wr_ref[1]
    logits = (jnp.dot(x_hi, w_hi, preferred_element_type=F32) + jnp.dot(x_lo, w_hi, preferred_element_type=F32)
              + jnp.dot(x_hi, w_lo, preferred_element_type=F32)) + br_ref[...]
    comb_ref[...] = _route(logits)


def _mix_ln_router(y_a, y_m, x2d, w_out, ln_g, ln_b, w_router, b_router, alpha, tm):
    n, d = x2d.shape
    const = lambda i: (0, 0)
    return pl.pallas_call(
        functools.partial(_mix_body, alpha=alpha),
        grid=(n // tm,),
        in_specs=[pl.BlockSpec((tm, D_A), lambda i: (i, 0)),
                  pl.BlockSpec((tm, D_M), lambda i: (i, 0)),
                  pl.BlockSpec((tm, d), lambda i: (i, 0)),
                  pl.BlockSpec((D_A + D_M, d), const),
                  pl.BlockSpec((1, d), const),
                  pl.BlockSpec((1, d), const),
                  pl.BlockSpec((2, d, LANES), lambda i: (0, 0, 0)),
                  pl.BlockSpec((1, LANES), const)],
        out_specs=[pl.BlockSpec((tm, d), lambda i: (i, 0)),
                   pl.BlockSpec((tm, LANES), lambda i: (i, 0))],
        out_shape=[jax.ShapeDtypeStruct((n, d), F32), jax.ShapeDtypeStruct((n, LANES), F32)],
        compiler_params=_cparams(("arbitrary",), 32),
        name="mix_ln_router",
    )(y_a, y_m, x2d, w_out, ln_g, ln_b, w_router, b_router)


def _moe_body(x1_ref, comb_ref, wg_ref, wu_ref, wd_ref, g_ref, b_ref, p_ref, wpg_ref, wpp_ref, y_ref, xb_ref, *, alpha):
    e = pl.program_id(1)

    @pl.when(e == 0)
    def _():
        xb_ref[...] = x1_ref[...].astype(BF16)
        y_ref[...] = jnp.zeros(y_ref.shape, F32)

    xb = xb_ref[...]
    gate = jnp.dot(xb, wg_ref[...], preferred_element_type=F32)
    up = jnp.dot(xb, wu_ref[...], preferred_element_type=F32)
    lane = lax.broadcasted_iota(I32, comb_ref.shape, 1)
    weight = jnp.sum(jnp.where(lane == e, comb_ref[...], 0.0), axis=1, keepdims=True)
    hidden = gate * jax.nn.sigmoid(gate) * up * weight
    y_ref[...] += jnp.dot(hidden.astype(BF16), wd_ref[...], preferred_element_type=F32)

    @pl.when(e == N_EXPERTS - 1)
    def _():
        x2 = _layer_norm(alpha * x1_ref[...] + y_ref[...], g_ref[...], b_ref[...])
        gate = jax.nn.sigmoid(jnp.dot(x2.astype(BF16), wpg_ref[...], preferred_element_type=F32))
        proj = jnp.dot(p_ref[...].astype(BF16), wpp_ref[...], preferred_element_type=F32)
        y_ref[...] = x2 + gate * proj


def _moe_ln_ple(x1, comb, w_gate, w_up, w_down, ln_g, ln_b, p2d, w_ple_gate, w_ple_proj, alpha, tm):
    n, d = x1.shape
    dp = p2d.shape[1]
    const = lambda i, e: (0, 0)
    return pl.pallas_call(
        functools.partial(_moe_body, alpha=alpha),
        grid=(n // tm, N_EXPERTS),
        in_specs=[pl.BlockSpec((tm, d), lambda i, e: (i, 0)),
                  pl.BlockSpec((tm, LANES), lambda i, e: (i, 0)),
                  pl.BlockSpec((None, d, D_EXPERT), lambda i, e: (e, 0, 0)),
                  pl.BlockSpec((None, d, D_EXPERT), lambda i, e: (e, 0, 0)),
                  pl.BlockSpec((None, D_EXPERT, d), lambda i, e: (e, 0, 0)),
                  pl.BlockSpec((1, d), const),
                  pl.BlockSpec((1, d), const),
                  pl.BlockSpec((tm, dp), lambda i, e: (i, 0)),
                  pl.BlockSpec((d, d), const),
                  pl.BlockSpec((dp, d), const)],
        out_specs=pl.BlockSpec((tm, d), lambda i, e: (i, 0)),
        out_shape=jax.ShapeDtypeStruct((n, d), F32),
        scratch_shapes=[pltpu.VMEM((tm, d), BF16)],
        compiler_params=_cparams(("arbitrary", "arbitrary"), 56),
        name="moe_ln_ple",
    )(x1, comb, w_gate, w_up, w_down, ln_g, ln_b, p2d, w_ple_gate, w_ple_proj)


def _tile(n, want):
    t = min(want, n)
    assert n % t == 0
    return t


def _layer(x, p_i, w_in, b_igate, b_fgate, conv_w, conv_b, gn_w, w_out, ln1_g, ln1_b,
           rg_w, rg_b, re_w, re_b, e_gate, e_up, e_down, ln2_g, ln2_b, ple_gate_w, ple_proj_w, alpha):
    b, s, d = x.shape
    n = b * s
    x2d = x.reshape(n, d)

    tk = _tile(s, TK_DSA)
    w_packed, w_vt = _pack_w_in(w_in)
    qk, v_t, q_idx, k4, gates, qk_m, v_m, o_m = _in_proj(x2d, w_packed, w_vt, tk)

    gates3 = gates.reshape(b, s, LANES)
    y_a = _dsa_attention(qk.reshape(2 * SLABS_PER_TENSOR, b, s, LANES),
                         v_t.reshape(SLABS_PER_TENSOR, b, s // tk, LANES, tk),
                         q_idx.reshape(b, s, N_IDX_HEADS * IDX_DIM), k4.reshape(b, s, LANES), gates3,
                         _tile(s, TQ_DSA), tk)

    gate_bias = jnp.concatenate([b_igate, b_fgate]).astype(F32)
    bias_slab = jnp.zeros((1, LANES), F32).at[0, G_I:G_I + 2 * N_HEADS_M].set(gate_bias)
    gates_t = jnp.swapaxes(gates3[:, :, G_I:G_I + 2 * N_HEADS_M], 1, 2)
    y_m = _mlstm(qk_m.reshape(b, s, 2 * D_M), v_m.reshape(b, s, D_M), o_m.reshape(b, s, D_M), gates3, gates_t,
                 conv_w, conv_b.reshape(1, -1), bias_slab, gate_bias.reshape(-1, 1), gn_w.reshape(1, -1),
                 _tile(s, CHUNK_MLSTM))

    w_router = jnp.concatenate([re_w, rg_w, jnp.zeros((d, LANES - N_EXPERTS - N_GROUPS), F32)], axis=1)
    w_router_hi = w_router.astype(BF16)
    w_router = jnp.stack([w_router_hi, (w_router - w_router_hi.astype(F32)).astype(BF16)])
    b_router = jnp.concatenate([re_b, rg_b, jnp.zeros((LANES - N_EXPERTS - N_GROUPS,), F32)]).reshape(1, LANES)
    x1, comb = _mix_ln_router(y_a.reshape(n, D_A), y_m.reshape(n, D_M), x2d, w_out.astype(BF16),
                              ln1_g.reshape(1, d), ln1_b.reshape(1, d), w_router, b_router, alpha, _tile(n, TM_MIX))

    out = _moe_ln_ple(x1, comb, e_gate.astype(BF16), e_up.astype(BF16), e_down.astype(BF16),
                      ln2_g.reshape(1, d), ln2_b.reshape(1, d), p_i.reshape(n, -1), ple_gate_w.astype(BF16),
                      ple_proj_w.astype(BF16), alpha, _tile(n, TM_MOE))
    return out.reshape(b, s, d)


def kernel(x, p, w_in, b_igate, b_fgate, conv_w, conv_b, gn_w, w_out, ln1_g, ln1_b, router_group_w, router_group_b,
           router_expert_w, router_expert_b, expert_w_gate, expert_w_up, expert_w_down, ln2_g, ln2_b, ple_gate_w,
           ple_proj_w):
    depth = w_in.shape[0]
    alpha = (2.0 * depth) ** 0.25
    for i in range(depth):
        x = _layer(x, p[i], w_in[i], b_igate[i], b_fgate[i], conv_w[i], conv_b[i], gn_w[i], w_out[i], ln1_g[i],
                   ln1_b[i], router_group_w[i], router_group_b[i], router_expert_w[i], router_expert_b[i],
                   expert_w_gate[i], expert_w_up[i], expert_w_down[i], ln2_g[i], ln2_b[i], ple_gate_w[i],
                   ple_proj_w[i], alpha)
    return x
```

```python
import functools

import numpy as np
import jax
import jax.numpy as jnp
from jax import lax
from jax.experimental import pallas as pl
from jax.experimental.pallas import tpu as pltpu

F32 = jnp.float32
BF16 = jnp.bfloat16
I32 = jnp.int32

N_HEADS_A = 8
HEAD_DIM_A = 64
D_A = N_HEADS_A * HEAD_DIM_A
N_IDX_HEADS = 8
IDX_DIM = 32
TOPK_MAX = 256
N_HEADS_M = 4
HEAD_DIM_M = 128
D_M = N_HEADS_M * HEAD_DIM_M
CONV_K = 4
N_GROUPS = 4
EXPERTS_PER_GROUP = 8
N_EXPERTS = N_GROUPS * EXPERTS_PER_GROUP
D_EXPERT = 256
LN_EPS = 1e-5
NEG_BIG = -1e30
IN_SPLITS = (D_A, D_A, D_A, N_IDX_HEADS * IDX_DIM, IDX_DIM, N_IDX_HEADS, D_M, D_M, D_M, D_M, N_HEADS_M, N_HEADS_M)

LANES = 128
SUBLANES = 8
MIB = 1024 * 1024

C_QK = 0
C_QIDX = C_QK + 2 * D_A
C_K4 = C_QIDX + N_IDX_HEADS * IDX_DIM
C_GATE = C_K4 + LANES
C_QKM = C_GATE + LANES
C_VM = C_QKM + 2 * D_M
C_OM = C_VM + D_M
C_END = C_OM + D_M
SLABS_PER_TENSOR = D_A // LANES
HEADS_PER_SLAB = LANES // HEAD_DIM_A
G_W = 0
G_I = N_IDX_HEADS
G_F = G_I + N_HEADS_M

INT_MIN = -2 ** 31

TQ_DSA = LANES
TK_DSA = 1024
TM_IN_PROJ = TK_DSA
CHUNK_MLSTM = 256
TM_MIX = 512
TM_MOE = 1024

NT_DIMS = (((1,), (1,)), ((), ()))


def _cparams(semantics, vmem_mib):
    return pltpu.CompilerParams(dimension_semantics=semantics, vmem_limit_bytes=int(vmem_mib * MIB))


def _layer_norm(r, g, b):
    mu = jnp.mean(r, axis=-1, keepdims=True)
    d = r - mu
    var = jnp.mean(d * d, axis=-1, keepdims=True)
    return d * lax.rsqrt(var + LN_EPS) * g + b


def _pack_w_in(w_in):
    d = w_in.shape[0]
    cuts = [int(c) for c in np.cumsum(IN_SPLITS)[:-1]]
    q_a, k_a, v_a, q_idx, k_idx, w_idx, q_m, k_m, v_m, o_m, i_m, f_m = jnp.split(w_in, cuts, axis=1)
    k4 = jnp.tile(k_idx, (1, LANES // IDX_DIM))
    gate = jnp.concatenate([w_idx, i_m, f_m, jnp.zeros((d, LANES - G_F - N_HEADS_M), w_in.dtype)], axis=1)
    packed = jnp.concatenate([q_a, k_a, q_idx, k4, gate, q_m, k_m, v_m, o_m], axis=1).astype(BF16)
    return packed, v_a.T.astype(BF16)


def _in_proj_body(x_ref, w_ref, wvt_ref, qk_ref, vt_ref, qi_ref, k4_ref, g_ref, qkm_ref, vm_ref, om_ref):
    xb = x_ref[...].astype(BF16)

    def proj(lo, hi):
        return jnp.dot(xb, w_ref[:, lo:hi], preferred_element_type=F32)

    qk = proj(C_QK, C_QIDX).astype(BF16)
    for c in range(2 * SLABS_PER_TENSOR):
        qk_ref[c] = qk[:, c * LANES:(c + 1) * LANES]
    v_t = lax.dot_general(wvt_ref[...], xb, NT_DIMS, preferred_element_type=F32).astype(BF16)
    for c in range(SLABS_PER_TENSOR):
        vt_ref[c, 0] = v_t[c * LANES:(c + 1) * LANES, :]
    qi_ref[...] = proj(C_QIDX, C_K4).astype(BF16)
    k4_ref[...] = proj(C_K4, C_GATE).astype(BF16)
    g_ref[...] = proj(C_GATE, C_QKM)
    qkm_ref[...] = proj(C_QKM, C_VM)
    vm_ref[...] = proj(C_VM, C_OM).astype(BF16)
    om_ref[...] = proj(C_OM, C_END)


def _in_proj(x2d, w_packed, w_vt, tm):
    n, d = x2d.shape
    widths = [(C_K4 - C_QIDX, BF16), (C_GATE - C_K4, BF16), (C_QKM - C_GATE, F32),
              (C_VM - C_QKM, F32), (C_OM - C_VM, BF16), (C_END - C_OM, F32)]
    return pl.pallas_call(
        _in_proj_body,
        grid=(n // tm,),
        in_specs=[pl.BlockSpec((tm, d), lambda i: (i, 0)),
                  pl.BlockSpec((d, C_END), lambda i: (0, 0)),
                  pl.BlockSpec((D_A, d), lambda i: (0, 0))],
        out_specs=([pl.BlockSpec((2 * SLABS_PER_TENSOR, tm, LANES), lambda i: (0, i, 0)),
                    pl.BlockSpec((SLABS_PER_TENSOR, 1, LANES, tm), lambda i: (0, i, 0, 0))]
                   + [pl.BlockSpec((tm, w), lambda i: (i, 0)) for w, _ in widths]),
        out_shape=([jax.ShapeDtypeStruct((2 * SLABS_PER_TENSOR, n, LANES), BF16),
                    jax.ShapeDtypeStruct((SLABS_PER_TENSOR, n // tm, LANES, tm), BF16)]
                   + [jax.ShapeDtypeStruct((n, w), dt) for w, dt in widths]),
        compiler_params=_cparams(("arbitrary",), 48),
        name="in_proj",
    )(x2d, w_packed, w_vt)


def _float_to_ordered_int(value):
    bits = lax.bitcast_convert_type(value, I32)
    return bits ^ ((bits >> 31) & jnp.int32(0x7FFFFFFF))


def _ordered_int_to_float(key):
    bits = key ^ ((key >> 31) & jnp.int32(0x7FFFFFFF))
    return lax.bitcast_convert_type(bits, F32)


SEARCH_FREE_PASSES = 16
SEARCH_MAX_PASSES = SEARCH_FREE_PASSES + 33
ZERO_TIE_MARGIN = 128
CHAINS = 4
POS_RADIX = 64


def _dsa_body(qa_ref, ka_ref, vt_ref, qi_ref, k4_ref, g_ref, o_ref,
              sc_ref, qm_ref, qh_ref, mask_ref, acc_ref, m_ref, l_ref, *, seq, topk, tq, tk):
    t0 = pl.program_id(1) * tq
    n_kb = (t0 + tq + tk - 1) // tk
    groups = tk // (CHAINS * SUBLANES)
    shape4 = (groups, CHAINS, SUBLANES, tq)

    def chunks(x):
        return x.reshape(shape4)

    def fold(x):
        out = x[0]
        for c in range(1, CHAINS):
            out = out + x[c]
        return out

    def rows8(row):
        return jnp.broadcast_to(row, (SUBLANES, tq))

    key_off = ((lax.broadcasted_iota(I32, shape4, 0) * CHAINS + lax.broadcasted_iota(I32, shape4, 1)) * SUBLANES
               + lax.broadcasted_iota(I32, shape4, 2))
    q_pos = t0 + lax.broadcasted_iota(I32, shape4, 3)
    lane = lax.broadcasted_iota(I32, (tq, LANES), 1)

    for h in range(N_IDX_HEADS):
        per_slab = LANES // IDX_DIM
        slab = qi_ref[:, (h // per_slab) * LANES:(h // per_slab + 1) * LANES]
        lo = (h % per_slab) * IDX_DIM
        qm_ref[h * tq:(h + 1) * tq, :] = jnp.where((lane >= lo) & (lane < lo + IDX_DIM), slab, jnp.zeros_like(slab))
    for h in range(N_HEADS_A):
        p, hh = divmod(h, HEADS_PER_SLAB)
        slope = 2.0 ** (-8.0 * (h + 1) / N_HEADS_A)
        slab = qa_ref[p].astype(F32) * (HEAD_DIM_A ** -0.5)
        in_head = (lane >= hh * HEAD_DIM_A) & (lane < (hh + 1) * HEAD_DIM_A)
        qh_ref[p, hh * tq:(hh + 1) * tq, 0:LANES] = jnp.where(in_head, slab, 0.0).astype(BF16)
        feat = jnp.where(lane == 0, slope * POS_RADIX, jnp.where(lane == 1, slope, 0.0))
        qh_ref[p, hh * tq:(hh + 1) * tq, LANES:2 * LANES] = feat.astype(BF16)
    w_rows = g_ref[...].T[G_W:G_W + N_IDX_HEADS, :] * ((IDX_DIM * N_IDX_HEADS) ** -0.5)

    def score_block(j, carry):
        hi_part, lo_part, zero_part = carry
        k0 = pl.multiple_of(j * tk, tk)
        logits = lax.dot_general(k4_ref[pl.ds(k0, tk), :], qm_ref[...], NT_DIMS,
                                 preferred_element_type=F32)
        score = jnp.zeros(shape4, F32)
        for h in range(N_IDX_HEADS):
            score = score + rows8(w_rows[h:h + 1, :]) * jnp.maximum(chunks(logits[:, h * tq:(h + 1) * tq]), 0.0)
        causal = k0 + key_off <= q_pos
        kept = jnp.where(causal, score, NEG_BIG)
        sc_ref[j] = kept
        hi_part = jnp.maximum(hi_part, jnp.max(kept, axis=0))
        lo_part = jnp.minimum(lo_part, jnp.min(jnp.where(causal, score, jnp.inf), axis=0))
        zero_part = zero_part + jnp.sum(jnp.where(kept >= 0.0, 1.0, 0.0), axis=0)
        return hi_part, lo_part, zero_part

    part_shape = (CHAINS, SUBLANES, tq)
    hi_part, lo_part, zero_part = lax.fori_loop(
        0, n_kb, score_block,
        (jnp.full(part_shape, -jnp.inf, F32), jnp.full(part_shape, jnp.inf, F32), jnp.zeros(part_shape, F32)))
    row_max = jnp.max(jnp.max(hi_part, axis=0), axis=0, keepdims=True)
    row_min = jnp.min(jnp.min(lo_part, axis=0), axis=0, keepdims=True)

    n_tail = (seq - n_kb * tk).astype(F32)

    def count(thr, strict):
        thr8 = rows8(thr)

        def body(j, acc):
            s = sc_ref[j]
            return acc + jnp.sum(jnp.where((s > thr8) if strict else (s >= thr8), 1.0, 0.0), axis=0)

        acc = lax.fori_loop(0, n_kb, body, jnp.zeros(part_shape, F32))
        tail_hit = (NEG_BIG > thr) if strict else (NEG_BIG >= thr)
        return jnp.sum(fold(acc), axis=0, keepdims=True) + jnp.where(tail_hit, n_tail, 0.0)

    n_causal = (t0 + 1 + lax.broadcasted_iota(I32, (1, tq), 1)).astype(F32)
    few = n_causal < topk
    lo0 = jnp.where(few, _float_to_ordered_int(jnp.full((1, tq), NEG_BIG, F32)), _float_to_ordered_int(row_min))
    cnt_lo0 = jnp.where(few, float(seq), n_causal)
    hi0 = _float_to_ordered_int(row_max) + 1
    log_k = float(np.log(topk))

    def log_excess(cnt):
        return jnp.log(jnp.maximum(cnt, 0.5)) - log_k

    def search_step(state):
        it, _, lo, hi, cnt_lo, f_lo, f_hi, last, done_i = state
        done = done_i != 0
        it_v = jnp.zeros((1, tq), I32) + it
        lo_v = _ordered_int_to_float(lo)
        hi_v = _ordered_int_to_float(hi)
        guess = _float_to_ordered_int(lo_v + (f_lo / (f_lo - f_hi)) * (hi_v - lo_v))
        guess = jnp.where((it_v == 1) & (lo == 0) & (cnt_lo < topk + ZERO_TIE_MARGIN), 1, guess)
        middle = lo + lax.shift_right_logical(hi - lo, jnp.ones((1, tq), I32))
        usable = (guess > lo) & (guess < hi) & (it_v < SEARCH_FREE_PASSES)
        cand = jnp.where(usable, guess, middle)
        cnt = count(_ordered_int_to_float(cand), strict=False)
        f_c = log_excess(cnt)
        take = (cnt >= topk) & ~done
        drop = (cnt < topk) & ~done
        f_hi = jnp.where(take, jnp.where(last == 1, 0.5 * f_hi, f_hi), jnp.where(drop, f_c, f_hi))
        f_lo = jnp.where(drop, jnp.where(last == -1, 0.5 * f_lo, f_lo), jnp.where(take, f_c, f_lo))
        last = jnp.where(take, 1, jnp.where(drop, -1, last))
        lo = jnp.where(take, cand, lo)
        cnt_lo = jnp.where(take, cnt, cnt_lo)
        hi = jnp.where(drop, cand, hi)
        done = done | (cnt_lo == topk) | ((hi - lo) == 1)
        active = jnp.sum(jnp.where(done, 0.0, 1.0))
        return it + 1, active, lo, hi, cnt_lo, f_lo, f_hi, last, done.astype(I32)

    def search_on(state):
        return (state[0] < SEARCH_MAX_PASSES) & (state[1] > 0.0)

    cnt_zero = jnp.sum(fold(zero_part), axis=0, keepdims=True)
    inside = (lo0 < 0) & (hi0 > 0) & ~few
    take0 = inside & (cnt_zero >= topk)
    drop0 = inside & (cnt_zero < topk)
    lo1 = jnp.where(take0, 0, lo0)
    hi1 = jnp.where(drop0, 0, hi0)
    cnt_lo1 = jnp.where(take0, cnt_zero, cnt_lo0)
    done1 = few | (cnt_lo1 == topk) | ((hi1 - lo1) == 1)
    state = lax.while_loop(search_on, search_step,
                           (jnp.int32(1), jnp.float32(1.0), lo1, hi1, cnt_lo1, log_excess(cnt_lo1),
                            log_excess(jnp.where(drop0, cnt_zero, 0.0)),
                            jnp.where(take0, 1, jnp.where(drop0, -1, 0)), done1.astype(I32)))
    thr = _ordered_int_to_float(state[2])
    cnt_thr = state[4]
    thr8 = rows8(thr)

    @pl.when(jnp.max(cnt_thr) > topk)
    def _():
        need = topk - count(thr, strict=True)
        tri_r = lax.broadcasted_iota(I32, (tk, tk), 0)
        tri_c = lax.broadcasted_iota(I32, (tk, tk), 1)
        earlier = (tri_c < tri_r).astype(BF16)

        def drop_late_ties(j, seen):
            s = sc_ref[j]
            eq = s == thr8
            eq_f = jnp.where(eq, 1.0, 0.0)
            rank = seen + jnp.dot(earlier, eq_f.reshape(tk, tq).astype(BF16), preferred_element_type=F32)
            sc_ref[j] = jnp.where(eq & (chunks(rank) >= rows8(need)), -jnp.inf, s)
            return seen + jnp.sum(fold(jnp.sum(eq_f, axis=0)), axis=0, keepdims=True)

        lax.fori_loop(0, n_kb, drop_late_ties, jnp.zeros((1, tq), F32))

    m_ref[...] = jnp.full(m_ref.shape, NEG_BIG, F32)
    l_ref[...] = jnp.zeros(l_ref.shape, F32)
    acc_ref[...] = jnp.zeros(acc_ref.shape, F32)
    pos_lane = lax.broadcasted_iota(I32, (tk, LANES), 1)
    pos_row = lax.broadcasted_iota(I32, (tk, LANES), 0)
    feat0 = jnp.where(pos_lane == 0, pos_row // POS_RADIX, jnp.where(pos_lane == 1, pos_row % POS_RADIX, 0)).astype(F32)
    hi_lane = jnp.where(pos_lane == 0, 1.0, 0.0)

    def attend(j, carry):
        k0 = pl.multiple_of(j * tk, tk)
        mask_ref[...] = jnp.where((sc_ref[j] >= thr8) & (k0 + key_off <= q_pos), 0.0, -jnp.inf)
        pos_feat = (feat0 + hi_lane * (j * (tk // POS_RADIX)).astype(F32)).astype(BF16)

        scores = []
        for p in range(SLABS_PER_TENSOR):
            keys = jnp.concatenate([ka_ref[p, pl.ds(k0, tk), :], pos_feat], axis=1)
            scores.append(lax.dot_general(keys, qh_ref[p], NT_DIMS, preferred_element_type=F32))
        weights = []
        for p in range(SLABS_PER_TENSOR):
            for hh in range(HEADS_PER_SLAB):
                s = chunks(scores[p][:, hh * tq:(hh + 1) * tq]) + mask_ref[...]
                m_old = m_ref[p, hh]
                m_blk = jnp.max(fold_max(jnp.max(s, axis=0)), axis=0, keepdims=True)
                m_new = jnp.maximum(m_old, rows8(m_blk))
                alpha = jnp.exp(m_old - m_new)
                m_ref[p, hh] = m_new
                weights.append((alpha, jnp.exp(s - m_new).reshape(tk, tq).astype(BF16)))
        ones_rows = jnp.ones((SUBLANES, tk), BF16)
        for p in range(SLABS_PER_TENSOR):
            for hh in range(HEADS_PER_SLAB):
                rows = slice(hh * HEAD_DIM_A, (hh + 1) * HEAD_DIM_A)
                alpha, probs = weights[p * HEADS_PER_SLAB + hh]
                values = jnp.concatenate([vt_ref[p, j, rows, :], ones_rows], axis=0)
                pv = jnp.dot(values, probs, preferred_element_type=F32)
                old = acc_ref[p, rows, :].reshape(HEAD_DIM_A // SUBLANES, SUBLANES, tq)
                acc_ref[p, rows, :] = (alpha * old).reshape(HEAD_DIM_A, tq) + pv[0:HEAD_DIM_A]
                l_ref[p, hh] = alpha * l_ref[p, hh] + pv[HEAD_DIM_A:HEAD_DIM_A + SUBLANES]
        return carry

    def fold_max(x):
        out = x[0]
        for c in range(1, CHAINS):
            out = jnp.maximum(out, x[c])
        return out

    lax.fori_loop(0, n_kb, attend, 0)

    for p in range(SLABS_PER_TENSOR):
        halves = []
        for hh in range(HEADS_PER_SLAB):
            rows = slice(hh * HEAD_DIM_A, (hh + 1) * HEAD_DIM_A)
            acc = acc_ref[p, rows, :].reshape(HEAD_DIM_A // SUBLANES, SUBLANES, tq)
            halves.append((acc / l_ref[p, hh]).reshape(HEAD_DIM_A, tq))
        o_ref[:, p * LANES:(p + 1) * LANES] = jnp.concatenate(halves, axis=0).T.astype(BF16)


def _dsa_attention(qk, v_t, q_idx, k4, gates, tq, tk):
    _, b, s, _ = qk.shape
    assert tq == LANES and s // POS_RADIX <= 256
    topk = min(TOPK_MAX, s // 4)
    once = pl.Buffered(1)
    groups = tk // (CHAINS * SUBLANES)
    body = functools.partial(_dsa_body, seq=s, topk=topk, tq=tq, tk=tk)
    return pl.pallas_call(
        body,
        grid=(b, s // tq),
        in_specs=[pl.BlockSpec((SLABS_PER_TENSOR, None, tq, LANES), lambda bi, qi: (0, bi, qi, 0)),
                  pl.BlockSpec((SLABS_PER_TENSOR, None, s, LANES), lambda bi, qi: (1, bi, 0, 0), pipeline_mode=once),
                  pl.BlockSpec((SLABS_PER_TENSOR, None, s // tk, LANES, tk), lambda bi, qi: (0, bi, 0, 0, 0),
                               pipeline_mode=once),
                  pl.BlockSpec((None, tq, N_IDX_HEADS * IDX_DIM), lambda bi, qi: (bi, qi, 0)),
                  pl.BlockSpec((None, s, LANES), lambda bi, qi: (bi, 0, 0), pipeline_mode=once),
                  pl.BlockSpec((None, tq, LANES), lambda bi, qi: (bi, qi, 0))],
        out_specs=pl.BlockSpec((None, tq, D_A), lambda bi, qi: (bi, qi, 0)),
        out_shape=jax.ShapeDtypeStruct((b, s, D_A), BF16),
        scratch_shapes=[pltpu.VMEM((s // tk, groups, CHAINS, SUBLANES, tq), F32),
                        pltpu.VMEM((N_IDX_HEADS * tq, LANES), BF16),
                        pltpu.VMEM((SLABS_PER_TENSOR, HEADS_PER_SLAB * tq, 2 * LANES), BF16),
                        pltpu.VMEM((groups, CHAINS, SUBLANES, tq), F32),
                        pltpu.VMEM((SLABS_PER_TENSOR, LANES, tq), F32),
                        pltpu.VMEM((SLABS_PER_TENSOR, HEADS_PER_SLAB, SUBLANES, tq), F32),
                        pltpu.VMEM((SLABS_PER_TENSOR, HEADS_PER_SLAB, SUBLANES, tq), F32)],
        compiler_params=_cparams(("arbitrary", "arbitrary"), 48),
        name="dsa_attention",
    )(qk, qk, v_t, q_idx, k4, gates)


def _log_sigmoid(x):
    return jnp.minimum(x, 0.0) - jnp.log(1.0 + jnp.exp(-jnp.abs(x)))


def _split3(a):
    hi = a.astype(BF16)
    r1 = a - hi.astype(F32)
    mid = r1.astype(BF16)
    lo = (r1 - mid.astype(F32)).astype(BF16)
    return hi, mid, lo


def _mlstm_body(qk_ref, v_ref, o_ref, g_ref, gt_ref, cw_ref, cb_ref, bg_ref, bgt_ref, gn_ref, y_ref,
                xe_ref, c_ref, m_ref, *, chunk, batch):
    @pl.when(pl.program_id(0) == 0)
    def _():
        xe_ref[:, 0:SUBLANES, :] = jnp.zeros((batch, SUBLANES, 2 * D_M), F32)
        c_ref[...] = jnp.zeros(c_ref.shape, F32)
        m_ref[...] = jnp.zeros(m_ref.shape, F32)

    gates = [_mlstm_gates(qk_ref.at[bi], g_ref.at[bi], gt_ref.at[bi], cw_ref, cb_ref, bg_ref, bgt_ref,
                          xe_ref.at[bi], chunk) for bi in range(batch)]
    for h in range(N_HEADS_M):
        for bi in range(batch):
            _mlstm_head(h, gates[bi], v_ref.at[bi], o_ref.at[bi], gn_ref, y_ref.at[bi], c_ref.at[bi], m_ref.at[bi],
                        chunk)


def _mlstm_gates(qk_ref, g_ref, gt_ref, cw_ref, cb_ref, bg_ref, bgt_ref, xe_ref, chunk):
    xe_ref[SUBLANES:SUBLANES + chunk, :] = qk_ref[...]
    conv = cb_ref[...]
    for j in range(CONV_K):
        start = SUBLANES - (CONV_K - 1) + j
        conv = conv + cw_ref[j:j + 1, :] * xe_ref[start:start + chunk, :]
    qk = conv * jax.nn.sigmoid(conv)
    xe_ref[0:SUBLANES, :] = xe_ref[chunk:chunk + SUBLANES, :]

    gb = g_ref[...] + bg_ref[...]
    gbt = gt_ref[...] + bgt_ref[...]
    r = lax.broadcasted_iota(I32, (chunk, chunk), 0)
    c = lax.broadcasted_iota(I32, (chunk, chunk), 1)
    causal = r >= c
    incl = causal.astype(BF16)
    incl_t = (r <= c).astype(BF16)
    cum = jnp.zeros((chunk, LANES), F32)
    for term in _split3(_log_sigmoid(gb)):
        cum = cum + jnp.dot(incl, term, preferred_element_type=F32)
    cum_t = jnp.zeros((2 * N_HEADS_M, chunk), F32)
    for term in _split3(_log_sigmoid(gbt)):
        cum_t = cum_t + jnp.dot(term, incl_t, preferred_element_type=F32)

    return qk, gb, gbt, cum, cum_t, causal


def _mlstm_head(h, gates, v_ref, o_ref, gn_ref, y_ref, c_ref, m_ref, chunk):
    qk, gb, gbt, cum, cum_t, causal = gates
    lane = lax.broadcasted_iota(I32, (chunk, LANES), 1)
    ones_col = jnp.where(lane == 0, 1.0, 0.0).astype(BF16)
    sl = slice(h * HEAD_DIM_M, (h + 1) * HEAD_DIM_M)
    q_h = (qk[:, sl] * (HEAD_DIM_M ** -0.5)).astype(BF16)
    k_f = qk[:, D_M + h * HEAD_DIM_M:D_M + (h + 1) * HEAD_DIM_M]
    v_aug = jnp.concatenate([v_ref[:, sl], ones_col], axis=1)
    b_col = cum[:, G_F + h:G_F + h + 1]
    i_col = gb[:, G_I + h:G_I + h + 1]
    b_row = cum_t[N_HEADS_M + h:N_HEADS_M + h + 1, :]
    i_row = gbt[h:h + 1, :]
    m_prev = m_ref[h:h + 1, 0:1]
    c_aug = c_ref[h]

    d_log = jnp.where(causal, b_col - b_row + i_row, -jnp.inf)
    inter = b_col + m_prev
    m_t = jnp.maximum(inter, jnp.max(d_log, axis=1, keepdims=True))
    qk_t = lax.dot_general(q_h, k_f.astype(BF16), NT_DIMS, preferred_element_type=F32)
    w_intra = qk_t * jnp.exp(d_log - m_t)
    w_inter = jnp.exp(inter - m_t)
    intra = jnp.dot(w_intra.astype(BF16), v_aug[:, 0:HEAD_DIM_M], preferred_element_type=F32)
    carried = jnp.dot(q_h, c_aug.astype(BF16), preferred_element_type=F32)
    num = intra + w_inter * carried[:, 0:HEAD_DIM_M]
    den = jnp.sum(w_intra, axis=1, keepdims=True) + w_inter * carried[:, HEAD_DIM_M:HEAD_DIM_M + 1]
    hid = num / jnp.maximum(jnp.abs(den), jnp.exp(-m_t))

    mu = jnp.mean(hid, axis=1, keepdims=True)
    dev = hid - mu
    var = jnp.mean(dev * dev, axis=1, keepdims=True)
    normed = dev * lax.rsqrt(var + LN_EPS) * gn_ref[:, sl]
    y_ref[:, sl] = (jax.nn.sigmoid(o_ref[:, sl]) * normed).astype(BF16)

    b_last = b_col[chunk - 1:chunk, :]
    w_log = b_last - b_col + i_col
    m_new = jnp.maximum(b_last + m_prev, jnp.max(w_log, axis=0, keepdims=True))
    decay = jnp.exp(b_last + m_prev - m_new)
    kw_t = (k_f * jnp.exp(w_log - m_new)).T.astype(BF16)
    c_ref[h] = decay * c_aug + jnp.dot(kw_t, v_aug, preferred_element_type=F32)
    m_ref[h:h + 1, :] = jnp.broadcast_to(m_new, (1, LANES))


def _mlstm(qk_m, v_m, o_m, gates, gates_t, conv_w, conv_b, bias_slab, bias_col, gn_w, chunk):
    b, s, _ = qk_m.shape
    body = functools.partial(_mlstm_body, chunk=chunk, batch=b)
    const = lambda ci: (0, 0)
    return pl.pallas_call(
        body,
        grid=(s // chunk,),
        in_specs=[pl.BlockSpec((b, chunk, 2 * D_M), lambda ci: (0, ci, 0)),
                  pl.BlockSpec((b, chunk, D_M), lambda ci: (0, ci, 0)),
                  pl.BlockSpec((b, chunk, D_M), lambda ci: (0, ci, 0)),
                  pl.BlockSpec((b, chunk, LANES), lambda ci: (0, ci, 0)),
                  pl.BlockSpec((b, 2 * N_HEADS_M, chunk), lambda ci: (0, 0, ci)),
                  pl.BlockSpec((CONV_K, 2 * D_M), const),
                  pl.BlockSpec((1, 2 * D_M), const),
                  pl.BlockSpec((1, LANES), const),
                  pl.BlockSpec((2 * N_HEADS_M, 1), const),
                  pl.BlockSpec((1, D_M), const)],
        out_specs=pl.BlockSpec((b, chunk, D_M), lambda ci: (0, ci, 0)),
        out_shape=jax.ShapeDtypeStruct((b, s, D_M), BF16),
        scratch_shapes=[pltpu.VMEM((b, chunk + SUBLANES, 2 * D_M), F32),
                        pltpu.VMEM((b, N_HEADS_M, HEAD_DIM_M, 2 * LANES), F32),
                        pltpu.VMEM((b, SUBLANES, LANES), F32)],
        compiler_params=_cparams(("arbitrary",), 32),
        name="mlstm",
    )(qk_m, v_m, o_m, gates, gates_t, conv_w, conv_b, bias_slab, bias_col, gn_w)


COMB_GROUP_LANE = N_EXPERTS


def _route(logits):
    lane = lax.broadcasted_iota(I32, logits.shape, 1).astype(F32)
    g_mask = (lane >= N_EXPERTS) & (lane < N_EXPERTS + N_GROUPS)
    g_logit = jnp.where(g_mask, logits, -jnp.inf)
    g_exp = jnp.exp(g_logit - jnp.max(g_logit, axis=1, keepdims=True))
    g_prob = g_exp / jnp.sum(g_exp, axis=1, keepdims=True)
    g_p = jnp.max(g_prob, axis=1, keepdims=True)
    g_sel = jnp.min(jnp.where(g_mask & (g_prob == g_p), lane, float(LANES)), axis=1, keepdims=True) - N_EXPERTS
    e_mask = (lane >= g_sel * EXPERTS_PER_GROUP) & (lane < (g_sel + 1.0) * EXPERTS_PER_GROUP)
    e_logit = jnp.where(e_mask, logits, -jnp.inf)
    e_exp = jnp.exp(e_logit - jnp.max(e_logit, axis=1, keepdims=True))
    e_prob = e_exp / jnp.sum(e_exp, axis=1, keepdims=True)
    p1 = jnp.max(jnp.where(e_mask, e_prob, -1.0), axis=1, keepdims=True)
    i1 = jnp.min(jnp.where(e_mask & (e_prob == p1), lane, float(LANES)), axis=1, keepdims=True)
    rest = e_mask & (lane != i1)
    p2 = jnp.max(jnp.where(rest, e_prob, -1.0), axis=1, keepdims=True)
    i2 = jnp.min(jnp.where(rest & (e_prob == p2), lane, float(LANES)), axis=1, keepdims=True)
    total = p1 + p2
    comb = (jnp.where(lane == i1, p1 / total, 0.0) + jnp.where(lane == i2, p2 / total, 0.0)) * g_p
    return comb + jnp.where(lane == COMB_GROUP_LANE, g_sel, 0.0)


def _mix_body(ya_ref, ym_ref, x_ref, wo_ref, g_ref, b_ref, wr_ref, br_ref, x1_ref, comb_ref, *, alpha):
    mix = (jnp.dot(ya_ref[...], wo_ref[0:D_A, :], preferred_element_type=F32)
           + jnp.dot(ym_ref[...], wo_ref[D_A:D_A + D_M, :], preferred_element_type=F32))
    x1 = _layer_norm(alpha * x_ref[...] + mix, g_ref[...], b_ref[...])
    x1_ref[...] = x1
    x_hi = x1.astype(BF16)
    x_lo = (x1 - x_hi.astype(F32)).astype(BF16)
    w_hi = wr_ref[0]
    w_lo = wr_ref[1]
    logits = (jnp.dot(x_hi, w_hi, preferred_element_type=F32) + jnp.dot(x_lo, w_hi, preferred_element_type=F32)
              + jnp.dot(x_hi, w_lo, preferred_element_type=F32)) + br_ref[...]
    comb_ref[...] = _route(logits)


def _mix_ln_router(y_a, y_m, x2d, w_out, ln_g, ln_b, w_router, b_router, alpha, tm):
    n, d = x2d.shape
    const = lambda i: (0, 0)
    return pl.pallas_call(
        functools.partial(_mix_body, alpha=alpha),
        grid=(n // tm,),
        in_specs=[pl.BlockSpec((tm, D_A), lambda i: (i, 0)),
                  pl.BlockSpec((tm, D_M), lambda i: (i, 0)),
                  pl.BlockSpec((tm, d), lambda i: (i, 0)),
                  pl.BlockSpec((D_A + D_M, d), const),
                  pl.BlockSpec((1, d), const),
                  pl.BlockSpec((1, d), const),
                  pl.BlockSpec((2, d, LANES), lambda i: (0, 0, 0)),
                  pl.BlockSpec((1, LANES), const)],
        out_specs=[pl.BlockSpec((tm, d), lambda i: (i, 0)),
                   pl.BlockSpec((tm, LANES), lambda i: (i, 0))],
        out_shape=[jax.ShapeDtypeStruct((n, d), F32), jax.ShapeDtypeStruct((n, LANES), F32)],
        compiler_params=_cparams(("arbitrary",), 32),
        name="mix_ln_router",
    )(y_a, y_m, x2d, w_out, ln_g, ln_b, w_router, b_router)


MOE_ROWS = 256


def _moe_body(x1_ref, comb_ref, wg_ref, wu_ref, wd_ref, g_ref, b_ref, p_ref, wpg_ref, wpp_ref, y_ref,
              xb_ref, before_ref, cs_ref, col_ref, row_ref, *, alpha, tm):
    g = pl.program_id(1)
    g_f = g.astype(F32)

    @pl.when((pl.program_id(0) == 0) & (g == 0))
    def _():
        r = lax.broadcasted_iota(I32, (tm, tm), 0)
        c = lax.broadcasted_iota(I32, (tm, tm), 1)
        before_ref[...] = (c < r).astype(BF16)

    @pl.when(g == 0)
    def _():
        xb_ref[...] = x1_ref[...].astype(BF16)
        y_ref[...] = jnp.zeros(y_ref.shape, F32)
        comb = comb_ref[...]
        for k, term in enumerate(_split3(comb)):
            cs_ref[k] = term
        lane = lax.broadcasted_iota(I32, (tm, LANES), 1)
        group_col = jnp.sum(jnp.where(lane == COMB_GROUP_LANE, comb, 0.0), axis=1, keepdims=True)
        member = (lane.astype(F32) == group_col) & (lane < N_GROUPS)
        ahead = jnp.dot(before_ref[...], jnp.where(member, 1.0, 0.0).astype(BF16), preferred_element_type=F32)
        rank_col = jnp.sum(jnp.where(member, ahead, 0.0), axis=1, keepdims=True)
        col_ref[0] = jnp.broadcast_to(group_col, (tm, LANES))
        col_ref[1] = jnp.broadcast_to(rank_col, (tm, LANES))
        group_row = comb.T[COMB_GROUP_LANE:COMB_GROUP_LANE + 1, :]
        sub = lax.broadcasted_iota(I32, (SUBLANES, tm), 0)
        member_t = (sub.astype(F32) == group_row) & (sub < N_GROUPS)
        ahead_t = lax.dot_general(jnp.where(member_t, 1.0, 0.0).astype(BF16), before_ref[...], NT_DIMS,
                                  preferred_element_type=F32)
        rank_row = jnp.sum(jnp.where(member_t, ahead_t, 0.0), axis=0, keepdims=True)
        row_ref[0] = jnp.broadcast_to(group_row, (SUBLANES, tm))
        row_ref[1] = jnp.broadcast_to(rank_row, (SUBLANES, tm))

    in_group_row = row_ref[0, 0:1, :] == g_f
    in_group_col = col_ref[0] == g_f
    n_tokens = jnp.sum(jnp.where(in_group_row, 1.0, 0.0))
    n_blocks = (n_tokens.astype(I32) + MOE_ROWS - 1) // MOE_ROWS
    slot = lax.broadcasted_iota(I32, (MOE_ROWS, tm), 0).astype(F32)
    slot_lane = lax.broadcasted_iota(I32, (tm, LANES), 1).astype(F32)
    comb_lane = lax.broadcasted_iota(I32, (MOE_ROWS, LANES), 1)

    def block(b, carry):
        base = (b * MOE_ROWS).astype(F32)
        gather = jnp.where(in_group_row & (row_ref[1, 0:1, :] - base == slot), 1.0, 0.0).astype(BF16)
        xc = jnp.dot(gather, xb_ref[...], preferred_element_type=F32).astype(BF16)
        comb_c = jnp.zeros((MOE_ROWS, LANES), F32)
        for k in range(3):
            comb_c = comb_c + jnp.dot(gather, cs_ref[k], preferred_element_type=F32)
        yc = jnp.zeros((MOE_ROWS, x1_ref.shape[1]), F32)
        for e in range(EXPERTS_PER_GROUP):
            gate = jnp.dot(xc, wg_ref[e], preferred_element_type=F32)
            up = jnp.dot(xc, wu_ref[e], preferred_element_type=F32)
            weight = jnp.sum(jnp.where(comb_lane == g * EXPERTS_PER_GROUP + e, comb_c, 0.0), axis=1, keepdims=True)
            hidden = gate * jax.nn.sigmoid(gate) * up * weight
            yc = yc + jnp.dot(hidden.astype(BF16), wd_ref[e], preferred_element_type=F32)
        rank = col_ref[1] - base
        scatter = jnp.concatenate(
            [jnp.where(in_group_col & (rank == slot_lane + float(c * LANES)), 1.0, 0.0).astype(BF16)
             for c in range(MOE_ROWS // LANES)], axis=1)
        yc_hi = yc.astype(BF16)
        yc_lo = (yc - yc_hi.astype(F32)).astype(BF16)
        y_ref[...] += (jnp.dot(scatter, yc_hi, preferred_element_type=F32)
                       + jnp.dot(scatter, yc_lo, preferred_element_type=F32))
        return carry

    lax.fori_loop(0, n_blocks, block, 0)

    @pl.when(g == N_GROUPS - 1)
    def _():
        x2 = _layer_norm(alpha * x1_ref[...] + y_ref[...], g_ref[...], b_ref[...])
        gate = jax.nn.sigmoid(jnp.dot(x2.astype(BF16), wpg_ref[...], preferred_element_type=F32))
        proj = jnp.dot(p_ref[...].astype(BF16), wpp_ref[...], preferred_element_type=F32)
        y_ref[...] = x2 + gate * proj


def _moe_ln_ple(x1, comb, w_gate, w_up, w_down, ln_g, ln_b, p2d, w_ple_gate, w_ple_proj, alpha, tm):
    n, d = x1.shape
    dp = p2d.shape[1]
    const = lambda i, g: (0, 0)
    once = pl.Buffered(1)
    grouped = lambda w: w.reshape((N_GROUPS, EXPERTS_PER_GROUP) + w.shape[1:])
    return pl.pallas_call(
        functools.partial(_moe_body, alpha=alpha, tm=tm),
        grid=(n // tm, N_GROUPS),
        in_specs=[pl.BlockSpec((tm, d), lambda i, g: (i, 0), pipeline_mode=once),
                  pl.BlockSpec((tm, LANES), lambda i, g: (i, 0)),
                  pl.BlockSpec((None, EXPERTS_PER_GROUP, d, D_EXPERT), lambda i, g: (g, 0, 0, 0)),
                  pl.BlockSpec((None, EXPERTS_PER_GROUP, d, D_EXPERT), lambda i, g: (g, 0, 0, 0)),
                  pl.BlockSpec((None, EXPERTS_PER_GROUP, D_EXPERT, d), lambda i, g: (g, 0, 0, 0)),
                  pl.BlockSpec((1, d), const),
                  pl.BlockSpec((1, d), const),
                  pl.BlockSpec((tm, dp), lambda i, g: (i, 0), pipeline_mode=once),
                  pl.BlockSpec((d, d), const, pipeline_mode=once),
                  pl.BlockSpec((dp, d), const, pipeline_mode=once)],
        out_specs=pl.BlockSpec((tm, d), lambda i, g: (i, 0)),
        out_shape=jax.ShapeDtypeStruct((n, d), F32),
        scratch_shapes=[pltpu.VMEM((tm, d), BF16),
                        pltpu.VMEM((tm, tm), BF16),
                        pltpu.VMEM((3, tm, LANES), BF16),
                        pltpu.VMEM((2, tm, LANES), F32),
                        pltpu.VMEM((2, SUBLANES, tm), F32)],
        compiler_params=_cparams(("arbitrary", "arbitrary"), 58),
        name="moe_ln_ple",
    )(x1, comb, grouped(w_gate), grouped(w_up), grouped(w_down), ln_g, ln_b, p2d, w_ple_gate, w_ple_proj)


def _tile(n, want):
    t = min(want, n)
    assert n % t == 0
    return t


def _layer(x, p_i, w_in, b_igate, b_fgate, conv_w, conv_b, gn_w, w_out, ln1_g, ln1_b,
           rg_w, rg_b, re_w, re_b, e_gate, e_up, e_down, ln2_g, ln2_b, ple_gate_w, ple_proj_w, alpha):
    b, s, d = x.shape
    n = b * s
    x2d = x.reshape(n, d)

    tk = _tile(s, TK_DSA)
    w_packed, w_vt = _pack_w_in(w_in)
    qk, v_t, q_idx, k4, gates, qk_m, v_m, o_m = _in_proj(x2d, w_packed, w_vt, tk)

    gates3 = gates.reshape(b, s, LANES)
    y_a = _dsa_attention(qk.reshape(2 * SLABS_PER_TENSOR, b, s, LANES),
                         v_t.reshape(SLABS_PER_TENSOR, b, s // tk, LANES, tk),
                         q_idx.reshape(b, s, N_IDX_HEADS * IDX_DIM), k4.reshape(b, s, LANES), gates3,
                         _tile(s, TQ_DSA), tk)

    gate_bias = jnp.concatenate([b_igate, b_fgate]).astype(F32)
    bias_slab = jnp.zeros((1, LANES), F32).at[0, G_I:G_I + 2 * N_HEADS_M].set(gate_bias)
    gates_t = jnp.swapaxes(gates3[:, :, G_I:G_I + 2 * N_HEADS_M], 1, 2)
    y_m = _mlstm(qk_m.reshape(b, s, 2 * D_M), v_m.reshape(b, s, D_M), o_m.reshape(b, s, D_M), gates3, gates_t,
                 conv_w, conv_b.reshape(1, -1), bias_slab, gate_bias.reshape(-1, 1), gn_w.reshape(1, -1),
                 _tile(s, CHUNK_MLSTM))

    w_router = jnp.concatenate([re_w, rg_w, jnp.zeros((d, LANES - N_EXPERTS - N_GROUPS), F32)], axis=1)
    w_router_hi = w_router.astype(BF16)
    w_router = jnp.stack([w_router_hi, (w_router - w_router_hi.astype(F32)).astype(BF16)])
    b_router = jnp.concatenate([re_b, rg_b, jnp.zeros((LANES - N_EXPERTS - N_GROUPS,), F32)]).reshape(1, LANES)
    x1, comb = _mix_ln_router(y_a.reshape(n, D_A), y_m.reshape(n, D_M), x2d, w_out.astype(BF16),
                              ln1_g.reshape(1, d), ln1_b.reshape(1, d), w_router, b_router, alpha, _tile(n, TM_MIX))

    out = _moe_ln_ple(x1, comb, e_gate.astype(BF16), e_up.astype(BF16), e_down.astype(BF16),
                      ln2_g.reshape(1, d), ln2_b.reshape(1, d), p_i.reshape(n, -1), ple_gate_w.astype(BF16),
                      ple_proj_w.astype(BF16), alpha, _tile(n, TM_MOE))
    return out.reshape(b, s, d)


def kernel(x, p, w_in, b_igate, b_fgate, conv_w, conv_b, gn_w, w_out, ln1_g, ln1_b, router_group_w, router_group_b,
           router_expert_w, router_expert_b, expert_w_gate, expert_w_up, expert_w_down, ln2_g, ln2_b, ple_gate_w,
           ple_proj_w):
    depth = w_in.shape[0]
    alpha = (2.0 * depth) ** 0.25
    for i in range(depth):
        x = _layer(x, p[i], w_in[i], b_igate[i], b_fgate[i], conv_w[i], conv_b[i], gn_w[i], w_out[i], ln1_g[i],
                   ln1_b[i], router_group_w[i], router_group_b[i], router_expert_w[i], router_expert_b[i],
                   expert_w_gate[i], expert_w_up[i], expert_w_down[i], ln2_g[i], ln2_b[i], ple_gate_w[i],
                   ple_proj_w[i], alpha)
    return x
```

```python
import functools

import numpy as np
import jax
import jax.numpy as jnp
from jax import lax
from jax.experimental import pallas as pl
from jax.experimental.pallas import tpu as pltpu

F32 = jnp.float32
BF16 = jnp.bfloat16
I32 = jnp.int32

N_HEADS_A = 8
HEAD_DIM_A = 64
D_A = N_HEADS_A * HEAD_DIM_A
N_IDX_HEADS = 8
IDX_DIM = 32
TOPK_MAX = 256
N_HEADS_M = 4
HEAD_DIM_M = 128
D_M = N_HEADS_M * HEAD_DIM_M
CONV_K = 4
N_GROUPS = 4
EXPERTS_PER_GROUP = 8
N_EXPERTS = N_GROUPS * EXPERTS_PER_GROUP
D_EXPERT = 256
LN_EPS = 1e-5
NEG_BIG = -1e30
IN_SPLITS = (D_A, D_A, D_A, N_IDX_HEADS * IDX_DIM, IDX_DIM, N_IDX_HEADS, D_M, D_M, D_M, D_M, N_HEADS_M, N_HEADS_M)

LANES = 128
SUBLANES = 8
MIB = 1024 * 1024

C_QK = 0
C_QIDX = C_QK + 2 * D_A
C_K4 = C_QIDX + N_IDX_HEADS * IDX_DIM
C_GATE = C_K4 + LANES
C_QKM = C_GATE + LANES
C_VM = C_QKM + 2 * D_M
C_OM = C_VM + D_M
C_END = C_OM + D_M
SLABS_PER_TENSOR = D_A // LANES
HEADS_PER_SLAB = LANES // HEAD_DIM_A
G_W = 0
G_I = N_IDX_HEADS
G_F = G_I + N_HEADS_M

INT_MIN = -2 ** 31

TQ_DSA = LANES
TK_DSA = 1024
TM_IN_PROJ = TK_DSA
CHUNK_MLSTM = 256
TM_MIX = 512
TM_MOE = 1024

NT_DIMS = (((1,), (1,)), ((), ()))


def _cparams(semantics, vmem_mib):
    return pltpu.CompilerParams(dimension_semantics=semantics, vmem_limit_bytes=int(vmem_mib * MIB))


def _layer_norm(r, g, b):
    mu = jnp.mean(r, axis=-1, keepdims=True)
    d = r - mu
    var = jnp.mean(d * d, axis=-1, keepdims=True)
    return d * lax.rsqrt(var + LN_EPS) * g + b


def _pack_w_in(w_in):
    d = w_in.shape[0]
    cuts = [int(c) for c in np.cumsum(IN_SPLITS)[:-1]]
    q_a, k_a, v_a, q_idx, k_idx, w_idx, q_m, k_m, v_m, o_m, i_m, f_m = jnp.split(w_in, cuts, axis=1)
    k4 = jnp.tile(k_idx, (1, LANES // IDX_DIM))
    gate = jnp.concatenate([w_idx, i_m, f_m, jnp.zeros((d, LANES - G_F - N_HEADS_M), w_in.dtype)], axis=1)
    packed = jnp.concatenate([q_a, k_a, q_idx, k4, gate, q_m, k_m, v_m, o_m], axis=1).astype(BF16)
    return packed, v_a.T.astype(BF16)


def _in_proj_body(x_ref, w_ref, wvt_ref, qk_ref, vt_ref, qi_ref, k4_ref, g_ref, qkm_ref, vm_ref, om_ref):
    xb = x_ref[...].astype(BF16)

    def proj(lo, hi):
        return jnp.dot(xb, w_ref[:, lo:hi], preferred_element_type=F32)

    qk = proj(C_QK, C_QIDX).astype(BF16)
    for c in range(2 * SLABS_PER_TENSOR):
        qk_ref[c] = qk[:, c * LANES:(c + 1) * LANES]
    v_t = lax.dot_general(wvt_ref[...], xb, NT_DIMS, preferred_element_type=F32).astype(BF16)
    for c in range(SLABS_PER_TENSOR):
        vt_ref[c, 0] = v_t[c * LANES:(c + 1) * LANES, :]
    qi_ref[...] = proj(C_QIDX, C_K4).astype(BF16)
    k4_ref[...] = proj(C_K4, C_GATE).astype(BF16)
    g_ref[...] = proj(C_GATE, C_QKM)
    qkm_ref[...] = proj(C_QKM, C_VM)
    vm_ref[...] = proj(C_VM, C_OM).astype(BF16)
    om_ref[...] = proj(C_OM, C_END)


def _in_proj(x2d, w_packed, w_vt, tm):
    n, d = x2d.shape
    widths = [(C_K4 - C_QIDX, BF16), (C_GATE - C_K4, BF16), (C_QKM - C_GATE, F32),
              (C_VM - C_QKM, F32), (C_OM - C_VM, BF16), (C_END - C_OM, F32)]
    return pl.pallas_call(
        _in_proj_body,
        grid=(n // tm,),
        in_specs=[pl.BlockSpec((tm, d), lambda i: (i, 0)),
                  pl.BlockSpec((d, C_END), lambda i: (0, 0)),
                  pl.BlockSpec((D_A, d), lambda i: (0, 0))],
        out_specs=([pl.BlockSpec((2 * SLABS_PER_TENSOR, tm, LANES), lambda i: (0, i, 0)),
                    pl.BlockSpec((SLABS_PER_TENSOR, 1, LANES, tm), lambda i: (0, i, 0, 0))]
                   + [pl.BlockSpec((tm, w), lambda i: (i, 0)) for w, _ in widths]),
        out_shape=([jax.ShapeDtypeStruct((2 * SLABS_PER_TENSOR, n, LANES), BF16),
                    jax.ShapeDtypeStruct((SLABS_PER_TENSOR, n // tm, LANES, tm), BF16)]
                   + [jax.ShapeDtypeStruct((n, w), dt) for w, dt in widths]),
        compiler_params=_cparams(("arbitrary",), 48),
        name="in_proj",
    )(x2d, w_packed, w_vt)


def _float_to_ordered_int(value):
    bits = lax.bitcast_convert_type(value, I32)
    return bits ^ ((bits >> 31) & jnp.int32(0x7FFFFFFF))


def _ordered_int_to_float(key):
    bits = key ^ ((key >> 31) & jnp.int32(0x7FFFFFFF))
    return lax.bitcast_convert_type(bits, F32)


SEARCH_FREE_PASSES = 16
SEARCH_MAX_PASSES = SEARCH_FREE_PASSES + 33
ZERO_TIE_MARGIN = 128
CHAINS = 4
POS_RADIX = 64


def _dsa_body(qa_ref, ka_ref, vt_ref, qi_ref, k4_ref, g_ref, o_ref,
              sc_ref, qm_ref, qh_ref, mask_ref, acc_ref, m_ref, l_ref, *, seq, topk, tq, tk):
    t0 = pl.program_id(1) * tq
    n_kb = (t0 + tq + tk - 1) // tk
    groups = tk // (CHAINS * SUBLANES)
    shape4 = (groups, CHAINS, SUBLANES, tq)

    def chunks(x):
        return x.reshape(shape4)

    def fold(x):
        out = x[0]
        for c in range(1, CHAINS):
            out = out + x[c]
        return out

    def rows8(row):
        return jnp.broadcast_to(row, (SUBLANES, tq))

    key_off = ((lax.broadcasted_iota(I32, shape4, 0) * CHAINS + lax.broadcasted_iota(I32, shape4, 1)) * SUBLANES
               + lax.broadcasted_iota(I32, shape4, 2))
    q_pos = t0 + lax.broadcasted_iota(I32, shape4, 3)
    lane = lax.broadcasted_iota(I32, (tq, LANES), 1)

    for h in range(N_IDX_HEADS):
        per_slab = LANES // IDX_DIM
        slab = qi_ref[:, (h // per_slab) * LANES:(h // per_slab + 1) * LANES]
        lo = (h % per_slab) * IDX_DIM
        qm_ref[h * tq:(h + 1) * tq, :] = jnp.where((lane >= lo) & (lane < lo + IDX_DIM), slab, jnp.zeros_like(slab))
    for h in range(N_HEADS_A):
        p, hh = divmod(h, HEADS_PER_SLAB)
        slope = 2.0 ** (-8.0 * (h + 1) / N_HEADS_A)
        slab = qa_ref[p].astype(F32) * (HEAD_DIM_A ** -0.5)
        in_head = (lane >= hh * HEAD_DIM_A) & (lane < (hh + 1) * HEAD_DIM_A)
        qh_ref[p, hh * tq:(hh + 1) * tq, 0:LANES] = jnp.where(in_head, slab, 0.0).astype(BF16)
        feat = jnp.where(lane == 0, slope * POS_RADIX, jnp.where(lane == 1, slope, 0.0))
        qh_ref[p, hh * tq:(hh + 1) * tq, LANES:2 * LANES] = feat.astype(BF16)
    w_rows = g_ref[...].T[G_W:G_W + N_IDX_HEADS, :] * ((IDX_DIM * N_IDX_HEADS) ** -0.5)

    def score_block(j, carry):
        hi_part, lo_part, zero_part = carry
        k0 = pl.multiple_of(j * tk, tk)
        logits = lax.dot_general(k4_ref[pl.ds(k0, tk), :], qm_ref[...], NT_DIMS,
                                 preferred_element_type=F32)
        score = jnp.zeros(shape4, F32)
        for h in range(N_IDX_HEADS):
            score = score + rows8(w_rows[h:h + 1, :]) * jnp.maximum(chunks(logits[:, h * tq:(h + 1) * tq]), 0.0)
        causal = k0 + key_off <= q_pos
        kept = jnp.where(causal, score, NEG_BIG)
        sc_ref[j] = kept
        hi_part = jnp.maximum(hi_part, jnp.max(kept, axis=0))
        lo_part = jnp.minimum(lo_part, jnp.min(jnp.where(causal, score, jnp.inf), axis=0))
        zero_part = zero_part + jnp.sum(jnp.where(kept >= 0.0, 1.0, 0.0), axis=0)
        return hi_part, lo_part, zero_part

    part_shape = (CHAINS, SUBLANES, tq)
    hi_part, lo_part, zero_part = lax.fori_loop(
        0, n_kb, score_block,
        (jnp.full(part_shape, -jnp.inf, F32), jnp.full(part_shape, jnp.inf, F32), jnp.zeros(part_shape, F32)))
    row_max = jnp.max(jnp.max(hi_part, axis=0), axis=0, keepdims=True)
    row_min = jnp.min(jnp.min(lo_part, axis=0), axis=0, keepdims=True)

    n_tail = (seq - n_kb * tk).astype(F32)

    def count(thr, strict):
        thr8 = rows8(thr)

        def body(j, acc):
            s = sc_ref[j]
            return acc + jnp.sum(jnp.where((s > thr8) if strict else (s >= thr8), 1.0, 0.0), axis=0)

        acc = lax.fori_loop(0, n_kb, body, jnp.zeros(part_shape, F32))
        tail_hit = (NEG_BIG > thr) if strict else (NEG_BIG >= thr)
        return jnp.sum(fold(acc), axis=0, keepdims=True) + jnp.where(tail_hit, n_tail, 0.0)

    n_causal = (t0 + 1 + lax.broadcasted_iota(I32, (1, tq), 1)).astype(F32)
    few = n_causal < topk
    lo0 = jnp.where(few, _float_to_ordered_int(jnp.full((1, tq), NEG_BIG, F32)), _float_to_ordered_int(row_min))
    cnt_lo0 = jnp.where(few, float(seq), n_causal)
    hi0 = _float_to_ordered_int(row_max) + 1
    log_k = float(np.log(topk))

    def log_excess(cnt):
        return jnp.log(jnp.maximum(cnt, 0.5)) - log_k

    def search_step(state):
        it, _, lo, hi, cnt_lo, f_lo, f_hi, last, done_i = state
        done = done_i != 0
        it_v = jnp.zeros((1, tq), I32) + it
        lo_v = _ordered_int_to_float(lo)
        hi_v = _ordered_int_to_float(hi)
        guess = _float_to_ordered_int(lo_v + (f_lo / (f_lo - f_hi)) * (hi_v - lo_v))
        guess = jnp.where((it_v == 1) & (lo == 0) & (cnt_lo < topk + ZERO_TIE_MARGIN), 1, guess)
        middle = lo + lax.shift_right_logical(hi - lo, jnp.ones((1, tq), I32))
        usable = (guess > lo) & (guess < hi) & (it_v < SEARCH_FREE_PASSES)
        cand = jnp.where(usable, guess, middle)
        cnt = count(_ordered_int_to_float(cand), strict=False)
        f_c = log_excess(cnt)
        take = (cnt >= topk) & ~done
        drop = (cnt < topk) & ~done
        f_hi = jnp.where(take, jnp.where(last == 1, 0.5 * f_hi, f_hi), jnp.where(drop, f_c, f_hi))
        f_lo = jnp.where(drop, jnp.where(last == -1, 0.5 * f_lo, f_lo), jnp.where(take, f_c, f_lo))
        last = jnp.where(take, 1, jnp.where(drop, -1, last))
        lo = jnp.where(take, cand, lo)
        cnt_lo = jnp.where(take, cnt, cnt_lo)
        hi = jnp.where(drop, cand, hi)
        done = done | (cnt_lo == topk) | ((hi - lo) == 1)
        active = jnp.sum(jnp.where(done, 0.0, 1.0))
        return it + 1, active, lo, hi, cnt_lo, f_lo, f_hi, last, done.astype(I32)

    def search_on(state):
        return (state[0] < SEARCH_MAX_PASSES) & (state[1] > 0.0)

    cnt_zero = jnp.sum(fold(zero_part), axis=0, keepdims=True)
    inside = (lo0 < 0) & (hi0 > 0) & ~few
    take0 = inside & (cnt_zero >= topk)
    drop0 = inside & (cnt_zero < topk)
    lo1 = jnp.where(take0, 0, lo0)
    hi1 = jnp.where(drop0, 0, hi0)
    cnt_lo1 = jnp.where(take0, cnt_zero, cnt_lo0)
    done1 = few | (cnt_lo1 == topk) | ((hi1 - lo1) == 1)
    state = lax.while_loop(search_on, search_step,
                           (jnp.int32(1), jnp.float32(1.0), lo1, hi1, cnt_lo1, log_excess(cnt_lo1),
                            log_excess(jnp.where(drop0, cnt_zero, 0.0)),
                            jnp.where(take0, 1, jnp.where(drop0, -1, 0)), done1.astype(I32)))
    thr = _ordered_int_to_float(state[2])
    cnt_thr = state[4]
    thr8 = rows8(thr)

    @pl.when(jnp.max(cnt_thr) > topk)
    def _():
        need = topk - count(thr, strict=True)
        tri_r = lax.broadcasted_iota(I32, (tk, tk), 0)
        tri_c = lax.broadcasted_iota(I32, (tk, tk), 1)
        earlier = (tri_c < tri_r).astype(BF16)

        def drop_late_ties(j, seen):
            s = sc_ref[j]
            eq = s == thr8
            eq_f = jnp.where(eq, 1.0, 0.0)
            rank = seen + jnp.dot(earlier, eq_f.reshape(tk, tq).astype(BF16), preferred_element_type=F32)
            sc_ref[j] = jnp.where(eq & (chunks(rank) >= rows8(need)), -jnp.inf, s)
            return seen + jnp.sum(fold(jnp.sum(eq_f, axis=0)), axis=0, keepdims=True)

        lax.fori_loop(0, n_kb, drop_late_ties, jnp.zeros((1, tq), F32))

    m_ref[...] = jnp.full(m_ref.shape, NEG_BIG, F32)
    l_ref[...] = jnp.zeros(l_ref.shape, F32)
    acc_ref[...] = jnp.zeros(acc_ref.shape, F32)
    pos_lane = lax.broadcasted_iota(I32, (tk, LANES), 1)
    pos_row = lax.broadcasted_iota(I32, (tk, LANES), 0)
    feat0 = jnp.where(pos_lane == 0, pos_row // POS_RADIX, jnp.where(pos_lane == 1, pos_row % POS_RADIX, 0)).astype(F32)
    hi_lane = jnp.where(pos_lane == 0, 1.0, 0.0)

    def attend(j, carry):
        k0 = pl.multiple_of(j * tk, tk)
        mask_ref[...] = jnp.where((sc_ref[j] >= thr8) & (k0 + key_off <= q_pos), 0.0, -jnp.inf)
        pos_feat = (feat0 + hi_lane * (j * (tk // POS_RADIX)).astype(F32)).astype(BF16)

        scores = []
        for p in range(SLABS_PER_TENSOR):
            keys = jnp.concatenate([ka_ref[p, pl.ds(k0, tk), :], pos_feat], axis=1)
            scores.append(lax.dot_general(keys, qh_ref[p], NT_DIMS, preferred_element_type=F32))
        weights = []
        for p in range(SLABS_PER_TENSOR):
            for hh in range(HEADS_PER_SLAB):
                s = chunks(scores[p][:, hh * tq:(hh + 1) * tq]) + mask_ref[...]
                m_old = m_ref[p, hh]
                m_blk = jnp.max(fold_max(jnp.max(s, axis=0)), axis=0, keepdims=True)
                m_new = jnp.maximum(m_old, rows8(m_blk))
                alpha = jnp.exp(m_old - m_new)
                m_ref[p, hh] = m_new
                weights.append((alpha, jnp.exp(s - m_new).reshape(tk, tq).astype(BF16)))
        ones_rows = jnp.ones((SUBLANES, tk), BF16)
        for p in range(SLABS_PER_TENSOR):
            for hh in range(HEADS_PER_SLAB):
                rows = slice(hh * HEAD_DIM_A, (hh + 1) * HEAD_DIM_A)
                alpha, probs = weights[p * HEADS_PER_SLAB + hh]
                values = jnp.concatenate([vt_ref[p, j, rows, :], ones_rows], axis=0)
                pv = jnp.dot(values, probs, preferred_element_type=F32)
                old = acc_ref[p, rows, :].reshape(HEAD_DIM_A // SUBLANES, SUBLANES, tq)
                acc_ref[p, rows, :] = (alpha * old).reshape(HEAD_DIM_A, tq) + pv[0:HEAD_DIM_A]
                l_ref[p, hh] = alpha * l_ref[p, hh] + pv[HEAD_DIM_A:HEAD_DIM_A + SUBLANES]
        return carry

    def fold_max(x):
        out = x[0]
        for c in range(1, CHAINS):
            out = jnp.maximum(out, x[c])
        return out

    lax.fori_loop(0, n_kb, attend, 0)

    for p in range(SLABS_PER_TENSOR):
        halves = []
        for hh in range(HEADS_PER_SLAB):
            rows = slice(hh * HEAD_DIM_A, (hh + 1) * HEAD_DIM_A)
            acc = acc_ref[p, rows, :].reshape(HEAD_DIM_A // SUBLANES, SUBLANES, tq)
            halves.append((acc / l_ref[p, hh]).reshape(HEAD_DIM_A, tq))
        o_ref[:, p * LANES:(p + 1) * LANES] = jnp.concatenate(halves, axis=0).T.astype(BF16)


def _dsa_attention(qk, v_t, q_idx, k4, gates, tq, tk):
    _, b, s, _ = qk.shape
    assert tq == LANES and s // POS_RADIX <= 256
    topk = min(TOPK_MAX, s // 4)
    once = pl.Buffered(1)
    groups = tk // (CHAINS * SUBLANES)
    body = functools.partial(_dsa_body, seq=s, topk=topk, tq=tq, tk=tk)
    return pl.pallas_call(
        body,
        grid=(b, s // tq),
        in_specs=[pl.BlockSpec((SLABS_PER_TENSOR, None, tq, LANES), lambda bi, qi: (0, bi, qi, 0)),
                  pl.BlockSpec((SLABS_PER_TENSOR, None, s, LANES), lambda bi, qi: (1, bi, 0, 0), pipeline_mode=once),
                  pl.BlockSpec((SLABS_PER_TENSOR, None, s // tk, LANES, tk), lambda bi, qi: (0, bi, 0, 0, 0),
                               pipeline_mode=once),
                  pl.BlockSpec((None, tq, N_IDX_HEADS * IDX_DIM), lambda bi, qi: (bi, qi, 0)),
                  pl.BlockSpec((None, s, LANES), lambda bi, qi: (bi, 0, 0), pipeline_mode=once),
                  pl.BlockSpec((None, tq, LANES), lambda bi, qi: (bi, qi, 0))],
        out_specs=pl.BlockSpec((None, tq, D_A), lambda bi, qi: (bi, qi, 0)),
        out_shape=jax.ShapeDtypeStruct((b, s, D_A), BF16),
        scratch_shapes=[pltpu.VMEM((s // tk, groups, CHAINS, SUBLANES, tq), F32),
                        pltpu.VMEM((N_IDX_HEADS * tq, LANES), BF16),
                        pltpu.VMEM((SLABS_PER_TENSOR, HEADS_PER_SLAB * tq, 2 * LANES), BF16),
                        pltpu.VMEM((groups, CHAINS, SUBLANES, tq), F32),
                        pltpu.VMEM((SLABS_PER_TENSOR, LANES, tq), F32),
                        pltpu.VMEM((SLABS_PER_TENSOR, HEADS_PER_SLAB, SUBLANES, tq), F32),
                        pltpu.VMEM((SLABS_PER_TENSOR, HEADS_PER_SLAB, SUBLANES, tq), F32)],
        compiler_params=_cparams(("arbitrary", "arbitrary"), 48),
        name="dsa_attention",
    )(qk, qk, v_t, q_idx, k4, gates)


def _log_sigmoid(x):
    return jnp.minimum(x, 0.0) - jnp.log(1.0 + jnp.exp(-jnp.abs(x)))


def _split3(a):
    hi = a.astype(BF16)
    r1 = a - hi.astype(F32)
    mid = r1.astype(BF16)
    lo = (r1 - mid.astype(F32)).astype(BF16)
    return hi, mid, lo


def _mlstm_body(qk_ref, v_ref, o_ref, g_ref, gt_ref, cw_ref, cb_ref, bg_ref, bgt_ref, gn_ref, y_ref,
                xe_ref, c_ref, m_ref, *, chunk, batch):
    @pl.when(pl.program_id(0) == 0)
    def _():
        xe_ref[:, 0:SUBLANES, :] = jnp.zeros((batch, SUBLANES, 2 * D_M), F32)
        c_ref[...] = jnp.zeros(c_ref.shape, F32)
        m_ref[...] = jnp.zeros(m_ref.shape, F32)

    gates = [_mlstm_gates(qk_ref.at[bi], g_ref.at[bi], gt_ref.at[bi], cw_ref, cb_ref, bg_ref, bgt_ref,
                          xe_ref.at[bi], chunk) for bi in range(batch)]
    for h in range(N_HEADS_M):
        for bi in range(batch):
            _mlstm_head(h, gates[bi], v_ref.at[bi], o_ref.at[bi], gn_ref, y_ref.at[bi], c_ref.at[bi], m_ref.at[bi],
                        chunk)


def _mlstm_gates(qk_ref, g_ref, gt_ref, cw_ref, cb_ref, bg_ref, bgt_ref, xe_ref, chunk):
    xe_ref[SUBLANES:SUBLANES + chunk, :] = qk_ref[...]
    conv = cb_ref[...]
    for j in range(CONV_K):
        start = SUBLANES - (CONV_K - 1) + j
        conv = conv + cw_ref[j:j + 1, :] * xe_ref[start:start + chunk, :]
    qk = conv * jax.nn.sigmoid(conv)
    xe_ref[0:SUBLANES, :] = xe_ref[chunk:chunk + SUBLANES, :]

    gb = g_ref[...] + bg_ref[...]
    gbt = gt_ref[...] + bgt_ref[...]
    r = lax.broadcasted_iota(I32, (chunk, chunk), 0)
    c = lax.broadcasted_iota(I32, (chunk, chunk), 1)
    causal = r >= c
    incl = causal.astype(BF16)
    incl_t = (r <= c).astype(BF16)
    cum = jnp.zeros((chunk, LANES), F32)
    for term in _split3(_log_sigmoid(gb)):
        cum = cum + jnp.dot(incl, term, preferred_element_type=F32)
    cum_t = jnp.zeros((2 * N_HEADS_M, chunk), F32)
    for term in _split3(_log_sigmoid(gbt)):
        cum_t = cum_t + jnp.dot(term, incl_t, preferred_element_type=F32)

    return qk, gb, gbt, cum, cum_t, causal


def _mlstm_head(h, gates, v_ref, o_ref, gn_ref, y_ref, c_ref, m_ref, chunk):
    qk, gb, gbt, cum, cum_t, causal = gates
    lane = lax.broadcasted_iota(I32, (chunk, LANES), 1)
    ones_col = jnp.where(lane == 0, 1.0, 0.0).astype(BF16)
    sl = slice(h * HEAD_DIM_M, (h + 1) * HEAD_DIM_M)
    q_h = (qk[:, sl] * (HEAD_DIM_M ** -0.5)).astype(BF16)
    k_f = qk[:, D_M + h * HEAD_DIM_M:D_M + (h + 1) * HEAD_DIM_M]
    v_aug = jnp.concatenate([v_ref[:, sl], ones_col], axis=1)
    b_col = cum[:, G_F + h:G_F + h + 1]
    i_col = gb[:, G_I + h:G_I + h + 1]
    b_row = cum_t[N_HEADS_M + h:N_HEADS_M + h + 1, :]
    i_row = gbt[h:h + 1, :]
    m_prev = m_ref[h:h + 1, 0:1]
    c_aug = c_ref[h]

    d_log = jnp.where(causal, b_col - b_row + i_row, -jnp.inf)
    inter = b_col + m_prev
    m_t = jnp.maximum(inter, jnp.max(d_log, axis=1, keepdims=True))
    qk_t = lax.dot_general(q_h, k_f.astype(BF16), NT_DIMS, preferred_element_type=F32)
    w_intra = qk_t * jnp.exp(d_log - m_t)
    w_inter = jnp.exp(inter - m_t)
    intra = jnp.dot(w_intra.astype(BF16), v_aug[:, 0:HEAD_DIM_M], preferred_element_type=F32)
    carried = jnp.dot(q_h, c_aug.astype(BF16), preferred_element_type=F32)
    num = intra + w_inter * carried[:, 0:HEAD_DIM_M]
    den = jnp.sum(w_intra, axis=1, keepdims=True) + w_inter * carried[:, HEAD_DIM_M:HEAD_DIM_M + 1]
    hid = num / jnp.maximum(jnp.abs(den), jnp.exp(-m_t))

    mu = jnp.mean(hid, axis=1, keepdims=True)
    dev = hid - mu
    var = jnp.mean(dev * dev, axis=1, keepdims=True)
    normed = dev * lax.rsqrt(var + LN_EPS) * gn_ref[:, sl]
    y_ref[:, sl] = (jax.nn.sigmoid(o_ref[:, sl]) * normed).astype(BF16)

    b_last = b_col[chunk - 1:chunk, :]
    w_log = b_last - b_col + i_col
    m_new = jnp.maximum(b_last + m_prev, jnp.max(w_log, axis=0, keepdims=True))
    decay = jnp.exp(b_last + m_prev - m_new)
    kw_t = (k_f * jnp.exp(w_log - m_new)).T.astype(BF16)
    c_ref[h] = decay * c_aug + jnp.dot(kw_t, v_aug, preferred_element_type=F32)
    m_ref[h:h + 1, :] = jnp.broadcast_to(m_new, (1, LANES))


def _mlstm(qk_m, v_m, o_m, gates, gates_t, conv_w, conv_b, bias_slab, bias_col, gn_w, chunk):
    b, s, _ = qk_m.shape
    body = functools.partial(_mlstm_body, chunk=chunk, batch=b)
    const = lambda ci: (0, 0)
    return pl.pallas_call(
        body,
        grid=(s // chunk,),
        in_specs=[pl.BlockSpec((b, chunk, 2 * D_M), lambda ci: (0, ci, 0)),
                  pl.BlockSpec((b, chunk, D_M), lambda ci: (0, ci, 0)),
                  pl.BlockSpec((b, chunk, D_M), lambda ci: (0, ci, 0)),
                  pl.BlockSpec((b, chunk, LANES), lambda ci: (0, ci, 0)),
                  pl.BlockSpec((b, 2 * N_HEADS_M, chunk), lambda ci: (0, 0, ci)),
                  pl.BlockSpec((CONV_K, 2 * D_M), const),
                  pl.BlockSpec((1, 2 * D_M), const),
                  pl.BlockSpec((1, LANES), const),
                  pl.BlockSpec((2 * N_HEADS_M, 1), const),
                  pl.BlockSpec((1, D_M), const)],
        out_specs=pl.BlockSpec((b, chunk, D_M), lambda ci: (0, ci, 0)),
        out_shape=jax.ShapeDtypeStruct((b, s, D_M), BF16),
        scratch_shapes=[pltpu.VMEM((b, chunk + SUBLANES, 2 * D_M), F32),
                        pltpu.VMEM((b, N_HEADS_M, HEAD_DIM_M, 2 * LANES), F32),
                        pltpu.VMEM((b, SUBLANES, LANES), F32)],
        compiler_params=_cparams(("arbitrary",), 32),
        name="mlstm",
    )(qk_m, v_m, o_m, gates, gates_t, conv_w, conv_b, bias_slab, bias_col, gn_w)


COMB_GROUP_LANE = N_EXPERTS


def _route(logits):
    lane = lax.broadcasted_iota(I32, logits.shape, 1).astype(F32)
    g_mask = (lane >= N_EXPERTS) & (lane < N_EXPERTS + N_GROUPS)
    g_logit = jnp.where(g_mask, logits, -jnp.inf)
    g_exp = jnp.exp(g_logit - jnp.max(g_logit, axis=1, keepdims=True))
    g_prob = g_exp / jnp.sum(g_exp, axis=1, keepdims=True)
    g_p = jnp.max(g_prob, axis=1, keepdims=True)
    g_sel = jnp.min(jnp.where(g_mask & (g_prob == g_p), lane, float(LANES)), axis=1, keepdims=True) - N_EXPERTS
    e_mask = (lane >= g_sel * EXPERTS_PER_GROUP) & (lane < (g_sel + 1.0) * EXPERTS_PER_GROUP)
    e_logit = jnp.where(e_mask, logits, -jnp.inf)
    e_exp = jnp.exp(e_logit - jnp.max(e_logit, axis=1, keepdims=True))
    e_prob = e_exp / jnp.sum(e_exp, axis=1, keepdims=True)
    p1 = jnp.max(jnp.where(e_mask, e_prob, -1.0), axis=1, keepdims=True)
    i1 = jnp.min(jnp.where(e_mask & (e_prob == p1), lane, float(LANES)), axis=1, keepdims=True)
    rest = e_mask & (lane != i1)
    p2 = jnp.max(jnp.where(rest, e_prob, -1.0), axis=1, keepdims=True)
    i2 = jnp.min(jnp.where(rest & (e_prob == p2), lane, float(LANES)), axis=1, keepdims=True)
    total = p1 + p2
    comb = (jnp.where(lane == i1, p1 / total, 0.0) + jnp.where(lane == i2, p2 / total, 0.0)) * g_p
    return comb + jnp.where(lane == COMB_GROUP_LANE, g_sel, 0.0)


def _mix_body(ya_ref, ym_ref, x_ref, wo_ref, g_ref, b_ref, wr_ref, br_ref, x1_ref, comb_ref, *, alpha):
    mix = (jnp.dot(ya_ref[...], wo_ref[0:D_A, :], preferred_element_type=F32)
           + jnp.dot(ym_ref[...], wo_ref[D_A:D_A + D_M, :], preferred_element_type=F32))
    x1 = _layer_norm(alpha * x_ref[...] + mix, g_ref[...], b_ref[...])
    x1_ref[...] = x1
    x_hi = x1.astype(BF16)
    x_lo = (x1 - x_hi.astype(F32)).astype(BF16)
    w_hi = wr_ref[0]
    w_lo = wr_ref[1]
    logits = (jnp.dot(x_hi, w_hi, preferred_element_type=F32) + jnp.dot(x_lo, w_hi, preferred_element_type=F32)
              + jnp.dot(x_hi, w_lo, preferred_element_type=F32)) + br_ref[...]
    comb_ref[...] = _route(logits)


def _mix_ln_router(y_a, y_m, x2d, w_out, ln_g, ln_b, w_router, b_router, alpha, tm):
    n, d = x2d.shape
    const = lambda i: (0, 0)
    return pl.pallas_call(
        functools.partial(_mix_body, alpha=alpha),
        grid=(n // tm,),
        in_specs=[pl.BlockSpec((tm, D_A), lambda i: (i, 0)),
                  pl.BlockSpec((tm, D_M), lambda i: (i, 0)),
                  pl.BlockSpec((tm, d), lambda i: (i, 0)),
                  pl.BlockSpec((D_A + D_M, d), const),
                  pl.BlockSpec((1, d), const),
                  pl.BlockSpec((1, d), const),
                  pl.BlockSpec((2, d, LANES), lambda i: (0, 0, 0)),
                  pl.BlockSpec((1, LANES), const)],
        out_specs=[pl.BlockSpec((tm, d), lambda i: (i, 0)),
                   pl.BlockSpec((tm, LANES), lambda i: (i, 0))],
        out_shape=[jax.ShapeDtypeStruct((n, d), F32), jax.ShapeDtypeStruct((n, LANES), F32)],
        compiler_params=_cparams(("arbitrary",), 32),
        name="mix_ln_router",
    )(y_a, y_m, x2d, w_out, ln_g, ln_b, w_router, b_router)


MOE_ROWS = TM_MOE // N_GROUPS + 32


def _moe_body(x1_ref, comb_ref, wg_ref, wu_ref, wd_ref, g_ref, b_ref, p_ref, wpg_ref, wpp_ref, y_ref,
              xb_ref, before_ref, cs_ref, row_ref, *, alpha, tm):
    g = pl.program_id(1)
    g_f = g.astype(F32)

    @pl.when((pl.program_id(0) == 0) & (g == 0))
    def _():
        r = lax.broadcasted_iota(I32, (tm, tm), 0)
        c = lax.broadcasted_iota(I32, (tm, tm), 1)
        before_ref[...] = (c < r).astype(BF16)

    @pl.when(g == 0)
    def _():
        xb_ref[...] = x1_ref[...].astype(BF16)
        y_ref[...] = jnp.zeros(y_ref.shape, F32)
        comb = comb_ref[...]
        for k, term in enumerate(_split3(comb)):
            cs_ref[k] = term
        group_row = comb.T[COMB_GROUP_LANE:COMB_GROUP_LANE + 1, :]
        sub = lax.broadcasted_iota(I32, (SUBLANES, tm), 0)
        member_t = (sub.astype(F32) == group_row) & (sub < N_GROUPS)
        ahead_t = lax.dot_general(jnp.where(member_t, 1.0, 0.0).astype(BF16), before_ref[...], NT_DIMS,
                                  preferred_element_type=F32)
        rank_row = jnp.sum(jnp.where(member_t, ahead_t, 0.0), axis=0, keepdims=True)
        row_ref[0] = jnp.broadcast_to(group_row, (SUBLANES, tm))
        row_ref[1] = jnp.broadcast_to(rank_row, (SUBLANES, tm))

    in_group_row = row_ref[0, 0:1, :] == g_f
    n_tokens = jnp.sum(jnp.where(in_group_row, 1.0, 0.0))
    n_blocks = (n_tokens.astype(I32) + MOE_ROWS - 1) // MOE_ROWS
    slot = lax.broadcasted_iota(I32, (MOE_ROWS, tm), 0).astype(F32)
    comb_lane = lax.broadcasted_iota(I32, (MOE_ROWS, LANES), 1)

    def block(b, carry):
        base = (b * MOE_ROWS).astype(F32)
        gather = jnp.where(in_group_row & (row_ref[1, 0:1, :] - base == slot), 1.0, 0.0).astype(BF16)
        xc = jnp.dot(gather, xb_ref[...], preferred_element_type=F32).astype(BF16)
        comb_c = jnp.zeros((MOE_ROWS, LANES), F32)
        for k in range(3):
            comb_c = comb_c + jnp.dot(gather, cs_ref[k], preferred_element_type=F32)
        yc = jnp.zeros((MOE_ROWS, x1_ref.shape[1]), F32)
        for e in range(EXPERTS_PER_GROUP):
            gate = jnp.dot(xc, wg_ref[e], preferred_element_type=F32)
            up = jnp.dot(xc, wu_ref[e], preferred_element_type=F32)
            weight = jnp.sum(jnp.where(comb_lane == g * EXPERTS_PER_GROUP + e, comb_c, 0.0), axis=1, keepdims=True)
            hidden = gate * jax.nn.sigmoid(gate) * up * weight
            yc = yc + jnp.dot(hidden.astype(BF16), wd_ref[e], preferred_element_type=F32)
        yc_hi = yc.astype(BF16)
        yc_lo = (yc - yc_hi.astype(F32)).astype(BF16)
        tn = (((0,), (0,)), ((), ()))
        y_ref[...] += (lax.dot_general(gather, yc_hi, tn, preferred_element_type=F32)
                       + lax.dot_general(gather, yc_lo, tn, preferred_element_type=F32))
        return carry

    lax.fori_loop(0, n_blocks, block, 0)

    @pl.when(g == N_GROUPS - 1)
    def _():
        x2 = _layer_norm(alpha * x1_ref[...] + y_ref[...], g_ref[...], b_ref[...])
        gate = jax.nn.sigmoid(jnp.dot(x2.astype(BF16), wpg_ref[...], preferred_element_type=F32))
        proj = jnp.dot(p_ref[...].astype(BF16), wpp_ref[...], preferred_element_type=F32)
        y_ref[...] = x2 + gate * proj


def _moe_ln_ple(x1, comb, w_gate, w_up, w_down, ln_g, ln_b, p2d, w_ple_gate, w_ple_proj, alpha, tm):
    n, d = x1.shape
    dp = p2d.shape[1]
    const = lambda i, g: (0, 0)
    once = pl.Buffered(1)
    grouped = lambda w: w.reshape((N_GROUPS, EXPERTS_PER_GROUP) + w.shape[1:])
    return pl.pallas_call(
        functools.partial(_moe_body, alpha=alpha, tm=tm),
        grid=(n // tm, N_GROUPS),
        in_specs=[pl.BlockSpec((tm, d), lambda i, g: (i, 0), pipeline_mode=once),
                  pl.BlockSpec((tm, LANES), lambda i, g: (i, 0)),
                  pl.BlockSpec((None, EXPERTS_PER_GROUP, d, D_EXPERT), lambda i, g: (g, 0, 0, 0)),
                  pl.BlockSpec((None, EXPERTS_PER_GROUP, d, D_EXPERT), lambda i, g: (g, 0, 0, 0)),
                  pl.BlockSpec((None, EXPERTS_PER_GROUP, D_EXPERT, d), lambda i, g: (g, 0, 0, 0)),
                  pl.BlockSpec((1, d), const),
                  pl.BlockSpec((1, d), const),
                  pl.BlockSpec((tm, dp), lambda i, g: (i, 0), pipeline_mode=once),
                  pl.BlockSpec((d, d), const, pipeline_mode=once),
                  pl.BlockSpec((dp, d), const, pipeline_mode=once)],
        out_specs=pl.BlockSpec((tm, d), lambda i, g: (i, 0)),
        out_shape=jax.ShapeDtypeStruct((n, d), F32),
        scratch_shapes=[pltpu.VMEM((tm, d), BF16),
                        pltpu.VMEM((tm, tm), BF16),
                        pltpu.VMEM((3, tm, LANES), BF16),
                        pltpu.VMEM((2, SUBLANES, tm), F32)],
        compiler_params=_cparams(("arbitrary", "arbitrary"), 58),
        name="moe_ln_ple",
    )(x1, comb, grouped(w_gate), grouped(w_up), grouped(w_down), ln_g, ln_b, p2d, w_ple_gate, w_ple_proj)


def _tile(n, want):
    t = min(want, n)
    assert n % t == 0
    return t


def _layer(x, p_i, w_in, b_igate, b_fgate, conv_w, conv_b, gn_w, w_out, ln1_g, ln1_b,
           rg_w, rg_b, re_w, re_b, e_gate, e_up, e_down, ln2_g, ln2_b, ple_gate_w, ple_proj_w, alpha):
    b, s, d = x.shape
    n = b * s
    x2d = x.reshape(n, d)

    tk = _tile(s, TK_DSA)
    w_packed, w_vt = _pack_w_in(w_in)
    qk, v_t, q_idx, k4, gates, qk_m, v_m, o_m = _in_proj(x2d, w_packed, w_vt, tk)

    gates3 = gates.reshape(b, s, LANES)
    y_a = _dsa_attention(qk.reshape(2 * SLABS_PER_TENSOR, b, s, LANES),
                         v_t.reshape(SLABS_PER_TENSOR, b, s // tk, LANES, tk),
                         q_idx.reshape(b, s, N_IDX_HEADS * IDX_DIM), k4.reshape(b, s, LANES), gates3,
                         _tile(s, TQ_DSA), tk)

    gate_bias = jnp.concatenate([b_igate, b_fgate]).astype(F32)
    bias_slab = jnp.zeros((1, LANES), F32).at[0, G_I:G_I + 2 * N_HEADS_M].set(gate_bias)
    gates_t = jnp.swapaxes(gates3[:, :, G_I:G_I + 2 * N_HEADS_M], 1, 2)
    y_m = _mlstm(qk_m.reshape(b, s, 2 * D_M), v_m.reshape(b, s, D_M), o_m.reshape(b, s, D_M), gates3, gates_t,
                 conv_w, conv_b.reshape(1, -1), bias_slab, gate_bias.reshape(-1, 1), gn_w.reshape(1, -1),
                 _tile(s, CHUNK_MLSTM))

    w_router = jnp.concatenate([re_w, rg_w, jnp.zeros((d, LANES - N_EXPERTS - N_GROUPS), F32)], axis=1)
    w_router_hi = w_router.astype(BF16)
    w_router = jnp.stack([w_router_hi, (w_router - w_router_hi.astype(F32)).astype(BF16)])
    b_router = jnp.concatenate([re_b, rg_b, jnp.zeros((LANES - N_EXPERTS - N_GROUPS,), F32)]).reshape(1, LANES)
    x1, comb = _mix_ln_router(y_a.reshape(n, D_A), y_m.reshape(n, D_M), x2d, w_out.astype(BF16),
                              ln1_g.reshape(1, d), ln1_b.reshape(1, d), w_router, b_router, alpha, _tile(n, TM_MIX))

    out = _moe_ln_ple(x1, comb, e_gate.astype(BF16), e_up.astype(BF16), e_down.astype(BF16),
                      ln2_g.reshape(1, d), ln2_b.reshape(1, d), p_i.reshape(n, -1), ple_gate_w.astype(BF16),
                      ple_proj_w.astype(BF16), alpha, _tile(n, TM_MOE))
    return out.reshape(b, s, d)


def kernel(x, p, w_in, b_igate, b_fgate, conv_w, conv_b, gn_w, w_out, ln1_g, ln1_b, router_group_w, router_group_b,
           router_expert_w, router_expert_b, expert_w_gate, expert_w_up, expert_w_down, ln2_g, ln2_b, ple_gate_w,
           ple_proj_w):
    depth = w_in.shape[0]
    alpha = (2.0 * depth) ** 0.25
    for i in range(depth):
        x = _layer(x, p[i], w_in[i], b_igate[i], b_fgate[i], conv_w[i], conv_b[i], gn_w[i], w_out[i], ln1_g[i],
                   ln1_b[i], router_group_w[i], router_group_b[i], router_expert_w[i], router_expert_b[i],
                   expert_w_gate[i], expert_w_up[i], expert_w_down[i], ln2_g[i], ln2_b[i], ple_gate_w[i],
                   ple_proj_w[i], alpha)
    return x
```

```python
import functools

import numpy as np
import jax
import jax.numpy as jnp
from jax import lax
from jax.experimental import pallas as pl
from jax.experimental.pallas import tpu as pltpu

F32 = jnp.float32
BF16 = jnp.bfloat16
I32 = jnp.int32

N_HEADS_A = 8
HEAD_DIM_A = 64
D_A = N_HEADS_A * HEAD_DIM_A
N_IDX_HEADS = 8
IDX_DIM = 32
TOPK_MAX = 256
N_HEADS_M = 4
HEAD_DIM_M = 128
D_M = N_HEADS_M * HEAD_DIM_M
CONV_K = 4
N_GROUPS = 4
EXPERTS_PER_GROUP = 8
N_EXPERTS = N_GROUPS * EXPERTS_PER_GROUP
D_EXPERT = 256
LN_EPS = 1e-5
NEG_BIG = -1e30
IN_SPLITS = (D_A, D_A, D_A, N_IDX_HEADS * IDX_DIM, IDX_DIM, N_IDX_HEADS, D_M, D_M, D_M, D_M, N_HEADS_M, N_HEADS_M)

LANES = 128
SUBLANES = 8
MIB = 1024 * 1024

C_QK = 0
C_QIDX = C_QK + 2 * D_A
C_K4 = C_QIDX + N_IDX_HEADS * IDX_DIM
C_GATE = C_K4 + LANES
C_QKM = C_GATE + LANES
C_VM = C_QKM + 2 * D_M
C_OM = C_VM + D_M
C_END = C_OM + D_M
SLABS_PER_TENSOR = D_A // LANES
HEADS_PER_SLAB = LANES // HEAD_DIM_A
G_W = 0
G_I = N_IDX_HEADS
G_F = G_I + N_HEADS_M

INT_MIN = -2 ** 31

TQ_DSA = LANES
TK_DSA = 1024
TM_IN_PROJ = TK_DSA
CHUNK_MLSTM = 256
TM_MIX = 512
TM_MOE = 1024

NT_DIMS = (((1,), (1,)), ((), ()))


def _cparams(semantics, vmem_mib):
    return pltpu.CompilerParams(dimension_semantics=semantics, vmem_limit_bytes=int(vmem_mib * MIB))


def _layer_norm(r, g, b):
    mu = jnp.mean(r, axis=-1, keepdims=True)
    d = r - mu
    var = jnp.mean(d * d, axis=-1, keepdims=True)
    return d * lax.rsqrt(var + LN_EPS) * g + b


def _pack_w_in(w_in):
    d = w_in.shape[0]
    cuts = [int(c) for c in np.cumsum(IN_SPLITS)[:-1]]
    q_a, k_a, v_a, q_idx, k_idx, w_idx, q_m, k_m, v_m, o_m, i_m, f_m = jnp.split(w_in, cuts, axis=1)
    k4 = jnp.tile(k_idx, (1, LANES // IDX_DIM))
    gate = jnp.concatenate([w_idx, i_m, f_m, jnp.zeros((d, LANES - G_F - N_HEADS_M), w_in.dtype)], axis=1)
    packed = jnp.concatenate([q_a, k_a, q_idx, k4, gate, q_m, k_m, v_m, o_m], axis=1).astype(BF16)
    return packed, v_a.T.astype(BF16)


def _in_proj_body(x_ref, w_ref, wvt_ref, qk_ref, vt_ref, qi_ref, k4_ref, g_ref, qkm_ref, vm_ref, om_ref):
    xb = x_ref[...].astype(BF16)

    def proj(lo, hi):
        return jnp.dot(xb, w_ref[:, lo:hi], preferred_element_type=F32)

    qk = proj(C_QK, C_QIDX).astype(BF16)
    for c in range(2 * SLABS_PER_TENSOR):
        qk_ref[c] = qk[:, c * LANES:(c + 1) * LANES]
    v_t = lax.dot_general(wvt_ref[...], xb, NT_DIMS, preferred_element_type=F32).astype(BF16)
    for c in range(SLABS_PER_TENSOR):
        vt_ref[c, 0] = v_t[c * LANES:(c + 1) * LANES, :]
    qi_ref[...] = proj(C_QIDX, C_K4).astype(BF16)
    k4_ref[...] = proj(C_K4, C_GATE).astype(BF16)
    g_ref[...] = proj(C_GATE, C_QKM)
    qkm_ref[...] = proj(C_QKM, C_VM)
    vm_ref[...] = proj(C_VM, C_OM).astype(BF16)
    om_ref[...] = proj(C_OM, C_END)


def _in_proj(x2d, w_packed, w_vt, tm):
    n, d = x2d.shape
    widths = [(C_K4 - C_QIDX, BF16), (C_GATE - C_K4, BF16), (C_QKM - C_GATE, F32),
              (C_VM - C_QKM, F32), (C_OM - C_VM, BF16), (C_END - C_OM, F32)]
    return pl.pallas_call(
        _in_proj_body,
        grid=(n // tm,),
        in_specs=[pl.BlockSpec((tm, d), lambda i: (i, 0)),
                  pl.BlockSpec((d, C_END), lambda i: (0, 0)),
                  pl.BlockSpec((D_A, d), lambda i: (0, 0))],
        out_specs=([pl.BlockSpec((2 * SLABS_PER_TENSOR, tm, LANES), lambda i: (0, i, 0)),
                    pl.BlockSpec((SLABS_PER_TENSOR, 1, LANES, tm), lambda i: (0, i, 0, 0))]
                   + [pl.BlockSpec((tm, w), lambda i: (i, 0)) for w, _ in widths]),
        out_shape=([jax.ShapeDtypeStruct((2 * SLABS_PER_TENSOR, n, LANES), BF16),
                    jax.ShapeDtypeStruct((SLABS_PER_TENSOR, n // tm, LANES, tm), BF16)]
                   + [jax.ShapeDtypeStruct((n, w), dt) for w, dt in widths]),
        compiler_params=_cparams(("arbitrary",), 48),
        name="in_proj",
    )(x2d, w_packed, w_vt)


def _float_to_ordered_int(value):
    bits = lax.bitcast_convert_type(value, I32)
    return bits ^ ((bits >> 31) & jnp.int32(0x7FFFFFFF))


def _ordered_int_to_float(key):
    bits = key ^ ((key >> 31) & jnp.int32(0x7FFFFFFF))
    return lax.bitcast_convert_type(bits, F32)


SEARCH_FREE_PASSES = 16
SEARCH_MAX_PASSES = SEARCH_FREE_PASSES + 33
ZERO_TIE_MARGIN = 128
CHAINS = 4
POS_RADIX = 64
POS_TERMS = 3
LOG2_E = float(np.log2(np.e))


def _dsa_body(qa_ref, ka_ref, vt_ref, qi_ref, k4_ref, g_ref, o_ref,
              sc_ref, qm_ref, qh_ref, mask_ref, acc_ref, m_ref, l_ref, *, seq, topk, tq, tk):
    t0 = pl.program_id(1) * tq
    n_kb = (t0 + tq + tk - 1) // tk
    groups = tk // (CHAINS * SUBLANES)
    shape4 = (groups, CHAINS, SUBLANES, tq)

    def chunks(x):
        return x.reshape(shape4)

    def fold(x):
        out = x[0]
        for c in range(1, CHAINS):
            out = out + x[c]
        return out

    def rows8(row):
        return jnp.broadcast_to(row, (SUBLANES, tq))

    key_off = ((lax.broadcasted_iota(I32, shape4, 0) * CHAINS + lax.broadcasted_iota(I32, shape4, 1)) * SUBLANES
               + lax.broadcasted_iota(I32, shape4, 2))
    q_pos = t0 + lax.broadcasted_iota(I32, shape4, 3)
    lane = lax.broadcasted_iota(I32, (tq, LANES), 1)

    for h in range(N_IDX_HEADS):
        per_slab = LANES // IDX_DIM
        slab = qi_ref[:, (h // per_slab) * LANES:(h // per_slab + 1) * LANES]
        lo = (h % per_slab) * IDX_DIM
        qm_ref[h * tq:(h + 1) * tq, :] = jnp.where((lane >= lo) & (lane < lo + IDX_DIM), slab, jnp.zeros_like(slab))
    for h in range(N_HEADS_A):
        p, hh = divmod(h, HEADS_PER_SLAB)
        slope = 2.0 ** (-8.0 * (h + 1) / N_HEADS_A)
        slab = qa_ref[p].astype(F32) * (HEAD_DIM_A ** -0.5 * LOG2_E)
        in_head = (lane >= hh * HEAD_DIM_A) & (lane < (hh + 1) * HEAD_DIM_A)
        qh_ref[p, hh * tq:(hh + 1) * tq, 0:LANES] = jnp.where(in_head, slab, 0.0).astype(BF16)
        weight = jnp.where(lane < POS_TERMS, slope * POS_RADIX * LOG2_E,
                           jnp.where(lane < 2 * POS_TERMS, slope * LOG2_E, 0.0))
        terms = _split3(weight)
        feat = jnp.where(lane % POS_TERMS == 0, terms[0], jnp.where(lane % POS_TERMS == 1, terms[1], terms[2]))
        qh_ref[p, hh * tq:(hh + 1) * tq, LANES:2 * LANES] = feat
    w_rows = g_ref[...].T[G_W:G_W + N_IDX_HEADS, :] * ((IDX_DIM * N_IDX_HEADS) ** -0.5)

    def score_block(j, carry):
        hi_part, lo_part, zero_part = carry
        k0 = pl.multiple_of(j * tk, tk)
        logits = lax.dot_general(k4_ref[pl.ds(k0, tk), :], qm_ref[...], NT_DIMS,
                                 preferred_element_type=F32)
        score = jnp.zeros(shape4, F32)
        for h in range(N_IDX_HEADS):
            score = score + rows8(w_rows[h:h + 1, :]) * jnp.maximum(chunks(logits[:, h * tq:(h + 1) * tq]), 0.0)
        causal = k0 + key_off <= q_pos
        kept = jnp.where(causal, score, NEG_BIG)
        sc_ref[j] = kept
        hi_part = jnp.maximum(hi_part, jnp.max(kept, axis=0))
        lo_part = jnp.minimum(lo_part, jnp.min(jnp.where(causal, score, jnp.inf), axis=0))
        zero_part = zero_part + jnp.sum(jnp.where(kept >= 0.0, 1.0, 0.0), axis=0)
        return hi_part, lo_part, zero_part

    part_shape = (CHAINS, SUBLANES, tq)
    hi_part, lo_part, zero_part = lax.fori_loop(
        0, n_kb, score_block,
        (jnp.full(part_shape, -jnp.inf, F32), jnp.full(part_shape, jnp.inf, F32), jnp.zeros(part_shape, F32)))
    row_max = jnp.max(jnp.max(hi_part, axis=0), axis=0, keepdims=True)
    row_min = jnp.min(jnp.min(lo_part, axis=0), axis=0, keepdims=True)

    n_tail = (seq - n_kb * tk).astype(F32)

    def count(thr, strict):
        thr8 = rows8(thr)

        def body(j, acc):
            s = sc_ref[j]
            return acc + jnp.sum(jnp.where((s > thr8) if strict else (s >= thr8), 1.0, 0.0), axis=0)

        acc = lax.fori_loop(0, n_kb, body, jnp.zeros(part_shape, F32))
        tail_hit = (NEG_BIG > thr) if strict else (NEG_BIG >= thr)
        return jnp.sum(fold(acc), axis=0, keepdims=True) + jnp.where(tail_hit, n_tail, 0.0)

    n_causal = (t0 + 1 + lax.broadcasted_iota(I32, (1, tq), 1)).astype(F32)
    few = n_causal < topk
    lo0 = jnp.where(few, _float_to_ordered_int(jnp.full((1, tq), NEG_BIG, F32)), _float_to_ordered_int(row_min))
    cnt_lo0 = jnp.where(few, float(seq), n_causal)
    hi0 = _float_to_ordered_int(row_max) + 1
    log_k = float(np.log(topk))

    def log_excess(cnt):
        return jnp.log(jnp.maximum(cnt, 0.5)) - log_k

    def search_step(state):
        it, _, lo, hi, cnt_lo, f_lo, f_hi, last, done_i = state
        done = done_i != 0
        it_v = jnp.zeros((1, tq), I32) + it
        lo_v = _ordered_int_to_float(lo)
        hi_v = _ordered_int_to_float(hi)
        guess = _float_to_ordered_int(lo_v + (f_lo / (f_lo - f_hi)) * (hi_v - lo_v))
        guess = jnp.where((it_v == 1) & (lo == 0) & (cnt_lo < topk + ZERO_TIE_MARGIN), 1, guess)
        middle = lo + lax.shift_right_logical(hi - lo, jnp.ones((1, tq), I32))
        usable = (guess > lo) & (guess < hi) & (it_v < SEARCH_FREE_PASSES)
        cand = jnp.where(usable, guess, middle)
        cnt = count(_ordered_int_to_float(cand), strict=False)
        f_c = log_excess(cnt)
        take = (cnt >= topk) & ~done
        drop = (cnt < topk) & ~done
        f_hi = jnp.where(take, jnp.where(last == 1, 0.5 * f_hi, f_hi), jnp.where(drop, f_c, f_hi))
        f_lo = jnp.where(drop, jnp.where(last == -1, 0.5 * f_lo, f_lo), jnp.where(take, f_c, f_lo))
        last = jnp.where(take, 1, jnp.where(drop, -1, last))
        lo = jnp.where(take, cand, lo)
        cnt_lo = jnp.where(take, cnt, cnt_lo)
        hi = jnp.where(drop, cand, hi)
        done = done | (cnt_lo == topk) | ((hi - lo) == 1)
        active = jnp.sum(jnp.where(done, 0.0, 1.0))
        return it + 1, active, lo, hi, cnt_lo, f_lo, f_hi, last, done.astype(I32)

    def search_on(state):
        return (state[0] < SEARCH_MAX_PASSES) & (state[1] > 0.0)

    cnt_zero = jnp.sum(fold(zero_part), axis=0, keepdims=True)
    inside = (lo0 < 0) & (hi0 > 0) & ~few
    take0 = inside & (cnt_zero >= topk)
    drop0 = inside & (cnt_zero < topk)
    lo1 = jnp.where(take0, 0, lo0)
    hi1 = jnp.where(drop0, 0, hi0)
    cnt_lo1 = jnp.where(take0, cnt_zero, cnt_lo0)
    done1 = few | (cnt_lo1 == topk) | ((hi1 - lo1) == 1)
    state = lax.while_loop(search_on, search_step,
                           (jnp.int32(1), jnp.float32(1.0), lo1, hi1, cnt_lo1, log_excess(cnt_lo1),
                            log_excess(jnp.where(drop0, cnt_zero, 0.0)),
                            jnp.where(take0, 1, jnp.where(drop0, -1, 0)), done1.astype(I32)))
    thr = _ordered_int_to_float(state[2])
    cnt_thr = state[4]
    thr8 = rows8(thr)

    @pl.when(jnp.max(cnt_thr) > topk)
    def _():
        need = topk - count(thr, strict=True)
        tri_r = lax.broadcasted_iota(I32, (tk, tk), 0)
        tri_c = lax.broadcasted_iota(I32, (tk, tk), 1)
        earlier = (tri_c < tri_r).astype(BF16)

        def drop_late_ties(j, seen):
            s = sc_ref[j]
            eq = s == thr8
            eq_f = jnp.where(eq, 1.0, 0.0)
            rank = seen + jnp.dot(earlier, eq_f.reshape(tk, tq).astype(BF16), preferred_element_type=F32)
            sc_ref[j] = jnp.where(eq & (chunks(rank) >= rows8(need)), -jnp.inf, s)
            return seen + jnp.sum(fold(jnp.sum(eq_f, axis=0)), axis=0, keepdims=True)

        lax.fori_loop(0, n_kb, drop_late_ties, jnp.zeros((1, tq), F32))

    m_ref[...] = jnp.full(m_ref.shape, NEG_BIG, F32)
    l_ref[...] = jnp.zeros(l_ref.shape, F32)
    acc_ref[...] = jnp.zeros(acc_ref.shape, F32)
    pos_lane = lax.broadcasted_iota(I32, (tk, LANES), 1)
    pos_row = lax.broadcasted_iota(I32, (tk, LANES), 0)
    feat0 = jnp.where(pos_lane < POS_TERMS, pos_row // POS_RADIX,
                      jnp.where(pos_lane < 2 * POS_TERMS, pos_row % POS_RADIX, 0)).astype(F32)
    hi_lane = jnp.where(pos_lane < POS_TERMS, 1.0, 0.0)

    def attend(j, carry):
        k0 = pl.multiple_of(j * tk, tk)
        mask_ref[...] = jnp.where((sc_ref[j] >= thr8) & (k0 + key_off <= q_pos), 0.0, -jnp.inf)
        pos_feat = (feat0 + hi_lane * (j * (tk // POS_RADIX)).astype(F32)).astype(BF16)

        scores = []
        for p in range(SLABS_PER_TENSOR):
            keys = jnp.concatenate([ka_ref[p, pl.ds(k0, tk), :], pos_feat], axis=1)
            scores.append(lax.dot_general(keys, qh_ref[p], NT_DIMS, preferred_element_type=F32))
        weights = []
        for p in range(SLABS_PER_TENSOR):
            for hh in range(HEADS_PER_SLAB):
                s = chunks(scores[p][:, hh * tq:(hh + 1) * tq]) + mask_ref[...]
                m_old = m_ref[p, hh]
                m_blk = jnp.max(fold_max(jnp.max(s, axis=0)), axis=0, keepdims=True)
                m_new = jnp.maximum(m_old, rows8(m_blk))
                alpha = jnp.exp2(m_old - m_new)
                m_ref[p, hh] = m_new
                weights.append((alpha, jnp.exp2(s - m_new).reshape(tk, tq).astype(BF16)))
        ones_rows = jnp.ones((SUBLANES, tk), BF16)
        for p in range(SLABS_PER_TENSOR):
            for hh in range(HEADS_PER_SLAB):
                rows = slice(hh * HEAD_DIM_A, (hh + 1) * HEAD_DIM_A)
                alpha, probs = weights[p * HEADS_PER_SLAB + hh]
                values = jnp.concatenate([vt_ref[p, j, rows, :], ones_rows], axis=0)
                pv = jnp.dot(values, probs, preferred_element_type=F32)
                old = acc_ref[p, rows, :].reshape(HEAD_DIM_A // SUBLANES, SUBLANES, tq)
                acc_ref[p, rows, :] = (alpha * old).reshape(HEAD_DIM_A, tq) + pv[0:HEAD_DIM_A]
                l_ref[p, hh] = alpha * l_ref[p, hh] + pv[HEAD_DIM_A:HEAD_DIM_A + SUBLANES]
        return carry

    def fold_max(x):
        out = x[0]
        for c in range(1, CHAINS):
            out = jnp.maximum(out, x[c])
        return out

    lax.fori_loop(0, n_kb, attend, 0)

    for p in range(SLABS_PER_TENSOR):
        halves = []
        for hh in range(HEADS_PER_SLAB):
            rows = slice(hh * HEAD_DIM_A, (hh + 1) * HEAD_DIM_A)
            acc = acc_ref[p, rows, :].reshape(HEAD_DIM_A // SUBLANES, SUBLANES, tq)
            halves.append((acc / l_ref[p, hh]).reshape(HEAD_DIM_A, tq))
        o_ref[:, p * LANES:(p + 1) * LANES] = jnp.concatenate(halves, axis=0).T.astype(BF16)


def _dsa_attention(qk, v_t, q_idx, k4, gates, tq, tk):
    _, b, s, _ = qk.shape
    assert tq == LANES and s // POS_RADIX <= 256
    topk = min(TOPK_MAX, s // 4)
    once = pl.Buffered(1)
    groups = tk // (CHAINS * SUBLANES)
    body = functools.partial(_dsa_body, seq=s, topk=topk, tq=tq, tk=tk)
    return pl.pallas_call(
        body,
        grid=(b, s // tq),
        in_specs=[pl.BlockSpec((SLABS_PER_TENSOR, None, tq, LANES), lambda bi, qi: (0, bi, qi, 0)),
                  pl.BlockSpec((SLABS_PER_TENSOR, None, s, LANES), lambda bi, qi: (1, bi, 0, 0), pipeline_mode=once),
                  pl.BlockSpec((SLABS_PER_TENSOR, None, s // tk, LANES, tk), lambda bi, qi: (0, bi, 0, 0, 0),
                               pipeline_mode=once),
                  pl.BlockSpec((None, tq, N_IDX_HEADS * IDX_DIM), lambda bi, qi: (bi, qi, 0)),
                  pl.BlockSpec((None, s, LANES), lambda bi, qi: (bi, 0, 0), pipeline_mode=once),
                  pl.BlockSpec((None, tq, LANES), lambda bi, qi: (bi, qi, 0))],
        out_specs=pl.BlockSpec((None, tq, D_A), lambda bi, qi: (bi, qi, 0)),
        out_shape=jax.ShapeDtypeStruct((b, s, D_A), BF16),
        scratch_shapes=[pltpu.VMEM((s // tk, groups, CHAINS, SUBLANES, tq), F32),
                        pltpu.VMEM((N_IDX_HEADS * tq, LANES), BF16),
                        pltpu.VMEM((SLABS_PER_TENSOR, HEADS_PER_SLAB * tq, 2 * LANES), BF16),
                        pltpu.VMEM((groups, CHAINS, SUBLANES, tq), F32),
                        pltpu.VMEM((SLABS_PER_TENSOR, LANES, tq), F32),
                        pltpu.VMEM((SLABS_PER_TENSOR, HEADS_PER_SLAB, SUBLANES, tq), F32),
                        pltpu.VMEM((SLABS_PER_TENSOR, HEADS_PER_SLAB, SUBLANES, tq), F32)],
        compiler_params=_cparams(("arbitrary", "arbitrary"), 48),
        name="dsa_attention",
    )(qk, qk, v_t, q_idx, k4, gates)


def _log_sigmoid(x):
    return jnp.minimum(x, 0.0) - jnp.log(1.0 + jnp.exp(-jnp.abs(x)))


def _split3(a):
    hi = a.astype(BF16)
    r1 = a - hi.astype(F32)
    mid = r1.astype(BF16)
    lo = (r1 - mid.astype(F32)).astype(BF16)
    return hi, mid, lo


def _mlstm_body(qk_ref, v_ref, o_ref, g_ref, gt_ref, cw_ref, cb_ref, bg_ref, bgt_ref, gn_ref, y_ref,
                xe_ref, c_ref, m_ref, *, chunk, batch):
    @pl.when(pl.program_id(0) == 0)
    def _():
        xe_ref[:, 0:SUBLANES, :] = jnp.zeros((batch, SUBLANES, 2 * D_M), F32)
        c_ref[...] = jnp.zeros(c_ref.shape, F32)
        m_ref[...] = jnp.zeros(m_ref.shape, F32)

    gates = [_mlstm_gates(qk_ref.at[bi], g_ref.at[bi], gt_ref.at[bi], cw_ref, cb_ref, bg_ref, bgt_ref,
                          xe_ref.at[bi], chunk) for bi in range(batch)]
    for h in range(N_HEADS_M):
        for bi in range(batch):
            _mlstm_head(h, gates[bi], v_ref.at[bi], o_ref.at[bi], gn_ref, y_ref.at[bi], c_ref.at[bi], m_ref.at[bi],
                        chunk)


def _mlstm_gates(qk_ref, g_ref, gt_ref, cw_ref, cb_ref, bg_ref, bgt_ref, xe_ref, chunk):
    xe_ref[SUBLANES:SUBLANES + chunk, :] = qk_ref[...]
    conv = cb_ref[...]
    for j in range(CONV_K):
        start = SUBLANES - (CONV_K - 1) + j
        conv = conv + cw_ref[j:j + 1, :] * xe_ref[start:start + chunk, :]
    qk = conv * jax.nn.sigmoid(conv)
    xe_ref[0:SUBLANES, :] = xe_ref[chunk:chunk + SUBLANES, :]

    gb = g_ref[...] + bg_ref[...]
    gbt = gt_ref[...] + bgt_ref[...]
    r = lax.broadcasted_iota(I32, (chunk, chunk), 0)
    c = lax.broadcasted_iota(I32, (chunk, chunk), 1)
    causal = r >= c
    incl = causal.astype(BF16)
    incl_t = (r <= c).astype(BF16)
    cum = jnp.zeros((chunk, LANES), F32)
    for term in _split3(_log_sigmoid(gb)):
        cum = cum + jnp.dot(incl, term, preferred_element_type=F32)
    cum_t = jnp.zeros((2 * N_HEADS_M, chunk), F32)
    for term in _split3(_log_sigmoid(gbt)):
        cum_t = cum_t + jnp.dot(term, incl_t, preferred_element_type=F32)

    return qk, gb, gbt, cum, cum_t, causal


def _mlstm_head(h, gates, v_ref, o_ref, gn_ref, y_ref, c_ref, m_ref, chunk):
    qk, gb, gbt, cum, cum_t, causal = gates
    lane = lax.broadcasted_iota(I32, (chunk, LANES), 1)
    ones_col = jnp.where(lane == 0, 1.0, 0.0).astype(BF16)
    sl = slice(h * HEAD_DIM_M, (h + 1) * HEAD_DIM_M)
    q_h = (qk[:, sl] * (HEAD_DIM_M ** -0.5)).astype(BF16)
    k_f = qk[:, D_M + h * HEAD_DIM_M:D_M + (h + 1) * HEAD_DIM_M]
    v_aug = jnp.concatenate([v_ref[:, sl], ones_col], axis=1)
    b_col = cum[:, G_F + h:G_F + h + 1]
    i_col = gb[:, G_I + h:G_I + h + 1]
    b_row = cum_t[N_HEADS_M + h:N_HEADS_M + h + 1, :]
    i_row = gbt[h:h + 1, :]
    m_prev = m_ref[h:h + 1, 0:1]
    c_aug = c_ref[h]

    d_log = jnp.where(causal, b_col - b_row + i_row, -jnp.inf)
    inter = b_col + m_prev
    m_t = jnp.maximum(inter, jnp.max(d_log, axis=1, keepdims=True))
    qk_t = lax.dot_general(q_h, k_f.astype(BF16), NT_DIMS, preferred_element_type=F32)
    w_intra = qk_t * jnp.exp(d_log - m_t)
    w_inter = jnp.exp(inter - m_t)
    intra = jnp.dot(w_intra.astype(BF16), v_aug[:, 0:HEAD_DIM_M], preferred_element_type=F32)
    carried = jnp.dot(q_h, c_aug.astype(BF16), preferred_element_type=F32)
    num = intra + w_inter * carried[:, 0:HEAD_DIM_M]
    den = jnp.sum(w_intra, axis=1, keepdims=True) + w_inter * carried[:, HEAD_DIM_M:HEAD_DIM_M + 1]
    hid = num / jnp.maximum(jnp.abs(den), jnp.exp(-m_t))

    mu = jnp.mean(hid, axis=1, keepdims=True)
    dev = hid - mu
    var = jnp.mean(dev * dev, axis=1, keepdims=True)
    normed = dev * lax.rsqrt(var + LN_EPS) * gn_ref[:, sl]
    y_ref[:, sl] = (jax.nn.sigmoid(o_ref[:, sl]) * normed).astype(BF16)

    b_last = b_col[chunk - 1:chunk, :]
    w_log = b_last - b_col + i_col
    m_new = jnp.maximum(b_last + m_prev, jnp.max(w_log, axis=0, keepdims=True))
    decay = jnp.exp(b_last + m_prev - m_new)
    kw_t = (k_f * jnp.exp(w_log - m_new)).T.astype(BF16)
    c_ref[h] = decay * c_aug + jnp.dot(kw_t, v_aug, preferred_element_type=F32)
    m_ref[h:h + 1, :] = jnp.broadcast_to(m_new, (1, LANES))


def _mlstm(qk_m, v_m, o_m, gates, gates_t, conv_w, conv_b, bias_slab, bias_col, gn_w, chunk):
    b, s, _ = qk_m.shape
    body = functools.partial(_mlstm_body, chunk=chunk, batch=b)
    const = lambda ci: (0, 0)
    return pl.pallas_call(
        body,
        grid=(s // chunk,),
        in_specs=[pl.BlockSpec((b, chunk, 2 * D_M), lambda ci: (0, ci, 0)),
                  pl.BlockSpec((b, chunk, D_M), lambda ci: (0, ci, 0)),
                  pl.BlockSpec((b, chunk, D_M), lambda ci: (0, ci, 0)),
                  pl.BlockSpec((b, chunk, LANES), lambda ci: (0, ci, 0)),
                  pl.BlockSpec((b, 2 * N_HEADS_M, chunk), lambda ci: (0, 0, ci)),
                  pl.BlockSpec((CONV_K, 2 * D_M), const),
                  pl.BlockSpec((1, 2 * D_M), const),
                  pl.BlockSpec((1, LANES), const),
                  pl.BlockSpec((2 * N_HEADS_M, 1), const),
                  pl.BlockSpec((1, D_M), const)],
        out_specs=pl.BlockSpec((b, chunk, D_M), lambda ci: (0, ci, 0)),
        out_shape=jax.ShapeDtypeStruct((b, s, D_M), BF16),
        scratch_shapes=[pltpu.VMEM((b, chunk + SUBLANES, 2 * D_M), F32),
                        pltpu.VMEM((b, N_HEADS_M, HEAD_DIM_M, 2 * LANES), F32),
                        pltpu.VMEM((b, SUBLANES, LANES), F32)],
        compiler_params=_cparams(("arbitrary",), 32),
        name="mlstm",
    )(qk_m, v_m, o_m, gates, gates_t, conv_w, conv_b, bias_slab, bias_col, gn_w)


COMB_GROUP_LANE = N_EXPERTS


def _route(logits):
    lane = lax.broadcasted_iota(I32, logits.shape, 1).astype(F32)
    g_mask = (lane >= N_EXPERTS) & (lane < N_EXPERTS + N_GROUPS)
    g_logit = jnp.where(g_mask, logits, -jnp.inf)
    g_exp = jnp.exp(g_logit - jnp.max(g_logit, axis=1, keepdims=True))
    g_prob = g_exp / jnp.sum(g_exp, axis=1, keepdims=True)
    g_p = jnp.max(g_prob, axis=1, keepdims=True)
    g_sel = jnp.min(jnp.where(g_mask & (g_prob == g_p), lane, float(LANES)), axis=1, keepdims=True) - N_EXPERTS
    e_mask = (lane >= g_sel * EXPERTS_PER_GROUP) & (lane < (g_sel + 1.0) * EXPERTS_PER_GROUP)
    e_logit = jnp.where(e_mask, logits, -jnp.inf)
    e_exp = jnp.exp(e_logit - jnp.max(e_logit, axis=1, keepdims=True))
    e_prob = e_exp / jnp.sum(e_exp, axis=1, keepdims=True)
    p1 = jnp.max(jnp.where(e_mask, e_prob, -1.0), axis=1, keepdims=True)
    i1 = jnp.min(jnp.where(e_mask & (e_prob == p1), lane, float(LANES)), axis=1, keepdims=True)
    rest = e_mask & (lane != i1)
    p2 = jnp.max(jnp.where(rest, e_prob, -1.0), axis=1, keepdims=True)
    i2 = jnp.min(jnp.where(rest & (e_prob == p2), lane, float(LANES)), axis=1, keepdims=True)
    total = p1 + p2
    comb = (jnp.where(lane == i1, p1 / total, 0.0) + jnp.where(lane == i2, p2 / total, 0.0)) * g_p
    return comb + jnp.where(lane == COMB_GROUP_LANE, g_sel, 0.0)


def _mix_body(ya_ref, ym_ref, x_ref, wo_ref, g_ref, b_ref, wr_ref, br_ref, x1_ref, comb_ref, *, alpha):
    mix = (jnp.dot(ya_ref[...], wo_ref[0:D_A, :], preferred_element_type=F32)
           + jnp.dot(ym_ref[...], wo_ref[D_A:D_A + D_M, :], preferred_element_type=F32))
    x1 = _layer_norm(alpha * x_ref[...] + mix, g_ref[...], b_ref[...])
    x1_ref[...] = x1
    x_hi = x1.astype(BF16)
    x_lo = (x1 - x_hi.astype(F32)).astype(BF16)
    w_hi = wr_ref[0]
    w_lo = wr_ref[1]
    logits = (jnp.dot(x_hi, w_hi, preferred_element_type=F32) + jnp.dot(x_lo, w_hi, preferred_element_type=F32)
              + jnp.dot(x_hi, w_lo, preferred_element_type=F32)) + br_ref[...]
    comb_ref[...] = _route(logits)


def _mix_ln_router(y_a, y_m, x2d, w_out, ln_g, ln_b, w_router, b_router, alpha, tm):
    n, d = x2d.shape
    const = lambda i: (0, 0)
    return pl.pallas_call(
        functools.partial(_mix_body, alpha=alpha),
        grid=(n // tm,),
        in_specs=[pl.BlockSpec((tm, D_A), lambda i: (i, 0)),
                  pl.BlockSpec((tm, D_M), lambda i: (i, 0)),
                  pl.BlockSpec((tm, d), lambda i: (i, 0)),
                  pl.BlockSpec((D_A + D_M, d), const),
                  pl.BlockSpec((1, d), const),
                  pl.BlockSpec((1, d), const),
                  pl.BlockSpec((2, d, LANES), lambda i: (0, 0, 0)),
                  pl.BlockSpec((1, LANES), const)],
        out_specs=[pl.BlockSpec((tm, d), lambda i: (i, 0)),
                   pl.BlockSpec((tm, LANES), lambda i: (i, 0))],
        out_shape=[jax.ShapeDtypeStruct((n, d), F32), jax.ShapeDtypeStruct((n, LANES), F32)],
        compiler_params=_cparams(("arbitrary",), 32),
        name="mix_ln_router",
    )(y_a, y_m, x2d, w_out, ln_g, ln_b, w_router, b_router)


MOE_ROWS = TM_MOE // N_GROUPS + 32


def _moe_body(x1_ref, comb_ref, wg_ref, wu_ref, wd_ref, g_ref, b_ref, p_ref, wpg_ref, wpp_ref, y_ref,
              xb_ref, before_ref, cs_ref, row_ref, *, alpha, tm):
    g = pl.program_id(1)
    g_f = g.astype(F32)

    @pl.when((pl.program_id(0) == 0) & (g == 0))
    def _():
        r = lax.broadcasted_iota(I32, (tm, tm), 0)
        c = lax.broadcasted_iota(I32, (tm, tm), 1)
        before_ref[...] = (c < r).astype(BF16)

    @pl.when(g == 0)
    def _():
        xb_ref[...] = x1_ref[...].astype(BF16)
        y_ref[...] = jnp.zeros(y_ref.shape, F32)
        comb = comb_ref[...]
        for k, term in enumerate(_split3(comb)):
            cs_ref[k] = term
        group_row = comb.T[COMB_GROUP_LANE:COMB_GROUP_LANE + 1, :]
        sub = lax.broadcasted_iota(I32, (SUBLANES, tm), 0)
        member_t = (sub.astype(F32) == group_row) & (sub < N_GROUPS)
        ahead_t = lax.dot_general(jnp.where(member_t, 1.0, 0.0).astype(BF16), before_ref[...], NT_DIMS,
                                  preferred_element_type=F32)
        rank_row = jnp.sum(jnp.where(member_t, ahead_t, 0.0), axis=0, keepdims=True)
        row_ref[0] = jnp.broadcast_to(group_row, (SUBLANES, tm))
        row_ref[1] = jnp.broadcast_to(rank_row, (SUBLANES, tm))

    in_group_row = row_ref[0, 0:1, :] == g_f
    n_tokens = jnp.sum(jnp.where(in_group_row, 1.0, 0.0))
    n_blocks = (n_tokens.astype(I32) + MOE_ROWS - 1) // MOE_ROWS
    slot = lax.broadcasted_iota(I32, (MOE_ROWS, tm), 0).astype(F32)
    comb_lane = lax.broadcasted_iota(I32, (MOE_ROWS, LANES), 1)

    def block(b, carry):
        base = (b * MOE_ROWS).astype(F32)
        gather = jnp.where(in_group_row & (row_ref[1, 0:1, :] - base == slot), 1.0, 0.0).astype(BF16)
        xc = jnp.dot(gather, xb_ref[...], preferred_element_type=F32).astype(BF16)
        comb_c = jnp.zeros((MOE_ROWS, LANES), F32)
        for k in range(3):
            comb_c = comb_c + jnp.dot(gather, cs_ref[k], preferred_element_type=F32)
        yc = jnp.zeros((MOE_ROWS, x1_ref.shape[1]), F32)
        for e in range(EXPERTS_PER_GROUP):
            gate = jnp.dot(xc, wg_ref[e], preferred_element_type=F32)
            up = jnp.dot(xc, wu_ref[e], preferred_element_type=F32)
            weight = jnp.sum(jnp.where(comb_lane == g * EXPERTS_PER_GROUP + e, comb_c, 0.0), axis=1, keepdims=True)
            hidden = gate * jax.nn.sigmoid(gate) * up * weight
            yc = yc + jnp.dot(hidden.astype(BF16), wd_ref[e], preferred_element_type=F32)
        yc_hi = yc.astype(BF16)
        yc_lo = (yc - yc_hi.astype(F32)).astype(BF16)
        tn = (((0,), (0,)), ((), ()))
        y_ref[...] += (lax.dot_general(gather, yc_hi, tn, preferred_element_type=F32)
                       + lax.dot_general(gather, yc_lo, tn, preferred_element_type=F32))
        return carry

    lax.fori_loop(0, n_blocks, block, 0)

    @pl.when(g == N_GROUPS - 1)
    def _():
        x2 = _layer_norm(alpha * x1_ref[...] + y_ref[...], g_ref[...], b_ref[...])
        gate = jax.nn.sigmoid(jnp.dot(x2.astype(BF16), wpg_ref[...], preferred_element_type=F32))
        proj = jnp.dot(p_ref[...].astype(BF16), wpp_ref[...], preferred_element_type=F32)
        y_ref[...] = x2 + gate * proj


def _moe_ln_ple(x1, comb, w_gate, w_up, w_down, ln_g, ln_b, p2d, w_ple_gate, w_ple_proj, alpha, tm):
    n, d = x1.shape
    dp = p2d.shape[1]
    const = lambda i, g: (0, 0)
    once = pl.Buffered(1)
    grouped = lambda w: w.reshape((N_GROUPS, EXPERTS_PER_GROUP) + w.shape[1:])
    return pl.pallas_call(
        functools.partial(_moe_body, alpha=alpha, tm=tm),
        grid=(n // tm, N_GROUPS),
        in_specs=[pl.BlockSpec((tm, d), lambda i, g: (i, 0), pipeline_mode=once),
                  pl.BlockSpec((tm, LANES), lambda i, g: (i, 0)),
                  pl.BlockSpec((None, EXPERTS_PER_GROUP, d, D_EXPERT), lambda i, g: (g, 0, 0, 0)),
                  pl.BlockSpec((None, EXPERTS_PER_GROUP, d, D_EXPERT), lambda i, g: (g, 0, 0, 0)),
                  pl.BlockSpec((None, EXPERTS_PER_GROUP, D_EXPERT, d), lambda i, g: (g, 0, 0, 0)),
                  pl.BlockSpec((1, d), const),
                  pl.BlockSpec((1, d), const),
                  pl.BlockSpec((tm, dp), lambda i, g: (i, 0), pipeline_mode=once),
                  pl.BlockSpec((d, d), const, pipeline_mode=once),
                  pl.BlockSpec((dp, d), const, pipeline_mode=once)],
        out_specs=pl.BlockSpec((tm, d), lambda i, g: (i, 0)),
        out_shape=jax.ShapeDtypeStruct((n, d), F32),
        scratch_shapes=[pltpu.VMEM((tm, d), BF16),
                        pltpu.VMEM((tm, tm), BF16),
                        pltpu.VMEM((3, tm, LANES), BF16),
                        pltpu.VMEM((2, SUBLANES, tm), F32)],
        compiler_params=_cparams(("arbitrary", "arbitrary"), 58),
        name="moe_ln_ple",
    )(x1, comb, grouped(w_gate), grouped(w_up), grouped(w_down), ln_g, ln_b, p2d, w_ple_gate, w_ple_proj)


def _tile(n, want):
    t = min(want, n)
    assert n % t == 0
    return t


def _layer(x, p_i, w_in, b_igate, b_fgate, conv_w, conv_b, gn_w, w_out, ln1_g, ln1_b,
           rg_w, rg_b, re_w, re_b, e_gate, e_up, e_down, ln2_g, ln2_b, ple_gate_w, ple_proj_w, alpha):
    b, s, d = x.shape
    n = b * s
    x2d = x.reshape(n, d)

    tk = _tile(s, TK_DSA)
    w_packed, w_vt = _pack_w_in(w_in)
    qk, v_t, q_idx, k4, gates, qk_m, v_m, o_m = _in_proj(x2d, w_packed, w_vt, tk)

    gates3 = gates.reshape(b, s, LANES)
    y_a = _dsa_attention(qk.reshape(2 * SLABS_PER_TENSOR, b, s, LANES),
                         v_t.reshape(SLABS_PER_TENSOR, b, s // tk, LANES, tk),
                         q_idx.reshape(b, s, N_IDX_HEADS * IDX_DIM), k4.reshape(b, s, LANES), gates3,
                         _tile(s, TQ_DSA), tk)

    gate_bias = jnp.concatenate([b_igate, b_fgate]).astype(F32)
    bias_slab = jnp.zeros((1, LANES), F32).at[0, G_I:G_I + 2 * N_HEADS_M].set(gate_bias)
    gates_t = jnp.swapaxes(gates3[:, :, G_I:G_I + 2 * N_HEADS_M], 1, 2)
    y_m = _mlstm(qk_m.reshape(b, s, 2 * D_M), v_m.reshape(b, s, D_M), o_m.reshape(b, s, D_M), gates3, gates_t,
                 conv_w, conv_b.reshape(1, -1), bias_slab, gate_bias.reshape(-1, 1), gn_w.reshape(1, -1),
                 _tile(s, CHUNK_MLSTM))

    w_router = jnp.concatenate([re_w, rg_w, jnp.zeros((d, LANES - N_EXPERTS - N_GROUPS), F32)], axis=1)
    w_router_hi = w_router.astype(BF16)
    w_router = jnp.stack([w_router_hi, (w_router - w_router_hi.astype(F32)).astype(BF16)])
    b_router = jnp.concatenate([re_b, rg_b, jnp.zeros((LANES - N_EXPERTS - N_GROUPS,), F32)]).reshape(1, LANES)
    x1, comb = _mix_ln_router(y_a.reshape(n, D_A), y_m.reshape(n, D_M), x2d, w_out.astype(BF16),
                              ln1_g.reshape(1, d), ln1_b.reshape(1, d), w_router, b_router, alpha, _tile(n, TM_MIX))

    out = _moe_ln_ple(x1, comb, e_gate.astype(BF16), e_up.astype(BF16), e_down.astype(BF16),
                      ln2_g.reshape(1, d), ln2_b.reshape(1, d), p_i.reshape(n, -1), ple_gate_w.astype(BF16),
                      ple_proj_w.astype(BF16), alpha, _tile(n, TM_MOE))
    return out.reshape(b, s, d)


def kernel(x, p, w_in, b_igate, b_fgate, conv_w, conv_b, gn_w, w_out, ln1_g, ln1_b, router_group_w, router_group_b,
           router_expert_w, router_expert_b, expert_w_gate, expert_w_up, expert_w_down, ln2_g, ln2_b, ple_gate_w,
           ple_proj_w):
    depth = w_in.shape[0]
    alpha = (2.0 * depth) ** 0.25
    for i in range(depth):
        x = _layer(x, p[i], w_in[i], b_igate[i], b_fgate[i], conv_w[i], conv_b[i], gn_w[i], w_out[i], ln1_g[i],
                   ln1_b[i], router_group_w[i], router_group_b[i], router_expert_w[i], router_expert_b[i],
                   expert_w_gate[i], expert_w_up[i], expert_w_down[i], ln2_g[i], ln2_b[i], ple_gate_w[i],
                   ple_proj_w[i], alpha)
    return x
```

```python
import functools

import numpy as np
import jax
import jax.numpy as jnp
from jax import lax
from jax.experimental import pallas as pl
from jax.experimental.pallas import tpu as pltpu

F32 = jnp.float32
BF16 = jnp.bfloat16
I32 = jnp.int32

N_HEADS_A = 8
HEAD_DIM_A = 64
D_A = N_HEADS_A * HEAD_DIM_A
N_IDX_HEADS = 8
IDX_DIM = 32
TOPK_MAX = 256
N_HEADS_M = 4
HEAD_DIM_M = 128
D_M = N_HEADS_M * HEAD_DIM_M
CONV_K = 4
N_GROUPS = 4
EXPERTS_PER_GROUP = 8
N_EXPERTS = N_GROUPS * EXPERTS_PER_GROUP
D_EXPERT = 256
LN_EPS = 1e-5
NEG_BIG = -1e30
IN_SPLITS = (D_A, D_A, D_A, N_IDX_HEADS * IDX_DIM, IDX_DIM, N_IDX_HEADS, D_M, D_M, D_M, D_M, N_HEADS_M, N_HEADS_M)

LANES = 128
SUBLANES = 8
MIB = 1024 * 1024

C_QK = 0
C_QIDX = C_QK + 2 * D_A
C_K4 = C_QIDX + N_IDX_HEADS * IDX_DIM
C_GATE = C_K4 + LANES
C_QKM = C_GATE + LANES
C_VM = C_QKM + 2 * D_M
C_OM = C_VM + D_M
C_END = C_OM + D_M
SLABS_PER_TENSOR = D_A // LANES
HEADS_PER_SLAB = LANES // HEAD_DIM_A
G_W = 0
G_I = N_IDX_HEADS
G_F = G_I + N_HEADS_M

INT_MIN = -2 ** 31

TQ_DSA = LANES
TK_DSA = 1024
TM_IN_PROJ = TK_DSA
CHUNK_MLSTM = 256
TM_MIX = 512
TM_MOE = 1024

NT_DIMS = (((1,), (1,)), ((), ()))


def _cparams(semantics, vmem_mib):
    return pltpu.CompilerParams(dimension_semantics=semantics, vmem_limit_bytes=int(vmem_mib * MIB))


def _layer_norm(r, g, b):
    mu = jnp.mean(r, axis=-1, keepdims=True)
    d = r - mu
    var = jnp.mean(d * d, axis=-1, keepdims=True)
    return d * lax.rsqrt(var + LN_EPS) * g + b


def _pack_w_in(w_in):
    d = w_in.shape[0]
    cuts = [int(c) for c in np.cumsum(IN_SPLITS)[:-1]]
    q_a, k_a, v_a, q_idx, k_idx, w_idx, q_m, k_m, v_m, o_m, i_m, f_m = jnp.split(w_in, cuts, axis=1)
    k4 = jnp.tile(k_idx, (1, LANES // IDX_DIM))
    gate = jnp.concatenate([w_idx, i_m, f_m, jnp.zeros((d, LANES - G_F - N_HEADS_M), w_in.dtype)], axis=1)
    packed = jnp.concatenate([q_a, k_a, q_idx, k4, gate, q_m, k_m, v_m, o_m], axis=1).astype(BF16)
    return packed, v_a.T.astype(BF16)


def _in_proj_body(x_ref, w_ref, wvt_ref, qk_ref, vt_ref, qi_ref, k4_ref, g_ref, qkm_ref, vm_ref, om_ref):
    xb = x_ref[...].astype(BF16)

    def proj(lo, hi):
        return jnp.dot(xb, w_ref[:, lo:hi], preferred_element_type=F32)

    qk = proj(C_QK, C_QIDX).astype(BF16)
    for c in range(2 * SLABS_PER_TENSOR):
        qk_ref[c] = qk[:, c * LANES:(c + 1) * LANES]
    v_t = lax.dot_general(wvt_ref[...], xb, NT_DIMS, preferred_element_type=F32).astype(BF16)
    for c in range(SLABS_PER_TENSOR):
        vt_ref[c, 0] = v_t[c * LANES:(c + 1) * LANES, :]
    qi_ref[...] = proj(C_QIDX, C_K4).astype(BF16)
    k4_ref[...] = proj(C_K4, C_GATE).astype(BF16)
    g_ref[...] = proj(C_GATE, C_QKM)
    qkm_ref[...] = proj(C_QKM, C_VM)
    vm_ref[...] = proj(C_VM, C_OM).astype(BF16)
    om_ref[...] = proj(C_OM, C_END)


def _in_proj(x2d, w_packed, w_vt, tm):
    n, d = x2d.shape
    widths = [(C_K4 - C_QIDX, BF16), (C_GATE - C_K4, BF16), (C_QKM - C_GATE, F32),
              (C_VM - C_QKM, F32), (C_OM - C_VM, BF16), (C_END - C_OM, F32)]
    return pl.pallas_call(
        _in_proj_body,
        grid=(n // tm,),
        in_specs=[pl.BlockSpec((tm, d), lambda i: (i, 0)),
                  pl.BlockSpec((d, C_END), lambda i: (0, 0)),
                  pl.BlockSpec((D_A, d), lambda i: (0, 0))],
        out_specs=([pl.BlockSpec((2 * SLABS_PER_TENSOR, tm, LANES), lambda i: (0, i, 0)),
                    pl.BlockSpec((SLABS_PER_TENSOR, 1, LANES, tm), lambda i: (0, i, 0, 0))]
                   + [pl.BlockSpec((tm, w), lambda i: (i, 0)) for w, _ in widths]),
        out_shape=([jax.ShapeDtypeStruct((2 * SLABS_PER_TENSOR, n, LANES), BF16),
                    jax.ShapeDtypeStruct((SLABS_PER_TENSOR, n // tm, LANES, tm), BF16)]
                   + [jax.ShapeDtypeStruct((n, w), dt) for w, dt in widths]),
        compiler_params=_cparams(("arbitrary",), 48),
        name="in_proj",
    )(x2d, w_packed, w_vt)


def _float_to_ordered_int(value):
    bits = lax.bitcast_convert_type(value, I32)
    return bits ^ ((bits >> 31) & jnp.int32(0x7FFFFFFF))


def _ordered_int_to_float(key):
    bits = key ^ ((key >> 31) & jnp.int32(0x7FFFFFFF))
    return lax.bitcast_convert_type(bits, F32)


SEARCH_FIXED_PASSES = 12
SEARCH_FREE_PASSES = 16
SEARCH_MAX_PASSES = SEARCH_FREE_PASSES + 33
ZERO_TIE_MARGIN = 128
CHAINS = 4
POS_RADIX = 64
POS_TERMS = 3
LOG2_E = float(np.log2(np.e))


def _dsa_body(qa_ref, ka_ref, vt_ref, qi_ref, k4_ref, g_ref, o_ref,
              sc_ref, qm_ref, qh_ref, mask_ref, acc_ref, m_ref, l_ref, *, seq, topk, tq, tk):
    t0 = pl.program_id(1) * tq
    n_kb = (t0 + tq + tk - 1) // tk
    groups = tk // (CHAINS * SUBLANES)
    shape4 = (groups, CHAINS, SUBLANES, tq)

    def chunks(x):
        return x.reshape(shape4)

    def fold(x):
        out = x[0]
        for c in range(1, CHAINS):
            out = out + x[c]
        return out

    def rows8(row):
        return jnp.broadcast_to(row, (SUBLANES, tq))

    key_off = ((lax.broadcasted_iota(I32, shape4, 0) * CHAINS + lax.broadcasted_iota(I32, shape4, 1)) * SUBLANES
               + lax.broadcasted_iota(I32, shape4, 2))
    q_pos = t0 + lax.broadcasted_iota(I32, shape4, 3)
    lane = lax.broadcasted_iota(I32, (tq, LANES), 1)

    for h in range(N_IDX_HEADS):
        per_slab = LANES // IDX_DIM
        slab = qi_ref[:, (h // per_slab) * LANES:(h // per_slab + 1) * LANES]
        lo = (h % per_slab) * IDX_DIM
        qm_ref[h * tq:(h + 1) * tq, :] = jnp.where((lane >= lo) & (lane < lo + IDX_DIM), slab, jnp.zeros_like(slab))
    for h in range(N_HEADS_A):
        p, hh = divmod(h, HEADS_PER_SLAB)
        slope = 2.0 ** (-8.0 * (h + 1) / N_HEADS_A)
        slab = qa_ref[p].astype(F32) * (HEAD_DIM_A ** -0.5 * LOG2_E)
        in_head = (lane >= hh * HEAD_DIM_A) & (lane < (hh + 1) * HEAD_DIM_A)
        qh_ref[p, hh * tq:(hh + 1) * tq, 0:LANES] = jnp.where(in_head, slab, 0.0).astype(BF16)
        weight = jnp.where(lane < POS_TERMS, slope * POS_RADIX * LOG2_E,
                           jnp.where(lane < 2 * POS_TERMS, slope * LOG2_E, 0.0))
        terms = _split3(weight)
        feat = jnp.where(lane % POS_TERMS == 0, terms[0], jnp.where(lane % POS_TERMS == 1, terms[1], terms[2]))
        qh_ref[p, hh * tq:(hh + 1) * tq, LANES:2 * LANES] = feat
    w_rows = g_ref[...].T[G_W:G_W + N_IDX_HEADS, :] * ((IDX_DIM * N_IDX_HEADS) ** -0.5)

    def score_block(j, carry):
        hi_part, lo_part, zero_part = carry
        k0 = pl.multiple_of(j * tk, tk)
        logits = lax.dot_general(k4_ref[pl.ds(k0, tk), :], qm_ref[...], NT_DIMS,
                                 preferred_element_type=F32)
        score = jnp.zeros(shape4, F32)
        for h in range(N_IDX_HEADS):
            score = score + rows8(w_rows[h:h + 1, :]) * jnp.maximum(chunks(logits[:, h * tq:(h + 1) * tq]), 0.0)
        causal = k0 + key_off <= q_pos
        kept = jnp.where(causal, score, NEG_BIG)
        sc_ref[j] = kept
        hi_part = jnp.maximum(hi_part, jnp.max(kept, axis=0))
        lo_part = jnp.minimum(lo_part, jnp.min(jnp.where(causal, score, jnp.inf), axis=0))
        zero_part = zero_part + jnp.sum(jnp.where(kept >= 0.0, 1.0, 0.0), axis=0)
        return hi_part, lo_part, zero_part

    part_shape = (CHAINS, SUBLANES, tq)
    hi_part, lo_part, zero_part = lax.fori_loop(
        0, n_kb, score_block,
        (jnp.full(part_shape, -jnp.inf, F32), jnp.full(part_shape, jnp.inf, F32), jnp.zeros(part_shape, F32)))
    row_max = jnp.max(jnp.max(hi_part, axis=0), axis=0, keepdims=True)
    row_min = jnp.min(jnp.min(lo_part, axis=0), axis=0, keepdims=True)

    n_tail = (seq - n_kb * tk).astype(F32)

    def count(thr, strict):
        thr8 = rows8(thr)

        def body(j, acc):
            s = sc_ref[j]
            return acc + jnp.sum(jnp.where((s > thr8) if strict else (s >= thr8), 1.0, 0.0), axis=0)

        acc = lax.fori_loop(0, n_kb, body, jnp.zeros(part_shape, F32))
        tail_hit = (NEG_BIG > thr) if strict else (NEG_BIG >= thr)
        return jnp.sum(fold(acc), axis=0, keepdims=True) + jnp.where(tail_hit, n_tail, 0.0)

    n_causal = (t0 + 1 + lax.broadcasted_iota(I32, (1, tq), 1)).astype(F32)
    few = n_causal < topk
    lo0 = jnp.where(few, _float_to_ordered_int(jnp.full((1, tq), NEG_BIG, F32)), _float_to_ordered_int(row_min))
    cnt_lo0 = jnp.where(few, float(seq), n_causal)
    hi0 = _float_to_ordered_int(row_max) + 1
    log_k = float(np.log(topk))

    def log_excess(cnt):
        return jnp.log(jnp.maximum(cnt, 0.5)) - log_k

    def probe(it, bracket):
        lo, hi, cnt_lo, f_lo, f_hi, last, done_i = bracket
        done = done_i != 0
        it_v = jnp.zeros((1, tq), I32) + it
        lo_v = _ordered_int_to_float(lo)
        hi_v = _ordered_int_to_float(hi)
        guess = _float_to_ordered_int(lo_v + (f_lo / (f_lo - f_hi)) * (hi_v - lo_v))
        guess = jnp.where((it_v == 1) & (lo == 0) & (cnt_lo < topk + ZERO_TIE_MARGIN), 1, guess)
        middle = lo + lax.shift_right_logical(hi - lo, jnp.ones((1, tq), I32))
        usable = (guess > lo) & (guess < hi) & (it_v < SEARCH_FREE_PASSES)
        cand = jnp.where(usable, guess, middle)
        cnt = count(_ordered_int_to_float(cand), strict=False)
        f_c = log_excess(cnt)
        take = (cnt >= topk) & ~done
        drop = (cnt < topk) & ~done
        f_hi = jnp.where(take, jnp.where(last == 1, 0.5 * f_hi, f_hi), jnp.where(drop, f_c, f_hi))
        f_lo = jnp.where(drop, jnp.where(last == -1, 0.5 * f_lo, f_lo), jnp.where(take, f_c, f_lo))
        last = jnp.where(take, 1, jnp.where(drop, -1, last))
        lo = jnp.where(take, cand, lo)
        cnt_lo = jnp.where(take, cnt, cnt_lo)
        hi = jnp.where(drop, cand, hi)
        done = done | (cnt_lo == topk) | ((hi - lo) == 1)
        return lo, hi, cnt_lo, f_lo, f_hi, last, done.astype(I32)

    def n_active(bracket):
        return jnp.sum(jnp.where(bracket[6] != 0, 0.0, 1.0))

    def search_step(state):
        bracket = probe(state[0], state[2:])
        return (state[0] + 1, n_active(bracket)) + bracket

    def search_on(state):
        return (state[0] < SEARCH_MAX_PASSES) & (state[1] > 0.0)

    cnt_zero = jnp.sum(fold(zero_part), axis=0, keepdims=True)
    inside = (lo0 < 0) & (hi0 > 0) & ~few
    take0 = inside & (cnt_zero >= topk)
    drop0 = inside & (cnt_zero < topk)
    lo1 = jnp.where(take0, 0, lo0)
    hi1 = jnp.where(drop0, 0, hi0)
    cnt_lo1 = jnp.where(take0, cnt_zero, cnt_lo0)
    done1 = few | (cnt_lo1 == topk) | ((hi1 - lo1) == 1)
    bracket = (lo1, hi1, cnt_lo1, log_excess(cnt_lo1), log_excess(jnp.where(drop0, cnt_zero, 0.0)),
               jnp.where(take0, 1, jnp.where(drop0, -1, 0)), done1.astype(I32))
    bracket = lax.fori_loop(1, 1 + SEARCH_FIXED_PASSES, probe, bracket)
    state = lax.while_loop(search_on, search_step,
                           (jnp.int32(1 + SEARCH_FIXED_PASSES), n_active(bracket)) + bracket)
    thr = _ordered_int_to_float(state[2])
    cnt_thr = state[4]
    thr8 = rows8(thr)

    @pl.when(jnp.max(cnt_thr) > topk)
    def _():
        need = topk - count(thr, strict=True)
        tri_r = lax.broadcasted_iota(I32, (tk, tk), 0)
        tri_c = lax.broadcasted_iota(I32, (tk, tk), 1)
        earlier = (tri_c < tri_r).astype(BF16)

        def drop_late_ties(j, seen):
            s = sc_ref[j]
            eq = s == thr8
            eq_f = jnp.where(eq, 1.0, 0.0)
            rank = seen + jnp.dot(earlier, eq_f.reshape(tk, tq).astype(BF16), preferred_element_type=F32)
            sc_ref[j] = jnp.where(eq & (chunks(rank) >= rows8(need)), -jnp.inf, s)
            return seen + jnp.sum(fold(jnp.sum(eq_f, axis=0)), axis=0, keepdims=True)

        lax.fori_loop(0, n_kb, drop_late_ties, jnp.zeros((1, tq), F32))

    m_ref[...] = jnp.full(m_ref.shape, NEG_BIG, F32)
    l_ref[...] = jnp.zeros(l_ref.shape, F32)
    acc_ref[...] = jnp.zeros(acc_ref.shape, F32)
    pos_lane = lax.broadcasted_iota(I32, (tk, LANES), 1)
    pos_row = lax.broadcasted_iota(I32, (tk, LANES), 0)
    feat0 = jnp.where(pos_lane < POS_TERMS, pos_row // POS_RADIX,
                      jnp.where(pos_lane < 2 * POS_TERMS, pos_row % POS_RADIX, 0)).astype(F32)
    hi_lane = jnp.where(pos_lane < POS_TERMS, 1.0, 0.0)

    def attend(j, carry):
        k0 = pl.multiple_of(j * tk, tk)
        mask_ref[...] = jnp.where((sc_ref[j] >= thr8) & (k0 + key_off <= q_pos), 0.0, -jnp.inf)
        pos_feat = (feat0 + hi_lane * (j * (tk // POS_RADIX)).astype(F32)).astype(BF16)

        scores = []
        for p in range(SLABS_PER_TENSOR):
            keys = jnp.concatenate([ka_ref[p, pl.ds(k0, tk), :], pos_feat], axis=1)
            scores.append(lax.dot_general(keys, qh_ref[p], NT_DIMS, preferred_element_type=F32))
        weights = []
        for p in range(SLABS_PER_TENSOR):
            for hh in range(HEADS_PER_SLAB):
                s = chunks(scores[p][:, hh * tq:(hh + 1) * tq]) + mask_ref[...]
                m_old = m_ref[p, hh]
                m_blk = jnp.max(fold_max(jnp.max(s, axis=0)), axis=0, keepdims=True)
                m_new = jnp.maximum(m_old, rows8(m_blk))
                alpha = jnp.exp2(m_old - m_new)
                m_ref[p, hh] = m_new
                weights.append((alpha, jnp.exp2(s - m_new).reshape(tk, tq).astype(BF16)))
        ones_rows = jnp.ones((SUBLANES, tk), BF16)
        for p in range(SLABS_PER_TENSOR):
            for hh in range(HEADS_PER_SLAB):
                rows = slice(hh * HEAD_DIM_A, (hh + 1) * HEAD_DIM_A)
                alpha, probs = weights[p * HEADS_PER_SLAB + hh]
                values = jnp.concatenate([vt_ref[p, j, rows, :], ones_rows], axis=0)
                pv = jnp.dot(values, probs, preferred_element_type=F32)
                old = acc_ref[p, rows, :].reshape(HEAD_DIM_A // SUBLANES, SUBLANES, tq)
                acc_ref[p, rows, :] = (alpha * old).reshape(HEAD_DIM_A, tq) + pv[0:HEAD_DIM_A]
                l_ref[p, hh] = alpha * l_ref[p, hh] + pv[HEAD_DIM_A:HEAD_DIM_A + SUBLANES]
        return carry

    def fold_max(x):
        out = x[0]
        for c in range(1, CHAINS):
            out = jnp.maximum(out, x[c])
        return out

    lax.fori_loop(0, n_kb, attend, 0)

    for p in range(SLABS_PER_TENSOR):
        halves = []
        for hh in range(HEADS_PER_SLAB):
            rows = slice(hh * HEAD_DIM_A, (hh + 1) * HEAD_DIM_A)
            acc = acc_ref[p, rows, :].reshape(HEAD_DIM_A // SUBLANES, SUBLANES, tq)
            halves.append((acc / l_ref[p, hh]).reshape(HEAD_DIM_A, tq))
        o_ref[:, p * LANES:(p + 1) * LANES] = jnp.concatenate(halves, axis=0).T.astype(BF16)


def _dsa_attention(qk, v_t, q_idx, k4, gates, tq, tk):
    _, b, s, _ = qk.shape
    assert tq == LANES and s // POS_RADIX <= 256
    topk = min(TOPK_MAX, s // 4)
    once = pl.Buffered(1)
    groups = tk // (CHAINS * SUBLANES)
    body = functools.partial(_dsa_body, seq=s, topk=topk, tq=tq, tk=tk)
    return pl.pallas_call(
        body,
        grid=(b, s // tq),
        in_specs=[pl.BlockSpec((SLABS_PER_TENSOR, None, tq, LANES), lambda bi, qi: (0, bi, qi, 0)),
                  pl.BlockSpec((SLABS_PER_TENSOR, None, s, LANES), lambda bi, qi: (1, bi, 0, 0), pipeline_mode=once),
                  pl.BlockSpec((SLABS_PER_TENSOR, None, s // tk, LANES, tk), lambda bi, qi: (0, bi, 0, 0, 0),
                               pipeline_mode=once),
                  pl.BlockSpec((None, tq, N_IDX_HEADS * IDX_DIM), lambda bi, qi: (bi, qi, 0)),
                  pl.BlockSpec((None, s, LANES), lambda bi, qi: (bi, 0, 0), pipeline_mode=once),
                  pl.BlockSpec((None, tq, LANES), lambda bi, qi: (bi, qi, 0))],
        out_specs=pl.BlockSpec((None, tq, D_A), lambda bi, qi: (bi, qi, 0)),
        out_shape=jax.ShapeDtypeStruct((b, s, D_A), BF16),
        scratch_shapes=[pltpu.VMEM((s // tk, groups, CHAINS, SUBLANES, tq), F32),
                        pltpu.VMEM((N_IDX_HEADS * tq, LANES), BF16),
                        pltpu.VMEM((SLABS_PER_TENSOR, HEADS_PER_SLAB * tq, 2 * LANES), BF16),
                        pltpu.VMEM((groups, CHAINS, SUBLANES, tq), F32),
                        pltpu.VMEM((SLABS_PER_TENSOR, LANES, tq), F32),
                        pltpu.VMEM((SLABS_PER_TENSOR, HEADS_PER_SLAB, SUBLANES, tq), F32),
                        pltpu.VMEM((SLABS_PER_TENSOR, HEADS_PER_SLAB, SUBLANES, tq), F32)],
        compiler_params=_cparams(("arbitrary", "arbitrary"), 48),
        name="dsa_attention",
    )(qk, qk, v_t, q_idx, k4, gates)


def _log_sigmoid(x):
    return jnp.minimum(x, 0.0) - jnp.log(1.0 + jnp.exp(-jnp.abs(x)))


def _split3(a):
    hi = a.astype(BF16)
    r1 = a - hi.astype(F32)
    mid = r1.astype(BF16)
    lo = (r1 - mid.astype(F32)).astype(BF16)
    return hi, mid, lo


def _mlstm_body(qk_ref, v_ref, o_ref, g_ref, gt_ref, cw_ref, cb_ref, bg_ref, bgt_ref, gn_ref, y_ref,
                xe_ref, c_ref, m_ref, *, chunk, batch):
    @pl.when(pl.program_id(0) == 0)
    def _():
        xe_ref[:, 0:SUBLANES, :] = jnp.zeros((batch, SUBLANES, 2 * D_M), F32)
        c_ref[...] = jnp.zeros(c_ref.shape, F32)
        m_ref[...] = jnp.zeros(m_ref.shape, F32)

    gates = [_mlstm_gates(qk_ref.at[bi], g_ref.at[bi], gt_ref.at[bi], cw_ref, cb_ref, bg_ref, bgt_ref,
                          xe_ref.at[bi], chunk) for bi in range(batch)]
    for h in range(N_HEADS_M):
        for bi in range(batch):
            _mlstm_head(h, gates[bi], v_ref.at[bi], o_ref.at[bi], gn_ref, y_ref.at[bi], c_ref.at[bi], m_ref.at[bi],
                        chunk)


def _mlstm_gates(qk_ref, g_ref, gt_ref, cw_ref, cb_ref, bg_ref, bgt_ref, xe_ref, chunk):
    xe_ref[SUBLANES:SUBLANES + chunk, :] = qk_ref[...]
    conv = cb_ref[...]
    for j in range(CONV_K):
        start = SUBLANES - (CONV_K - 1) + j
        conv = conv + cw_ref[j:j + 1, :] * xe_ref[start:start + chunk, :]
    qk = conv * jax.nn.sigmoid(conv)
    xe_ref[0:SUBLANES, :] = xe_ref[chunk:chunk + SUBLANES, :]

    gb = g_ref[...] + bg_ref[...]
    gbt = gt_ref[...] + bgt_ref[...]
    r = lax.broadcasted_iota(I32, (chunk, chunk), 0)
    c = lax.broadcasted_iota(I32, (chunk, chunk), 1)
    causal = r >= c
    incl = causal.astype(BF16)
    incl_t = (r <= c).astype(BF16)
    cum = jnp.zeros((chunk, LANES), F32)
    for term in _split3(_log_sigmoid(gb)):
        cum = cum + jnp.dot(incl, term, preferred_element_type=F32)
    cum_t = jnp.zeros((2 * N_HEADS_M, chunk), F32)
    for term in _split3(_log_sigmoid(gbt)):
        cum_t = cum_t + jnp.dot(term, incl_t, preferred_element_type=F32)

    return qk, gb, gbt, cum, cum_t, causal


def _mlstm_head(h, gates, v_ref, o_ref, gn_ref, y_ref, c_ref, m_ref, chunk):
    qk, gb, gbt, cum, cum_t, causal = gates
    lane = lax.broadcasted_iota(I32, (chunk, LANES), 1)
    ones_col = jnp.where(lane == 0, 1.0, 0.0).astype(BF16)
    sl = slice(h * HEAD_DIM_M, (h + 1) * HEAD_DIM_M)
    q_h = (qk[:, sl] * (HEAD_DIM_M ** -0.5)).astype(BF16)
    k_f = qk[:, D_M + h * HEAD_DIM_M:D_M + (h + 1) * HEAD_DIM_M]
    v_aug = jnp.concatenate([v_ref[:, sl], ones_col], axis=1)
    b_col = cum[:, G_F + h:G_F + h + 1]
    i_col = gb[:, G_I + h:G_I + h + 1]
    b_row = cum_t[N_HEADS_M + h:N_HEADS_M + h + 1, :]
    i_row = gbt[h:h + 1, :]
    m_prev = m_ref[h:h + 1, 0:1]
    c_aug = c_ref[h]

    d_log = jnp.where(causal, b_col - b_row + i_row, -jnp.inf)
    inter = b_col + m_prev
    m_t = jnp.maximum(inter, jnp.max(d_log, axis=1, keepdims=True))
    qk_t = lax.dot_general(q_h, k_f.astype(BF16), NT_DIMS, preferred_element_type=F32)
    w_intra = qk_t * jnp.exp(d_log - m_t)
    w_inter = jnp.exp(inter - m_t)
    intra = jnp.dot(w_intra.astype(BF16), v_aug[:, 0:HEAD_DIM_M], preferred_element_type=F32)
    carried = jnp.dot(q_h, c_aug.astype(BF16), preferred_element_type=F32)
    num = intra + w_inter * carried[:, 0:HEAD_DIM_M]
    den = jnp.sum(w_intra, axis=1, keepdims=True) + w_inter * carried[:, HEAD_DIM_M:HEAD_DIM_M + 1]
    hid = num / jnp.maximum(jnp.abs(den), jnp.exp(-m_t))

    mu = jnp.mean(hid, axis=1, keepdims=True)
    dev = hid - mu
    var = jnp.mean(dev * dev, axis=1, keepdims=True)
    normed = dev * lax.rsqrt(var + LN_EPS) * gn_ref[:, sl]
    y_ref[:, sl] = (jax.nn.sigmoid(o_ref[:, sl]) * normed).astype(BF16)

    b_last = b_col[chunk - 1:chunk, :]
    w_log = b_last - b_col + i_col
    m_new = jnp.maximum(b_last + m_prev, jnp.max(w_log, axis=0, keepdims=True))
    decay = jnp.exp(b_last + m_prev - m_new)
    kw_t = (k_f * jnp.exp(w_log - m_new)).T.astype(BF16)
    c_ref[h] = decay * c_aug + jnp.dot(kw_t, v_aug, preferred_element_type=F32)
    m_ref[h:h + 1, :] = jnp.broadcast_to(m_new, (1, LANES))


def _mlstm(qk_m, v_m, o_m, gates, gates_t, conv_w, conv_b, bias_slab, bias_col, gn_w, chunk):
    b, s, _ = qk_m.shape
    body = functools.partial(_mlstm_body, chunk=chunk, batch=b)
    const = lambda ci: (0, 0)
    return pl.pallas_call(
        body,
        grid=(s // chunk,),
        in_specs=[pl.BlockSpec((b, chunk, 2 * D_M), lambda ci: (0, ci, 0)),
                  pl.BlockSpec((b, chunk, D_M), lambda ci: (0, ci, 0)),
                  pl.BlockSpec((b, chunk, D_M), lambda ci: (0, ci, 0)),
                  pl.BlockSpec((b, chunk, LANES), lambda ci: (0, ci, 0)),
                  pl.BlockSpec((b, 2 * N_HEADS_M, chunk), lambda ci: (0, 0, ci)),
                  pl.BlockSpec((CONV_K, 2 * D_M), const),
                  pl.BlockSpec((1, 2 * D_M), const),
                  pl.BlockSpec((1, LANES), const),
                  pl.BlockSpec((2 * N_HEADS_M, 1), const),
                  pl.BlockSpec((1, D_M), const)],
        out_specs=pl.BlockSpec((b, chunk, D_M), lambda ci: (0, ci, 0)),
        out_shape=jax.ShapeDtypeStruct((b, s, D_M), BF16),
        scratch_shapes=[pltpu.VMEM((b, chunk + SUBLANES, 2 * D_M), F32),
                        pltpu.VMEM((b, N_HEADS_M, HEAD_DIM_M, 2 * LANES), F32),
                        pltpu.VMEM((b, SUBLANES, LANES), F32)],
        compiler_params=_cparams(("arbitrary",), 32),
        name="mlstm",
    )(qk_m, v_m, o_m, gates, gates_t, conv_w, conv_b, bias_slab, bias_col, gn_w)


COMB_GROUP_LANE = N_EXPERTS


def _route(logits):
    lane = lax.broadcasted_iota(I32, logits.shape, 1).astype(F32)
    g_mask = (lane >= N_EXPERTS) & (lane < N_EXPERTS + N_GROUPS)
    g_logit = jnp.where(g_mask, logits, -jnp.inf)
    g_exp = jnp.exp(g_logit - jnp.max(g_logit, axis=1, keepdims=True))
    g_prob = g_exp / jnp.sum(g_exp, axis=1, keepdims=True)
    g_p = jnp.max(g_prob, axis=1, keepdims=True)
    g_sel = jnp.min(jnp.where(g_mask & (g_prob == g_p), lane, float(LANES)), axis=1, keepdims=True) - N_EXPERTS
    e_mask = (lane >= g_sel * EXPERTS_PER_GROUP) & (lane < (g_sel + 1.0) * EXPERTS_PER_GROUP)
    e_logit = jnp.where(e_mask, logits, -jnp.inf)
    e_exp = jnp.exp(e_logit - jnp.max(e_logit, axis=1, keepdims=True))
    e_prob = e_exp / jnp.sum(e_exp, axis=1, keepdims=True)
    p1 = jnp.max(jnp.where(e_mask, e_prob, -1.0), axis=1, keepdims=True)
    i1 = jnp.min(jnp.where(e_mask & (e_prob == p1), lane, float(LANES)), axis=1, keepdims=True)
    rest = e_mask & (lane != i1)
    p2 = jnp.max(jnp.where(rest, e_prob, -1.0), axis=1, keepdims=True)
    i2 = jnp.min(jnp.where(rest & (e_prob == p2), lane, float(LANES)), axis=1, keepdims=True)
    total = p1 + p2
    comb = (jnp.where(lane == i1, p1 / total, 0.0) + jnp.where(lane == i2, p2 / total, 0.0)) * g_p
    return comb + jnp.where(lane == COMB_GROUP_LANE, g_sel, 0.0)


def _mix_body(ya_ref, ym_ref, x_ref, wo_ref, g_ref, b_ref, wr_ref, br_ref, x1_ref, comb_ref, *, alpha):
    mix = (jnp.dot(ya_ref[...], wo_ref[0:D_A, :], preferred_element_type=F32)
           + jnp.dot(ym_ref[...], wo_ref[D_A:D_A + D_M, :], preferred_element_type=F32))
    x1 = _layer_norm(alpha * x_ref[...] + mix, g_ref[...], b_ref[...])
    x1_ref[...] = x1
    x_hi = x1.astype(BF16)
    x_lo = (x1 - x_hi.astype(F32)).astype(BF16)
    w_hi = wr_ref[0]
    w_lo = wr_ref[1]
    logits = (jnp.dot(x_hi, w_hi, preferred_element_type=F32) + jnp.dot(x_lo, w_hi, preferred_element_type=F32)
              + jnp.dot(x_hi, w_lo, preferred_element_type=F32)) + br_ref[...]
    comb_ref[...] = _route(logits)


def _mix_ln_router(y_a, y_m, x2d, w_out, ln_g, ln_b, w_router, b_router, alpha, tm):
    n, d = x2d.shape
    const = lambda i: (0, 0)
    return pl.pallas_call(
        functools.partial(_mix_body, alpha=alpha),
        grid=(n // tm,),
        in_specs=[pl.BlockSpec((tm, D_A), lambda i: (i, 0)),
                  pl.BlockSpec((tm, D_M), lambda i: (i, 0)),
                  pl.BlockSpec((tm, d), lambda i: (i, 0)),
                  pl.BlockSpec((D_A + D_M, d), const),
                  pl.BlockSpec((1, d), const),
                  pl.BlockSpec((1, d), const),
                  pl.BlockSpec((2, d, LANES), lambda i: (0, 0, 0)),
                  pl.BlockSpec((1, LANES), const)],
        out_specs=[pl.BlockSpec((tm, d), lambda i: (i, 0)),
                   pl.BlockSpec((tm, LANES), lambda i: (i, 0))],
        out_shape=[jax.ShapeDtypeStruct((n, d), F32), jax.ShapeDtypeStruct((n, LANES), F32)],
        compiler_params=_cparams(("arbitrary",), 32),
        name="mix_ln_router",
    )(y_a, y_m, x2d, w_out, ln_g, ln_b, w_router, b_router)


MOE_ROWS = TM_MOE // N_GROUPS + 32


def _moe_body(x1_ref, comb_ref, wg_ref, wu_ref, wd_ref, g_ref, b_ref, p_ref, wpg_ref, wpp_ref, y_ref,
              xb_ref, before_ref, cs_ref, row_ref, *, alpha, tm):
    g = pl.program_id(1)
    g_f = g.astype(F32)

    @pl.when((pl.program_id(0) == 0) & (g == 0))
    def _():
        r = lax.broadcasted_iota(I32, (tm, tm), 0)
        c = lax.broadcasted_iota(I32, (tm, tm), 1)
        before_ref[...] = (c < r).astype(BF16)

    @pl.when(g == 0)
    def _():
        xb_ref[...] = x1_ref[...].astype(BF16)
        y_ref[...] = jnp.zeros(y_ref.shape, F32)
        comb = comb_ref[...]
        for k, term in enumerate(_split3(comb)):
            cs_ref[k] = term
        group_row = comb.T[COMB_GROUP_LANE:COMB_GROUP_LANE + 1, :]
        sub = lax.broadcasted_iota(I32, (SUBLANES, tm), 0)
        member_t = (sub.astype(F32) == group_row) & (sub < N_GROUPS)
        ahead_t = lax.dot_general(jnp.where(member_t, 1.0, 0.0).astype(BF16), before_ref[...], NT_DIMS,
                                  preferred_element_type=F32)
        rank_row = jnp.sum(jnp.where(member_t, ahead_t, 0.0), axis=0, keepdims=True)
        row_ref[0] = jnp.broadcast_to(group_row, (SUBLANES, tm))
        row_ref[1] = jnp.broadcast_to(rank_row, (SUBLANES, tm))

    in_group_row = row_ref[0, 0:1, :] == g_f
    n_tokens = jnp.sum(jnp.where(in_group_row, 1.0, 0.0))
    n_blocks = (n_tokens.astype(I32) + MOE_ROWS - 1) // MOE_ROWS
    slot = lax.broadcasted_iota(I32, (MOE_ROWS, tm), 0).astype(F32)
    comb_lane = lax.broadcasted_iota(I32, (MOE_ROWS, LANES), 1)

    def block(b, carry):
        base = (b * MOE_ROWS).astype(F32)
        gather = jnp.where(in_group_row & (row_ref[1, 0:1, :] - base == slot), 1.0, 0.0).astype(BF16)
        xc = jnp.dot(gather, xb_ref[...], preferred_element_type=F32).astype(BF16)
        comb_c = jnp.zeros((MOE_ROWS, LANES), F32)
        for k in range(3):
            comb_c = comb_c + jnp.dot(gather, cs_ref[k], preferred_element_type=F32)
        yc = jnp.zeros((MOE_ROWS, x1_ref.shape[1]), F32)
        for e in range(EXPERTS_PER_GROUP):
            gate = jnp.dot(xc, wg_ref[e], preferred_element_type=F32)
            up = jnp.dot(xc, wu_ref[e], preferred_element_type=F32)
            weight = jnp.sum(jnp.where(comb_lane == g * EXPERTS_PER_GROUP + e, comb_c, 0.0), axis=1, keepdims=True)
            hidden = gate * jax.nn.sigmoid(gate) * up * weight
            yc = yc + jnp.dot(hidden.astype(BF16), wd_ref[e], preferred_element_type=F32)
        yc_hi = yc.astype(BF16)
        yc_lo = (yc - yc_hi.astype(F32)).astype(BF16)
        tn = (((0,), (0,)), ((), ()))
        y_ref[...] += (lax.dot_general(gather, yc_hi, tn, preferred_element_type=F32)
                       + lax.dot_general(gather, yc_lo, tn, preferred_element_type=F32))
        return carry

    lax.fori_loop(0, n_blocks, block, 0)

    @pl.when(g == N_GROUPS - 1)
    def _():
        x2 = _layer_norm(alpha * x1_ref[...] + y_ref[...], g_ref[...], b_ref[...])
        gate = jax.nn.sigmoid(jnp.dot(x2.astype(BF16), wpg_ref[...], preferred_element_type=F32))
        proj = jnp.dot(p_ref[...].astype(BF16), wpp_ref[...], preferred_element_type=F32)
        y_ref[...] = x2 + gate * proj


def _moe_ln_ple(x1, comb, w_gate, w_up, w_down, ln_g, ln_b, p2d, w_ple_gate, w_ple_proj, alpha, tm):
    n, d = x1.shape
    dp = p2d.shape[1]
    const = lambda i, g: (0, 0)
    once = pl.Buffered(1)
    grouped = lambda w: w.reshape((N_GROUPS, EXPERTS_PER_GROUP) + w.shape[1:])
    return pl.pallas_call(
        functools.partial(_moe_body, alpha=alpha, tm=tm),
        grid=(n // tm, N_GROUPS),
        in_specs=[pl.BlockSpec((tm, d), lambda i, g: (i, 0), pipeline_mode=once),
                  pl.BlockSpec((tm, LANES), lambda i, g: (i, 0)),
                  pl.BlockSpec((None, EXPERTS_PER_GROUP, d, D_EXPERT), lambda i, g: (g, 0, 0, 0)),
                  pl.BlockSpec((None, EXPERTS_PER_GROUP, d, D_EXPERT), lambda i, g: (g, 0, 0, 0)),
                  pl.BlockSpec((None, EXPERTS_PER_GROUP, D_EXPERT, d), lambda i, g: (g, 0, 0, 0)),
                  pl.BlockSpec((1, d), const),
                  pl.BlockSpec((1, d), const),
                  pl.BlockSpec((tm, dp), lambda i, g: (i, 0), pipeline_mode=once),
                  pl.BlockSpec((d, d), const, pipeline_mode=once),
                  pl.BlockSpec((dp, d), const, pipeline_mode=once)],
        out_specs=pl.BlockSpec((tm, d), lambda i, g: (i, 0)),
        out_shape=jax.ShapeDtypeStruct((n, d), F32),
        scratch_shapes=[pltpu.VMEM((tm, d), BF16),
                        pltpu.VMEM((tm, tm), BF16),
                        pltpu.VMEM((3, tm, LANES), BF16),
                        pltpu.VMEM((2, SUBLANES, tm), F32)],
        compiler_params=_cparams(("arbitrary", "arbitrary"), 58),
        name="moe_ln_ple",
    )(x1, comb, grouped(w_gate), grouped(w_up), grouped(w_down), ln_g, ln_b, p2d, w_ple_gate, w_ple_proj)


def _tile(n, want):
    t = min(want, n)
    assert n % t == 0
    return t


def _layer(x, p_i, w_in, b_igate, b_fgate, conv_w, conv_b, gn_w, w_out, ln1_g, ln1_b,
           rg_w, rg_b, re_w, re_b, e_gate, e_up, e_down, ln2_g, ln2_b, ple_gate_w, ple_proj_w, alpha):
    b, s, d = x.shape
    n = b * s
    x2d = x.reshape(n, d)

    tk = _tile(s, TK_DSA)
    w_packed, w_vt = _pack_w_in(w_in)
    qk, v_t, q_idx, k4, gates, qk_m, v_m, o_m = _in_proj(x2d, w_packed, w_vt, tk)

    gates3 = gates.reshape(b, s, LANES)
    y_a = _dsa_attention(qk.reshape(2 * SLABS_PER_TENSOR, b, s, LANES),
                         v_t.reshape(SLABS_PER_TENSOR, b, s // tk, LANES, tk),
                         q_idx.reshape(b, s, N_IDX_HEADS * IDX_DIM), k4.reshape(b, s, LANES), gates3,
                         _tile(s, TQ_DSA), tk)

    gate_bias = jnp.concatenate([b_igate, b_fgate]).astype(F32)
    bias_slab = jnp.zeros((1, LANES), F32).at[0, G_I:G_I + 2 * N_HEADS_M].set(gate_bias)
    gates_t = jnp.swapaxes(gates3[:, :, G_I:G_I + 2 * N_HEADS_M], 1, 2)
    y_m = _mlstm(qk_m.reshape(b, s, 2 * D_M), v_m.reshape(b, s, D_M), o_m.reshape(b, s, D_M), gates3, gates_t,
                 conv_w, conv_b.reshape(1, -1), bias_slab, gate_bias.reshape(-1, 1), gn_w.reshape(1, -1),
                 _tile(s, CHUNK_MLSTM))

    w_router = jnp.concatenate([re_w, rg_w, jnp.zeros((d, LANES - N_EXPERTS - N_GROUPS), F32)], axis=1)
    w_router_hi = w_router.astype(BF16)
    w_router = jnp.stack([w_router_hi, (w_router - w_router_hi.astype(F32)).astype(BF16)])
    b_router = jnp.concatenate([re_b, rg_b, jnp.zeros((LANES - N_EXPERTS - N_GROUPS,), F32)]).reshape(1, LANES)
    x1, comb = _mix_ln_router(y_a.reshape(n, D_A), y_m.reshape(n, D_M), x2d, w_out.astype(BF16),
                              ln1_g.reshape(1, d), ln1_b.reshape(1, d), w_router, b_router, alpha, _tile(n, TM_MIX))

    out = _moe_ln_ple(x1, comb, e_gate.astype(BF16), e_up.astype(BF16), e_down.astype(BF16),
                      ln2_g.reshape(1, d), ln2_b.reshape(1, d), p_i.reshape(n, -1), ple_gate_w.astype(BF16),
                      ple_proj_w.astype(BF16), alpha, _tile(n, TM_MOE))
    return out.reshape(b, s, d)


def kernel(x, p, w_in, b_igate, b_fgate, conv_w, conv_b, gn_w, w_out, ln1_g, ln1_b, router_group_w, router_group_b,
           router_expert_w, router_expert_b, expert_w_gate, expert_w_up, expert_w_down, ln2_g, ln2_b, ple_gate_w,
           ple_proj_w):
    depth = w_in.shape[0]
    alpha = (2.0 * depth) ** 0.25
    for i in range(depth):
        x = _layer(x, p[i], w_in[i], b_igate[i], b_fgate[i], conv_w[i], conv_b[i], gn_w[i], w_out[i], ln1_g[i],
                   ln1_b[i], router_group_w[i], router_group_b[i], router_expert_w[i], router_expert_b[i],
                   expert_w_gate[i], expert_w_up[i], expert_w_down[i], ln2_g[i], ln2_b[i], ple_gate_w[i],
                   ple_proj_w[i], alpha)
    return x
```

```python
import functools

import numpy as np
import jax
import jax.numpy as jnp
from jax import lax
from jax.experimental import pallas as pl
from jax.experimental.pallas import tpu as pltpu

F32 = jnp.float32
BF16 = jnp.bfloat16
I32 = jnp.int32

N_HEADS_A = 8
HEAD_DIM_A = 64
D_A = N_HEADS_A * HEAD_DIM_A
N_IDX_HEADS = 8
IDX_DIM = 32
TOPK_MAX = 256
N_HEADS_M = 4
HEAD_DIM_M = 128
D_M = N_HEADS_M * HEAD_DIM_M
CONV_K = 4
N_GROUPS = 4
EXPERTS_PER_GROUP = 8
N_EXPERTS = N_GROUPS * EXPERTS_PER_GROUP
D_EXPERT = 256
LN_EPS = 1e-5
NEG_BIG = -1e30
IN_SPLITS = (D_A, D_A, D_A, N_IDX_HEADS * IDX_DIM, IDX_DIM, N_IDX_HEADS, D_M, D_M, D_M, D_M, N_HEADS_M, N_HEADS_M)

LANES = 128
SUBLANES = 8
MIB = 1024 * 1024

C_QK = 0
C_QIDX = C_QK + 2 * D_A
C_K4 = C_QIDX + N_IDX_HEADS * IDX_DIM
C_GATE = C_K4 + LANES
C_QKM = C_GATE + LANES
C_VM = C_QKM + 2 * D_M
C_OM = C_VM + D_M
C_END = C_OM + D_M
SLABS_PER_TENSOR = D_A // LANES
HEADS_PER_SLAB = LANES // HEAD_DIM_A
G_W = 0
G_I = N_IDX_HEADS
G_F = G_I + N_HEADS_M

INT_MIN = -2 ** 31

TQ_DSA = LANES
TK_DSA = 1024
TM_IN_PROJ = TK_DSA
CHUNK_MLSTM = 256
TM_MIX = 512
TM_MOE = 1024

NT_DIMS = (((1,), (1,)), ((), ()))


def _cparams(semantics, vmem_mib):
    return pltpu.CompilerParams(dimension_semantics=semantics, vmem_limit_bytes=int(vmem_mib * MIB))


def _layer_norm(r, g, b):
    mu = jnp.mean(r, axis=-1, keepdims=True)
    d = r - mu
    var = jnp.mean(d * d, axis=-1, keepdims=True)
    return d * lax.rsqrt(var + LN_EPS) * g + b


def _pack_w_in(w_in):
    d = w_in.shape[0]
    cuts = [int(c) for c in np.cumsum(IN_SPLITS)[:-1]]
    q_a, k_a, v_a, q_idx, k_idx, w_idx, q_m, k_m, v_m, o_m, i_m, f_m = jnp.split(w_in, cuts, axis=1)
    k4 = jnp.tile(k_idx, (1, LANES // IDX_DIM))
    gate = jnp.concatenate([w_idx, i_m, f_m, jnp.zeros((d, LANES - G_F - N_HEADS_M), w_in.dtype)], axis=1)
    packed = jnp.concatenate([q_a, k_a, q_idx, k4, gate, q_m, k_m, v_m, o_m], axis=1).astype(BF16)
    return packed, v_a.T.astype(BF16)


def _in_proj_body(x_ref, w_ref, wvt_ref, qk_ref, vt_ref, qi_ref, k4_ref, g_ref, qkm_ref, vm_ref, om_ref):
    xb = x_ref[...].astype(BF16)

    def proj(lo, hi):
        return jnp.dot(xb, w_ref[:, lo:hi], preferred_element_type=F32)

    qk = proj(C_QK, C_QIDX).astype(BF16)
    for c in range(2 * SLABS_PER_TENSOR):
        qk_ref[c] = qk[:, c * LANES:(c + 1) * LANES]
    v_t = lax.dot_general(wvt_ref[...], xb, NT_DIMS, preferred_element_type=F32).astype(BF16)
    for c in range(SLABS_PER_TENSOR):
        vt_ref[c, 0] = v_t[c * LANES:(c + 1) * LANES, :]
    qi_ref[...] = proj(C_QIDX, C_K4).astype(BF16)
    k4_ref[...] = proj(C_K4, C_GATE).astype(BF16)
    g_ref[...] = proj(C_GATE, C_QKM)
    qkm_ref[...] = proj(C_QKM, C_VM)
    vm_ref[...] = proj(C_VM, C_OM).astype(BF16)
    om_ref[...] = proj(C_OM, C_END)


def _in_proj(x2d, w_packed, w_vt, tm):
    n, d = x2d.shape
    widths = [(C_K4 - C_QIDX, BF16), (C_GATE - C_K4, BF16), (C_QKM - C_GATE, F32),
              (C_VM - C_QKM, F32), (C_OM - C_VM, BF16), (C_END - C_OM, F32)]
    return pl.pallas_call(
        _in_proj_body,
        grid=(n // tm,),
        in_specs=[pl.BlockSpec((tm, d), lambda i: (i, 0)),
                  pl.BlockSpec((d, C_END), lambda i: (0, 0)),
                  pl.BlockSpec((D_A, d), lambda i: (0, 0))],
        out_specs=([pl.BlockSpec((2 * SLABS_PER_TENSOR, tm, LANES), lambda i: (0, i, 0)),
                    pl.BlockSpec((SLABS_PER_TENSOR, 1, LANES, tm), lambda i: (0, i, 0, 0))]
                   + [pl.BlockSpec((tm, w), lambda i: (i, 0)) for w, _ in widths]),
        out_shape=([jax.ShapeDtypeStruct((2 * SLABS_PER_TENSOR, n, LANES), BF16),
                    jax.ShapeDtypeStruct((SLABS_PER_TENSOR, n // tm, LANES, tm), BF16)]
                   + [jax.ShapeDtypeStruct((n, w), dt) for w, dt in widths]),
        compiler_params=_cparams(("arbitrary",), 48),
        name="in_proj",
    )(x2d, w_packed, w_vt)


def _float_to_ordered_int(value):
    bits = lax.bitcast_convert_type(value, I32)
    return bits ^ ((bits >> 31) & jnp.int32(0x7FFFFFFF))


def _ordered_int_to_float(key):
    bits = key ^ ((key >> 31) & jnp.int32(0x7FFFFFFF))
    return lax.bitcast_convert_type(bits, F32)


SEARCH_FIXED_PASSES = 12
SEARCH_FREE_PASSES = 16
SEARCH_MAX_PASSES = SEARCH_FREE_PASSES + 33
ZERO_TIE_MARGIN = 128
CHAINS = 4
POS_RADIX = 64
POS_TERMS = 3
LOG2_E = float(np.log2(np.e))


def _dsa_body(qa_ref, ka_ref, vt_ref, qi_ref, k4_ref, g_ref, o_ref,
              sc_ref, qm_ref, qh_ref, mask_ref, acc_ref, m_ref, l_ref, *, seq, topk, tq, tk):
    t0 = pl.program_id(1) * tq
    n_kb = (t0 + tq + tk - 1) // tk
    groups = tk // (CHAINS * SUBLANES)
    shape4 = (groups, CHAINS, SUBLANES, tq)

    def chunks(x):
        return x.reshape(shape4)

    def fold(x):
        out = x[0]
        for c in range(1, CHAINS):
            out = out + x[c]
        return out

    def rows8(row):
        return jnp.broadcast_to(row, (SUBLANES, tq))

    key_off = ((lax.broadcasted_iota(I32, shape4, 0) * CHAINS + lax.broadcasted_iota(I32, shape4, 1)) * SUBLANES
               + lax.broadcasted_iota(I32, shape4, 2))
    q_pos = t0 + lax.broadcasted_iota(I32, shape4, 3)
    lane = lax.broadcasted_iota(I32, (tq, LANES), 1)

    for h in range(N_IDX_HEADS):
        per_slab = LANES // IDX_DIM
        slab = qi_ref[:, (h // per_slab) * LANES:(h // per_slab + 1) * LANES]
        lo = (h % per_slab) * IDX_DIM
        qm_ref[h * tq:(h + 1) * tq, :] = jnp.where((lane >= lo) & (lane < lo + IDX_DIM), slab, jnp.zeros_like(slab))
    for h in range(N_HEADS_A):
        p, hh = divmod(h, HEADS_PER_SLAB)
        slope = 2.0 ** (-8.0 * (h + 1) / N_HEADS_A)
        slab = qa_ref[p].astype(F32) * (HEAD_DIM_A ** -0.5 * LOG2_E)
        in_head = (lane >= hh * HEAD_DIM_A) & (lane < (hh + 1) * HEAD_DIM_A)
        qh_ref[p, hh * tq:(hh + 1) * tq, 0:LANES] = jnp.where(in_head, slab, 0.0).astype(BF16)
        weight = jnp.where(lane < POS_TERMS, slope * POS_RADIX * LOG2_E,
                           jnp.where(lane < 2 * POS_TERMS, slope * LOG2_E, 0.0))
        terms = _split3(weight)
        feat = jnp.where(lane % POS_TERMS == 0, terms[0], jnp.where(lane % POS_TERMS == 1, terms[1], terms[2]))
        qh_ref[p, hh * tq:(hh + 1) * tq, LANES:2 * LANES] = feat
    w_rows = g_ref[...].T[G_W:G_W + N_IDX_HEADS, :] * ((IDX_DIM * N_IDX_HEADS) ** -0.5)

    def score_block(j, carry):
        hi_part, lo_part, zero_part = carry
        k0 = pl.multiple_of(j * tk, tk)
        logits = lax.dot_general(k4_ref[pl.ds(k0, tk), :], qm_ref[...], NT_DIMS,
                                 preferred_element_type=F32)
        score = jnp.zeros(shape4, F32)
        for h in range(N_IDX_HEADS):
            score = score + rows8(w_rows[h:h + 1, :]) * jnp.maximum(chunks(logits[:, h * tq:(h + 1) * tq]), 0.0)
        causal = k0 + key_off <= q_pos
        kept = jnp.where(causal, score, NEG_BIG)
        sc_ref[j] = kept
        hi_part = jnp.maximum(hi_part, jnp.max(kept, axis=0))
        lo_part = jnp.minimum(lo_part, jnp.min(jnp.where(causal, score, jnp.inf), axis=0))
        zero_part = zero_part + jnp.sum(jnp.where(kept >= 0.0, 1.0, 0.0), axis=0)
        return hi_part, lo_part, zero_part

    part_shape = (CHAINS, SUBLANES, tq)
    hi_part, lo_part, zero_part = lax.fori_loop(
        0, n_kb, score_block,
        (jnp.full(part_shape, -jnp.inf, F32), jnp.full(part_shape, jnp.inf, F32), jnp.zeros(part_shape, F32)))
    row_max = jnp.max(jnp.max(hi_part, axis=0), axis=0, keepdims=True)
    row_min = jnp.min(jnp.min(lo_part, axis=0), axis=0, keepdims=True)

    n_tail = (seq - n_kb * tk).astype(F32)

    def count(thr, strict):
        thr8 = rows8(thr)

        def body(j, acc):
            s = sc_ref[j]
            return acc + jnp.sum(jnp.where((s > thr8) if strict else (s >= thr8), 1.0, 0.0), axis=0)

        acc = lax.fori_loop(0, n_kb, body, jnp.zeros(part_shape, F32))
        tail_hit = (NEG_BIG > thr) if strict else (NEG_BIG >= thr)
        return jnp.sum(fold(acc), axis=0, keepdims=True) + jnp.where(tail_hit, n_tail, 0.0)

    n_causal = (t0 + 1 + lax.broadcasted_iota(I32, (1, tq), 1)).astype(F32)
    few = n_causal < topk
    lo0 = jnp.where(few, _float_to_ordered_int(jnp.full((1, tq), NEG_BIG, F32)), _float_to_ordered_int(row_min))
    cnt_lo0 = jnp.where(few, float(seq), n_causal)
    hi0 = _float_to_ordered_int(row_max) + 1
    log_k = float(np.log(topk))

    def log_excess(cnt):
        return jnp.log(jnp.maximum(cnt, 0.5)) - log_k

    def probe(it, bracket):
        lo, hi, cnt_lo, f_lo, f_hi, last, done_i = bracket
        done = done_i != 0
        it_v = jnp.zeros((1, tq), I32) + it
        lo_v = _ordered_int_to_float(lo)
        hi_v = _ordered_int_to_float(hi)
        guess = _float_to_ordered_int(lo_v + (f_lo / (f_lo - f_hi)) * (hi_v - lo_v))
        guess = jnp.where((it_v == 1) & (lo == 0) & (cnt_lo < topk + ZERO_TIE_MARGIN), 1, guess)
        middle = lo + lax.shift_right_logical(hi - lo, jnp.ones((1, tq), I32))
        usable = (guess > lo) & (guess < hi) & (it_v < SEARCH_FREE_PASSES)
        cand = jnp.where(usable, guess, middle)
        cnt = count(_ordered_int_to_float(cand), strict=False)
        f_c = log_excess(cnt)
        take = (cnt >= topk) & ~done
        drop = (cnt < topk) & ~done
        f_hi = jnp.where(take, jnp.where(last == 1, 0.5 * f_hi, f_hi), jnp.where(drop, f_c, f_hi))
        f_lo = jnp.where(drop, jnp.where(last == -1, 0.5 * f_lo, f_lo), jnp.where(take, f_c, f_lo))
        last = jnp.where(take, 1, jnp.where(drop, -1, last))
        lo = jnp.where(take, cand, lo)
        cnt_lo = jnp.where(take, cnt, cnt_lo)
        hi = jnp.where(drop, cand, hi)
        done = done | (cnt_lo == topk) | ((hi - lo) == 1)
        return lo, hi, cnt_lo, f_lo, f_hi, last, done.astype(I32)

    def n_active(bracket):
        return jnp.sum(jnp.where(bracket[6] != 0, 0.0, 1.0))

    def search_step(state):
        bracket = probe(state[0], state[2:])
        return (state[0] + 1, n_active(bracket)) + bracket

    def search_on(state):
        return (state[0] < SEARCH_MAX_PASSES) & (state[1] > 0.0)

    cnt_zero = jnp.sum(fold(zero_part), axis=0, keepdims=True)
    inside = (lo0 < 0) & (hi0 > 0) & ~few
    take0 = inside & (cnt_zero >= topk)
    drop0 = inside & (cnt_zero < topk)
    lo1 = jnp.where(take0, 0, lo0)
    hi1 = jnp.where(drop0, 0, hi0)
    cnt_lo1 = jnp.where(take0, cnt_zero, cnt_lo0)
    done1 = few | (cnt_lo1 == topk) | ((hi1 - lo1) == 1)
    bracket = (lo1, hi1, cnt_lo1, log_excess(cnt_lo1), log_excess(jnp.where(drop0, cnt_zero, 0.0)),
               jnp.where(take0, 1, jnp.where(drop0, -1, 0)), done1.astype(I32))
    bracket = lax.fori_loop(1, 1 + SEARCH_FIXED_PASSES, probe, bracket)
    state = lax.while_loop(search_on, search_step,
                           (jnp.int32(1 + SEARCH_FIXED_PASSES), n_active(bracket)) + bracket)
    thr = _ordered_int_to_float(state[2])
    cnt_thr = state[4]
    thr8 = rows8(thr)

    @pl.when(jnp.max(cnt_thr) > topk)
    def _():
        need = topk - count(thr, strict=True)
        tri_r = lax.broadcasted_iota(I32, (tk, tk), 0)
        tri_c = lax.broadcasted_iota(I32, (tk, tk), 1)
        earlier = (tri_c < tri_r).astype(BF16)

        def drop_late_ties(j, seen):
            s = sc_ref[j]
            eq = s == thr8
            eq_f = jnp.where(eq, 1.0, 0.0)
            rank = seen + jnp.dot(earlier, eq_f.reshape(tk, tq).astype(BF16), preferred_element_type=F32)
            sc_ref[j] = jnp.where(eq & (chunks(rank) >= rows8(need)), -jnp.inf, s)
            return seen + jnp.sum(fold(jnp.sum(eq_f, axis=0)), axis=0, keepdims=True)

        lax.fori_loop(0, n_kb, drop_late_ties, jnp.zeros((1, tq), F32))

    m_ref[...] = jnp.full(m_ref.shape, NEG_BIG, F32)
    l_ref[...] = jnp.zeros(l_ref.shape, F32)
    acc_ref[...] = jnp.zeros(acc_ref.shape, F32)
    pos_lane = lax.broadcasted_iota(I32, (tk, LANES), 1)
    pos_row = lax.broadcasted_iota(I32, (tk, LANES), 0)
    feat0 = jnp.where(pos_lane < POS_TERMS, pos_row // POS_RADIX,
                      jnp.where(pos_lane < 2 * POS_TERMS, pos_row % POS_RADIX, 0)).astype(F32)
    hi_lane = jnp.where(pos_lane < POS_TERMS, 1.0, 0.0)

    def attend(j, carry):
        k0 = pl.multiple_of(j * tk, tk)
        mask_ref[...] = jnp.where((sc_ref[j] >= thr8) & (k0 + key_off <= q_pos), 0.0, -jnp.inf)
        pos_feat = (feat0 + hi_lane * (j * (tk // POS_RADIX)).astype(F32)).astype(BF16)

        scores = []
        for p in range(SLABS_PER_TENSOR):
            keys = jnp.concatenate([ka_ref[p, pl.ds(k0, tk), :], pos_feat], axis=1)
            scores.append(lax.dot_general(keys, qh_ref[p], NT_DIMS, preferred_element_type=F32))
        weights = []
        for p in range(SLABS_PER_TENSOR):
            for hh in range(HEADS_PER_SLAB):
                s = chunks(scores[p][:, hh * tq:(hh + 1) * tq]) + mask_ref[...]
                m_old = m_ref[p, hh]
                m_blk = jnp.max(fold_max(jnp.max(s, axis=0)), axis=0, keepdims=True)
                m_new = jnp.maximum(m_old, rows8(m_blk))
                alpha = jnp.exp2(m_old - m_new)
                m_ref[p, hh] = m_new
                weights.append((alpha, jnp.exp2(s - m_new).reshape(tk, tq).astype(BF16)))
        ones_rows = jnp.ones((SUBLANES, tk), BF16)
        for p in range(SLABS_PER_TENSOR):
            for hh in range(HEADS_PER_SLAB):
                rows = slice(hh * HEAD_DIM_A, (hh + 1) * HEAD_DIM_A)
                alpha, probs = weights[p * HEADS_PER_SLAB + hh]
                values = jnp.concatenate([vt_ref[p, j, rows, :], ones_rows], axis=0)
                pv = jnp.dot(values, probs, preferred_element_type=F32)
                old = acc_ref[p, rows, :].reshape(HEAD_DIM_A // SUBLANES, SUBLANES, tq)
                acc_ref[p, rows, :] = (alpha * old).reshape(HEAD_DIM_A, tq) + pv[0:HEAD_DIM_A]
                l_ref[p, hh] = alpha * l_ref[p, hh] + pv[HEAD_DIM_A:HEAD_DIM_A + SUBLANES]
        return carry

    def fold_max(x):
        out = x[0]
        for c in range(1, CHAINS):
            out = jnp.maximum(out, x[c])
        return out

    lax.fori_loop(0, n_kb, attend, 0)

    for p in range(SLABS_PER_TENSOR):
        halves = []
        for hh in range(HEADS_PER_SLAB):
            rows = slice(hh * HEAD_DIM_A, (hh + 1) * HEAD_DIM_A)
            acc = acc_ref[p, rows, :].reshape(HEAD_DIM_A // SUBLANES, SUBLANES, tq)
            halves.append((acc / l_ref[p, hh]).reshape(HEAD_DIM_A, tq))
        o_ref[:, p * LANES:(p + 1) * LANES] = jnp.concatenate(halves, axis=0).T.astype(BF16)


def _dsa_attention(qk, v_t, q_idx, k4, gates, tq, tk):
    _, b, s, _ = qk.shape
    assert tq == LANES and s // POS_RADIX <= 256
    topk = min(TOPK_MAX, s // 4)
    once = pl.Buffered(1)
    groups = tk // (CHAINS * SUBLANES)
    body = functools.partial(_dsa_body, seq=s, topk=topk, tq=tq, tk=tk)
    return pl.pallas_call(
        body,
        grid=(b, s // tq),
        in_specs=[pl.BlockSpec((SLABS_PER_TENSOR, None, tq, LANES), lambda bi, qi: (0, bi, qi, 0)),
                  pl.BlockSpec((SLABS_PER_TENSOR, None, s, LANES), lambda bi, qi: (1, bi, 0, 0), pipeline_mode=once),
                  pl.BlockSpec((SLABS_PER_TENSOR, None, s // tk, LANES, tk), lambda bi, qi: (0, bi, 0, 0, 0),
                               pipeline_mode=once),
                  pl.BlockSpec((None, tq, N_IDX_HEADS * IDX_DIM), lambda bi, qi: (bi, qi, 0)),
                  pl.BlockSpec((None, s, LANES), lambda bi, qi: (bi, 0, 0), pipeline_mode=once),
                  pl.BlockSpec((None, tq, LANES), lambda bi, qi: (bi, qi, 0))],
        out_specs=pl.BlockSpec((None, tq, D_A), lambda bi, qi: (bi, qi, 0)),
        out_shape=jax.ShapeDtypeStruct((b, s, D_A), BF16),
        scratch_shapes=[pltpu.VMEM((s // tk, groups, CHAINS, SUBLANES, tq), F32),
                        pltpu.VMEM((N_IDX_HEADS * tq, LANES), BF16),
                        pltpu.VMEM((SLABS_PER_TENSOR, HEADS_PER_SLAB * tq, 2 * LANES), BF16),
                        pltpu.VMEM((groups, CHAINS, SUBLANES, tq), F32),
                        pltpu.VMEM((SLABS_PER_TENSOR, LANES, tq), F32),
                        pltpu.VMEM((SLABS_PER_TENSOR, HEADS_PER_SLAB, SUBLANES, tq), F32),
                        pltpu.VMEM((SLABS_PER_TENSOR, HEADS_PER_SLAB, SUBLANES, tq), F32)],
        compiler_params=_cparams(("arbitrary", "arbitrary"), 48),
        name="dsa_attention",
    )(qk, qk, v_t, q_idx, k4, gates)


def _log_sigmoid(x):
    return jnp.minimum(x, 0.0) - jnp.log(1.0 + jnp.exp(-jnp.abs(x)))


def _split3(a):
    hi = a.astype(BF16)
    r1 = a - hi.astype(F32)
    mid = r1.astype(BF16)
    lo = (r1 - mid.astype(F32)).astype(BF16)
    return hi, mid, lo


def _mlstm_body(qk_ref, v_ref, o_ref, g_ref, gt_ref, cw_ref, cb_ref, bg_ref, bgt_ref, gn_ref, y_ref,
                xe_ref, c_ref, m_ref, *, chunk, batch):
    @pl.when(pl.program_id(0) == 0)
    def _():
        xe_ref[:, 0:SUBLANES, :] = jnp.zeros((batch, SUBLANES, 2 * D_M), F32)
        c_ref[...] = jnp.zeros(c_ref.shape, F32)
        m_ref[...] = jnp.zeros(m_ref.shape, F32)

    gates = [_mlstm_gates(qk_ref.at[bi], g_ref.at[bi], gt_ref.at[bi], cw_ref, cb_ref, bg_ref, bgt_ref,
                          xe_ref.at[bi], chunk) for bi in range(batch)]
    for h in range(N_HEADS_M):
        for bi in range(batch):
            _mlstm_head(h, gates[bi], v_ref.at[bi], o_ref.at[bi], gn_ref, y_ref.at[bi], c_ref.at[bi], m_ref.at[bi],
                        chunk)


def _mlstm_gates(qk_ref, g_ref, gt_ref, cw_ref, cb_ref, bg_ref, bgt_ref, xe_ref, chunk):
    xe_ref[SUBLANES:SUBLANES + chunk, :] = qk_ref[...]
    conv = cb_ref[...]
    for j in range(CONV_K):
        start = SUBLANES - (CONV_K - 1) + j
        conv = conv + cw_ref[j:j + 1, :] * xe_ref[start:start + chunk, :]
    qk = conv * jax.nn.sigmoid(conv)
    xe_ref[0:SUBLANES, :] = xe_ref[chunk:chunk + SUBLANES, :]

    gb = g_ref[...] + bg_ref[...]
    gbt = gt_ref[...] + bgt_ref[...]
    r = lax.broadcasted_iota(I32, (chunk, chunk), 0)
    c = lax.broadcasted_iota(I32, (chunk, chunk), 1)
    causal = r >= c
    incl = causal.astype(BF16)
    incl_t = (r <= c).astype(BF16)
    cum = jnp.zeros((chunk, LANES), F32)
    for term in _split3(_log_sigmoid(gb)):
        cum = cum + jnp.dot(incl, term, preferred_element_type=F32)
    cum_t = jnp.zeros((2 * N_HEADS_M, chunk), F32)
    for term in _split3(_log_sigmoid(gbt)):
        cum_t = cum_t + jnp.dot(term, incl_t, preferred_element_type=F32)

    return qk, gb, gbt, cum, cum_t, causal


def _mlstm_head(h, gates, v_ref, o_ref, gn_ref, y_ref, c_ref, m_ref, chunk):
    qk, gb, gbt, cum, cum_t, causal = gates
    lane = lax.broadcasted_iota(I32, (chunk, LANES), 1)
    ones_col = jnp.where(lane == 0, 1.0, 0.0).astype(BF16)
    sl = slice(h * HEAD_DIM_M, (h + 1) * HEAD_DIM_M)
    q_h = (qk[:, sl] * (HEAD_DIM_M ** -0.5)).astype(BF16)
    k_f = qk[:, D_M + h * HEAD_DIM_M:D_M + (h + 1) * HEAD_DIM_M]
    v_aug = jnp.concatenate([v_ref[:, sl], ones_col], axis=1)
    b_col = cum[:, G_F + h:G_F + h + 1]
    i_col = gb[:, G_I + h:G_I + h + 1]
    b_row = cum_t[N_HEADS_M + h:N_HEADS_M + h + 1, :]
    i_row = gbt[h:h + 1, :]
    m_prev = m_ref[h:h + 1, 0:1]
    c_aug = c_ref[h]

    d_log = jnp.where(causal, b_col - b_row + i_row, -jnp.inf)
    inter = b_col + m_prev
    m_t = jnp.maximum(inter, jnp.max(d_log, axis=1, keepdims=True))
    qk_t = lax.dot_general(q_h, k_f.astype(BF16), NT_DIMS, preferred_element_type=F32)
    w_intra = qk_t * jnp.exp(d_log - m_t)
    w_inter = jnp.exp(inter - m_t)
    intra = jnp.dot(w_intra.astype(BF16), v_aug[:, 0:HEAD_DIM_M], preferred_element_type=F32)
    carried = jnp.dot(q_h, c_aug.astype(BF16), preferred_element_type=F32)
    num = intra + w_inter * carried[:, 0:HEAD_DIM_M]
    den = jnp.sum(w_intra, axis=1, keepdims=True) + w_inter * carried[:, HEAD_DIM_M:HEAD_DIM_M + 1]
    hid = num / jnp.maximum(jnp.abs(den), jnp.exp(-m_t))

    mu = jnp.mean(hid, axis=1, keepdims=True)
    dev = hid - mu
    var = jnp.mean(dev * dev, axis=1, keepdims=True)
    normed = dev * lax.rsqrt(var + LN_EPS) * gn_ref[:, sl]
    y_ref[:, sl] = (jax.nn.sigmoid(o_ref[:, sl]) * normed).astype(BF16)

    b_last = b_col[chunk - 1:chunk, :]
    w_log = b_last - b_col + i_col
    m_new = jnp.maximum(b_last + m_prev, jnp.max(w_log, axis=0, keepdims=True))
    decay = jnp.exp(b_last + m_prev - m_new)
    kw_t = (k_f * jnp.exp(w_log - m_new)).T.astype(BF16)
    c_ref[h] = decay * c_aug + jnp.dot(kw_t, v_aug, preferred_element_type=F32)
    m_ref[h:h + 1, :] = jnp.broadcast_to(m_new, (1, LANES))


def _mlstm(qk_m, v_m, o_m, gates, gates_t, conv_w, conv_b, bias_slab, bias_col, gn_w, chunk):
    b, s, _ = qk_m.shape
    body = functools.partial(_mlstm_body, chunk=chunk, batch=b)
    const = lambda ci: (0, 0)
    return pl.pallas_call(
        body,
        grid=(s // chunk,),
        in_specs=[pl.BlockSpec((b, chunk, 2 * D_M), lambda ci: (0, ci, 0)),
                  pl.BlockSpec((b, chunk, D_M), lambda ci: (0, ci, 0)),
                  pl.BlockSpec((b, chunk, D_M), lambda ci: (0, ci, 0)),
                  pl.BlockSpec((b, chunk, LANES), lambda ci: (0, ci, 0)),
                  pl.BlockSpec((b, 2 * N_HEADS_M, chunk), lambda ci: (0, 0, ci)),
                  pl.BlockSpec((CONV_K, 2 * D_M), const),
                  pl.BlockSpec((1, 2 * D_M), const),
                  pl.BlockSpec((1, LANES), const),
                  pl.BlockSpec((2 * N_HEADS_M, 1), const),
                  pl.BlockSpec((1, D_M), const)],
        out_specs=pl.BlockSpec((b, chunk, D_M), lambda ci: (0, ci, 0)),
        out_shape=jax.ShapeDtypeStruct((b, s, D_M), BF16),
        scratch_shapes=[pltpu.VMEM((b, chunk + SUBLANES, 2 * D_M), F32),
                        pltpu.VMEM((b, N_HEADS_M, HEAD_DIM_M, 2 * LANES), F32),
                        pltpu.VMEM((b, SUBLANES, LANES), F32)],
        compiler_params=_cparams(("arbitrary",), 32),
        name="mlstm",
    )(qk_m, v_m, o_m, gates, gates_t, conv_w, conv_b, bias_slab, bias_col, gn_w)


COMB_GROUP_LANE = N_EXPERTS


def _route(logits):
    lane = lax.broadcasted_iota(I32, logits.shape, 1).astype(F32)
    g_mask = (lane >= N_EXPERTS) & (lane < N_EXPERTS + N_GROUPS)
    g_logit = jnp.where(g_mask, logits, -jnp.inf)
    g_exp = jnp.exp(g_logit - jnp.max(g_logit, axis=1, keepdims=True))
    g_prob = g_exp / jnp.sum(g_exp, axis=1, keepdims=True)
    g_p = jnp.max(g_prob, axis=1, keepdims=True)
    g_sel = jnp.min(jnp.where(g_mask & (g_prob == g_p), lane, float(LANES)), axis=1, keepdims=True) - N_EXPERTS
    e_mask = (lane >= g_sel * EXPERTS_PER_GROUP) & (lane < (g_sel + 1.0) * EXPERTS_PER_GROUP)
    e_logit = jnp.where(e_mask, logits, -jnp.inf)
    e_exp = jnp.exp(e_logit - jnp.max(e_logit, axis=1, keepdims=True))
    e_prob = e_exp / jnp.sum(e_exp, axis=1, keepdims=True)
    p1 = jnp.max(jnp.where(e_mask, e_prob, -1.0), axis=1, keepdims=True)
    i1 = jnp.min(jnp.where(e_mask & (e_prob == p1), lane, float(LANES)), axis=1, keepdims=True)
    rest = e_mask & (lane != i1)
    p2 = jnp.max(jnp.where(rest, e_prob, -1.0), axis=1, keepdims=True)
    i2 = jnp.min(jnp.where(rest & (e_prob == p2), lane, float(LANES)), axis=1, keepdims=True)
    total = p1 + p2
    comb = (jnp.where(lane == i1, p1 / total, 0.0) + jnp.where(lane == i2, p2 / total, 0.0)) * g_p
    return comb + jnp.where(lane == COMB_GROUP_LANE, g_sel, 0.0)


def _mix_body(ya_ref, ym_ref, x_ref, wo_ref, g_ref, b_ref, wr_ref, br_ref, x1_ref, comb_ref, *, alpha):
    mix = (jnp.dot(ya_ref[...], wo_ref[0:D_A, :], preferred_element_type=F32)
           + jnp.dot(ym_ref[...], wo_ref[D_A:D_A + D_M, :], preferred_element_type=F32))
    x1 = _layer_norm(alpha * x_ref[...] + mix, g_ref[...], b_ref[...])
    x1_ref[...] = x1
    x_hi = x1.astype(BF16)
    x_lo = (x1 - x_hi.astype(F32)).astype(BF16)
    w_hi = wr_ref[0]
    w_lo = wr_ref[1]
    logits = (jnp.dot(x_hi, w_hi, preferred_element_type=F32) + jnp.dot(x_lo, w_hi, preferred_element_type=F32)
              + jnp.dot(x_hi, w_lo, preferred_element_type=F32)) + br_ref[...]
    comb_ref[...] = _route(logits)


def _mix_ln_router(y_a, y_m, x2d, w_out, ln_g, ln_b, w_router, b_router, alpha, tm):
    n, d = x2d.shape
    const = lambda i: (0, 0)
    return pl.pallas_call(
        functools.partial(_mix_body, alpha=alpha),
        grid=(n // tm,),
        in_specs=[pl.BlockSpec((tm, D_A), lambda i: (i, 0)),
                  pl.BlockSpec((tm, D_M), lambda i: (i, 0)),
                  pl.BlockSpec((tm, d), lambda i: (i, 0)),
                  pl.BlockSpec((D_A + D_M, d), const),
                  pl.BlockSpec((1, d), const),
                  pl.BlockSpec((1, d), const),
                  pl.BlockSpec((2, d, LANES), lambda i: (0, 0, 0)),
                  pl.BlockSpec((1, LANES), const)],
        out_specs=[pl.BlockSpec((tm, d), lambda i: (i, 0)),
                   pl.BlockSpec((tm, LANES), lambda i: (i, 0))],
        out_shape=[jax.ShapeDtypeStruct((n, d), F32), jax.ShapeDtypeStruct((n, LANES), F32)],
        compiler_params=_cparams(("arbitrary",), 32),
        name="mix_ln_router",
    )(y_a, y_m, x2d, w_out, ln_g, ln_b, w_router, b_router)


MOE_ROWS = TM_MOE // N_GROUPS + 32


def _moe_body(x1_ref, comb_ref, wg_ref, wu_ref, wd_ref, g_ref, b_ref, p_ref, wpg_ref, wpp_ref, y_ref,
              xb_ref, before_ref, row_ref, *, alpha, tm):
    g = pl.program_id(1)
    g_f = g.astype(F32)

    @pl.when((pl.program_id(0) == 0) & (g == 0))
    def _():
        r = lax.broadcasted_iota(I32, (tm, tm), 0)
        c = lax.broadcasted_iota(I32, (tm, tm), 1)
        before_ref[...] = (c < r).astype(BF16)

    @pl.when(g == 0)
    def _():
        d = x1_ref.shape[1]
        xb_ref[:, 0:d] = x1_ref[...].astype(BF16)
        y_ref[...] = jnp.zeros(y_ref.shape, F32)
        comb = comb_ref[...]
        for k, term in enumerate(_split3(comb)):
            xb_ref[:, d + k * LANES:d + (k + 1) * LANES] = term
        group_row = comb.T[COMB_GROUP_LANE:COMB_GROUP_LANE + 1, :]
        sub = lax.broadcasted_iota(I32, (SUBLANES, tm), 0)
        member_t = (sub.astype(F32) == group_row) & (sub < N_GROUPS)
        ahead_t = lax.dot_general(jnp.where(member_t, 1.0, 0.0).astype(BF16), before_ref[...], NT_DIMS,
                                  preferred_element_type=F32)
        rank_row = jnp.sum(jnp.where(member_t, ahead_t, 0.0), axis=0, keepdims=True)
        row_ref[0] = jnp.broadcast_to(group_row, (SUBLANES, tm))
        row_ref[1] = jnp.broadcast_to(rank_row, (SUBLANES, tm))

    in_group_row = row_ref[0, 0:1, :] == g_f
    n_tokens = jnp.sum(jnp.where(in_group_row, 1.0, 0.0))
    n_blocks = (n_tokens.astype(I32) + MOE_ROWS - 1) // MOE_ROWS
    slot = lax.broadcasted_iota(I32, (MOE_ROWS, tm), 0).astype(F32)
    comb_lane = lax.broadcasted_iota(I32, (MOE_ROWS, LANES), 1)

    def block(b, carry):
        base = (b * MOE_ROWS).astype(F32)
        gather = jnp.where(in_group_row & (row_ref[1, 0:1, :] - base == slot), 1.0, 0.0).astype(BF16)
        d = x1_ref.shape[1]
        picked = jnp.dot(gather, xb_ref[...], preferred_element_type=F32)
        xc = picked[:, 0:d].astype(BF16)
        comb_c = picked[:, d:d + LANES] + picked[:, d + LANES:d + 2 * LANES] + picked[:, d + 2 * LANES:d + 3 * LANES]
        yc = jnp.zeros((MOE_ROWS, d), F32)
        for e in range(EXPERTS_PER_GROUP):
            gate = jnp.dot(xc, wg_ref[e], preferred_element_type=F32)
            up = jnp.dot(xc, wu_ref[e], preferred_element_type=F32)
            weight = jnp.sum(jnp.where(comb_lane == g * EXPERTS_PER_GROUP + e, comb_c, 0.0), axis=1, keepdims=True)
            hidden = gate * jax.nn.sigmoid(gate) * up * weight
            yc = yc + jnp.dot(hidden.astype(BF16), wd_ref[e], preferred_element_type=F32)
        yc_hi = yc.astype(BF16)
        yc_lo = (yc - yc_hi.astype(F32)).astype(BF16)
        tn = (((0,), (0,)), ((), ()))
        y_ref[...] += lax.dot_general(jnp.concatenate([gather, gather], axis=0),
                                      jnp.concatenate([yc_hi, yc_lo], axis=0), tn, preferred_element_type=F32)
        return carry

    lax.fori_loop(0, n_blocks, block, 0)

    @pl.when(g == N_GROUPS - 1)
    def _():
        x2 = _layer_norm(alpha * x1_ref[...] + y_ref[...], g_ref[...], b_ref[...])
        gate = jax.nn.sigmoid(jnp.dot(x2.astype(BF16), wpg_ref[...], preferred_element_type=F32))
        proj = jnp.dot(p_ref[...].astype(BF16), wpp_ref[...], preferred_element_type=F32)
        y_ref[...] = x2 + gate * proj


def _moe_ln_ple(x1, comb, w_gate, w_up, w_down, ln_g, ln_b, p2d, w_ple_gate, w_ple_proj, alpha, tm):
    n, d = x1.shape
    dp = p2d.shape[1]
    const = lambda i, g: (0, 0)
    once = pl.Buffered(1)
    grouped = lambda w: w.reshape((N_GROUPS, EXPERTS_PER_GROUP) + w.shape[1:])
    return pl.pallas_call(
        functools.partial(_moe_body, alpha=alpha, tm=tm),
        grid=(n // tm, N_GROUPS),
        in_specs=[pl.BlockSpec((tm, d), lambda i, g: (i, 0), pipeline_mode=once),
                  pl.BlockSpec((tm, LANES), lambda i, g: (i, 0)),
                  pl.BlockSpec((None, EXPERTS_PER_GROUP, d, D_EXPERT), lambda i, g: (g, 0, 0, 0)),
                  pl.BlockSpec((None, EXPERTS_PER_GROUP, d, D_EXPERT), lambda i, g: (g, 0, 0, 0)),
                  pl.BlockSpec((None, EXPERTS_PER_GROUP, D_EXPERT, d), lambda i, g: (g, 0, 0, 0)),
                  pl.BlockSpec((1, d), const),
                  pl.BlockSpec((1, d), const),
                  pl.BlockSpec((tm, dp), lambda i, g: (i, 0), pipeline_mode=once),
                  pl.BlockSpec((d, d), const, pipeline_mode=once),
                  pl.BlockSpec((dp, d), const, pipeline_mode=once)],
        out_specs=pl.BlockSpec((tm, d), lambda i, g: (i, 0)),
        out_shape=jax.ShapeDtypeStruct((n, d), F32),
        scratch_shapes=[pltpu.VMEM((tm, d + 3 * LANES), BF16),
                        pltpu.VMEM((tm, tm), BF16),
                        pltpu.VMEM((2, SUBLANES, tm), F32)],
        compiler_params=_cparams(("arbitrary", "arbitrary"), 58),
        name="moe_ln_ple",
    )(x1, comb, grouped(w_gate), grouped(w_up), grouped(w_down), ln_g, ln_b, p2d, w_ple_gate, w_ple_proj)


def _tile(n, want):
    t = min(want, n)
    assert n % t == 0
    return t


def _layer(x, p_i, w_in, b_igate, b_fgate, conv_w, conv_b, gn_w, w_out, ln1_g, ln1_b,
           rg_w, rg_b, re_w, re_b, e_gate, e_up, e_down, ln2_g, ln2_b, ple_gate_w, ple_proj_w, alpha):
    b, s, d = x.shape
    n = b * s
    x2d = x.reshape(n, d)

    tk = _tile(s, TK_DSA)
    w_packed, w_vt = _pack_w_in(w_in)
    qk, v_t, q_idx, k4, gates, qk_m, v_m, o_m = _in_proj(x2d, w_packed, w_vt, tk)

    gates3 = gates.reshape(b, s, LANES)
    y_a = _dsa_attention(qk.reshape(2 * SLABS_PER_TENSOR, b, s, LANES),
                         v_t.reshape(SLABS_PER_TENSOR, b, s // tk, LANES, tk),
                         q_idx.reshape(b, s, N_IDX_HEADS * IDX_DIM), k4.reshape(b, s, LANES), gates3,
                         _tile(s, TQ_DSA), tk)

    gate_bias = jnp.concatenate([b_igate, b_fgate]).astype(F32)
    bias_slab = jnp.zeros((1, LANES), F32).at[0, G_I:G_I + 2 * N_HEADS_M].set(gate_bias)
    gates_t = jnp.swapaxes(gates3[:, :, G_I:G_I + 2 * N_HEADS_M], 1, 2)
    y_m = _mlstm(qk_m.reshape(b, s, 2 * D_M), v_m.reshape(b, s, D_M), o_m.reshape(b, s, D_M), gates3, gates_t,
                 conv_w, conv_b.reshape(1, -1), bias_slab, gate_bias.reshape(-1, 1), gn_w.reshape(1, -1),
                 _tile(s, CHUNK_MLSTM))

    w_router = jnp.concatenate([re_w, rg_w, jnp.zeros((d, LANES - N_EXPERTS - N_GROUPS), F32)], axis=1)
    w_router_hi = w_router.astype(BF16)
    w_router = jnp.stack([w_router_hi, (w_router - w_router_hi.astype(F32)).astype(BF16)])
    b_router = jnp.concatenate([re_b, rg_b, jnp.zeros((LANES - N_EXPERTS - N_GROUPS,), F32)]).reshape(1, LANES)
    x1, comb = _mix_ln_router(y_a.reshape(n, D_A), y_m.reshape(n, D_M), x2d, w_out.astype(BF16),
                              ln1_g.reshape(1, d), ln1_b.reshape(1, d), w_router, b_router, alpha, _tile(n, TM_MIX))

    out = _moe_ln_ple(x1, comb, e_gate.astype(BF16), e_up.astype(BF16), e_down.astype(BF16),
                      ln2_g.reshape(1, d), ln2_b.reshape(1, d), p_i.reshape(n, -1), ple_gate_w.astype(BF16),
                      ple_proj_w.astype(BF16), alpha, _tile(n, TM_MOE))
    return out.reshape(b, s, d)


def kernel(x, p, w_in, b_igate, b_fgate, conv_w, conv_b, gn_w, w_out, ln1_g, ln1_b, router_group_w, router_group_b,
           router_expert_w, router_expert_b, expert_w_gate, expert_w_up, expert_w_down, ln2_g, ln2_b, ple_gate_w,
           ple_proj_w):
    depth = w_in.shape[0]
    alpha = (2.0 * depth) ** 0.25
    for i in range(depth):
        x = _layer(x, p[i], w_in[i], b_igate[i], b_fgate[i], conv_w[i], conv_b[i], gn_w[i], w_out[i], ln1_g[i],
                   ln1_b[i], router_group_w[i], router_group_b[i], router_expert_w[i], router_expert_b[i],
                   expert_w_gate[i], expert_w_up[i], expert_w_down[i], ln2_g[i], ln2_b[i], ple_gate_w[i],
                   ple_proj_w[i], alpha)
    return x
```

```python
import functools

import numpy as np
import jax
import jax.numpy as jnp
from jax import lax
from jax.experimental import pallas as pl
from jax.experimental.pallas import tpu as pltpu

F32 = jnp.float32
BF16 = jnp.bfloat16
I32 = jnp.int32

N_HEADS_A = 8
HEAD_DIM_A = 64
D_A = N_HEADS_A * HEAD_DIM_A
N_IDX_HEADS = 8
IDX_DIM = 32
TOPK_MAX = 256
N_HEADS_M = 4
HEAD_DIM_M = 128
D_M = N_HEADS_M * HEAD_DIM_M
CONV_K = 4
N_GROUPS = 4
EXPERTS_PER_GROUP = 8
N_EXPERTS = N_GROUPS * EXPERTS_PER_GROUP
D_EXPERT = 256
LN_EPS = 1e-5
NEG_BIG = -1e30
IN_SPLITS = (D_A, D_A, D_A, N_IDX_HEADS * IDX_DIM, IDX_DIM, N_IDX_HEADS, D_M, D_M, D_M, D_M, N_HEADS_M, N_HEADS_M)

LANES = 128
SUBLANES = 8
MIB = 1024 * 1024

C_QK = 0
C_QIDX = C_QK + 2 * D_A
C_K4 = C_QIDX + N_IDX_HEADS * IDX_DIM
C_GATE = C_K4 + LANES
C_QKM = C_GATE + LANES
C_VM = C_QKM + 2 * D_M
C_OM = C_VM + D_M
C_END = C_OM + D_M
SLABS_PER_TENSOR = D_A // LANES
HEADS_PER_SLAB = LANES // HEAD_DIM_A
G_W = 0
G_I = N_IDX_HEADS
G_F = G_I + N_HEADS_M

INT_MIN = -2 ** 31

TQ_DSA = LANES
TK_DSA = 1024
TM_IN_PROJ = TK_DSA
CHUNK_MLSTM = 256
TM_MIX = 1024
TM_MOE = 1024

NT_DIMS = (((1,), (1,)), ((), ()))


def _cparams(semantics, vmem_mib):
    return pltpu.CompilerParams(dimension_semantics=semantics, vmem_limit_bytes=int(vmem_mib * MIB))


def _layer_norm(r, g, b):
    mu = jnp.mean(r, axis=-1, keepdims=True)
    d = r - mu
    var = jnp.mean(d * d, axis=-1, keepdims=True)
    return d * lax.rsqrt(var + LN_EPS) * g + b


def _pack_w_in(w_in):
    d = w_in.shape[0]
    cuts = [int(c) for c in np.cumsum(IN_SPLITS)[:-1]]
    q_a, k_a, v_a, q_idx, k_idx, w_idx, q_m, k_m, v_m, o_m, i_m, f_m = jnp.split(w_in, cuts, axis=1)
    k4 = jnp.tile(k_idx, (1, LANES // IDX_DIM))
    gate = jnp.concatenate([w_idx, i_m, f_m, jnp.zeros((d, LANES - G_F - N_HEADS_M), w_in.dtype)], axis=1)
    packed = jnp.concatenate([q_a, k_a, q_idx, k4, gate, q_m, k_m, v_m, o_m], axis=1).astype(BF16)
    return packed, v_a.T.astype(BF16)


def _in_proj_body(x_ref, w_ref, wvt_ref, qk_ref, vt_ref, qi_ref, k4_ref, g_ref, qkm_ref, vm_ref, om_ref):
    xb = x_ref[...].astype(BF16)

    def proj(lo, hi):
        return jnp.dot(xb, w_ref[:, lo:hi], preferred_element_type=F32)

    qk = proj(C_QK, C_QIDX).astype(BF16)
    for c in range(2 * SLABS_PER_TENSOR):
        qk_ref[c] = qk[:, c * LANES:(c + 1) * LANES]
    v_t = lax.dot_general(wvt_ref[...], xb, NT_DIMS, preferred_element_type=F32).astype(BF16)
    for c in range(SLABS_PER_TENSOR):
        vt_ref[c, 0] = v_t[c * LANES:(c + 1) * LANES, :]
    qi_ref[...] = proj(C_QIDX, C_K4).astype(BF16)
    k4_ref[...] = proj(C_K4, C_GATE).astype(BF16)
    g_ref[...] = proj(C_GATE, C_QKM)
    qkm_ref[...] = proj(C_QKM, C_VM)
    vm_ref[...] = proj(C_VM, C_OM).astype(BF16)
    om_ref[...] = proj(C_OM, C_END)


def _in_proj(x2d, w_packed, w_vt, tm):
    n, d = x2d.shape
    widths = [(C_K4 - C_QIDX, BF16), (C_GATE - C_K4, BF16), (C_QKM - C_GATE, F32),
              (C_VM - C_QKM, F32), (C_OM - C_VM, BF16), (C_END - C_OM, F32)]
    return pl.pallas_call(
        _in_proj_body,
        grid=(n // tm,),
        in_specs=[pl.BlockSpec((tm, d), lambda i: (i, 0)),
                  pl.BlockSpec((d, C_END), lambda i: (0, 0)),
                  pl.BlockSpec((D_A, d), lambda i: (0, 0))],
        out_specs=([pl.BlockSpec((2 * SLABS_PER_TENSOR, tm, LANES), lambda i: (0, i, 0)),
                    pl.BlockSpec((SLABS_PER_TENSOR, 1, LANES, tm), lambda i: (0, i, 0, 0))]
                   + [pl.BlockSpec((tm, w), lambda i: (i, 0)) for w, _ in widths]),
        out_shape=([jax.ShapeDtypeStruct((2 * SLABS_PER_TENSOR, n, LANES), BF16),
                    jax.ShapeDtypeStruct((SLABS_PER_TENSOR, n // tm, LANES, tm), BF16)]
                   + [jax.ShapeDtypeStruct((n, w), dt) for w, dt in widths]),
        compiler_params=_cparams(("arbitrary",), 48),
        name="in_proj",
    )(x2d, w_packed, w_vt)


def _float_to_ordered_int(value):
    bits = lax.bitcast_convert_type(value, I32)
    return bits ^ ((bits >> 31) & jnp.int32(0x7FFFFFFF))


def _ordered_int_to_float(key):
    bits = key ^ ((key >> 31) & jnp.int32(0x7FFFFFFF))
    return lax.bitcast_convert_type(bits, F32)


SEARCH_FIXED_PASSES = 12
SEARCH_FREE_PASSES = 16
SEARCH_MAX_PASSES = SEARCH_FREE_PASSES + 33
ZERO_TIE_MARGIN = 128
CHAINS = 4
POS_RADIX = 64
POS_TERMS = 3
LOG2_E = float(np.log2(np.e))


def _dsa_body(qa_ref, ka_ref, vt_ref, qi_ref, k4_ref, g_ref, o_ref,
              sc_ref, qm_ref, qh_ref, mask_ref, acc_ref, m_ref, l_ref, *, seq, topk, tq, tk):
    t0 = pl.program_id(1) * tq
    n_kb = (t0 + tq + tk - 1) // tk
    groups = tk // (CHAINS * SUBLANES)
    shape4 = (groups, CHAINS, SUBLANES, tq)

    def chunks(x):
        return x.reshape(shape4)

    def fold(x):
        out = x[0]
        for c in range(1, CHAINS):
            out = out + x[c]
        return out

    def rows8(row):
        return jnp.broadcast_to(row, (SUBLANES, tq))

    key_off = ((lax.broadcasted_iota(I32, shape4, 0) * CHAINS + lax.broadcasted_iota(I32, shape4, 1)) * SUBLANES
               + lax.broadcasted_iota(I32, shape4, 2))
    q_pos = t0 + lax.broadcasted_iota(I32, shape4, 3)
    lane = lax.broadcasted_iota(I32, (tq, LANES), 1)

    for h in range(N_IDX_HEADS):
        per_slab = LANES // IDX_DIM
        slab = qi_ref[:, (h // per_slab) * LANES:(h // per_slab + 1) * LANES]
        lo = (h % per_slab) * IDX_DIM
        qm_ref[h * tq:(h + 1) * tq, :] = jnp.where((lane >= lo) & (lane < lo + IDX_DIM), slab, jnp.zeros_like(slab))
    for h in range(N_HEADS_A):
        p, hh = divmod(h, HEADS_PER_SLAB)
        slope = 2.0 ** (-8.0 * (h + 1) / N_HEADS_A)
        slab = qa_ref[p].astype(F32) * (HEAD_DIM_A ** -0.5 * LOG2_E)
        in_head = (lane >= hh * HEAD_DIM_A) & (lane < (hh + 1) * HEAD_DIM_A)
        qh_ref[p, hh * tq:(hh + 1) * tq, 0:LANES] = jnp.where(in_head, slab, 0.0).astype(BF16)
        weight = jnp.where(lane < POS_TERMS, slope * POS_RADIX * LOG2_E,
                           jnp.where(lane < 2 * POS_TERMS, slope * LOG2_E, 0.0))
        terms = _split3(weight)
        feat = jnp.where(lane % POS_TERMS == 0, terms[0], jnp.where(lane % POS_TERMS == 1, terms[1], terms[2]))
        qh_ref[p, hh * tq:(hh + 1) * tq, LANES:2 * LANES] = feat
    w_rows = g_ref[...].T[G_W:G_W + N_IDX_HEADS, :] * ((IDX_DIM * N_IDX_HEADS) ** -0.5)

    def score_block(j, carry):
        hi_part, lo_part, zero_part = carry
        k0 = pl.multiple_of(j * tk, tk)
        logits = lax.dot_general(k4_ref[pl.ds(k0, tk), :], qm_ref[...], NT_DIMS,
                                 preferred_element_type=F32)
        score = jnp.zeros(shape4, F32)
        for h in range(N_IDX_HEADS):
            score = score + rows8(w_rows[h:h + 1, :]) * jnp.maximum(chunks(logits[:, h * tq:(h + 1) * tq]), 0.0)
        causal = k0 + key_off <= q_pos
        kept = jnp.where(causal, score, NEG_BIG)
        sc_ref[j] = kept
        hi_part = jnp.maximum(hi_part, jnp.max(kept, axis=0))
        lo_part = jnp.minimum(lo_part, jnp.min(jnp.where(causal, score, jnp.inf), axis=0))
        zero_part = zero_part + jnp.sum(jnp.where(kept >= 0.0, 1.0, 0.0), axis=0)
        return hi_part, lo_part, zero_part

    part_shape = (CHAINS, SUBLANES, tq)
    hi_part, lo_part, zero_part = lax.fori_loop(
        0, n_kb, score_block,
        (jnp.full(part_shape, -jnp.inf, F32), jnp.full(part_shape, jnp.inf, F32), jnp.zeros(part_shape, F32)))
    row_max = jnp.max(jnp.max(hi_part, axis=0), axis=0, keepdims=True)
    row_min = jnp.min(jnp.min(lo_part, axis=0), axis=0, keepdims=True)

    n_tail = (seq - n_kb * tk).astype(F32)

    def count(thr, strict):
        thr8 = rows8(thr)

        def body(j, acc):
            s = sc_ref[j]
            return acc + jnp.sum(jnp.where((s > thr8) if strict else (s >= thr8), 1.0, 0.0), axis=0)

        acc = lax.fori_loop(0, n_kb, body, jnp.zeros(part_shape, F32))
        tail_hit = (NEG_BIG > thr) if strict else (NEG_BIG >= thr)
        return jnp.sum(fold(acc), axis=0, keepdims=True) + jnp.where(tail_hit, n_tail, 0.0)

    n_causal = (t0 + 1 + lax.broadcasted_iota(I32, (1, tq), 1)).astype(F32)
    few = n_causal < topk
    lo0 = jnp.where(few, _float_to_ordered_int(jnp.full((1, tq), NEG_BIG, F32)), _float_to_ordered_int(row_min))
    cnt_lo0 = jnp.where(few, float(seq), n_causal)
    hi0 = _float_to_ordered_int(row_max) + 1
    log_k = float(np.log(topk))

    def log_excess(cnt):
        return jnp.log(jnp.maximum(cnt, 0.5)) - log_k

    def probe(it, bracket):
        lo, hi, cnt_lo, f_lo, f_hi, last, done_i = bracket
        done = done_i != 0
        it_v = jnp.zeros((1, tq), I32) + it
        lo_v = _ordered_int_to_float(lo)
        hi_v = _ordered_int_to_float(hi)
        guess = _float_to_ordered_int(lo_v + (f_lo / (f_lo - f_hi)) * (hi_v - lo_v))
        guess = jnp.where((it_v == 1) & (lo == 0) & (cnt_lo < topk + ZERO_TIE_MARGIN), 1, guess)
        middle = lo + lax.shift_right_logical(hi - lo, jnp.ones((1, tq), I32))
        usable = (guess > lo) & (guess < hi) & (it_v < SEARCH_FREE_PASSES)
        cand = jnp.where(usable, guess, middle)
        cnt = count(_ordered_int_to_float(cand), strict=False)
        f_c = log_excess(cnt)
        take = (cnt >= topk) & ~done
        drop = (cnt < topk) & ~done
        f_hi = jnp.where(take, jnp.where(last == 1, 0.5 * f_hi, f_hi), jnp.where(drop, f_c, f_hi))
        f_lo = jnp.where(drop, jnp.where(last == -1, 0.5 * f_lo, f_lo), jnp.where(take, f_c, f_lo))
        last = jnp.where(take, 1, jnp.where(drop, -1, last))
        lo = jnp.where(take, cand, lo)
        cnt_lo = jnp.where(take, cnt, cnt_lo)
        hi = jnp.where(drop, cand, hi)
        done = done | (cnt_lo == topk) | ((hi - lo) == 1)
        return lo, hi, cnt_lo, f_lo, f_hi, last, done.astype(I32)

    def n_active(bracket):
        return jnp.sum(jnp.where(bracket[6] != 0, 0.0, 1.0))

    def search_step(state):
        bracket = probe(state[0], state[2:])
        return (state[0] + 1, n_active(bracket)) + bracket

    def search_on(state):
        return (state[0] < SEARCH_MAX_PASSES) & (state[1] > 0.0)

    cnt_zero = jnp.sum(fold(zero_part), axis=0, keepdims=True)
    inside = (lo0 < 0) & (hi0 > 0) & ~few
    take0 = inside & (cnt_zero >= topk)
    drop0 = inside & (cnt_zero < topk)
    lo1 = jnp.where(take0, 0, lo0)
    hi1 = jnp.where(drop0, 0, hi0)
    cnt_lo1 = jnp.where(take0, cnt_zero, cnt_lo0)
    done1 = few | (cnt_lo1 == topk) | ((hi1 - lo1) == 1)
    bracket = (lo1, hi1, cnt_lo1, log_excess(cnt_lo1), log_excess(jnp.where(drop0, cnt_zero, 0.0)),
               jnp.where(take0, 1, jnp.where(drop0, -1, 0)), done1.astype(I32))
    bracket = lax.fori_loop(1, 1 + SEARCH_FIXED_PASSES, probe, bracket)
    state = lax.while_loop(search_on, search_step,
                           (jnp.int32(1 + SEARCH_FIXED_PASSES), n_active(bracket)) + bracket)
    thr = _ordered_int_to_float(state[2])
    cnt_thr = state[4]
    thr8 = rows8(thr)

    @pl.when(jnp.max(cnt_thr) > topk)
    def _():
        need = topk - count(thr, strict=True)
        tri_r = lax.broadcasted_iota(I32, (tk, tk), 0)
        tri_c = lax.broadcasted_iota(I32, (tk, tk), 1)
        earlier = (tri_c < tri_r).astype(BF16)

        def drop_late_ties(j, seen):
            s = sc_ref[j]
            eq = s == thr8
            eq_f = jnp.where(eq, 1.0, 0.0)
            rank = seen + jnp.dot(earlier, eq_f.reshape(tk, tq).astype(BF16), preferred_element_type=F32)
            sc_ref[j] = jnp.where(eq & (chunks(rank) >= rows8(need)), -jnp.inf, s)
            return seen + jnp.sum(fold(jnp.sum(eq_f, axis=0)), axis=0, keepdims=True)

        lax.fori_loop(0, n_kb, drop_late_ties, jnp.zeros((1, tq), F32))

    m_ref[...] = jnp.full(m_ref.shape, NEG_BIG, F32)
    l_ref[...] = jnp.zeros(l_ref.shape, F32)
    acc_ref[...] = jnp.zeros(acc_ref.shape, F32)
    pos_lane = lax.broadcasted_iota(I32, (tk, LANES), 1)
    pos_row = lax.broadcasted_iota(I32, (tk, LANES), 0)
    feat0 = jnp.where(pos_lane < POS_TERMS, pos_row // POS_RADIX,
                      jnp.where(pos_lane < 2 * POS_TERMS, pos_row % POS_RADIX, 0)).astype(F32)
    hi_lane = jnp.where(pos_lane < POS_TERMS, 1.0, 0.0)

    def attend(j, carry):
        k0 = pl.multiple_of(j * tk, tk)
        mask_ref[...] = jnp.where((sc_ref[j] >= thr8) & (k0 + key_off <= q_pos), 0.0, -jnp.inf)
        pos_feat = (feat0 + hi_lane * (j * (tk // POS_RADIX)).astype(F32)).astype(BF16)

        scores = []
        for p in range(SLABS_PER_TENSOR):
            keys = jnp.concatenate([ka_ref[p, pl.ds(k0, tk), :], pos_feat], axis=1)
            scores.append(lax.dot_general(keys, qh_ref[p], NT_DIMS, preferred_element_type=F32))
        weights = []
        for p in range(SLABS_PER_TENSOR):
            for hh in range(HEADS_PER_SLAB):
                s = chunks(scores[p][:, hh * tq:(hh + 1) * tq]) + mask_ref[...]
                m_old = m_ref[p, hh]
                m_blk = jnp.max(fold_max(jnp.max(s, axis=0)), axis=0, keepdims=True)
                m_new = jnp.maximum(m_old, rows8(m_blk))
                alpha = jnp.exp2(m_old - m_new)
                m_ref[p, hh] = m_new
                weights.append((alpha, jnp.exp2(s - m_new).reshape(tk, tq).astype(BF16)))
        ones_rows = jnp.ones((SUBLANES, tk), BF16)
        for p in range(SLABS_PER_TENSOR):
            for hh in range(HEADS_PER_SLAB):
                rows = slice(hh * HEAD_DIM_A, (hh + 1) * HEAD_DIM_A)
                alpha, probs = weights[p * HEADS_PER_SLAB + hh]
                values = jnp.concatenate([vt_ref[p, j, rows, :], ones_rows], axis=0)
                pv = jnp.dot(values, probs, preferred_element_type=F32)
                old = acc_ref[p, rows, :].reshape(HEAD_DIM_A // SUBLANES, SUBLANES, tq)
                acc_ref[p, rows, :] = (alpha * old).reshape(HEAD_DIM_A, tq) + pv[0:HEAD_DIM_A]
                l_ref[p, hh] = alpha * l_ref[p, hh] + pv[HEAD_DIM_A:HEAD_DIM_A + SUBLANES]
        return carry

    def fold_max(x):
        out = x[0]
        for c in range(1, CHAINS):
            out = jnp.maximum(out, x[c])
        return out

    lax.fori_loop(0, n_kb, attend, 0)

    for p in range(SLABS_PER_TENSOR):
        halves = []
        for hh in range(HEADS_PER_SLAB):
            rows = slice(hh * HEAD_DIM_A, (hh + 1) * HEAD_DIM_A)
            acc = acc_ref[p, rows, :].reshape(HEAD_DIM_A // SUBLANES, SUBLANES, tq)
            halves.append((acc / l_ref[p, hh]).reshape(HEAD_DIM_A, tq))
        o_ref[:, p * LANES:(p + 1) * LANES] = jnp.concatenate(halves, axis=0).T.astype(BF16)


def _dsa_attention(qk, v_t, q_idx, k4, gates, tq, tk):
    _, b, s, _ = qk.shape
    assert tq == LANES and s // POS_RADIX <= 256
    topk = min(TOPK_MAX, s // 4)
    once = pl.Buffered(1)
    groups = tk // (CHAINS * SUBLANES)
    body = functools.partial(_dsa_body, seq=s, topk=topk, tq=tq, tk=tk)
    return pl.pallas_call(
        body,
        grid=(b, s // tq),
        in_specs=[pl.BlockSpec((SLABS_PER_TENSOR, None, tq, LANES), lambda bi, qi: (0, bi, qi, 0)),
                  pl.BlockSpec((SLABS_PER_TENSOR, None, s, LANES), lambda bi, qi: (1, bi, 0, 0), pipeline_mode=once),
                  pl.BlockSpec((SLABS_PER_TENSOR, None, s // tk, LANES, tk), lambda bi, qi: (0, bi, 0, 0, 0),
                               pipeline_mode=once),
                  pl.BlockSpec((None, tq, N_IDX_HEADS * IDX_DIM), lambda bi, qi: (bi, qi, 0)),
                  pl.BlockSpec((None, s, LANES), lambda bi, qi: (bi, 0, 0), pipeline_mode=once),
                  pl.BlockSpec((None, tq, LANES), lambda bi, qi: (bi, qi, 0))],
        out_specs=pl.BlockSpec((None, tq, D_A), lambda bi, qi: (bi, qi, 0)),
        out_shape=jax.ShapeDtypeStruct((b, s, D_A), BF16),
        scratch_shapes=[pltpu.VMEM((s // tk, groups, CHAINS, SUBLANES, tq), F32),
                        pltpu.VMEM((N_IDX_HEADS * tq, LANES), BF16),
                        pltpu.VMEM((SLABS_PER_TENSOR, HEADS_PER_SLAB * tq, 2 * LANES), BF16),
                        pltpu.VMEM((groups, CHAINS, SUBLANES, tq), F32),
                        pltpu.VMEM((SLABS_PER_TENSOR, LANES, tq), F32),
                        pltpu.VMEM((SLABS_PER_TENSOR, HEADS_PER_SLAB, SUBLANES, tq), F32),
                        pltpu.VMEM((SLABS_PER_TENSOR, HEADS_PER_SLAB, SUBLANES, tq), F32)],
        compiler_params=_cparams(("arbitrary", "arbitrary"), 48),
        name="dsa_attention",
    )(qk, qk, v_t, q_idx, k4, gates)


def _log_sigmoid(x):
    return jnp.minimum(x, 0.0) - jnp.log(1.0 + jnp.exp(-jnp.abs(x)))


def _split3(a):
    hi = a.astype(BF16)
    r1 = a - hi.astype(F32)
    mid = r1.astype(BF16)
    lo = (r1 - mid.astype(F32)).astype(BF16)
    return hi, mid, lo


def _mlstm_body(qk_ref, v_ref, o_ref, g_ref, gt_ref, cw_ref, cb_ref, bg_ref, bgt_ref, gn_ref, y_ref,
                xe_ref, c_ref, m_ref, *, chunk, batch):
    @pl.when(pl.program_id(0) == 0)
    def _():
        xe_ref[:, 0:SUBLANES, :] = jnp.zeros((batch, SUBLANES, 2 * D_M), F32)
        c_ref[...] = jnp.zeros(c_ref.shape, F32)
        m_ref[...] = jnp.zeros(m_ref.shape, F32)

    gates = [_mlstm_gates(qk_ref.at[bi], g_ref.at[bi], gt_ref.at[bi], cw_ref, cb_ref, bg_ref, bgt_ref,
                          xe_ref.at[bi], chunk) for bi in range(batch)]
    chains = [(h, bi) for h in range(N_HEADS_M) for bi in range(batch)]
    scores = [_mlstm_head_scores(h, gates[bi], v_ref.at[bi], c_ref.at[bi], m_ref.at[bi], chunk) for h, bi in chains]
    for (h, bi), sc in zip(chains, scores):
        _mlstm_head(h, gates[bi], sc, o_ref.at[bi], gn_ref, y_ref.at[bi], chunk)
    for (h, bi), sc in zip(chains, scores):
        _mlstm_head_state(h, gates[bi], sc, c_ref.at[bi], m_ref.at[bi], chunk)


def _mlstm_gates(qk_ref, g_ref, gt_ref, cw_ref, cb_ref, bg_ref, bgt_ref, xe_ref, chunk):
    xe_ref[SUBLANES:SUBLANES + chunk, :] = qk_ref[...]
    conv = cb_ref[...]
    for j in range(CONV_K):
        start = SUBLANES - (CONV_K - 1) + j
        conv = conv + cw_ref[j:j + 1, :] * xe_ref[start:start + chunk, :]
    qk = conv * jax.nn.sigmoid(conv)
    xe_ref[0:SUBLANES, :] = xe_ref[chunk:chunk + SUBLANES, :]

    gb = g_ref[...] + bg_ref[...]
    gbt = gt_ref[...] + bgt_ref[...]
    r = lax.broadcasted_iota(I32, (chunk, chunk), 0)
    c = lax.broadcasted_iota(I32, (chunk, chunk), 1)
    causal = r >= c
    incl = causal.astype(BF16)
    incl_t = (r <= c).astype(BF16)
    cum = jnp.zeros((chunk, LANES), F32)
    for term in _split3(_log_sigmoid(gb)):
        cum = cum + jnp.dot(incl, term, preferred_element_type=F32)
    cum_t = jnp.zeros((2 * N_HEADS_M, chunk), F32)
    for term in _split3(_log_sigmoid(gbt)):
        cum_t = cum_t + jnp.dot(term, incl_t, preferred_element_type=F32)

    return qk, gb, gbt, cum, cum_t, causal


def _mlstm_head_scores(h, gates, v_ref, c_ref, m_ref, chunk):
    qk = gates[0]
    lane = lax.broadcasted_iota(I32, (chunk, LANES), 1)
    ones_col = jnp.where(lane == 0, 1.0, 0.0).astype(BF16)
    sl = slice(h * HEAD_DIM_M, (h + 1) * HEAD_DIM_M)
    q_h = (qk[:, sl] * (HEAD_DIM_M ** -0.5)).astype(BF16)
    k_f = qk[:, D_M + h * HEAD_DIM_M:D_M + (h + 1) * HEAD_DIM_M]
    v_aug = jnp.concatenate([v_ref[:, sl], ones_col], axis=1)
    m_prev = m_ref[h:h + 1, 0:1]
    c_aug = c_ref[h]
    qk_t = lax.dot_general(q_h, k_f.astype(BF16), NT_DIMS, preferred_element_type=F32)
    carried = jnp.dot(q_h, c_aug.astype(BF16), preferred_element_type=F32)
    return k_f, v_aug, m_prev, c_aug, qk_t, carried


def _mlstm_head(h, gates, scores, o_ref, gn_ref, y_ref, chunk):
    _, _, gbt, cum, cum_t, causal = gates
    _, v_aug, m_prev, _, qk_t, carried = scores
    sl = slice(h * HEAD_DIM_M, (h + 1) * HEAD_DIM_M)
    b_col = cum[:, G_F + h:G_F + h + 1]
    b_row = cum_t[N_HEADS_M + h:N_HEADS_M + h + 1, :]
    i_row = gbt[h:h + 1, :]

    d_log = jnp.where(causal, b_col - b_row + i_row, -jnp.inf)
    inter = b_col + m_prev
    m_t = jnp.maximum(inter, jnp.max(d_log, axis=1, keepdims=True))
    w_intra = qk_t * jnp.exp(d_log - m_t)
    w_inter = jnp.exp(inter - m_t)
    intra = jnp.dot(w_intra.astype(BF16), v_aug[:, 0:HEAD_DIM_M], preferred_element_type=F32)
    num = intra + w_inter * carried[:, 0:HEAD_DIM_M]
    den = jnp.sum(w_intra, axis=1, keepdims=True) + w_inter * carried[:, HEAD_DIM_M:HEAD_DIM_M + 1]
    hid = num / jnp.maximum(jnp.abs(den), jnp.exp(-m_t))

    mu = jnp.mean(hid, axis=1, keepdims=True)
    dev = hid - mu
    var = jnp.mean(dev * dev, axis=1, keepdims=True)
    normed = dev * lax.rsqrt(var + LN_EPS) * gn_ref[:, sl]
    y_ref[:, sl] = (jax.nn.sigmoid(o_ref[:, sl]) * normed).astype(BF16)


def _mlstm_head_state(h, gates, scores, c_ref, m_ref, chunk):
    _, gb, _, cum, _, _ = gates
    k_f, v_aug, m_prev, c_aug, _, _ = scores
    b_col = cum[:, G_F + h:G_F + h + 1]
    i_col = gb[:, G_I + h:G_I + h + 1]
    b_last = b_col[chunk - 1:chunk, :]
    w_log = b_last - b_col + i_col
    m_new = jnp.maximum(b_last + m_prev, jnp.max(w_log, axis=0, keepdims=True))
    decay = jnp.exp(b_last + m_prev - m_new)
    kw_t = (k_f * jnp.exp(w_log - m_new)).T.astype(BF16)
    c_ref[h] = decay * c_aug + jnp.dot(kw_t, v_aug, preferred_element_type=F32)
    m_ref[h:h + 1, :] = jnp.broadcast_to(m_new, (1, LANES))


def _mlstm(qk_m, v_m, o_m, gates, gates_t, conv_w, conv_b, bias_slab, bias_col, gn_w, chunk):
    b, s, _ = qk_m.shape
    body = functools.partial(_mlstm_body, chunk=chunk, batch=b)
    const = lambda ci: (0, 0)
    return pl.pallas_call(
        body,
        grid=(s // chunk,),
        in_specs=[pl.BlockSpec((b, chunk, 2 * D_M), lambda ci: (0, ci, 0)),
                  pl.BlockSpec((b, chunk, D_M), lambda ci: (0, ci, 0)),
                  pl.BlockSpec((b, chunk, D_M), lambda ci: (0, ci, 0)),
                  pl.BlockSpec((b, chunk, LANES), lambda ci: (0, ci, 0)),
                  pl.BlockSpec((b, 2 * N_HEADS_M, chunk), lambda ci: (0, 0, ci)),
                  pl.BlockSpec((CONV_K, 2 * D_M), const),
                  pl.BlockSpec((1, 2 * D_M), const),
                  pl.BlockSpec((1, LANES), const),
                  pl.BlockSpec((2 * N_HEADS_M, 1), const),
                  pl.BlockSpec((1, D_M), const)],
        out_specs=pl.BlockSpec((b, chunk, D_M), lambda ci: (0, ci, 0)),
        out_shape=jax.ShapeDtypeStruct((b, s, D_M), BF16),
        scratch_shapes=[pltpu.VMEM((b, chunk + SUBLANES, 2 * D_M), F32),
                        pltpu.VMEM((b, N_HEADS_M, HEAD_DIM_M, 2 * LANES), F32),
                        pltpu.VMEM((b, SUBLANES, LANES), F32)],
        compiler_params=_cparams(("arbitrary",), 32),
        name="mlstm",
    )(qk_m, v_m, o_m, gates, gates_t, conv_w, conv_b, bias_slab, bias_col, gn_w)


COMB_GROUP_LANE = N_EXPERTS


def _route(logits):
    lane = lax.broadcasted_iota(I32, logits.shape, 1).astype(F32)
    g_mask = (lane >= N_EXPERTS) & (lane < N_EXPERTS + N_GROUPS)
    g_logit = jnp.where(g_mask, logits, -jnp.inf)
    g_exp = jnp.exp(g_logit - jnp.max(g_logit, axis=1, keepdims=True))
    g_prob = g_exp / jnp.sum(g_exp, axis=1, keepdims=True)
    g_p = jnp.max(g_prob, axis=1, keepdims=True)
    g_sel = jnp.min(jnp.where(g_mask & (g_prob == g_p), lane, float(LANES)), axis=1, keepdims=True) - N_EXPERTS
    e_mask = (lane >= g_sel * EXPERTS_PER_GROUP) & (lane < (g_sel + 1.0) * EXPERTS_PER_GROUP)
    e_logit = jnp.where(e_mask, logits, -jnp.inf)
    e_exp = jnp.exp(e_logit - jnp.max(e_logit, axis=1, keepdims=True))
    e_prob = e_exp / jnp.sum(e_exp, axis=1, keepdims=True)
    p1 = jnp.max(jnp.where(e_mask, e_prob, -1.0), axis=1, keepdims=True)
    i1 = jnp.min(jnp.where(e_mask & (e_prob == p1), lane, float(LANES)), axis=1, keepdims=True)
    rest = e_mask & (lane != i1)
    p2 = jnp.max(jnp.where(rest, e_prob, -1.0), axis=1, keepdims=True)
    i2 = jnp.min(jnp.where(rest & (e_prob == p2), lane, float(LANES)), axis=1, keepdims=True)
    total = p1 + p2
    comb = (jnp.where(lane == i1, p1 / total, 0.0) + jnp.where(lane == i2, p2 / total, 0.0)) * g_p
    return comb + jnp.where(lane == COMB_GROUP_LANE, g_sel, 0.0)


def _mix_body(ya_ref, ym_ref, x_ref, wo_ref, g_ref, b_ref, wr_ref, br_ref, x1_ref, comb_ref, *, alpha):
    mix = (jnp.dot(ya_ref[...], wo_ref[0:D_A, :], preferred_element_type=F32)
           + jnp.dot(ym_ref[...], wo_ref[D_A:D_A + D_M, :], preferred_element_type=F32))
    x1 = _layer_norm(alpha * x_ref[...] + mix, g_ref[...], b_ref[...])
    x1_ref[...] = x1
    x_hi = x1.astype(BF16)
    x_lo = (x1 - x_hi.astype(F32)).astype(BF16)
    w_hi = wr_ref[0]
    w_lo = wr_ref[1]
    logits = (jnp.dot(x_hi, w_hi, preferred_element_type=F32) + jnp.dot(x_lo, w_hi, preferred_element_type=F32)
              + jnp.dot(x_hi, w_lo, preferred_element_type=F32)) + br_ref[...]
    comb_ref[...] = _route(logits)


def _mix_ln_router(y_a, y_m, x2d, w_out, ln_g, ln_b, w_router, b_router, alpha, tm):
    n, d = x2d.shape
    const = lambda i: (0, 0)
    return pl.pallas_call(
        functools.partial(_mix_body, alpha=alpha),
        grid=(n // tm,),
        in_specs=[pl.BlockSpec((tm, D_A), lambda i: (i, 0)),
                  pl.BlockSpec((tm, D_M), lambda i: (i, 0)),
                  pl.BlockSpec((tm, d), lambda i: (i, 0)),
                  pl.BlockSpec((D_A + D_M, d), const),
                  pl.BlockSpec((1, d), const),
                  pl.BlockSpec((1, d), const),
                  pl.BlockSpec((2, d, LANES), lambda i: (0, 0, 0)),
                  pl.BlockSpec((1, LANES), const)],
        out_specs=[pl.BlockSpec((tm, d), lambda i: (i, 0)),
                   pl.BlockSpec((tm, LANES), lambda i: (i, 0))],
        out_shape=[jax.ShapeDtypeStruct((n, d), F32), jax.ShapeDtypeStruct((n, LANES), F32)],
        compiler_params=_cparams(("arbitrary",), 32),
        name="mix_ln_router",
    )(y_a, y_m, x2d, w_out, ln_g, ln_b, w_router, b_router)


MOE_ROWS = TM_MOE // N_GROUPS + 32


def _moe_body(x1_ref, comb_ref, wg_ref, wu_ref, wd_ref, g_ref, b_ref, p_ref, wpg_ref, wpp_ref, y_ref,
              xb_ref, before_ref, row_ref, *, alpha, tm):
    g = pl.program_id(1)
    g_f = g.astype(F32)

    @pl.when((pl.program_id(0) == 0) & (g == 0))
    def _():
        r = lax.broadcasted_iota(I32, (tm, tm), 0)
        c = lax.broadcasted_iota(I32, (tm, tm), 1)
        before_ref[...] = (c < r).astype(BF16)

    @pl.when(g == 0)
    def _():
        d = x1_ref.shape[1]
        xb_ref[:, 0:d] = x1_ref[...].astype(BF16)
        y_ref[...] = jnp.zeros(y_ref.shape, F32)
        comb = comb_ref[...]
        for k, term in enumerate(_split3(comb)):
            xb_ref[:, d + k * LANES:d + (k + 1) * LANES] = term
        group_row = comb.T[COMB_GROUP_LANE:COMB_GROUP_LANE + 1, :]
        sub = lax.broadcasted_iota(I32, (SUBLANES, tm), 0)
        member_t = (sub.astype(F32) == group_row) & (sub < N_GROUPS)
        ahead_t = lax.dot_general(jnp.where(member_t, 1.0, 0.0).astype(BF16), before_ref[...], NT_DIMS,
                                  preferred_element_type=F32)
        rank_row = jnp.sum(jnp.where(member_t, ahead_t, 0.0), axis=0, keepdims=True)
        row_ref[0] = jnp.broadcast_to(group_row, (SUBLANES, tm))
        row_ref[1] = jnp.broadcast_to(rank_row, (SUBLANES, tm))

    in_group_row = row_ref[0, 0:1, :] == g_f
    n_tokens = jnp.sum(jnp.where(in_group_row, 1.0, 0.0))
    n_blocks = (n_tokens.astype(I32) + MOE_ROWS - 1) // MOE_ROWS
    slot = lax.broadcasted_iota(I32, (MOE_ROWS, tm), 0).astype(F32)
    comb_lane = lax.broadcasted_iota(I32, (MOE_ROWS, LANES), 1)

    def block(b, carry):
        base = (b * MOE_ROWS).astype(F32)
        gather = jnp.where(in_group_row & (row_ref[1, 0:1, :] - base == slot), 1.0, 0.0).astype(BF16)
        d = x1_ref.shape[1]
        picked = jnp.dot(gather, xb_ref[...], preferred_element_type=F32)
        xc = picked[:, 0:d].astype(BF16)
        comb_c = picked[:, d:d + LANES] + picked[:, d + LANES:d + 2 * LANES] + picked[:, d + 2 * LANES:d + 3 * LANES]
        yc = jnp.zeros((MOE_ROWS, d), F32)
        for e in range(EXPERTS_PER_GROUP):
            gate = jnp.dot(xc, wg_ref[e], preferred_element_type=F32)
            up = jnp.dot(xc, wu_ref[e], preferred_element_type=F32)
            weight = jnp.sum(jnp.where(comb_lane == g * EXPERTS_PER_GROUP + e, comb_c, 0.0), axis=1, keepdims=True)
            hidden = gate * jax.nn.sigmoid(gate) * up * weight
            yc = yc + jnp.dot(hidden.astype(BF16), wd_ref[e], preferred_element_type=F32)
        yc_hi = yc.astype(BF16)
        yc_lo = (yc - yc_hi.astype(F32)).astype(BF16)
        tn = (((0,), (0,)), ((), ()))
        y_ref[...] += lax.dot_general(jnp.concatenate([gather, gather], axis=0),
                                      jnp.concatenate([yc_hi, yc_lo], axis=0), tn, preferred_element_type=F32)
        return carry

    lax.fori_loop(0, n_blocks, block, 0)

    @pl.when(g == N_GROUPS - 1)
    def _():
        x2 = _layer_norm(alpha * x1_ref[...] + y_ref[...], g_ref[...], b_ref[...])
        gate = jax.nn.sigmoid(jnp.dot(x2.astype(BF16), wpg_ref[...], preferred_element_type=F32))
        proj = jnp.dot(p_ref[...].astype(BF16), wpp_ref[...], preferred_element_type=F32)
        y_ref[...] = x2 + gate * proj


def _moe_ln_ple(x1, comb, w_gate, w_up, w_down, ln_g, ln_b, p2d, w_ple_gate, w_ple_proj, alpha, tm):
    n, d = x1.shape
    dp = p2d.shape[1]
    const = lambda i, g: (0, 0)
    once = pl.Buffered(1)
    grouped = lambda w: w.reshape((N_GROUPS, EXPERTS_PER_GROUP) + w.shape[1:])
    return pl.pallas_call(
        functools.partial(_moe_body, alpha=alpha, tm=tm),
        grid=(n // tm, N_GROUPS),
        in_specs=[pl.BlockSpec((tm, d), lambda i, g: (i, 0), pipeline_mode=once),
                  pl.BlockSpec((tm, LANES), lambda i, g: (i, 0)),
                  pl.BlockSpec((None, EXPERTS_PER_GROUP, d, D_EXPERT), lambda i, g: (g, 0, 0, 0)),
                  pl.BlockSpec((None, EXPERTS_PER_GROUP, d, D_EXPERT), lambda i, g: (g, 0, 0, 0)),
                  pl.BlockSpec((None, EXPERTS_PER_GROUP, D_EXPERT, d), lambda i, g: (g, 0, 0, 0)),
                  pl.BlockSpec((1, d), const),
                  pl.BlockSpec((1, d), const),
                  pl.BlockSpec((tm, dp), lambda i, g: (i, 0), pipeline_mode=once),
                  pl.BlockSpec((d, d), const, pipeline_mode=once),
                  pl.BlockSpec((dp, d), const, pipeline_mode=once)],
        out_specs=pl.BlockSpec((tm, d), lambda i, g: (i, 0)),
        out_shape=jax.ShapeDtypeStruct((n, d), F32),
        scratch_shapes=[pltpu.VMEM((tm, d + 3 * LANES), BF16),
                        pltpu.VMEM((tm, tm), BF16),
                        pltpu.VMEM((2, SUBLANES, tm), F32)],
        compiler_params=_cparams(("arbitrary", "arbitrary"), 58),
        name="moe_ln_ple",
    )(x1, comb, grouped(w_gate), grouped(w_up), grouped(w_down), ln_g, ln_b, p2d, w_ple_gate, w_ple_proj)


def _tile(n, want):
    t = min(want, n)
    assert n % t == 0
    return t


def _layer(x, p_i, w_in, b_igate, b_fgate, conv_w, conv_b, gn_w, w_out, ln1_g, ln1_b,
           rg_w, rg_b, re_w, re_b, e_gate, e_up, e_down, ln2_g, ln2_b, ple_gate_w, ple_proj_w, alpha):
    b, s, d = x.shape
    n = b * s
    x2d = x.reshape(n, d)

    tk = _tile(s, TK_DSA)
    w_packed, w_vt = _pack_w_in(w_in)
    qk, v_t, q_idx, k4, gates, qk_m, v_m, o_m = _in_proj(x2d, w_packed, w_vt, tk)

    gates3 = gates.reshape(b, s, LANES)
    y_a = _dsa_attention(qk.reshape(2 * SLABS_PER_TENSOR, b, s, LANES),
                         v_t.reshape(SLABS_PER_TENSOR, b, s // tk, LANES, tk),
                         q_idx.reshape(b, s, N_IDX_HEADS * IDX_DIM), k4.reshape(b, s, LANES), gates3,
                         _tile(s, TQ_DSA), tk)

    gate_bias = jnp.concatenate([b_igate, b_fgate]).astype(F32)
    bias_slab = jnp.zeros((1, LANES), F32).at[0, G_I:G_I + 2 * N_HEADS_M].set(gate_bias)
    gates_t = jnp.swapaxes(gates3[:, :, G_I:G_I + 2 * N_HEADS_M], 1, 2)
    y_m = _mlstm(qk_m.reshape(b, s, 2 * D_M), v_m.reshape(b, s, D_M), o_m.reshape(b, s, D_M), gates3, gates_t,
                 conv_w, conv_b.reshape(1, -1), bias_slab, gate_bias.reshape(-1, 1), gn_w.reshape(1, -1),
                 _tile(s, CHUNK_MLSTM))

    w_router = jnp.concatenate([re_w, rg_w, jnp.zeros((d, LANES - N_EXPERTS - N_GROUPS), F32)], axis=1)
    w_router_hi = w_router.astype(BF16)
    w_router = jnp.stack([w_router_hi, (w_router - w_router_hi.astype(F32)).astype(BF16)])
    b_router = jnp.concatenate([re_b, rg_b, jnp.zeros((LANES - N_EXPERTS - N_GROUPS,), F32)]).reshape(1, LANES)
    x1, comb = _mix_ln_router(y_a.reshape(n, D_A), y_m.reshape(n, D_M), x2d, w_out.astype(BF16),
                              ln1_g.reshape(1, d), ln1_b.reshape(1, d), w_router, b_router, alpha, _tile(n, TM_MIX))

    out = _moe_ln_ple(x1, comb, e_gate.astype(BF16), e_up.astype(BF16), e_down.astype(BF16),
                      ln2_g.reshape(1, d), ln2_b.reshape(1, d), p_i.reshape(n, -1), ple_gate_w.astype(BF16),
                      ple_proj_w.astype(BF16), alpha, _tile(n, TM_MOE))
    return out.reshape(b, s, d)


def kernel(x, p, w_in, b_igate, b_fgate, conv_w, conv_b, gn_w, w_out, ln1_g, ln1_b, router_group_w, router_group_b,
           router_expert_w, router_expert_b, expert_w_gate, expert_w_up, expert_w_down, ln2_g, ln2_b, ple_gate_w,
           ple_proj_w):
    depth = w_in.shape[0]
    alpha = (2.0 * depth) ** 0.25
    for i in range(depth):
        x = _layer(x, p[i], w_in[i], b_igate[i], b_fgate[i], conv_w[i], conv_b[i], gn_w[i], w_out[i], ln1_g[i],
                   ln1_b[i], router_group_w[i], router_group_b[i], router_expert_w[i], router_expert_b[i],
                   expert_w_gate[i], expert_w_up[i], expert_w_down[i], ln2_g[i], ln2_b[i], ple_gate_w[i],
                   ple_proj_w[i], alpha)
    return x
```

```python
import functools

import numpy as np
import jax
import jax.numpy as jnp
from jax import lax
from jax.experimental import pallas as pl
from jax.experimental.pallas import tpu as pltpu

F32 = jnp.float32
BF16 = jnp.bfloat16
I32 = jnp.int32

N_HEADS_A = 8
HEAD_DIM_A = 64
D_A = N_HEADS_A * HEAD_DIM_A
N_IDX_HEADS = 8
IDX_DIM = 32
TOPK_MAX = 256
N_HEADS_M = 4
HEAD_DIM_M = 128
D_M = N_HEADS_M * HEAD_DIM_M
CONV_K = 4
N_GROUPS = 4
EXPERTS_PER_GROUP = 8
N_EXPERTS = N_GROUPS * EXPERTS_PER_GROUP
D_EXPERT = 256
LN_EPS = 1e-5
NEG_BIG = -1e30
IN_SPLITS = (D_A, D_A, D_A, N_IDX_HEADS * IDX_DIM, IDX_DIM, N_IDX_HEADS, D_M, D_M, D_M, D_M, N_HEADS_M, N_HEADS_M)

LANES = 128
SUBLANES = 8
MIB = 1024 * 1024

C_QK = 0
C_QIDX = C_QK + 2 * D_A
C_K4 = C_QIDX + N_IDX_HEADS * IDX_DIM
C_GATE = C_K4 + LANES
C_QKM = C_GATE + LANES
C_VM = C_QKM + 2 * D_M
C_OM = C_VM + D_M
C_END = C_OM + D_M
SLABS_PER_TENSOR = D_A // LANES
HEADS_PER_SLAB = LANES // HEAD_DIM_A
G_W = 0
G_I = N_IDX_HEADS
G_F = G_I + N_HEADS_M

INT_MIN = -2 ** 31

TQ_DSA = LANES
TK_DSA = 1024
TM_IN_PROJ = TK_DSA
CHUNK_MLSTM = 256
TM_MIX = 1024
TM_MOE = 1024

NT_DIMS = (((1,), (1,)), ((), ()))


def _cparams(semantics, vmem_mib):
    return pltpu.CompilerParams(dimension_semantics=semantics, vmem_limit_bytes=int(vmem_mib * MIB))


def _layer_norm(r, g, b):
    mu = jnp.mean(r, axis=-1, keepdims=True)
    d = r - mu
    var = jnp.mean(d * d, axis=-1, keepdims=True)
    return d * lax.rsqrt(var + LN_EPS) * g + b


def _pack_w_in(w_in):
    d = w_in.shape[0]
    cuts = [int(c) for c in np.cumsum(IN_SPLITS)[:-1]]
    q_a, k_a, v_a, q_idx, k_idx, w_idx, q_m, k_m, v_m, o_m, i_m, f_m = jnp.split(w_in, cuts, axis=1)
    k4 = jnp.tile(k_idx, (1, LANES // IDX_DIM))
    gate = jnp.concatenate([w_idx, i_m, f_m, jnp.zeros((d, LANES - G_F - N_HEADS_M), w_in.dtype)], axis=1)
    packed = jnp.concatenate([q_a, k_a, q_idx, k4, gate, q_m, k_m, v_m, o_m], axis=1).astype(BF16)
    return packed, v_a.T.astype(BF16)


def _in_proj_body(x_ref, w_ref, wvt_ref, qk_ref, vt_ref, qi_ref, k4_ref, g_ref, qkm_ref, vm_ref, om_ref):
    xb = x_ref[...].astype(BF16)

    def proj(lo, hi):
        return jnp.dot(xb, w_ref[:, lo:hi], preferred_element_type=F32)

    qk = proj(C_QK, C_QIDX).astype(BF16)
    for c in range(2 * SLABS_PER_TENSOR):
        qk_ref[c] = qk[:, c * LANES:(c + 1) * LANES]
    v_t = lax.dot_general(wvt_ref[...], xb, NT_DIMS, preferred_element_type=F32).astype(BF16)
    for c in range(SLABS_PER_TENSOR):
        vt_ref[c, 0] = v_t[c * LANES:(c + 1) * LANES, :]
    qi_ref[...] = proj(C_QIDX, C_K4).astype(BF16)
    k4_ref[...] = proj(C_K4, C_GATE).astype(BF16)
    g_ref[...] = proj(C_GATE, C_QKM)
    qkm_ref[...] = proj(C_QKM, C_VM)
    vm_ref[...] = proj(C_VM, C_OM).astype(BF16)
    om_ref[...] = proj(C_OM, C_END)


def _in_proj(x2d, w_packed, w_vt, tm):
    n, d = x2d.shape
    widths = [(C_K4 - C_QIDX, BF16), (C_GATE - C_K4, BF16), (C_QKM - C_GATE, F32),
              (C_VM - C_QKM, F32), (C_OM - C_VM, BF16), (C_END - C_OM, F32)]
    return pl.pallas_call(
        _in_proj_body,
        grid=(n // tm,),
        in_specs=[pl.BlockSpec((tm, d), lambda i: (i, 0)),
                  pl.BlockSpec((d, C_END), lambda i: (0, 0)),
                  pl.BlockSpec((D_A, d), lambda i: (0, 0))],
        out_specs=([pl.BlockSpec((2 * SLABS_PER_TENSOR, tm, LANES), lambda i: (0, i, 0)),
                    pl.BlockSpec((SLABS_PER_TENSOR, 1, LANES, tm), lambda i: (0, i, 0, 0))]
                   + [pl.BlockSpec((tm, w), lambda i: (i, 0)) for w, _ in widths]),
        out_shape=([jax.ShapeDtypeStruct((2 * SLABS_PER_TENSOR, n, LANES), BF16),
                    jax.ShapeDtypeStruct((SLABS_PER_TENSOR, n // tm, LANES, tm), BF16)]
                   + [jax.ShapeDtypeStruct((n, w), dt) for w, dt in widths]),
        compiler_params=_cparams(("arbitrary",), 48),
        name="in_proj",
    )(x2d, w_packed, w_vt)


def _float_to_ordered_int(value):
    bits = lax.bitcast_convert_type(value, I32)
    return bits ^ ((bits >> 31) & jnp.int32(0x7FFFFFFF))


def _ordered_int_to_float(key):
    bits = key ^ ((key >> 31) & jnp.int32(0x7FFFFFFF))
    return lax.bitcast_convert_type(bits, F32)


SEARCH_FIXED_PASSES = 12
SEARCH_FREE_PASSES = 16
SEARCH_MAX_PASSES = SEARCH_FREE_PASSES + 33
ZERO_TIE_MARGIN = 128
CHAINS = 4
POS_RADIX = 64
POS_TERMS = 3
LOG2_E = float(np.log2(np.e))


def _dsa_body(qa_ref, ka_ref, vt_ref, qi_ref, k4_ref, g_ref, o_ref,
              sc_ref, qm_ref, qh_ref, pos_ref, mask_ref, acc_ref, m_ref, l_ref, *, seq, topk, tq, tk):
    t0 = pl.program_id(1) * tq
    n_kb = (t0 + tq + tk - 1) // tk
    groups = tk // (CHAINS * SUBLANES)
    shape4 = (groups, CHAINS, SUBLANES, tq)

    def chunks(x):
        return x.reshape(shape4)

    def fold(x):
        out = x[0]
        for c in range(1, CHAINS):
            out = out + x[c]
        return out

    def rows8(row):
        return jnp.broadcast_to(row, (SUBLANES, tq))

    key_off = ((lax.broadcasted_iota(I32, shape4, 0) * CHAINS + lax.broadcasted_iota(I32, shape4, 1)) * SUBLANES
               + lax.broadcasted_iota(I32, shape4, 2))
    q_pos = t0 + lax.broadcasted_iota(I32, shape4, 3)
    lane = lax.broadcasted_iota(I32, (tq, LANES), 1)

    for h in range(N_IDX_HEADS):
        per_slab = LANES // IDX_DIM
        slab = qi_ref[:, (h // per_slab) * LANES:(h // per_slab + 1) * LANES]
        lo = (h % per_slab) * IDX_DIM
        qm_ref[h * tq:(h + 1) * tq, :] = jnp.where((lane >= lo) & (lane < lo + IDX_DIM), slab, jnp.zeros_like(slab))
    for h in range(N_HEADS_A):
        p, hh = divmod(h, HEADS_PER_SLAB)
        slab = qa_ref[p].astype(F32) * (HEAD_DIM_A ** -0.5 * LOG2_E)
        in_head = (lane >= hh * HEAD_DIM_A) & (lane < (hh + 1) * HEAD_DIM_A)
        qh_ref[p, hh * tq:(hh + 1) * tq, 0:LANES] = jnp.where(in_head, slab, 0.0).astype(BF16)

    @pl.when((pl.program_id(0) == 0) & (pl.program_id(1) == 0))
    def _():
        for h in range(N_HEADS_A):
            p, hh = divmod(h, HEADS_PER_SLAB)
            slope = 2.0 ** (-8.0 * (h + 1) / N_HEADS_A)
            weight = jnp.where(lane < POS_TERMS, slope * POS_RADIX * LOG2_E,
                               jnp.where(lane < 2 * POS_TERMS, slope * LOG2_E, 0.0))
            terms = _split3(weight)
            feat = jnp.where(lane % POS_TERMS == 0, terms[0], jnp.where(lane % POS_TERMS == 1, terms[1], terms[2]))
            qh_ref[p, hh * tq:(hh + 1) * tq, LANES:2 * LANES] = feat
        pos_lane = lax.broadcasted_iota(I32, (tk, LANES), 1)
        pos_row = lax.broadcasted_iota(I32, (tk, LANES), 0)
        for j in range(seq // tk):
            pos = pos_row + j * tk
            pos_ref[j * tk:(j + 1) * tk, :] = jnp.where(
                pos_lane < POS_TERMS, pos // POS_RADIX,
                jnp.where(pos_lane < 2 * POS_TERMS, pos % POS_RADIX, 0)).astype(F32).astype(BF16)
    w_rows = g_ref[...].T[G_W:G_W + N_IDX_HEADS, :] * ((IDX_DIM * N_IDX_HEADS) ** -0.5)

    def score_block(j, carry):
        hi_part, lo_part, zero_part = carry
        k0 = pl.multiple_of(j * tk, tk)
        logits = lax.dot_general(k4_ref[pl.ds(k0, tk), :], qm_ref[...], NT_DIMS,
                                 preferred_element_type=F32)
        score = jnp.zeros(shape4, F32)
        for h in range(N_IDX_HEADS):
            score = score + rows8(w_rows[h:h + 1, :]) * jnp.maximum(chunks(logits[:, h * tq:(h + 1) * tq]), 0.0)
        causal = k0 + key_off <= q_pos
        kept = jnp.where(causal, score, NEG_BIG)
        sc_ref[j] = kept
        hi_part = jnp.maximum(hi_part, jnp.max(kept, axis=0))
        lo_part = jnp.minimum(lo_part, jnp.min(score, axis=0))
        zero_part = zero_part + jnp.sum(jnp.where(kept >= 0.0, 1.0, 0.0), axis=0)
        return hi_part, lo_part, zero_part

    part_shape = (CHAINS, SUBLANES, tq)
    hi_part, lo_part, zero_part = lax.fori_loop(
        0, n_kb, score_block,
        (jnp.full(part_shape, -jnp.inf, F32), jnp.full(part_shape, jnp.inf, F32), jnp.zeros(part_shape, F32)))
    row_max = jnp.max(jnp.max(hi_part, axis=0), axis=0, keepdims=True)
    row_min = jnp.min(jnp.min(lo_part, axis=0), axis=0, keepdims=True)

    n_tail = (seq - n_kb * tk).astype(F32)

    def count(thr, strict):
        thr8 = rows8(thr)

        def body(j, acc):
            s = sc_ref[j]
            return acc + jnp.sum(jnp.where((s > thr8) if strict else (s >= thr8), 1.0, 0.0), axis=0)

        acc = lax.fori_loop(0, n_kb, body, jnp.zeros(part_shape, F32))
        tail_hit = (NEG_BIG > thr) if strict else (NEG_BIG >= thr)
        return jnp.sum(fold(acc), axis=0, keepdims=True) + jnp.where(tail_hit, n_tail, 0.0)

    n_causal = (t0 + 1 + lax.broadcasted_iota(I32, (1, tq), 1)).astype(F32)
    few = n_causal < topk
    lo0 = jnp.where(few, _float_to_ordered_int(jnp.full((1, tq), NEG_BIG, F32)), _float_to_ordered_int(row_min))
    cnt_lo0 = jnp.where(few, float(seq), n_causal)
    hi0 = _float_to_ordered_int(row_max) + 1
    log_k = float(np.log(topk))

    def log_excess(cnt):
        return jnp.log(jnp.maximum(cnt, 0.5)) - log_k

    def probe(it, bracket):
        lo, hi, cnt_lo, f_lo, f_hi, last, done_i = bracket
        done = done_i != 0
        it_v = jnp.zeros((1, tq), I32) + it
        lo_v = _ordered_int_to_float(lo)
        hi_v = _ordered_int_to_float(hi)
        guess = _float_to_ordered_int(lo_v + (f_lo / (f_lo - f_hi)) * (hi_v - lo_v))
        guess = jnp.where((it_v == 1) & (lo == 0) & (cnt_lo < topk + ZERO_TIE_MARGIN), 1, guess)
        middle = lo + lax.shift_right_logical(hi - lo, jnp.ones((1, tq), I32))
        usable = (guess > lo) & (guess < hi) & (it_v < SEARCH_FREE_PASSES)
        cand = jnp.where(usable, guess, middle)
        cnt = count(_ordered_int_to_float(cand), strict=False)
        f_c = log_excess(cnt)
        take = (cnt >= topk) & ~done
        drop = (cnt < topk) & ~done
        f_hi = jnp.where(take, jnp.where(last == 1, 0.5 * f_hi, f_hi), jnp.where(drop, f_c, f_hi))
        f_lo = jnp.where(drop, jnp.where(last == -1, 0.5 * f_lo, f_lo), jnp.where(take, f_c, f_lo))
        last = jnp.where(take, 1, jnp.where(drop, -1, last))
        lo = jnp.where(take, cand, lo)
        cnt_lo = jnp.where(take, cnt, cnt_lo)
        hi = jnp.where(drop, cand, hi)
        done = done | (cnt_lo == topk) | ((hi - lo) == 1)
        return lo, hi, cnt_lo, f_lo, f_hi, last, done.astype(I32)

    def n_active(bracket):
        return jnp.sum(jnp.where(bracket[6] != 0, 0.0, 1.0))

    def search_step(state):
        bracket = probe(state[0], state[2:])
        return (state[0] + 1, n_active(bracket)) + bracket

    def search_on(state):
        return (state[0] < SEARCH_MAX_PASSES) & (state[1] > 0.0)

    cnt_zero = jnp.sum(fold(zero_part), axis=0, keepdims=True)
    inside = (lo0 < 0) & (hi0 > 0) & ~few
    take0 = inside & (cnt_zero >= topk)
    drop0 = inside & (cnt_zero < topk)
    lo1 = jnp.where(take0, 0, lo0)
    hi1 = jnp.where(drop0, 0, hi0)
    cnt_lo1 = jnp.where(take0, cnt_zero, cnt_lo0)
    done1 = few | (cnt_lo1 == topk) | ((hi1 - lo1) == 1)
    bracket = (lo1, hi1, cnt_lo1, log_excess(cnt_lo1), log_excess(jnp.where(drop0, cnt_zero, 0.0)),
               jnp.where(take0, 1, jnp.where(drop0, -1, 0)), done1.astype(I32))
    bracket = lax.fori_loop(1, 1 + SEARCH_FIXED_PASSES, probe, bracket)
    state = lax.while_loop(search_on, search_step,
                           (jnp.int32(1 + SEARCH_FIXED_PASSES), n_active(bracket)) + bracket)
    thr = _ordered_int_to_float(state[2])
    cnt_thr = state[4]
    thr8 = rows8(thr)

    @pl.when(jnp.max(cnt_thr) > topk)
    def _():
        need = topk - count(thr, strict=True)
        tri_r = lax.broadcasted_iota(I32, (tk, tk), 0)
        tri_c = lax.broadcasted_iota(I32, (tk, tk), 1)
        earlier = (tri_c < tri_r).astype(BF16)

        def drop_late_ties(j, seen):
            s = sc_ref[j]
            eq = s == thr8
            eq_f = jnp.where(eq, 1.0, 0.0)
            rank = seen + jnp.dot(earlier, eq_f.reshape(tk, tq).astype(BF16), preferred_element_type=F32)
            sc_ref[j] = jnp.where(eq & (chunks(rank) >= rows8(need)), -jnp.inf, s)
            return seen + jnp.sum(fold(jnp.sum(eq_f, axis=0)), axis=0, keepdims=True)

        lax.fori_loop(0, n_kb, drop_late_ties, jnp.zeros((1, tq), F32))

    m_ref[...] = jnp.full(m_ref.shape, NEG_BIG, F32)
    l_ref[...] = jnp.zeros(l_ref.shape, F32)
    acc_ref[...] = jnp.zeros(acc_ref.shape, F32)

    def attend(j, carry):
        k0 = pl.multiple_of(j * tk, tk)
        mask_ref[...] = jnp.where((sc_ref[j] >= thr8) & (k0 + key_off <= q_pos), 0.0, -jnp.inf)
        pos_feat = pos_ref[pl.ds(k0, tk), :]

        scores = []
        for p in range(SLABS_PER_TENSOR):
            keys = jnp.concatenate([ka_ref[p, pl.ds(k0, tk), :], pos_feat], axis=1)
            scores.append(lax.dot_general(keys, qh_ref[p], NT_DIMS, preferred_element_type=F32))
        weights = []
        for p in range(SLABS_PER_TENSOR):
            for hh in range(HEADS_PER_SLAB):
                s = chunks(scores[p][:, hh * tq:(hh + 1) * tq]) + mask_ref[...]
                m_old = m_ref[p, hh]
                m_blk = jnp.max(fold_max(jnp.max(s, axis=0)), axis=0, keepdims=True)
                m_new = jnp.maximum(m_old, rows8(m_blk))
                alpha = jnp.exp2(m_old - m_new)
                m_ref[p, hh] = m_new
                weights.append((alpha, jnp.exp2(s - m_new).reshape(tk, tq).astype(BF16)))
        ones_rows = jnp.ones((SUBLANES, tk), BF16)
        for p in range(SLABS_PER_TENSOR):
            for hh in range(HEADS_PER_SLAB):
                rows = slice(hh * HEAD_DIM_A, (hh + 1) * HEAD_DIM_A)
                alpha, probs = weights[p * HEADS_PER_SLAB + hh]
                values = jnp.concatenate([vt_ref[p, j, rows, :], ones_rows], axis=0)
                pv = jnp.dot(values, probs, preferred_element_type=F32)
                old = acc_ref[p, rows, :].reshape(HEAD_DIM_A // SUBLANES, SUBLANES, tq)
                acc_ref[p, rows, :] = (alpha * old).reshape(HEAD_DIM_A, tq) + pv[0:HEAD_DIM_A]
                l_ref[p, hh] = alpha * l_ref[p, hh] + pv[HEAD_DIM_A:HEAD_DIM_A + SUBLANES]
        return carry

    def fold_max(x):
        out = x[0]
        for c in range(1, CHAINS):
            out = jnp.maximum(out, x[c])
        return out

    lax.fori_loop(0, n_kb, attend, 0)

    for p in range(SLABS_PER_TENSOR):
        halves = []
        for hh in range(HEADS_PER_SLAB):
            rows = slice(hh * HEAD_DIM_A, (hh + 1) * HEAD_DIM_A)
            acc = acc_ref[p, rows, :].reshape(HEAD_DIM_A // SUBLANES, SUBLANES, tq)
            halves.append((acc / l_ref[p, hh]).reshape(HEAD_DIM_A, tq))
        o_ref[:, p * LANES:(p + 1) * LANES] = jnp.concatenate(halves, axis=0).T.astype(BF16)


def _dsa_attention(qk, v_t, q_idx, k4, gates, tq, tk):
    _, b, s, _ = qk.shape
    assert tq == LANES and s // POS_RADIX <= 256
    topk = min(TOPK_MAX, s // 4)
    once = pl.Buffered(1)
    groups = tk // (CHAINS * SUBLANES)
    body = functools.partial(_dsa_body, seq=s, topk=topk, tq=tq, tk=tk)
    return pl.pallas_call(
        body,
        grid=(b, s // tq),
        in_specs=[pl.BlockSpec((SLABS_PER_TENSOR, None, tq, LANES), lambda bi, qi: (0, bi, qi, 0)),
                  pl.BlockSpec((SLABS_PER_TENSOR, None, s, LANES), lambda bi, qi: (1, bi, 0, 0), pipeline_mode=once),
                  pl.BlockSpec((SLABS_PER_TENSOR, None, s // tk, LANES, tk), lambda bi, qi: (0, bi, 0, 0, 0),
                               pipeline_mode=once),
                  pl.BlockSpec((None, tq, N_IDX_HEADS * IDX_DIM), lambda bi, qi: (bi, qi, 0)),
                  pl.BlockSpec((None, s, LANES), lambda bi, qi: (bi, 0, 0), pipeline_mode=once),
                  pl.BlockSpec((None, tq, LANES), lambda bi, qi: (bi, qi, 0))],
        out_specs=pl.BlockSpec((None, tq, D_A), lambda bi, qi: (bi, qi, 0)),
        out_shape=jax.ShapeDtypeStruct((b, s, D_A), BF16),
        scratch_shapes=[pltpu.VMEM((s // tk, groups, CHAINS, SUBLANES, tq), F32),
                        pltpu.VMEM((N_IDX_HEADS * tq, LANES), BF16),
                        pltpu.VMEM((SLABS_PER_TENSOR, HEADS_PER_SLAB * tq, 2 * LANES), BF16),
                        pltpu.VMEM((s, LANES), BF16),
                        pltpu.VMEM((groups, CHAINS, SUBLANES, tq), F32),
                        pltpu.VMEM((SLABS_PER_TENSOR, LANES, tq), F32),
                        pltpu.VMEM((SLABS_PER_TENSOR, HEADS_PER_SLAB, SUBLANES, tq), F32),
                        pltpu.VMEM((SLABS_PER_TENSOR, HEADS_PER_SLAB, SUBLANES, tq), F32)],
        compiler_params=_cparams(("arbitrary", "arbitrary"), 48),
        name="dsa_attention",
    )(qk, qk, v_t, q_idx, k4, gates)


def _log_sigmoid(x):
    return jnp.minimum(x, 0.0) - jnp.log(1.0 + jnp.exp(-jnp.abs(x)))


def _split3(a):
    hi = a.astype(BF16)
    r1 = a - hi.astype(F32)
    mid = r1.astype(BF16)
    lo = (r1 - mid.astype(F32)).astype(BF16)
    return hi, mid, lo


def _mlstm_body(qk_ref, v_ref, o_ref, g_ref, gt_ref, cw_ref, cb_ref, bg_ref, bgt_ref, gn_ref, y_ref,
                xe_ref, c_ref, m_ref, *, chunk, batch):
    @pl.when(pl.program_id(0) == 0)
    def _():
        xe_ref[:, 0:SUBLANES, :] = jnp.zeros((batch, SUBLANES, 2 * D_M), F32)
        c_ref[...] = jnp.zeros(c_ref.shape, F32)
        m_ref[...] = jnp.zeros(m_ref.shape, F32)

    gates = [_mlstm_gates(qk_ref.at[bi], g_ref.at[bi], gt_ref.at[bi], cw_ref, cb_ref, bg_ref, bgt_ref,
                          xe_ref.at[bi], chunk) for bi in range(batch)]
    chains = [(h, bi) for h in range(N_HEADS_M) for bi in range(batch)]
    scores = [_mlstm_head_scores(h, gates[bi], v_ref.at[bi], c_ref.at[bi], m_ref.at[bi], chunk) for h, bi in chains]
    for (h, bi), sc in zip(chains, scores):
        _mlstm_head(h, gates[bi], sc, o_ref.at[bi], gn_ref, y_ref.at[bi], chunk)
    for (h, bi), sc in zip(chains, scores):
        _mlstm_head_state(h, gates[bi], sc, c_ref.at[bi], m_ref.at[bi], chunk)


def _mlstm_gates(qk_ref, g_ref, gt_ref, cw_ref, cb_ref, bg_ref, bgt_ref, xe_ref, chunk):
    xe_ref[SUBLANES:SUBLANES + chunk, :] = qk_ref[...]
    conv = cb_ref[...]
    for j in range(CONV_K):
        start = SUBLANES - (CONV_K - 1) + j
        conv = conv + cw_ref[j:j + 1, :] * xe_ref[start:start + chunk, :]
    qk = conv * jax.nn.sigmoid(conv)
    xe_ref[0:SUBLANES, :] = xe_ref[chunk:chunk + SUBLANES, :]

    gb = g_ref[...] + bg_ref[...]
    gbt = gt_ref[...] + bgt_ref[...]
    r = lax.broadcasted_iota(I32, (chunk, chunk), 0)
    c = lax.broadcasted_iota(I32, (chunk, chunk), 1)
    causal = r >= c
    incl = causal.astype(BF16)
    incl_t = (r <= c).astype(BF16)
    cum = jnp.zeros((chunk, LANES), F32)
    for term in _split3(_log_sigmoid(gb)):
        cum = cum + jnp.dot(incl, term, preferred_element_type=F32)
    cum_t = jnp.zeros((2 * N_HEADS_M, chunk), F32)
    for term in _split3(_log_sigmoid(gbt)):
        cum_t = cum_t + jnp.dot(term, incl_t, preferred_element_type=F32)

    return qk, gb, gbt, cum, cum_t, causal


def _mlstm_head_scores(h, gates, v_ref, c_ref, m_ref, chunk):
    qk = gates[0]
    lane = lax.broadcasted_iota(I32, (chunk, LANES), 1)
    ones_col = jnp.where(lane == 0, 1.0, 0.0).astype(BF16)
    sl = slice(h * HEAD_DIM_M, (h + 1) * HEAD_DIM_M)
    q_h = (qk[:, sl] * (HEAD_DIM_M ** -0.5)).astype(BF16)
    k_f = qk[:, D_M + h * HEAD_DIM_M:D_M + (h + 1) * HEAD_DIM_M]
    v_aug = jnp.concatenate([v_ref[:, sl], ones_col], axis=1)
    m_prev = m_ref[h:h + 1, 0:1]
    c_aug = c_ref[h]
    qk_t = lax.dot_general(q_h, k_f.astype(BF16), NT_DIMS, preferred_element_type=F32)
    carried = jnp.dot(q_h, c_aug.astype(BF16), preferred_element_type=F32)
    return k_f, v_aug, m_prev, c_aug, qk_t, carried


def _mlstm_head(h, gates, scores, o_ref, gn_ref, y_ref, chunk):
    _, _, gbt, cum, cum_t, causal = gates
    _, v_aug, m_prev, _, qk_t, carried = scores
    sl = slice(h * HEAD_DIM_M, (h + 1) * HEAD_DIM_M)
    b_col = cum[:, G_F + h:G_F + h + 1]
    b_row = cum_t[N_HEADS_M + h:N_HEADS_M + h + 1, :]
    i_row = gbt[h:h + 1, :]

    d_log = jnp.where(causal, b_col - b_row + i_row, -jnp.inf)
    inter = b_col + m_prev
    m_t = jnp.maximum(inter, jnp.max(d_log, axis=1, keepdims=True))
    w_intra = qk_t * jnp.exp(d_log - m_t)
    w_inter = jnp.exp(inter - m_t)
    intra = jnp.dot(w_intra.astype(BF16), v_aug[:, 0:HEAD_DIM_M], preferred_element_type=F32)
    num = intra + w_inter * carried[:, 0:HEAD_DIM_M]
    den = jnp.sum(w_intra, axis=1, keepdims=True) + w_inter * carried[:, HEAD_DIM_M:HEAD_DIM_M + 1]
    hid = num / jnp.maximum(jnp.abs(den), jnp.exp(-m_t))

    mu = jnp.mean(hid, axis=1, keepdims=True)
    dev = hid - mu
    var = jnp.mean(dev * dev, axis=1, keepdims=True)
    normed = dev * lax.rsqrt(var + LN_EPS) * gn_ref[:, sl]
    y_ref[:, sl] = (jax.nn.sigmoid(o_ref[:, sl]) * normed).astype(BF16)


def _mlstm_head_state(h, gates, scores, c_ref, m_ref, chunk):
    _, gb, _, cum, _, _ = gates
    k_f, v_aug, m_prev, c_aug, _, _ = scores
    b_col = cum[:, G_F + h:G_F + h + 1]
    i_col = gb[:, G_I + h:G_I + h + 1]
    b_last = b_col[chunk - 1:chunk, :]
    w_log = b_last - b_col + i_col
    m_new = jnp.maximum(b_last + m_prev, jnp.max(w_log, axis=0, keepdims=True))
    decay = jnp.exp(b_last + m_prev - m_new)
    kw_t = (k_f * jnp.exp(w_log - m_new)).T.astype(BF16)
    c_ref[h] = decay * c_aug + jnp.dot(kw_t, v_aug, preferred_element_type=F32)
    m_ref[h:h + 1, :] = jnp.broadcast_to(m_new, (1, LANES))


def _mlstm(qk_m, v_m, o_m, gates, gates_t, conv_w, conv_b, bias_slab, bias_col, gn_w, chunk):
    b, s, _ = qk_m.shape
    body = functools.partial(_mlstm_body, chunk=chunk, batch=b)
    const = lambda ci: (0, 0)
    return pl.pallas_call(
        body,
        grid=(s // chunk,),
        in_specs=[pl.BlockSpec((b, chunk, 2 * D_M), lambda ci: (0, ci, 0)),
                  pl.BlockSpec((b, chunk, D_M), lambda ci: (0, ci, 0)),
                  pl.BlockSpec((b, chunk, D_M), lambda ci: (0, ci, 0)),
                  pl.BlockSpec((b, chunk, LANES), lambda ci: (0, ci, 0)),
                  pl.BlockSpec((b, 2 * N_HEADS_M, chunk), lambda ci: (0, 0, ci)),
                  pl.BlockSpec((CONV_K, 2 * D_M), const),
                  pl.BlockSpec((1, 2 * D_M), const),
                  pl.BlockSpec((1, LANES), const),
                  pl.BlockSpec((2 * N_HEADS_M, 1), const),
                  pl.BlockSpec((1, D_M), const)],
        out_specs=pl.BlockSpec((b, chunk, D_M), lambda ci: (0, ci, 0)),
        out_shape=jax.ShapeDtypeStruct((b, s, D_M), BF16),
        scratch_shapes=[pltpu.VMEM((b, chunk + SUBLANES, 2 * D_M), F32),
                        pltpu.VMEM((b, N_HEADS_M, HEAD_DIM_M, 2 * LANES), F32),
                        pltpu.VMEM((b, SUBLANES, LANES), F32)],
        compiler_params=_cparams(("arbitrary",), 32),
        name="mlstm",
    )(qk_m, v_m, o_m, gates, gates_t, conv_w, conv_b, bias_slab, bias_col, gn_w)


COMB_GROUP_LANE = N_EXPERTS


def _route(logits):
    lane = lax.broadcasted_iota(I32, logits.shape, 1).astype(F32)
    g_mask = (lane >= N_EXPERTS) & (lane < N_EXPERTS + N_GROUPS)
    g_logit = jnp.where(g_mask, logits, -jnp.inf)
    g_exp = jnp.exp(g_logit - jnp.max(g_logit, axis=1, keepdims=True))
    g_prob = g_exp / jnp.sum(g_exp, axis=1, keepdims=True)
    g_p = jnp.max(g_prob, axis=1, keepdims=True)
    g_sel = jnp.min(jnp.where(g_mask & (g_prob == g_p), lane, float(LANES)), axis=1, keepdims=True) - N_EXPERTS
    e_mask = (lane >= g_sel * EXPERTS_PER_GROUP) & (lane < (g_sel + 1.0) * EXPERTS_PER_GROUP)
    e_logit = jnp.where(e_mask, logits, -jnp.inf)
    e_exp = jnp.exp(e_logit - jnp.max(e_logit, axis=1, keepdims=True))
    e_prob = e_exp / jnp.sum(e_exp, axis=1, keepdims=True)
    p1 = jnp.max(jnp.where(e_mask, e_prob, -1.0), axis=1, keepdims=True)
    i1 = jnp.min(jnp.where(e_mask & (e_prob == p1), lane, float(LANES)), axis=1, keepdims=True)
    rest = e_mask & (lane != i1)
    p2 = jnp.max(jnp.where(rest, e_prob, -1.0), axis=1, keepdims=True)
    i2 = jnp.min(jnp.where(rest & (e_prob == p2), lane, float(LANES)), axis=1, keepdims=True)
    total = p1 + p2
    comb = (jnp.where(lane == i1, p1 / total, 0.0) + jnp.where(lane == i2, p2 / total, 0.0)) * g_p
    return comb + jnp.where(lane == COMB_GROUP_LANE, g_sel, 0.0)


def _mix_body(ya_ref, ym_ref, x_ref, wo_ref, g_ref, b_ref, wr_ref, br_ref, x1_ref, comb_ref, *, alpha):
    mix = (jnp.dot(ya_ref[...], wo_ref[0:D_A, :], preferred_element_type=F32)
           + jnp.dot(ym_ref[...], wo_ref[D_A:D_A + D_M, :], preferred_element_type=F32))
    x1 = _layer_norm(alpha * x_ref[...] + mix, g_ref[...], b_ref[...])
    x1_ref[...] = x1
    x_hi = x1.astype(BF16)
    x_lo = (x1 - x_hi.astype(F32)).astype(BF16)
    w_hi = wr_ref[0]
    w_lo = wr_ref[1]
    logits = (jnp.dot(x_hi, w_hi, preferred_element_type=F32) + jnp.dot(x_lo, w_hi, preferred_element_type=F32)
              + jnp.dot(x_hi, w_lo, preferred_element_type=F32)) + br_ref[...]
    comb_ref[...] = _route(logits)


def _mix_ln_router(y_a, y_m, x2d, w_out, ln_g, ln_b, w_router, b_router, alpha, tm):
    n, d = x2d.shape
    const = lambda i: (0, 0)
    return pl.pallas_call(
        functools.partial(_mix_body, alpha=alpha),
        grid=(n // tm,),
        in_specs=[pl.BlockSpec((tm, D_A), lambda i: (i, 0)),
                  pl.BlockSpec((tm, D_M), lambda i: (i, 0)),
                  pl.BlockSpec((tm, d), lambda i: (i, 0)),
                  pl.BlockSpec((D_A + D_M, d), const),
                  pl.BlockSpec((1, d), const),
                  pl.BlockSpec((1, d), const),
                  pl.BlockSpec((2, d, LANES), lambda i: (0, 0, 0)),
                  pl.BlockSpec((1, LANES), const)],
        out_specs=[pl.BlockSpec((tm, d), lambda i: (i, 0)),
                   pl.BlockSpec((tm, LANES), lambda i: (i, 0))],
        out_shape=[jax.ShapeDtypeStruct((n, d), F32), jax.ShapeDtypeStruct((n, LANES), F32)],
        compiler_params=_cparams(("arbitrary",), 32),
        name="mix_ln_router",
    )(y_a, y_m, x2d, w_out, ln_g, ln_b, w_router, b_router)


MOE_ROWS = TM_MOE // N_GROUPS + 32


def _moe_body(x1_ref, comb_ref, wg_ref, wu_ref, wd_ref, g_ref, b_ref, p_ref, wpg_ref, wpp_ref, y_ref,
              xb_ref, before_ref, row_ref, *, alpha, tm):
    g = pl.program_id(1)
    g_f = g.astype(F32)

    @pl.when((pl.program_id(0) == 0) & (g == 0))
    def _():
        r = lax.broadcasted_iota(I32, (tm, tm), 0)
        c = lax.broadcasted_iota(I32, (tm, tm), 1)
        before_ref[...] = (c < r).astype(BF16)

    @pl.when(g == 0)
    def _():
        d = x1_ref.shape[1]
        xb_ref[:, 0:d] = x1_ref[...].astype(BF16)
        y_ref[...] = jnp.zeros(y_ref.shape, F32)
        comb = comb_ref[...]
        for k, term in enumerate(_split3(comb)):
            xb_ref[:, d + k * LANES:d + (k + 1) * LANES] = term
        group_row = comb.T[COMB_GROUP_LANE:COMB_GROUP_LANE + 1, :]
        sub = lax.broadcasted_iota(I32, (SUBLANES, tm), 0)
        member_t = (sub.astype(F32) == group_row) & (sub < N_GROUPS)
        ahead_t = lax.dot_general(jnp.where(member_t, 1.0, 0.0).astype(BF16), before_ref[...], NT_DIMS,
                                  preferred_element_type=F32)
        rank_row = jnp.sum(jnp.where(member_t, ahead_t, 0.0), axis=0, keepdims=True)
        row_ref[0] = jnp.broadcast_to(group_row, (SUBLANES, tm))
        row_ref[1] = jnp.broadcast_to(rank_row, (SUBLANES, tm))

    in_group_row = row_ref[0, 0:1, :] == g_f
    n_tokens = jnp.sum(jnp.where(in_group_row, 1.0, 0.0))
    n_blocks = (n_tokens.astype(I32) + MOE_ROWS - 1) // MOE_ROWS
    slot = lax.broadcasted_iota(I32, (MOE_ROWS, tm), 0).astype(F32)
    comb_lane = lax.broadcasted_iota(I32, (MOE_ROWS, LANES), 1)

    def block(b, carry):
        base = (b * MOE_ROWS).astype(F32)
        gather = jnp.where(in_group_row & (row_ref[1, 0:1, :] - base == slot), 1.0, 0.0).astype(BF16)
        d = x1_ref.shape[1]
        picked = jnp.dot(gather, xb_ref[...], preferred_element_type=F32)
        xc = picked[:, 0:d].astype(BF16)
        comb_c = picked[:, d:d + LANES] + picked[:, d + LANES:d + 2 * LANES] + picked[:, d + 2 * LANES:d + 3 * LANES]
        yc = jnp.zeros((MOE_ROWS, d), F32)
        for e in range(EXPERTS_PER_GROUP):
            gate = jnp.dot(xc, wg_ref[e], preferred_element_type=F32)
            up = jnp.dot(xc, wu_ref[e], preferred_element_type=F32)
            weight = jnp.sum(jnp.where(comb_lane == g * EXPERTS_PER_GROUP + e, comb_c, 0.0), axis=1, keepdims=True)
            hidden = gate * jax.nn.sigmoid(gate) * up * weight
            yc = yc + jnp.dot(hidden.astype(BF16), wd_ref[e], preferred_element_type=F32)
        yc_hi = yc.astype(BF16)
        yc_lo = (yc - yc_hi.astype(F32)).astype(BF16)
        tn = (((0,), (0,)), ((), ()))
        y_ref[...] += lax.dot_general(jnp.concatenate([gather, gather], axis=0),
                                      jnp.concatenate([yc_hi, yc_lo], axis=0), tn, preferred_element_type=F32)
        return carry

    lax.fori_loop(0, n_blocks, block, 0)

    @pl.when(g == N_GROUPS - 1)
    def _():
        x2 = _layer_norm(alpha * x1_ref[...] + y_ref[...], g_ref[...], b_ref[...])
        gate = jax.nn.sigmoid(jnp.dot(x2.astype(BF16), wpg_ref[...], preferred_element_type=F32))
        proj = jnp.dot(p_ref[...].astype(BF16), wpp_ref[...], preferred_element_type=F32)
        y_ref[...] = x2 + gate * proj


def _moe_ln_ple(x1, comb, w_gate, w_up, w_down, ln_g, ln_b, p2d, w_ple_gate, w_ple_proj, alpha, tm):
    n, d = x1.shape
    dp = p2d.shape[1]
    const = lambda i, g: (0, 0)
    once = pl.Buffered(1)
    grouped = lambda w: w.reshape((N_GROUPS, EXPERTS_PER_GROUP) + w.shape[1:])
    return pl.pallas_call(
        functools.partial(_moe_body, alpha=alpha, tm=tm),
        grid=(n // tm, N_GROUPS),
        in_specs=[pl.BlockSpec((tm, d), lambda i, g: (i, 0), pipeline_mode=once),
                  pl.BlockSpec((tm, LANES), lambda i, g: (i, 0)),
                  pl.BlockSpec((None, EXPERTS_PER_GROUP, d, D_EXPERT), lambda i, g: (g, 0, 0, 0)),
                  pl.BlockSpec((None, EXPERTS_PER_GROUP, d, D_EXPERT), lambda i, g: (g, 0, 0, 0)),
                  pl.BlockSpec((None, EXPERTS_PER_GROUP, D_EXPERT, d), lambda i, g: (g, 0, 0, 0)),
                  pl.BlockSpec((1, d), const),
                  pl.BlockSpec((1, d), const),
                  pl.BlockSpec((tm, dp), lambda i, g: (i, 0), pipeline_mode=once),
                  pl.BlockSpec((d, d), const, pipeline_mode=once),
                  pl.BlockSpec((dp, d), const, pipeline_mode=once)],
        out_specs=pl.BlockSpec((tm, d), lambda i, g: (i, 0)),
        out_shape=jax.ShapeDtypeStruct((n, d), F32),
        scratch_shapes=[pltpu.VMEM((tm, d + 3 * LANES), BF16),
                        pltpu.VMEM((tm, tm), BF16),
                        pltpu.VMEM((2, SUBLANES, tm), F32)],
        compiler_params=_cparams(("arbitrary", "arbitrary"), 58),
        name="moe_ln_ple",
    )(x1, comb, grouped(w_gate), grouped(w_up), grouped(w_down), ln_g, ln_b, p2d, w_ple_gate, w_ple_proj)


def _tile(n, want):
    t = min(want, n)
    assert n % t == 0
    return t


def _layer(x, p_i, w_in, b_igate, b_fgate, conv_w, conv_b, gn_w, w_out, ln1_g, ln1_b,
           rg_w, rg_b, re_w, re_b, e_gate, e_up, e_down, ln2_g, ln2_b, ple_gate_w, ple_proj_w, alpha):
    b, s, d = x.shape
    n = b * s
    x2d = x.reshape(n, d)

    tk = _tile(s, TK_DSA)
    w_packed, w_vt = _pack_w_in(w_in)
    qk, v_t, q_idx, k4, gates, qk_m, v_m, o_m = _in_proj(x2d, w_packed, w_vt, tk)

    gates3 = gates.reshape(b, s, LANES)
    y_a = _dsa_attention(qk.reshape(2 * SLABS_PER_TENSOR, b, s, LANES),
                         v_t.reshape(SLABS_PER_TENSOR, b, s // tk, LANES, tk),
                         q_idx.reshape(b, s, N_IDX_HEADS * IDX_DIM), k4.reshape(b, s, LANES), gates3,
                         _tile(s, TQ_DSA), tk)

    gate_bias = jnp.concatenate([b_igate, b_fgate]).astype(F32)
    bias_slab = jnp.zeros((1, LANES), F32).at[0, G_I:G_I + 2 * N_HEADS_M].set(gate_bias)
    gates_t = jnp.swapaxes(gates3[:, :, G_I:G_I + 2 * N_HEADS_M], 1, 2)
    y_m = _mlstm(qk_m.reshape(b, s, 2 * D_M), v_m.reshape(b, s, D_M), o_m.reshape(b, s, D_M), gates3, gates_t,
                 conv_w, conv_b.reshape(1, -1), bias_slab, gate_bias.reshape(-1, 1), gn_w.reshape(1, -1),
                 _tile(s, CHUNK_MLSTM))

    w_router = jnp.concatenate([re_w, rg_w, jnp.zeros((d, LANES - N_EXPERTS - N_GROUPS), F32)], axis=1)
    w_router_hi = w_router.astype(BF16)
    w_router = jnp.stack([w_router_hi, (w_router - w_router_hi.astype(F32)).astype(BF16)])
    b_router = jnp.concatenate([re_b, rg_b, jnp.zeros((LANES - N_EXPERTS - N_GROUPS,), F32)]).reshape(1, LANES)
    x1, comb = _mix_ln_router(y_a.reshape(n, D_A), y_m.reshape(n, D_M), x2d, w_out.astype(BF16),
                              ln1_g.reshape(1, d), ln1_b.reshape(1, d), w_router, b_router, alpha, _tile(n, TM_MIX))

    out = _moe_ln_ple(x1, comb, e_gate.astype(BF16), e_up.astype(BF16), e_down.astype(BF16),
                      ln2_g.reshape(1, d), ln2_b.reshape(1, d), p_i.reshape(n, -1), ple_gate_w.astype(BF16),
                      ple_proj_w.astype(BF16), alpha, _tile(n, TM_MOE))
    return out.reshape(b, s, d)


def kernel(x, p, w_in, b_igate, b_fgate, conv_w, conv_b, gn_w, w_out, ln1_g, ln1_b, router_group_w, router_group_b,
           router_expert_w, router_expert_b, expert_w_gate, expert_w_up, expert_w_down, ln2_g, ln2_b, ple_gate_w,
           ple_proj_w):
    depth = w_in.shape[0]
    alpha = (2.0 * depth) ** 0.25
    for i in range(depth):
        x = _layer(x, p[i], w_in[i], b_igate[i], b_fgate[i], conv_w[i], conv_b[i], gn_w[i], w_out[i], ln1_g[i],
                   ln1_b[i], router_group_w[i], router_group_b[i], router_expert_w[i], router_expert_b[i],
                   expert_w_gate[i], expert_w_up[i], expert_w_down[i], ln2_g[i], ln2_b[i], ple_gate_w[i],
                   ple_proj_w[i], alpha)
    return x
```

```python
import functools

import numpy as np
import jax
import jax.numpy as jnp
from jax import lax
from jax.experimental import pallas as pl
from jax.experimental.pallas import tpu as pltpu

F32 = jnp.float32
BF16 = jnp.bfloat16
I32 = jnp.int32

N_HEADS_A = 8
HEAD_DIM_A = 64
D_A = N_HEADS_A * HEAD_DIM_A
N_IDX_HEADS = 8
IDX_DIM = 32
TOPK_MAX = 256
N_HEADS_M = 4
HEAD_DIM_M = 128
D_M = N_HEADS_M * HEAD_DIM_M
CONV_K = 4
N_GROUPS = 4
EXPERTS_PER_GROUP = 8
N_EXPERTS = N_GROUPS * EXPERTS_PER_GROUP
D_EXPERT = 256
LN_EPS = 1e-5
NEG_BIG = -1e30
IN_SPLITS = (D_A, D_A, D_A, N_IDX_HEADS * IDX_DIM, IDX_DIM, N_IDX_HEADS, D_M, D_M, D_M, D_M, N_HEADS_M, N_HEADS_M)

LANES = 128
SUBLANES = 8
MIB = 1024 * 1024

C_QK = 0
C_QIDX = C_QK + 2 * D_A
C_K4 = C_QIDX + N_IDX_HEADS * IDX_DIM
C_GATE = C_K4 + LANES
C_QKM = C_GATE + LANES
C_VM = C_QKM + 2 * D_M
C_OM = C_VM + D_M
C_END = C_OM + D_M
SLABS_PER_TENSOR = D_A // LANES
HEADS_PER_SLAB = LANES // HEAD_DIM_A
G_W = 0
G_I = N_IDX_HEADS
G_F = G_I + N_HEADS_M

INT_MIN = -2 ** 31

TQ_DSA = LANES
TK_DSA = 1024
TM_IN_PROJ = TK_DSA
CHUNK_MLSTM = 256
TM_MIX = 1024
TM_MOE = 1024

NT_DIMS = (((1,), (1,)), ((), ()))


def _cparams(semantics, vmem_mib):
    return pltpu.CompilerParams(dimension_semantics=semantics, vmem_limit_bytes=int(vmem_mib * MIB))


def _layer_norm(r, g, b):
    mu = jnp.mean(r, axis=-1, keepdims=True)
    d = r - mu
    var = jnp.mean(d * d, axis=-1, keepdims=True)
    return d * lax.rsqrt(var + LN_EPS) * g + b


def _pack_w_in(w_in):
    d = w_in.shape[0]
    cuts = [int(c) for c in np.cumsum(IN_SPLITS)[:-1]]
    q_a, k_a, v_a, q_idx, k_idx, w_idx, q_m, k_m, v_m, o_m, i_m, f_m = jnp.split(w_in, cuts, axis=1)
    k4 = jnp.tile(k_idx, (1, LANES // IDX_DIM))
    gate = jnp.concatenate([w_idx, i_m, f_m, jnp.zeros((d, LANES - G_F - N_HEADS_M), w_in.dtype)], axis=1)
    packed = jnp.concatenate([q_a, k_a, q_idx, k4, gate, q_m, k_m, v_m, o_m], axis=1).astype(BF16)
    return packed, v_a.T.astype(BF16)


def _in_proj_body(x_ref, w_ref, wvt_ref, qk_ref, vt_ref, qi_ref, k4_ref, g_ref, qkm_ref, vm_ref, om_ref):
    xb = x_ref[...].astype(BF16)

    def proj(lo, hi):
        return jnp.dot(xb, w_ref[:, lo:hi], preferred_element_type=F32)

    qk = proj(C_QK, C_QIDX).astype(BF16)
    for c in range(2 * SLABS_PER_TENSOR):
        qk_ref[c] = qk[:, c * LANES:(c + 1) * LANES]
    v_t = lax.dot_general(wvt_ref[...], xb, NT_DIMS, preferred_element_type=F32).astype(BF16)
    for c in range(SLABS_PER_TENSOR):
        vt_ref[c, 0] = v_t[c * LANES:(c + 1) * LANES, :]
    qi_ref[...] = proj(C_QIDX, C_K4).astype(BF16)
    k4_ref[...] = proj(C_K4, C_GATE).astype(BF16)
    g_ref[...] = proj(C_GATE, C_QKM)
    qkm_ref[...] = proj(C_QKM, C_VM)
    vm_ref[...] = proj(C_VM, C_OM).astype(BF16)
    om_ref[...] = proj(C_OM, C_END)


def _in_proj(x2d, w_packed, w_vt, tm):
    n, d = x2d.shape
    widths = [(C_K4 - C_QIDX, BF16), (C_GATE - C_K4, BF16), (C_QKM - C_GATE, F32),
              (C_VM - C_QKM, F32), (C_OM - C_VM, BF16), (C_END - C_OM, F32)]
    return pl.pallas_call(
        _in_proj_body,
        grid=(n // tm,),
        in_specs=[pl.BlockSpec((tm, d), lambda i: (i, 0)),
                  pl.BlockSpec((d, C_END), lambda i: (0, 0)),
                  pl.BlockSpec((D_A, d), lambda i: (0, 0))],
        out_specs=([pl.BlockSpec((2 * SLABS_PER_TENSOR, tm, LANES), lambda i: (0, i, 0)),
                    pl.BlockSpec((SLABS_PER_TENSOR, 1, LANES, tm), lambda i: (0, i, 0, 0))]
                   + [pl.BlockSpec((tm, w), lambda i: (i, 0)) for w, _ in widths]),
        out_shape=([jax.ShapeDtypeStruct((2 * SLABS_PER_TENSOR, n, LANES), BF16),
                    jax.ShapeDtypeStruct((SLABS_PER_TENSOR, n // tm, LANES, tm), BF16)]
                   + [jax.ShapeDtypeStruct((n, w), dt) for w, dt in widths]),
        compiler_params=_cparams(("arbitrary",), 48),
        name="in_proj",
    )(x2d, w_packed, w_vt)


def _float_to_ordered_int(value):
    bits = lax.bitcast_convert_type(value, I32)
    return bits ^ ((bits >> 31) & jnp.int32(0x7FFFFFFF))


def _ordered_int_to_float(key):
    bits = key ^ ((key >> 31) & jnp.int32(0x7FFFFFFF))
    return lax.bitcast_convert_type(bits, F32)


SEARCH_FIXED_PASSES = 12
SEARCH_FREE_PASSES = 16
SEARCH_MAX_PASSES = SEARCH_FREE_PASSES + 33
ZERO_TIE_MARGIN = 128
CHAINS = 4
POS_RADIX = 64
POS_TERMS = 3
LOG2_E = float(np.log2(np.e))


def _dsa_body(qa_ref, ka_ref, vt_ref, qi_ref, k4_ref, g_ref, o_ref,
              sc_ref, qm_ref, qh_ref, pos_ref, mask_ref, acc_ref, m_ref, l_ref, *, seq, topk, tq, tk):
    t0 = pl.program_id(1) * tq
    n_kb = (t0 + tq + tk - 1) // tk
    groups = tk // (CHAINS * SUBLANES)
    shape4 = (groups, CHAINS, SUBLANES, tq)

    def chunks(x):
        return x.reshape(shape4)

    def fold(x):
        out = x[0]
        for c in range(1, CHAINS):
            out = out + x[c]
        return out

    def rows8(row):
        return jnp.broadcast_to(row, (SUBLANES, tq))

    key_off = ((lax.broadcasted_iota(I32, shape4, 0) * CHAINS + lax.broadcasted_iota(I32, shape4, 1)) * SUBLANES
               + lax.broadcasted_iota(I32, shape4, 2))
    q_pos = t0 + lax.broadcasted_iota(I32, shape4, 3)
    lane = lax.broadcasted_iota(I32, (tq, LANES), 1)

    for h in range(N_IDX_HEADS):
        per_slab = LANES // IDX_DIM
        slab = qi_ref[:, (h // per_slab) * LANES:(h // per_slab + 1) * LANES]
        lo = (h % per_slab) * IDX_DIM
        qm_ref[h * tq:(h + 1) * tq, :] = jnp.where((lane >= lo) & (lane < lo + IDX_DIM), slab, jnp.zeros_like(slab))
    for h in range(N_HEADS_A):
        p, hh = divmod(h, HEADS_PER_SLAB)
        slab = qa_ref[p].astype(F32) * (HEAD_DIM_A ** -0.5 * LOG2_E)
        in_head = (lane >= hh * HEAD_DIM_A) & (lane < (hh + 1) * HEAD_DIM_A)
        qh_ref[p, hh * tq:(hh + 1) * tq, 0:LANES] = jnp.where(in_head, slab, 0.0).astype(BF16)

    @pl.when((pl.program_id(0) == 0) & (pl.program_id(1) == 0))
    def _():
        for h in range(N_HEADS_A):
            p, hh = divmod(h, HEADS_PER_SLAB)
            slope = 2.0 ** (-8.0 * (h + 1) / N_HEADS_A)
            weight = jnp.where(lane < POS_TERMS, slope * POS_RADIX * LOG2_E,
                               jnp.where(lane < 2 * POS_TERMS, slope * LOG2_E, 0.0))
            terms = _split3(weight)
            feat = jnp.where(lane % POS_TERMS == 0, terms[0], jnp.where(lane % POS_TERMS == 1, terms[1], terms[2]))
            qh_ref[p, hh * tq:(hh + 1) * tq, LANES:2 * LANES] = feat
        pos_lane = lax.broadcasted_iota(I32, (tk, LANES), 1)
        pos_row = lax.broadcasted_iota(I32, (tk, LANES), 0)
        for j in range(seq // tk):
            pos = pos_row + j * tk
            pos_ref[j * tk:(j + 1) * tk, :] = jnp.where(
                pos_lane < POS_TERMS, pos // POS_RADIX,
                jnp.where(pos_lane < 2 * POS_TERMS, pos % POS_RADIX, 0)).astype(F32).astype(BF16)
    w_rows = g_ref[...].T[G_W:G_W + N_IDX_HEADS, :] * ((IDX_DIM * N_IDX_HEADS) ** -0.5)

    def score_block(j, carry):
        hi_part, lo_part, zero_part = carry
        k0 = pl.multiple_of(j * tk, tk)
        logits = lax.dot_general(k4_ref[pl.ds(k0, tk), :], qm_ref[...], NT_DIMS,
                                 preferred_element_type=F32)
        score = jnp.zeros(shape4, F32)
        for h in range(N_IDX_HEADS):
            score = score + rows8(w_rows[h:h + 1, :]) * jnp.maximum(chunks(logits[:, h * tq:(h + 1) * tq]), 0.0)
        causal = k0 + key_off <= q_pos
        kept = jnp.where(causal, score, NEG_BIG)
        sc_ref[j] = kept
        hi_part = jnp.maximum(hi_part, jnp.max(kept, axis=0))
        lo_part = jnp.minimum(lo_part, jnp.min(score, axis=0))
        zero_part = zero_part + jnp.sum(jnp.where(kept >= 0.0, 1.0, 0.0), axis=0)
        return hi_part, lo_part, zero_part

    part_shape = (CHAINS, SUBLANES, tq)
    hi_part, lo_part, zero_part = lax.fori_loop(
        0, n_kb, score_block,
        (jnp.full(part_shape, -jnp.inf, F32), jnp.full(part_shape, jnp.inf, F32), jnp.zeros(part_shape, F32)))
    row_max = jnp.max(jnp.max(hi_part, axis=0), axis=0, keepdims=True)
    row_min = jnp.min(jnp.min(lo_part, axis=0), axis=0, keepdims=True)

    n_tail = (seq - n_kb * tk).astype(F32)

    def count(thr, strict):
        thr8 = rows8(thr)

        def body(j, acc):
            s = sc_ref[j]
            return acc + jnp.sum(jnp.where((s > thr8) if strict else (s >= thr8), 1.0, 0.0), axis=0)

        acc = lax.fori_loop(0, n_kb, body, jnp.zeros(part_shape, F32))
        tail_hit = (NEG_BIG > thr) if strict else (NEG_BIG >= thr)
        return jnp.sum(fold(acc), axis=0, keepdims=True) + jnp.where(tail_hit, n_tail, 0.0)

    n_causal = (t0 + 1 + lax.broadcasted_iota(I32, (1, tq), 1)).astype(F32)
    few = n_causal < topk
    lo0 = jnp.where(few, _float_to_ordered_int(jnp.full((1, tq), NEG_BIG, F32)), _float_to_ordered_int(row_min))
    cnt_lo0 = jnp.where(few, float(seq), n_causal)
    hi0 = _float_to_ordered_int(row_max) + 1
    log_k = float(np.log(topk))

    def log_excess(cnt):
        return jnp.log(jnp.maximum(cnt, 0.5)) - log_k

    def probe(it, bracket):
        lo, hi, cnt_lo, f_lo, f_hi, last, done_i = bracket
        done = done_i != 0
        it_v = jnp.zeros((1, tq), I32) + it
        lo_v = _ordered_int_to_float(lo)
        hi_v = _ordered_int_to_float(hi)
        guess = _float_to_ordered_int(lo_v + (f_lo / (f_lo - f_hi)) * (hi_v - lo_v))
        guess = jnp.where((it_v == 1) & (lo == 0) & (cnt_lo < topk + ZERO_TIE_MARGIN), 1, guess)
        middle = lo + lax.shift_right_logical(hi - lo, jnp.ones((1, tq), I32))
        usable = (guess > lo) & (guess < hi) & (it_v < SEARCH_FREE_PASSES)
        cand = jnp.where(usable, guess, middle)
        cnt = count(_ordered_int_to_float(cand), strict=False)
        f_c = log_excess(cnt)
        take = (cnt >= topk) & ~done
        drop = (cnt < topk) & ~done
        f_hi = jnp.where(take, jnp.where(last == 1, 0.5 * f_hi, f_hi), jnp.where(drop, f_c, f_hi))
        f_lo = jnp.where(drop, jnp.where(last == -1, 0.5 * f_lo, f_lo), jnp.where(take, f_c, f_lo))
        last = jnp.where(take, 1, jnp.where(drop, -1, last))
        lo = jnp.where(take, cand, lo)
        cnt_lo = jnp.where(take, cnt, cnt_lo)
        hi = jnp.where(drop, cand, hi)
        done = done | (cnt_lo == topk) | ((hi - lo) == 1)
        return lo, hi, cnt_lo, f_lo, f_hi, last, done.astype(I32)

    def search_step(state):
        bracket = probe(state[0], state[3:])
        n_active = jnp.sum(jnp.where(bracket[6] != 0, 0.0, 1.0))
        return (state[0] + 1, n_active, jnp.max(bracket[2])) + bracket

    def search_on(state):
        return (state[0] < SEARCH_MAX_PASSES) & (state[1] > 0.0)

    cnt_zero = jnp.sum(fold(zero_part), axis=0, keepdims=True)
    inside = (lo0 < 0) & (hi0 > 0) & ~few
    take0 = inside & (cnt_zero >= topk)
    drop0 = inside & (cnt_zero < topk)
    lo1 = jnp.where(take0, 0, lo0)
    hi1 = jnp.where(drop0, 0, hi0)
    cnt_lo1 = jnp.where(take0, cnt_zero, cnt_lo0)
    done1 = few | (cnt_lo1 == topk) | ((hi1 - lo1) == 1)
    bracket = (lo1, hi1, cnt_lo1, log_excess(cnt_lo1), log_excess(jnp.where(drop0, cnt_zero, 0.0)),
               jnp.where(take0, 1, jnp.where(drop0, -1, 0)), done1.astype(I32))
    bracket = lax.fori_loop(1, 1 + SEARCH_FIXED_PASSES, probe, bracket)
    state = search_step((jnp.int32(1 + SEARCH_FIXED_PASSES), jnp.float32(0.0), jnp.float32(0.0)) + bracket)
    state = lax.while_loop(search_on, search_step, state)
    max_cnt_thr = state[2]
    thr = _ordered_int_to_float(state[3])
    thr8 = rows8(thr)

    @pl.when(max_cnt_thr > topk)
    def _():
        need = topk - count(thr, strict=True)
        tri_r = lax.broadcasted_iota(I32, (tk, tk), 0)
        tri_c = lax.broadcasted_iota(I32, (tk, tk), 1)
        earlier = (tri_c < tri_r).astype(BF16)

        def drop_late_ties(j, seen):
            s = sc_ref[j]
            eq = s == thr8
            eq_f = jnp.where(eq, 1.0, 0.0)
            rank = seen + jnp.dot(earlier, eq_f.reshape(tk, tq).astype(BF16), preferred_element_type=F32)
            sc_ref[j] = jnp.where(eq & (chunks(rank) >= rows8(need)), -jnp.inf, s)
            return seen + jnp.sum(fold(jnp.sum(eq_f, axis=0)), axis=0, keepdims=True)

        lax.fori_loop(0, n_kb, drop_late_ties, jnp.zeros((1, tq), F32))

    m_ref[...] = jnp.full(m_ref.shape, NEG_BIG, F32)
    l_ref[...] = jnp.zeros(l_ref.shape, F32)
    acc_ref[...] = jnp.zeros(acc_ref.shape, F32)

    def attend(j, carry):
        k0 = pl.multiple_of(j * tk, tk)
        mask_ref[...] = jnp.where((sc_ref[j] >= thr8) & (k0 + key_off <= q_pos), 0.0, -jnp.inf)
        pos_feat = pos_ref[pl.ds(k0, tk), :]

        scores = []
        for p in range(SLABS_PER_TENSOR):
            keys = jnp.concatenate([ka_ref[p, pl.ds(k0, tk), :], pos_feat], axis=1)
            scores.append(lax.dot_general(keys, qh_ref[p], NT_DIMS, preferred_element_type=F32))
        weights = []
        for p in range(SLABS_PER_TENSOR):
            for hh in range(HEADS_PER_SLAB):
                s = chunks(scores[p][:, hh * tq:(hh + 1) * tq]) + mask_ref[...]
                m_old = m_ref[p, hh]
                m_blk = jnp.max(fold_max(jnp.max(s, axis=0)), axis=0, keepdims=True)
                m_new = jnp.maximum(m_old, rows8(m_blk))
                alpha = jnp.exp2(m_old - m_new)
                m_ref[p, hh] = m_new
                weights.append((alpha, jnp.exp2(s - m_new).reshape(tk, tq).astype(BF16)))
        ones_rows = jnp.ones((SUBLANES, tk), BF16)
        for p in range(SLABS_PER_TENSOR):
            for hh in range(HEADS_PER_SLAB):
                rows = slice(hh * HEAD_DIM_A, (hh + 1) * HEAD_DIM_A)
                alpha, probs = weights[p * HEADS_PER_SLAB + hh]
                values = jnp.concatenate([vt_ref[p, j, rows, :], ones_rows], axis=0)
                pv = jnp.dot(values, probs, preferred_element_type=F32)
                old = acc_ref[p, rows, :].reshape(HEAD_DIM_A // SUBLANES, SUBLANES, tq)
                acc_ref[p, rows, :] = (alpha * old).reshape(HEAD_DIM_A, tq) + pv[0:HEAD_DIM_A]
                l_ref[p, hh] = alpha * l_ref[p, hh] + pv[HEAD_DIM_A:HEAD_DIM_A + SUBLANES]
        return carry

    def fold_max(x):
        out = x[0]
        for c in range(1, CHAINS):
            out = jnp.maximum(out, x[c])
        return out

    lax.fori_loop(0, n_kb, attend, 0)

    for p in range(SLABS_PER_TENSOR):
        halves = []
        for hh in range(HEADS_PER_SLAB):
            rows = slice(hh * HEAD_DIM_A, (hh + 1) * HEAD_DIM_A)
            acc = acc_ref[p, rows, :].reshape(HEAD_DIM_A // SUBLANES, SUBLANES, tq)
            halves.append((acc / l_ref[p, hh]).reshape(HEAD_DIM_A, tq))
        o_ref[:, p * LANES:(p + 1) * LANES] = jnp.concatenate(halves, axis=0).T.astype(BF16)


def _dsa_attention(qk, v_t, q_idx, k4, gates, tq, tk):
    _, b, s, _ = qk.shape
    assert tq == LANES and s // POS_RADIX <= 256
    topk = min(TOPK_MAX, s // 4)
    once = pl.Buffered(1)
    groups = tk // (CHAINS * SUBLANES)
    body = functools.partial(_dsa_body, seq=s, topk=topk, tq=tq, tk=tk)
    return pl.pallas_call(
        body,
        grid=(b, s // tq),
        in_specs=[pl.BlockSpec((SLABS_PER_TENSOR, None, tq, LANES), lambda bi, qi: (0, bi, qi, 0)),
                  pl.BlockSpec((SLABS_PER_TENSOR, None, s, LANES), lambda bi, qi: (1, bi, 0, 0), pipeline_mode=once),
                  pl.BlockSpec((SLABS_PER_TENSOR, None, s // tk, LANES, tk), lambda bi, qi: (0, bi, 0, 0, 0),
                               pipeline_mode=once),
                  pl.BlockSpec((None, tq, N_IDX_HEADS * IDX_DIM), lambda bi, qi: (bi, qi, 0)),
                  pl.BlockSpec((None, s, LANES), lambda bi, qi: (bi, 0, 0), pipeline_mode=once),
                  pl.BlockSpec((None, tq, LANES), lambda bi, qi: (bi, qi, 0))],
        out_specs=pl.BlockSpec((None, tq, D_A), lambda bi, qi: (bi, qi, 0)),
        out_shape=jax.ShapeDtypeStruct((b, s, D_A), BF16),
        scratch_shapes=[pltpu.VMEM((s // tk, groups, CHAINS, SUBLANES, tq), F32),
                        pltpu.VMEM((N_IDX_HEADS * tq, LANES), BF16),
                        pltpu.VMEM((SLABS_PER_TENSOR, HEADS_PER_SLAB * tq, 2 * LANES), BF16),
                        pltpu.VMEM((s, LANES), BF16),
                        pltpu.VMEM((groups, CHAINS, SUBLANES, tq), F32),
                        pltpu.VMEM((SLABS_PER_TENSOR, LANES, tq), F32),
                        pltpu.VMEM((SLABS_PER_TENSOR, HEADS_PER_SLAB, SUBLANES, tq), F32),
                        pltpu.VMEM((SLABS_PER_TENSOR, HEADS_PER_SLAB, SUBLANES, tq), F32)],
        compiler_params=_cparams(("arbitrary", "arbitrary"), 48),
        name="dsa_attention",
    )(qk, qk, v_t, q_idx, k4, gates)


def _log_sigmoid(x):
    return jnp.minimum(x, 0.0) - jnp.log(1.0 + jnp.exp(-jnp.abs(x)))


def _split3(a):
    hi = a.astype(BF16)
    r1 = a - hi.astype(F32)
    mid = r1.astype(BF16)
    lo = (r1 - mid.astype(F32)).astype(BF16)
    return hi, mid, lo


def _mlstm_body(qk_ref, v_ref, o_ref, g_ref, gt_ref, cw_ref, cb_ref, bg_ref, bgt_ref, gn_ref, y_ref,
                xe_ref, c_ref, m_ref, *, chunk, batch):
    @pl.when(pl.program_id(0) == 0)
    def _():
        xe_ref[:, 0:SUBLANES, :] = jnp.zeros((batch, SUBLANES, 2 * D_M), F32)
        c_ref[...] = jnp.zeros(c_ref.shape, F32)
        m_ref[...] = jnp.zeros(m_ref.shape, F32)

    gates = [_mlstm_gates(qk_ref.at[bi], g_ref.at[bi], gt_ref.at[bi], cw_ref, cb_ref, bg_ref, bgt_ref,
                          xe_ref.at[bi], chunk) for bi in range(batch)]
    chains = [(h, bi) for h in range(N_HEADS_M) for bi in range(batch)]
    scores = [_mlstm_head_scores(h, gates[bi], v_ref.at[bi], c_ref.at[bi], m_ref.at[bi], chunk) for h, bi in chains]
    for (h, bi), sc in zip(chains, scores):
        _mlstm_head(h, gates[bi], sc, o_ref.at[bi], gn_ref, y_ref.at[bi], chunk)
    for (h, bi), sc in zip(chains, scores):
        _mlstm_head_state(h, gates[bi], sc, c_ref.at[bi], m_ref.at[bi], chunk)


def _mlstm_gates(qk_ref, g_ref, gt_ref, cw_ref, cb_ref, bg_ref, bgt_ref, xe_ref, chunk):
    xe_ref[SUBLANES:SUBLANES + chunk, :] = qk_ref[...]
    conv = cb_ref[...]
    for j in range(CONV_K):
        start = SUBLANES - (CONV_K - 1) + j
        conv = conv + cw_ref[j:j + 1, :] * xe_ref[start:start + chunk, :]
    qk = conv * jax.nn.sigmoid(conv)
    xe_ref[0:SUBLANES, :] = xe_ref[chunk:chunk + SUBLANES, :]

    gb = g_ref[...] + bg_ref[...]
    gbt = gt_ref[...] + bgt_ref[...]
    r = lax.broadcasted_iota(I32, (chunk, chunk), 0)
    c = lax.broadcasted_iota(I32, (chunk, chunk), 1)
    causal = r >= c
    incl = causal.astype(BF16)
    incl_t = (r <= c).astype(BF16)
    cum = jnp.zeros((chunk, LANES), F32)
    for term in _split3(_log_sigmoid(gb)):
        cum = cum + jnp.dot(incl, term, preferred_element_type=F32)
    cum_t = jnp.zeros((2 * N_HEADS_M, chunk), F32)
    for term in _split3(_log_sigmoid(gbt)):
        cum_t = cum_t + jnp.dot(term, incl_t, preferred_element_type=F32)

    return qk, gb, gbt, cum, cum_t, causal


def _mlstm_head_scores(h, gates, v_ref, c_ref, m_ref, chunk):
    qk = gates[0]
    lane = lax.broadcasted_iota(I32, (chunk, LANES), 1)
    ones_col = jnp.where(lane == 0, 1.0, 0.0).astype(BF16)
    sl = slice(h * HEAD_DIM_M, (h + 1) * HEAD_DIM_M)
    q_h = (qk[:, sl] * (HEAD_DIM_M ** -0.5)).astype(BF16)
    k_f = qk[:, D_M + h * HEAD_DIM_M:D_M + (h + 1) * HEAD_DIM_M]
    v_aug = jnp.concatenate([v_ref[:, sl], ones_col], axis=1)
    m_prev = m_ref[h:h + 1, 0:1]
    c_aug = c_ref[h]
    qk_t = lax.dot_general(q_h, k_f.astype(BF16), NT_DIMS, preferred_element_type=F32)
    carried = jnp.dot(q_h, c_aug.astype(BF16), preferred_element_type=F32)
    return k_f, v_aug, m_prev, c_aug, qk_t, carried


def _mlstm_head(h, gates, scores, o_ref, gn_ref, y_ref, chunk):
    _, _, gbt, cum, cum_t, causal = gates
    _, v_aug, m_prev, _, qk_t, carried = scores
    sl = slice(h * HEAD_DIM_M, (h + 1) * HEAD_DIM_M)
    b_col = cum[:, G_F + h:G_F + h + 1]
    b_row = cum_t[N_HEADS_M + h:N_HEADS_M + h + 1, :]
    i_row = gbt[h:h + 1, :]

    d_log = jnp.where(causal, b_col - b_row + i_row, -jnp.inf)
    inter = b_col + m_prev
    m_t = jnp.maximum(inter, jnp.max(d_log, axis=1, keepdims=True))
    w_intra = qk_t * jnp.exp(d_log - m_t)
    w_inter = jnp.exp(inter - m_t)
    intra = jnp.dot(w_intra.astype(BF16), v_aug[:, 0:HEAD_DIM_M], preferred_element_type=F32)
    num = intra + w_inter * carried[:, 0:HEAD_DIM_M]
    den = jnp.sum(w_intra, axis=1, keepdims=True) + w_inter * carried[:, HEAD_DIM_M:HEAD_DIM_M + 1]
    hid = num / jnp.maximum(jnp.abs(den), jnp.exp(-m_t))

    mu = jnp.mean(hid, axis=1, keepdims=True)
    dev = hid - mu
    var = jnp.mean(dev * dev, axis=1, keepdims=True)
    normed = dev * lax.rsqrt(var + LN_EPS) * gn_ref[:, sl]
    y_ref[:, sl] = (jax.nn.sigmoid(o_ref[:, sl]) * normed).astype(BF16)


def _mlstm_head_state(h, gates, scores, c_ref, m_ref, chunk):
    _, gb, _, cum, _, _ = gates
    k_f, v_aug, m_prev, c_aug, _, _ = scores
    b_col = cum[:, G_F + h:G_F + h + 1]
    i_col = gb[:, G_I + h:G_I + h + 1]
    b_last = b_col[chunk - 1:chunk, :]
    w_log = b_last - b_col + i_col
    m_new = jnp.maximum(b_last + m_prev, jnp.max(w_log, axis=0, keepdims=True))
    decay = jnp.exp(b_last + m_prev - m_new)
    kw_t = (k_f * jnp.exp(w_log - m_new)).T.astype(BF16)
    c_ref[h] = decay * c_aug + jnp.dot(kw_t, v_aug, preferred_element_type=F32)
    m_ref[h:h + 1, :] = jnp.broadcast_to(m_new, (1, LANES))


def _mlstm(qk_m, v_m, o_m, gates, gates_t, conv_w, conv_b, bias_slab, bias_col, gn_w, chunk):
    b, s, _ = qk_m.shape
    body = functools.partial(_mlstm_body, chunk=chunk, batch=b)
    const = lambda ci: (0, 0)
    return pl.pallas_call(
        body,
        grid=(s // chunk,),
        in_specs=[pl.BlockSpec((b, chunk, 2 * D_M), lambda ci: (0, ci, 0)),
                  pl.BlockSpec((b, chunk, D_M), lambda ci: (0, ci, 0)),
                  pl.BlockSpec((b, chunk, D_M), lambda ci: (0, ci, 0)),
                  pl.BlockSpec((b, chunk, LANES), lambda ci: (0, ci, 0)),
                  pl.BlockSpec((b, 2 * N_HEADS_M, chunk), lambda ci: (0, 0, ci)),
                  pl.BlockSpec((CONV_K, 2 * D_M), const),
                  pl.BlockSpec((1, 2 * D_M), const),
                  pl.BlockSpec((1, LANES), const),
                  pl.BlockSpec((2 * N_HEADS_M, 1), const),
                  pl.BlockSpec((1, D_M), const)],
        out_specs=pl.BlockSpec((b, chunk, D_M), lambda ci: (0, ci, 0)),
        out_shape=jax.ShapeDtypeStruct((b, s, D_M), BF16),
        scratch_shapes=[pltpu.VMEM((b, chunk + SUBLANES, 2 * D_M), F32),
                        pltpu.VMEM((b, N_HEADS_M, HEAD_DIM_M, 2 * LANES), F32),
                        pltpu.VMEM((b, SUBLANES, LANES), F32)],
        compiler_params=_cparams(("arbitrary",), 32),
        name="mlstm",
    )(qk_m, v_m, o_m, gates, gates_t, conv_w, conv_b, bias_slab, bias_col, gn_w)


COMB_GROUP_LANE = N_EXPERTS


def _route(logits):
    lane = lax.broadcasted_iota(I32, logits.shape, 1).astype(F32)
    g_mask = (lane >= N_EXPERTS) & (lane < N_EXPERTS + N_GROUPS)
    g_logit = jnp.where(g_mask, logits, -jnp.inf)
    g_exp = jnp.exp(g_logit - jnp.max(g_logit, axis=1, keepdims=True))
    g_prob = g_exp / jnp.sum(g_exp, axis=1, keepdims=True)
    g_p = jnp.max(g_prob, axis=1, keepdims=True)
    g_sel = jnp.min(jnp.where(g_mask & (g_prob == g_p), lane, float(LANES)), axis=1, keepdims=True) - N_EXPERTS
    e_mask = (lane >= g_sel * EXPERTS_PER_GROUP) & (lane < (g_sel + 1.0) * EXPERTS_PER_GROUP)
    e_logit = jnp.where(e_mask, logits, -jnp.inf)
    e_exp = jnp.exp(e_logit - jnp.max(e_logit, axis=1, keepdims=True))
    e_prob = e_exp / jnp.sum(e_exp, axis=1, keepdims=True)
    p1 = jnp.max(jnp.where(e_mask, e_prob, -1.0), axis=1, keepdims=True)
    i1 = jnp.min(jnp.where(e_mask & (e_prob == p1), lane, float(LANES)), axis=1, keepdims=True)
    rest = e_mask & (lane != i1)
    p2 = jnp.max(jnp.where(rest, e_prob, -1.0), axis=1, keepdims=True)
    i2 = jnp.min(jnp.where(rest & (e_prob == p2), lane, float(LANES)), axis=1, keepdims=True)
    total = p1 + p2
    comb = (jnp.where(lane == i1, p1 / total, 0.0) + jnp.where(lane == i2, p2 / total, 0.0)) * g_p
    return comb + jnp.where(lane == COMB_GROUP_LANE, g_sel, 0.0)


def _mix_body(ya_ref, ym_ref, x_ref, wo_ref, g_ref, b_ref, wr_ref, br_ref, x1_ref, comb_ref, *, alpha):
    mix = (jnp.dot(ya_ref[...], wo_ref[0:D_A, :], preferred_element_type=F32)
           + jnp.dot(ym_ref[...], wo_ref[D_A:D_A + D_M, :], preferred_element_type=F32))
    x1 = _layer_norm(alpha * x_ref[...] + mix, g_ref[...], b_ref[...])
    x1_ref[...] = x1
    x_hi = x1.astype(BF16)
    x_lo = (x1 - x_hi.astype(F32)).astype(BF16)
    w_hi = wr_ref[0]
    w_lo = wr_ref[1]
    logits = (jnp.dot(x_hi, w_hi, preferred_element_type=F32) + jnp.dot(x_lo, w_hi, preferred_element_type=F32)
              + jnp.dot(x_hi, w_lo, preferred_element_type=F32)) + br_ref[...]
    comb_ref[...] = _route(logits)


def _mix_ln_router(y_a, y_m, x2d, w_out, ln_g, ln_b, w_router, b_router, alpha, tm):
    n, d = x2d.shape
    const = lambda i: (0, 0)
    return pl.pallas_call(
        functools.partial(_mix_body, alpha=alpha),
        grid=(n // tm,),
        in_specs=[pl.BlockSpec((tm, D_A), lambda i: (i, 0)),
                  pl.BlockSpec((tm, D_M), lambda i: (i, 0)),
                  pl.BlockSpec((tm, d), lambda i: (i, 0)),
                  pl.BlockSpec((D_A + D_M, d), const),
                  pl.BlockSpec((1, d), const),
                  pl.BlockSpec((1, d), const),
                  pl.BlockSpec((2, d, LANES), lambda i: (0, 0, 0)),
                  pl.BlockSpec((1, LANES), const)],
        out_specs=[pl.BlockSpec((tm, d), lambda i: (i, 0)),
                   pl.BlockSpec((tm, LANES), lambda i: (i, 0))],
        out_shape=[jax.ShapeDtypeStruct((n, d), F32), jax.ShapeDtypeStruct((n, LANES), F32)],
        compiler_params=_cparams(("arbitrary",), 32),
        name="mix_ln_router",
    )(y_a, y_m, x2d, w_out, ln_g, ln_b, w_router, b_router)


MOE_ROWS = TM_MOE // N_GROUPS + 32


def _moe_body(x1_ref, comb_ref, wg_ref, wu_ref, wd_ref, g_ref, b_ref, p_ref, wpg_ref, wpp_ref, y_ref,
              xb_ref, before_ref, row_ref, *, alpha, tm):
    g = pl.program_id(1)
    g_f = g.astype(F32)

    @pl.when((pl.program_id(0) == 0) & (g == 0))
    def _():
        r = lax.broadcasted_iota(I32, (tm, tm), 0)
        c = lax.broadcasted_iota(I32, (tm, tm), 1)
        before_ref[...] = (c < r).astype(BF16)

    @pl.when(g == 0)
    def _():
        d = x1_ref.shape[1]
        xb_ref[:, 0:d] = x1_ref[...].astype(BF16)
        y_ref[...] = jnp.zeros(y_ref.shape, F32)
        comb = comb_ref[...]
        for k, term in enumerate(_split3(comb)):
            xb_ref[:, d + k * LANES:d + (k + 1) * LANES] = term
        group_row = comb.T[COMB_GROUP_LANE:COMB_GROUP_LANE + 1, :]
        sub = lax.broadcasted_iota(I32, (SUBLANES, tm), 0)
        member_t = (sub.astype(F32) == group_row) & (sub < N_GROUPS)
        ahead_t = lax.dot_general(jnp.where(member_t, 1.0, 0.0).astype(BF16), before_ref[...], NT_DIMS,
                                  preferred_element_type=F32)
        rank_row = jnp.sum(jnp.where(member_t, ahead_t, 0.0), axis=0, keepdims=True)
        row_ref[0] = jnp.broadcast_to(group_row, (SUBLANES, tm))
        row_ref[1] = jnp.broadcast_to(rank_row, (SUBLANES, tm))

    in_group_row = row_ref[0, 0:1, :] == g_f
    n_tokens = jnp.sum(jnp.where(in_group_row, 1.0, 0.0))
    n_blocks = (n_tokens.astype(I32) + MOE_ROWS - 1) // MOE_ROWS
    slot = lax.broadcasted_iota(I32, (MOE_ROWS, tm), 0).astype(F32)
    comb_lane = lax.broadcasted_iota(I32, (MOE_ROWS, LANES), 1)

    def block(b, carry):
        base = (b * MOE_ROWS).astype(F32)
        gather = jnp.where(in_group_row & (row_ref[1, 0:1, :] - base == slot), 1.0, 0.0).astype(BF16)
        d = x1_ref.shape[1]
        picked = jnp.dot(gather, xb_ref[...], preferred_element_type=F32)
        xc = picked[:, 0:d].astype(BF16)
        comb_c = picked[:, d:d + LANES] + picked[:, d + LANES:d + 2 * LANES] + picked[:, d + 2 * LANES:d + 3 * LANES]
        yc = jnp.zeros((MOE_ROWS, d), F32)
        for e in range(EXPERTS_PER_GROUP):
            gate = jnp.dot(xc, wg_ref[e], preferred_element_type=F32)
            up = jnp.dot(xc, wu_ref[e], preferred_element_type=F32)
            weight = jnp.sum(jnp.where(comb_lane == g * EXPERTS_PER_GROUP + e, comb_c, 0.0), axis=1, keepdims=True)
            hidden = gate * jax.nn.sigmoid(gate) * up * weight
            yc = yc + jnp.dot(hidden.astype(BF16), wd_ref[e], preferred_element_type=F32)
        yc_hi = yc.astype(BF16)
        yc_lo = (yc - yc_hi.astype(F32)).astype(BF16)
        tn = (((0,), (0,)), ((), ()))
        y_ref[...] += lax.dot_general(jnp.concatenate([gather, gather], axis=0),
                                      jnp.concatenate([yc_hi, yc_lo], axis=0), tn, preferred_element_type=F32)
        return carry

    lax.fori_loop(0, n_blocks, block, 0)

    @pl.when(g == N_GROUPS - 1)
    def _():
        x2 = _layer_norm(alpha * x1_ref[...] + y_ref[...], g_ref[...], b_ref[...])
        gate = jax.nn.sigmoid(jnp.dot(x2.astype(BF16), wpg_ref[...], preferred_element_type=F32))
        proj = jnp.dot(p_ref[...].astype(BF16), wpp_ref[...], preferred_element_type=F32)
        y_ref[...] = x2 + gate * proj


def _moe_ln_ple(x1, comb, w_gate, w_up, w_down, ln_g, ln_b, p2d, w_ple_gate, w_ple_proj, alpha, tm):
    n, d = x1.shape
    dp = p2d.shape[1]
    const = lambda i, g: (0, 0)
    once = pl.Buffered(1)
    grouped = lambda w: w.reshape((N_GROUPS, EXPERTS_PER_GROUP) + w.shape[1:])
    return pl.pallas_call(
        functools.partial(_moe_body, alpha=alpha, tm=tm),
        grid=(n // tm, N_GROUPS),
        in_specs=[pl.BlockSpec((tm, d), lambda i, g: (i, 0), pipeline_mode=once),
                  pl.BlockSpec((tm, LANES), lambda i, g: (i, 0)),
                  pl.BlockSpec((None, EXPERTS_PER_GROUP, d, D_EXPERT), lambda i, g: (g, 0, 0, 0)),
                  pl.BlockSpec((None, EXPERTS_PER_GROUP, d, D_EXPERT), lambda i, g: (g, 0, 0, 0)),
                  pl.BlockSpec((None, EXPERTS_PER_GROUP, D_EXPERT, d), lambda i, g: (g, 0, 0, 0)),
                  pl.BlockSpec((1, d), const),
                  pl.BlockSpec((1, d), const),
                  pl.BlockSpec((tm, dp), lambda i, g: (i, 0), pipeline_mode=once),
                  pl.BlockSpec((d, d), const, pipeline_mode=once),
                  pl.BlockSpec((dp, d), const, pipeline_mode=once)],
        out_specs=pl.BlockSpec((tm, d), lambda i, g: (i, 0)),
        out_shape=jax.ShapeDtypeStruct((n, d), F32),
        scratch_shapes=[pltpu.VMEM((tm, d + 3 * LANES), BF16),
                        pltpu.VMEM((tm, tm), BF16),
                        pltpu.VMEM((2, SUBLANES, tm), F32)],
        compiler_params=_cparams(("arbitrary", "arbitrary"), 58),
        name="moe_ln_ple",
    )(x1, comb, grouped(w_gate), grouped(w_up), grouped(w_down), ln_g, ln_b, p2d, w_ple_gate, w_ple_proj)


def _tile(n, want):
    t = min(want, n)
    assert n % t == 0
    return t


def _layer(x, p_i, w_in, b_igate, b_fgate, conv_w, conv_b, gn_w, w_out, ln1_g, ln1_b,
           rg_w, rg_b, re_w, re_b, e_gate, e_up, e_down, ln2_g, ln2_b, ple_gate_w, ple_proj_w, alpha):
    b, s, d = x.shape
    n = b * s
    x2d = x.reshape(n, d)

    tk = _tile(s, TK_DSA)
    w_packed, w_vt = _pack_w_in(w_in)
    qk, v_t, q_idx, k4, gates, qk_m, v_m, o_m = _in_proj(x2d, w_packed, w_vt, tk)

    gates3 = gates.reshape(b, s, LANES)
    y_a = _dsa_attention(qk.reshape(2 * SLABS_PER_TENSOR, b, s, LANES),
                         v_t.reshape(SLABS_PER_TENSOR, b, s // tk, LANES, tk),
                         q_idx.reshape(b, s, N_IDX_HEADS * IDX_DIM), k4.reshape(b, s, LANES), gates3,
                         _tile(s, TQ_DSA), tk)

    gate_bias = jnp.concatenate([b_igate, b_fgate]).astype(F32)
    bias_slab = jnp.zeros((1, LANES), F32).at[0, G_I:G_I + 2 * N_HEADS_M].set(gate_bias)
    gates_t = jnp.swapaxes(gates3[:, :, G_I:G_I + 2 * N_HEADS_M], 1, 2)
    y_m = _mlstm(qk_m.reshape(b, s, 2 * D_M), v_m.reshape(b, s, D_M), o_m.reshape(b, s, D_M), gates3, gates_t,
                 conv_w, conv_b.reshape(1, -1), bias_slab, gate_bias.reshape(-1, 1), gn_w.reshape(1, -1),
                 _tile(s, CHUNK_MLSTM))

    w_router = jnp.concatenate([re_w, rg_w, jnp.zeros((d, LANES - N_EXPERTS - N_GROUPS), F32)], axis=1)
    w_router_hi = w_router.astype(BF16)
    w_router = jnp.stack([w_router_hi, (w_router - w_router_hi.astype(F32)).astype(BF16)])
    b_router = jnp.concatenate([re_b, rg_b, jnp.zeros((LANES - N_EXPERTS - N_GROUPS,), F32)]).reshape(1, LANES)
    x1, comb = _mix_ln_router(y_a.reshape(n, D_A), y_m.reshape(n, D_M), x2d, w_out.astype(BF16),
                              ln1_g.reshape(1, d), ln1_b.reshape(1, d), w_router, b_router, alpha, _tile(n, TM_MIX))

    out = _moe_ln_ple(x1, comb, e_gate.astype(BF16), e_up.astype(BF16), e_down.astype(BF16),
                      ln2_g.reshape(1, d), ln2_b.reshape(1, d), p_i.reshape(n, -1), ple_gate_w.astype(BF16),
                      ple_proj_w.astype(BF16), alpha, _tile(n, TM_MOE))
    return out.reshape(b, s, d)


def kernel(x, p, w_in, b_igate, b_fgate, conv_w, conv_b, gn_w, w_out, ln1_g, ln1_b, router_group_w, router_group_b,
           router_expert_w, router_expert_b, expert_w_gate, expert_w_up, expert_w_down, ln2_g, ln2_b, ple_gate_w,
           ple_proj_w):
    depth = w_in.shape[0]
    alpha = (2.0 * depth) ** 0.25
    for i in range(depth):
        x = _layer(x, p[i], w_in[i], b_igate[i], b_fgate[i], conv_w[i], conv_b[i], gn_w[i], w_out[i], ln1_g[i],
                   ln1_b[i], router_group_w[i], router_group_b[i], router_expert_w[i], router_expert_b[i],
                   expert_w_gate[i], expert_w_up[i], expert_w_down[i], ln2_g[i], ln2_b[i], ple_gate_w[i],
                   ple_proj_w[i], alpha)
    return x
```

```python
import functools

import numpy as np
import jax
import jax.numpy as jnp
from jax import lax
from jax.experimental import pallas as pl
from jax.experimental.pallas import tpu as pltpu

F32 = jnp.float32
BF16 = jnp.bfloat16
I32 = jnp.int32

N_HEADS_A = 8
HEAD_DIM_A = 64
D_A = N_HEADS_A * HEAD_DIM_A
N_IDX_HEADS = 8
IDX_DIM = 32
TOPK_MAX = 256
N_HEADS_M = 4
HEAD_DIM_M = 128
D_M = N_HEADS_M * HEAD_DIM_M
CONV_K = 4
N_GROUPS = 4
EXPERTS_PER_GROUP = 8
N_EXPERTS = N_GROUPS * EXPERTS_PER_GROUP
D_EXPERT = 256
LN_EPS = 1e-5
NEG_BIG = -1e30
IN_SPLITS = (D_A, D_A, D_A, N_IDX_HEADS * IDX_DIM, IDX_DIM, N_IDX_HEADS, D_M, D_M, D_M, D_M, N_HEADS_M, N_HEADS_M)

LANES = 128
SUBLANES = 8
MIB = 1024 * 1024

C_QK = 0
C_QIDX = C_QK + 2 * D_A
C_K4 = C_QIDX + N_IDX_HEADS * IDX_DIM
C_GATE = C_K4 + LANES
C_QKM = C_GATE + LANES
C_VM = C_QKM + 2 * D_M
C_OM = C_VM + D_M
C_END = C_OM + D_M
SLABS_PER_TENSOR = D_A // LANES
HEADS_PER_SLAB = LANES // HEAD_DIM_A
G_W = 0
G_I = N_IDX_HEADS
G_F = G_I + N_HEADS_M

INT_MIN = -2 ** 31

TQ_DSA = LANES
TK_DSA = 1024
TM_IN_PROJ = TK_DSA
CHUNK_MLSTM = 256
TM_MIX = 1024
TM_MOE = 1024

NT_DIMS = (((1,), (1,)), ((), ()))
TN_DIMS = (((0,), (0,)), ((), ()))


def _cparams(semantics, vmem_mib):
    return pltpu.CompilerParams(dimension_semantics=semantics, vmem_limit_bytes=int(vmem_mib * MIB))


def _layer_norm(r, g, b):
    mu = jnp.mean(r, axis=-1, keepdims=True)
    d = r - mu
    var = jnp.mean(d * d, axis=-1, keepdims=True)
    return d * lax.rsqrt(var + LN_EPS) * g + b


def _pack_w_in(w_in):
    d = w_in.shape[0]
    cuts = [int(c) for c in np.cumsum(IN_SPLITS)[:-1]]
    q_a, k_a, v_a, q_idx, k_idx, w_idx, q_m, k_m, v_m, o_m, i_m, f_m = jnp.split(w_in, cuts, axis=1)
    k4 = jnp.tile(k_idx, (1, LANES // IDX_DIM))
    gate = jnp.concatenate([w_idx, i_m, f_m, jnp.zeros((d, LANES - G_F - N_HEADS_M), w_in.dtype)], axis=1)
    packed = jnp.concatenate([q_a, k_a, q_idx, k4, gate, q_m, k_m, v_m, o_m], axis=1).astype(BF16)
    return packed, v_a.T.astype(BF16)


def _in_proj_body(x_ref, w_ref, wvt_ref, qk_ref, vt_ref, qi_ref, k4_ref, g_ref, qkm_ref, vm_ref, om_ref):
    xb = x_ref[...].astype(BF16)

    def proj(lo, hi):
        return jnp.dot(xb, w_ref[:, lo:hi], preferred_element_type=F32)

    qk = proj(C_QK, C_QIDX).astype(BF16)
    for c in range(2 * SLABS_PER_TENSOR):
        qk_ref[c] = qk[:, c * LANES:(c + 1) * LANES]
    v_t = lax.dot_general(wvt_ref[...], xb, NT_DIMS, preferred_element_type=F32).astype(BF16)
    for c in range(SLABS_PER_TENSOR):
        vt_ref[c, 0] = v_t[c * LANES:(c + 1) * LANES, :]
    qi_ref[...] = proj(C_QIDX, C_K4).astype(BF16)
    k4_ref[...] = proj(C_K4, C_GATE).astype(BF16)
    g_ref[...] = proj(C_GATE, C_QKM)
    qkm_ref[...] = proj(C_QKM, C_VM)
    vm_ref[...] = proj(C_VM, C_OM).astype(BF16)
    om_ref[...] = proj(C_OM, C_END)


def _in_proj(x2d, w_packed, w_vt, tm):
    n, d = x2d.shape
    widths = [(C_K4 - C_QIDX, BF16), (C_GATE - C_K4, BF16), (C_QKM - C_GATE, F32),
              (C_VM - C_QKM, F32), (C_OM - C_VM, BF16), (C_END - C_OM, F32)]
    return pl.pallas_call(
        _in_proj_body,
        grid=(n // tm,),
        in_specs=[pl.BlockSpec((tm, d), lambda i: (i, 0)),
                  pl.BlockSpec((d, C_END), lambda i: (0, 0)),
                  pl.BlockSpec((D_A, d), lambda i: (0, 0))],
        out_specs=([pl.BlockSpec((2 * SLABS_PER_TENSOR, tm, LANES), lambda i: (0, i, 0)),
                    pl.BlockSpec((SLABS_PER_TENSOR, 1, LANES, tm), lambda i: (0, i, 0, 0))]
                   + [pl.BlockSpec((tm, w), lambda i: (i, 0)) for w, _ in widths]),
        out_shape=([jax.ShapeDtypeStruct((2 * SLABS_PER_TENSOR, n, LANES), BF16),
                    jax.ShapeDtypeStruct((SLABS_PER_TENSOR, n // tm, LANES, tm), BF16)]
                   + [jax.ShapeDtypeStruct((n, w), dt) for w, dt in widths]),
        compiler_params=_cparams(("arbitrary",), 48),
        name="in_proj",
    )(x2d, w_packed, w_vt)


def _float_to_ordered_int(value):
    bits = lax.bitcast_convert_type(value, I32)
    return bits ^ ((bits >> 31) & jnp.int32(0x7FFFFFFF))


def _ordered_int_to_float(key):
    bits = key ^ ((key >> 31) & jnp.int32(0x7FFFFFFF))
    return lax.bitcast_convert_type(bits, F32)


SEARCH_FIXED_PASSES = 12
SEARCH_FREE_PASSES = 16
SEARCH_MAX_PASSES = SEARCH_FREE_PASSES + 33
ZERO_TIE_MARGIN = 128
CHAINS = 4
ATTEND_BLOCKS = 2
POS_RADIX = 64
POS_TERMS = 3
LOG2_E = float(np.log2(np.e))


def _dsa_body(qa_ref, ka_ref, vt_ref, qi_ref, k4_ref, g_ref, o_ref,
              sc_ref, qm_ref, qh_ref, pos_ref, mask_ref, acc_ref, m_ref, l_ref, *, seq, topk, tq, tk):
    t0 = pl.program_id(1) * tq
    n_kb = (t0 + tq + tk - 1) // tk
    groups = tk // (CHAINS * SUBLANES)
    shape4 = (groups, CHAINS, SUBLANES, tq)

    def chunks(x):
        return x.reshape(shape4)

    def fold(x):
        out = x[0]
        for c in range(1, CHAINS):
            out = out + x[c]
        return out

    def rows8(row):
        return jnp.broadcast_to(row, (SUBLANES, tq))

    key_off = ((lax.broadcasted_iota(I32, shape4, 0) * CHAINS + lax.broadcasted_iota(I32, shape4, 1)) * SUBLANES
               + lax.broadcasted_iota(I32, shape4, 2))
    q_pos = t0 + lax.broadcasted_iota(I32, shape4, 3)
    lane = lax.broadcasted_iota(I32, (tq, LANES), 1)

    for h in range(N_IDX_HEADS):
        per_slab = LANES // IDX_DIM
        slab = qi_ref[:, (h // per_slab) * LANES:(h // per_slab + 1) * LANES]
        lo = (h % per_slab) * IDX_DIM
        qm_ref[h * tq:(h + 1) * tq, :] = jnp.where((lane >= lo) & (lane < lo + IDX_DIM), slab, jnp.zeros_like(slab))
    for h in range(N_HEADS_A):
        p, hh = divmod(h, HEADS_PER_SLAB)
        slab = qa_ref[p].astype(F32) * (HEAD_DIM_A ** -0.5 * LOG2_E)
        in_head = (lane >= hh * HEAD_DIM_A) & (lane < (hh + 1) * HEAD_DIM_A)
        qh_ref[p, hh * tq:(hh + 1) * tq, 0:LANES] = jnp.where(in_head, slab, 0.0).astype(BF16)
    w_rows = g_ref[...].T[G_W:G_W + N_IDX_HEADS, :] * ((IDX_DIM * N_IDX_HEADS) ** -0.5)

    @pl.when((pl.program_id(0) == 0) & (pl.program_id(1) == 0))
    def _():
        for h in range(N_HEADS_A):
            p, hh = divmod(h, HEADS_PER_SLAB)
            slope = 2.0 ** (-8.0 * (h + 1) / N_HEADS_A)
            weight = jnp.where(lane < POS_TERMS, slope * POS_RADIX * LOG2_E,
                               jnp.where(lane < 2 * POS_TERMS, slope * LOG2_E, 0.0))
            terms = _split3(weight)
            feat = jnp.where(lane % POS_TERMS == 0, terms[0], jnp.where(lane % POS_TERMS == 1, terms[1], terms[2]))
            qh_ref[p, hh * tq:(hh + 1) * tq, LANES:2 * LANES] = feat
        pos_lane = lax.broadcasted_iota(I32, (tk, LANES), 1)
        pos_row = lax.broadcasted_iota(I32, (tk, LANES), 0)
        for j in range(seq // tk):
            pos = pos_row + j * tk
            pos_ref[j * tk:(j + 1) * tk, :] = jnp.where(
                pos_lane < POS_TERMS, pos // POS_RADIX,
                jnp.where(pos_lane < 2 * POS_TERMS, pos % POS_RADIX, 0)).astype(F32).astype(BF16)

    def score_block(j, carry):
        hi_part, lo_part, zero_part = carry
        k0 = pl.multiple_of(j * tk, tk)
        logits = lax.dot_general(k4_ref[pl.ds(k0, tk), :], qm_ref[...], NT_DIMS,
                                 preferred_element_type=F32)
        score = jnp.zeros(shape4, F32)
        for h in range(N_IDX_HEADS):
            score = score + rows8(w_rows[h:h + 1, :]) * jnp.maximum(chunks(logits[:, h * tq:(h + 1) * tq]), 0.0)
        causal = k0 + key_off <= q_pos
        kept = jnp.where(causal, score, NEG_BIG)
        sc_ref[j] = kept
        hi_part = jnp.maximum(hi_part, jnp.max(kept, axis=0))
        lo_part = jnp.minimum(lo_part, jnp.min(score, axis=0))
        zero_part = zero_part + jnp.sum(jnp.where(kept >= 0.0, 1.0, 0.0), axis=0)
        return hi_part, lo_part, zero_part

    part_shape = (CHAINS, SUBLANES, tq)
    hi_part, lo_part, zero_part = lax.fori_loop(
        0, n_kb, score_block,
        (jnp.full(part_shape, -jnp.inf, F32), jnp.full(part_shape, jnp.inf, F32), jnp.zeros(part_shape, F32)))
    row_max = jnp.max(jnp.max(hi_part, axis=0), axis=0, keepdims=True)
    row_min = jnp.min(jnp.min(lo_part, axis=0), axis=0, keepdims=True)

    n_tail = (seq - n_kb * tk).astype(F32)

    def count(thr, strict):
        thr8 = rows8(thr)

        def body(j, acc):
            s = sc_ref[j]
            return acc + jnp.sum(jnp.where((s > thr8) if strict else (s >= thr8), 1.0, 0.0), axis=0)

        acc = lax.fori_loop(0, n_kb, body, jnp.zeros(part_shape, F32))
        tail_hit = (NEG_BIG > thr) if strict else (NEG_BIG >= thr)
        return jnp.sum(fold(acc), axis=0, keepdims=True) + jnp.where(tail_hit, n_tail, 0.0)

    n_causal = (t0 + 1 + lax.broadcasted_iota(I32, (1, tq), 1)).astype(F32)
    few = n_causal < topk
    lo0 = jnp.where(few, _float_to_ordered_int(jnp.full((1, tq), NEG_BIG, F32)), _float_to_ordered_int(row_min))
    cnt_lo0 = jnp.where(few, float(seq), n_causal)
    hi0 = _float_to_ordered_int(row_max) + 1
    log_k = float(np.log(topk))

    def log_excess(cnt):
        return jnp.log(jnp.maximum(cnt, 0.5)) - log_k

    def probe(it, bracket):
        lo, hi, cnt_lo, f_lo, f_hi, last, done_i = bracket
        done = done_i != 0
        it_v = jnp.zeros((1, tq), I32) + it
        lo_v = _ordered_int_to_float(lo)
        hi_v = _ordered_int_to_float(hi)
        guess = _float_to_ordered_int(lo_v + (f_lo / (f_lo - f_hi)) * (hi_v - lo_v))
        guess = jnp.where((it_v == 1) & (lo == 0) & (cnt_lo < topk + ZERO_TIE_MARGIN), 1, guess)
        middle = lo + lax.shift_right_logical(hi - lo, jnp.ones((1, tq), I32))
        usable = (guess > lo) & (guess < hi) & (it_v < SEARCH_FREE_PASSES)
        cand = jnp.where(usable, guess, middle)
        cnt = count(_ordered_int_to_float(cand), strict=False)
        f_c = log_excess(cnt)
        take = (cnt >= topk) & ~done
        drop = (cnt < topk) & ~done
        f_hi = jnp.where(take, jnp.where(last == 1, 0.5 * f_hi, f_hi), jnp.where(drop, f_c, f_hi))
        f_lo = jnp.where(drop, jnp.where(last == -1, 0.5 * f_lo, f_lo), jnp.where(take, f_c, f_lo))
        last = jnp.where(take, 1, jnp.where(drop, -1, last))
        lo = jnp.where(take, cand, lo)
        cnt_lo = jnp.where(take, cnt, cnt_lo)
        hi = jnp.where(drop, cand, hi)
        done = done | (cnt_lo == topk) | ((hi - lo) == 1)
        return lo, hi, cnt_lo, f_lo, f_hi, last, done.astype(I32)

    def search_step(state):
        bracket = probe(state[0], state[3:])
        n_active = jnp.sum(jnp.where(bracket[6] != 0, 0.0, 1.0))
        return (state[0] + 1, n_active, jnp.max(bracket[2])) + bracket

    def search_on(state):
        return (state[0] < SEARCH_MAX_PASSES) & (state[1] > 0.0)

    cnt_zero = jnp.sum(fold(zero_part), axis=0, keepdims=True)
    inside = (lo0 < 0) & (hi0 > 0) & ~few
    take0 = inside & (cnt_zero >= topk)
    drop0 = inside & (cnt_zero < topk)
    lo1 = jnp.where(take0, 0, lo0)
    hi1 = jnp.where(drop0, 0, hi0)
    cnt_lo1 = jnp.where(take0, cnt_zero, cnt_lo0)
    done1 = few | (cnt_lo1 == topk) | ((hi1 - lo1) == 1)
    bracket = (lo1, hi1, cnt_lo1, log_excess(cnt_lo1), log_excess(jnp.where(drop0, cnt_zero, 0.0)),
               jnp.where(take0, 1, jnp.where(drop0, -1, 0)), done1.astype(I32))
    bracket = lax.fori_loop(1, 1 + SEARCH_FIXED_PASSES, probe, bracket)
    state = search_step((jnp.int32(1 + SEARCH_FIXED_PASSES), jnp.float32(0.0), jnp.float32(0.0)) + bracket)
    state = lax.while_loop(search_on, search_step, state)
    max_cnt_thr = state[2]
    thr = _ordered_int_to_float(state[3])
    thr8 = rows8(thr)

    @pl.when(max_cnt_thr > topk)
    def _():
        need = topk - count(thr, strict=True)
        tri_r = lax.broadcasted_iota(I32, (tk, tk), 0)
        tri_c = lax.broadcasted_iota(I32, (tk, tk), 1)
        earlier = (tri_c < tri_r).astype(BF16)

        def drop_late_ties(j, seen):
            s = sc_ref[j]
            eq = s == thr8
            eq_f = jnp.where(eq, 1.0, 0.0)
            rank = seen + jnp.dot(earlier, eq_f.reshape(tk, tq).astype(BF16), preferred_element_type=F32)
            sc_ref[j] = jnp.where(eq & (chunks(rank) >= rows8(need)), -jnp.inf, s)
            return seen + jnp.sum(fold(jnp.sum(eq_f, axis=0)), axis=0, keepdims=True)

        lax.fori_loop(0, n_kb, drop_late_ties, jnp.zeros((1, tq), F32))

    m_ref[...] = jnp.full(m_ref.shape, NEG_BIG, F32)
    l_ref[...] = jnp.zeros(l_ref.shape, F32)
    acc_ref[...] = jnp.zeros(acc_ref.shape, F32)

    def fold_max(x):
        out = x[0]
        for c in range(1, CHAINS):
            out = jnp.maximum(out, x[c])
        return out

    def attend(n_blocks, first, step, carry):
        blocks = [first + step * n_blocks + u for u in range(n_blocks)]
        starts = [pl.multiple_of(j * tk, tk) for j in blocks]
        for u, (j, k0) in enumerate(zip(blocks, starts)):
            mask_ref[u] = jnp.where((sc_ref[j] >= thr8) & (k0 + key_off <= q_pos), 0.0, -jnp.inf)

        scores = {}
        for p in range(SLABS_PER_TENSOR):
            for u, k0 in enumerate(starts):
                keys = jnp.concatenate([ka_ref[p, pl.ds(k0, tk), :], pos_ref[pl.ds(k0, tk), :]], axis=1)
                scores[p, u] = lax.dot_general(keys, qh_ref[p], NT_DIMS, preferred_element_type=F32)
        weights = {}
        for p in range(SLABS_PER_TENSOR):
            for hh in range(HEADS_PER_SLAB):
                s = [chunks(scores[p, u][:, hh * tq:(hh + 1) * tq]) + mask_ref[u] for u in range(n_blocks)]
                m_old = m_ref[p, hh]
                part = jnp.max(s[0], axis=0)
                for u in range(1, n_blocks):
                    part = jnp.maximum(part, jnp.max(s[u], axis=0))
                m_new = jnp.maximum(m_old, rows8(jnp.max(fold_max(part), axis=0, keepdims=True)))
                alpha = jnp.exp2(m_old - m_new)
                m_ref[p, hh] = m_new
                weights[p, hh] = (alpha, [jnp.exp2(su - m_new).reshape(tk, tq).astype(BF16) for su in s])
        ones_rows = jnp.ones((SUBLANES, tk), BF16)
        for p in range(SLABS_PER_TENSOR):
            for hh in range(HEADS_PER_SLAB):
                rows = slice(hh * HEAD_DIM_A, (hh + 1) * HEAD_DIM_A)
                alpha, probs = weights[p, hh]
                pv = jnp.zeros((HEAD_DIM_A + SUBLANES, tq), F32)
                for u, j in enumerate(blocks):
                    values = jnp.concatenate([vt_ref[p, j, rows, :], ones_rows], axis=0)
                    pv = pv + jnp.dot(values, probs[u], preferred_element_type=F32)
                old = acc_ref[p, rows, :].reshape(HEAD_DIM_A // SUBLANES, SUBLANES, tq)
                acc_ref[p, rows, :] = (alpha * old).reshape(HEAD_DIM_A, tq) + pv[0:HEAD_DIM_A]
                l_ref[p, hh] = alpha * l_ref[p, hh] + pv[HEAD_DIM_A:HEAD_DIM_A + SUBLANES]
        return carry

    n_pairs = n_kb // ATTEND_BLOCKS
    lax.fori_loop(0, n_pairs, functools.partial(attend, ATTEND_BLOCKS, 0), 0)
    lax.fori_loop(0, n_kb - ATTEND_BLOCKS * n_pairs, functools.partial(attend, 1, ATTEND_BLOCKS * n_pairs), 0)

    for p in range(SLABS_PER_TENSOR):
        for hh in range(HEADS_PER_SLAB):
            rows = slice(hh * HEAD_DIM_A, (hh + 1) * HEAD_DIM_A)
            acc = acc_ref[p, rows, :].reshape(HEAD_DIM_A // SUBLANES, SUBLANES, tq)
            out_rows = slice(p * LANES + hh * HEAD_DIM_A, p * LANES + (hh + 1) * HEAD_DIM_A)
            o_ref[out_rows, :] = (acc / l_ref[p, hh]).reshape(HEAD_DIM_A, tq).astype(BF16)


def _dsa_attention(qk, v_t, q_idx, k4, gates, tq, tk):
    _, b, s, _ = qk.shape
    assert tq == LANES and s // POS_RADIX <= 256
    topk = min(TOPK_MAX, s // 4)
    once = pl.Buffered(1)
    groups = tk // (CHAINS * SUBLANES)
    body = functools.partial(_dsa_body, seq=s, topk=topk, tq=tq, tk=tk)
    return pl.pallas_call(
        body,
        grid=(b, s // tq),
        in_specs=[pl.BlockSpec((SLABS_PER_TENSOR, None, tq, LANES), lambda bi, qi: (0, bi, qi, 0)),
                  pl.BlockSpec((SLABS_PER_TENSOR, None, s, LANES), lambda bi, qi: (1, bi, 0, 0), pipeline_mode=once),
                  pl.BlockSpec((SLABS_PER_TENSOR, None, s // tk, LANES, tk), lambda bi, qi: (0, bi, 0, 0, 0),
                               pipeline_mode=once),
                  pl.BlockSpec((None, tq, N_IDX_HEADS * IDX_DIM), lambda bi, qi: (bi, qi, 0)),
                  pl.BlockSpec((None, s, LANES), lambda bi, qi: (bi, 0, 0), pipeline_mode=once),
                  pl.BlockSpec((None, tq, LANES), lambda bi, qi: (bi, qi, 0))],
        out_specs=pl.BlockSpec((None, D_A, tq), lambda bi, qi: (bi, 0, qi)),
        out_shape=jax.ShapeDtypeStruct((b, D_A, s), BF16),
        scratch_shapes=[pltpu.VMEM((s // tk, groups, CHAINS, SUBLANES, tq), F32),
                        pltpu.VMEM((N_IDX_HEADS * tq, LANES), BF16),
                        pltpu.VMEM((SLABS_PER_TENSOR, HEADS_PER_SLAB * tq, 2 * LANES), BF16),
                        pltpu.VMEM((s, LANES), BF16),
                        pltpu.VMEM((ATTEND_BLOCKS, groups, CHAINS, SUBLANES, tq), F32),
                        pltpu.VMEM((SLABS_PER_TENSOR, LANES, tq), F32),
                        pltpu.VMEM((SLABS_PER_TENSOR, HEADS_PER_SLAB, SUBLANES, tq), F32),
                        pltpu.VMEM((SLABS_PER_TENSOR, HEADS_PER_SLAB, SUBLANES, tq), F32)],
        compiler_params=_cparams(("arbitrary", "arbitrary"), 48),
        name="dsa_attention",
    )(qk, qk, v_t, q_idx, k4, gates)


def _log_sigmoid(x):
    return jnp.minimum(x, 0.0) - jnp.log(1.0 + jnp.exp(-jnp.abs(x)))


def _split3(a):
    hi = a.astype(BF16)
    r1 = a - hi.astype(F32)
    mid = r1.astype(BF16)
    lo = (r1 - mid.astype(F32)).astype(BF16)
    return hi, mid, lo


def _mlstm_body(qk_ref, v_ref, o_ref, g_ref, gt_ref, cw_ref, cb_ref, bg_ref, bgt_ref, gn_ref, y_ref,
                xe_ref, c_ref, m_ref, *, chunk, batch):
    @pl.when(pl.program_id(0) == 0)
    def _():
        xe_ref[:, 0:SUBLANES, :] = jnp.zeros((batch, SUBLANES, 2 * D_M), F32)
        c_ref[...] = jnp.zeros(c_ref.shape, F32)
        m_ref[...] = jnp.zeros(m_ref.shape, F32)

    gates = [_mlstm_gates(qk_ref.at[bi], g_ref.at[bi], gt_ref.at[bi], cw_ref, cb_ref, bg_ref, bgt_ref,
                          xe_ref.at[bi], chunk) for bi in range(batch)]
    chains = [(h, bi) for h in range(N_HEADS_M) for bi in range(batch)]
    scores = [_mlstm_head_scores(h, gates[bi], v_ref.at[bi], c_ref.at[bi], m_ref.at[bi], chunk) for h, bi in chains]
    for (h, bi), sc in zip(chains, scores):
        _mlstm_head(h, gates[bi], sc, o_ref.at[bi], gn_ref, y_ref.at[bi], chunk)
    for (h, bi), sc in zip(chains, scores):
        _mlstm_head_state(h, gates[bi], sc, c_ref.at[bi], m_ref.at[bi], chunk)


def _mlstm_gates(qk_ref, g_ref, gt_ref, cw_ref, cb_ref, bg_ref, bgt_ref, xe_ref, chunk):
    xe_ref[SUBLANES:SUBLANES + chunk, :] = qk_ref[...]
    conv = cb_ref[...]
    for j in range(CONV_K):
        start = SUBLANES - (CONV_K - 1) + j
        conv = conv + cw_ref[j:j + 1, :] * xe_ref[start:start + chunk, :]
    qk = conv * jax.nn.sigmoid(conv)
    xe_ref[0:SUBLANES, :] = xe_ref[chunk:chunk + SUBLANES, :]

    gb = g_ref[...] + bg_ref[...]
    gbt = gt_ref[...] + bgt_ref[...]
    r = lax.broadcasted_iota(I32, (chunk, chunk), 0)
    c = lax.broadcasted_iota(I32, (chunk, chunk), 1)
    causal = r >= c
    incl = causal.astype(BF16)
    incl_t = (r <= c).astype(BF16)
    cum = jnp.zeros((chunk, LANES), F32)
    for term in _split3(_log_sigmoid(gb)):
        cum = cum + jnp.dot(incl, term, preferred_element_type=F32)
    cum_t = jnp.zeros((2 * N_HEADS_M, chunk), F32)
    for term in _split3(_log_sigmoid(gbt)):
        cum_t = cum_t + jnp.dot(term, incl_t, preferred_element_type=F32)

    return qk, gb, gbt, cum, cum_t, causal


def _mlstm_head_scores(h, gates, v_ref, c_ref, m_ref, chunk):
    qk = gates[0]
    lane = lax.broadcasted_iota(I32, (chunk, LANES), 1)
    ones_col = jnp.where(lane == 0, 1.0, 0.0).astype(BF16)
    sl = slice(h * HEAD_DIM_M, (h + 1) * HEAD_DIM_M)
    q_h = (qk[:, sl] * (HEAD_DIM_M ** -0.5)).astype(BF16)
    k_f = qk[:, D_M + h * HEAD_DIM_M:D_M + (h + 1) * HEAD_DIM_M]
    v_aug = jnp.concatenate([v_ref[:, sl], ones_col], axis=1)
    m_prev = m_ref[h:h + 1, 0:1]
    c_aug = c_ref[h]
    qk_t = lax.dot_general(q_h, k_f.astype(BF16), NT_DIMS, preferred_element_type=F32)
    carried = jnp.dot(q_h, c_aug.astype(BF16), preferred_element_type=F32)
    return k_f, v_aug, m_prev, c_aug, qk_t, carried


def _mlstm_head(h, gates, scores, o_ref, gn_ref, y_ref, chunk):
    _, _, gbt, cum, cum_t, causal = gates
    _, v_aug, m_prev, _, qk_t, carried = scores
    sl = slice(h * HEAD_DIM_M, (h + 1) * HEAD_DIM_M)
    b_col = cum[:, G_F + h:G_F + h + 1]
    b_row = cum_t[N_HEADS_M + h:N_HEADS_M + h + 1, :]
    i_row = gbt[h:h + 1, :]

    d_log = jnp.where(causal, b_col - b_row + i_row, -jnp.inf)
    inter = b_col + m_prev
    m_t = jnp.maximum(inter, jnp.max(d_log, axis=1, keepdims=True))
    w_intra = qk_t * jnp.exp(d_log - m_t)
    w_inter = jnp.exp(inter - m_t)
    intra = jnp.dot(w_intra.astype(BF16), v_aug[:, 0:HEAD_DIM_M], preferred_element_type=F32)
    num = intra + w_inter * carried[:, 0:HEAD_DIM_M]
    den = jnp.sum(w_intra, axis=1, keepdims=True) + w_inter * carried[:, HEAD_DIM_M:HEAD_DIM_M + 1]
    hid = num / jnp.maximum(jnp.abs(den), jnp.exp(-m_t))

    mu = jnp.mean(hid, axis=1, keepdims=True)
    dev = hid - mu
    var = jnp.mean(dev * dev, axis=1, keepdims=True)
    normed = dev * lax.rsqrt(var + LN_EPS) * gn_ref[:, sl]
    y_ref[:, sl] = (jax.nn.sigmoid(o_ref[:, sl]) * normed).astype(BF16)


def _mlstm_head_state(h, gates, scores, c_ref, m_ref, chunk):
    _, gb, _, cum, _, _ = gates
    k_f, v_aug, m_prev, c_aug, _, _ = scores
    b_col = cum[:, G_F + h:G_F + h + 1]
    i_col = gb[:, G_I + h:G_I + h + 1]
    b_last = b_col[chunk - 1:chunk, :]
    w_log = b_last - b_col + i_col
    m_new = jnp.maximum(b_last + m_prev, jnp.max(w_log, axis=0, keepdims=True))
    decay = jnp.exp(b_last + m_prev - m_new)
    kw_t = (k_f * jnp.exp(w_log - m_new)).T.astype(BF16)
    c_ref[h] = decay * c_aug + jnp.dot(kw_t, v_aug, preferred_element_type=F32)
    m_ref[h:h + 1, :] = jnp.broadcast_to(m_new, (1, LANES))


def _mlstm(qk_m, v_m, o_m, gates, gates_t, conv_w, conv_b, bias_slab, bias_col, gn_w, chunk):
    b, s, _ = qk_m.shape
    body = functools.partial(_mlstm_body, chunk=chunk, batch=b)
    const = lambda ci: (0, 0)
    return pl.pallas_call(
        body,
        grid=(s // chunk,),
        in_specs=[pl.BlockSpec((b, chunk, 2 * D_M), lambda ci: (0, ci, 0)),
                  pl.BlockSpec((b, chunk, D_M), lambda ci: (0, ci, 0)),
                  pl.BlockSpec((b, chunk, D_M), lambda ci: (0, ci, 0)),
                  pl.BlockSpec((b, chunk, LANES), lambda ci: (0, ci, 0)),
                  pl.BlockSpec((b, 2 * N_HEADS_M, chunk), lambda ci: (0, 0, ci)),
                  pl.BlockSpec((CONV_K, 2 * D_M), const),
                  pl.BlockSpec((1, 2 * D_M), const),
                  pl.BlockSpec((1, LANES), const),
                  pl.BlockSpec((2 * N_HEADS_M, 1), const),
                  pl.BlockSpec((1, D_M), const)],
        out_specs=pl.BlockSpec((b, chunk, D_M), lambda ci: (0, ci, 0)),
        out_shape=jax.ShapeDtypeStruct((b, s, D_M), BF16),
        scratch_shapes=[pltpu.VMEM((b, chunk + SUBLANES, 2 * D_M), F32),
                        pltpu.VMEM((b, N_HEADS_M, HEAD_DIM_M, 2 * LANES), F32),
                        pltpu.VMEM((b, SUBLANES, LANES), F32)],
        compiler_params=_cparams(("arbitrary",), 32),
        name="mlstm",
    )(qk_m, v_m, o_m, gates, gates_t, conv_w, conv_b, bias_slab, bias_col, gn_w)


COMB_GROUP_LANE = N_EXPERTS


def _route(logits):
    lane = lax.broadcasted_iota(I32, logits.shape, 1).astype(F32)
    g_mask = (lane >= N_EXPERTS) & (lane < N_EXPERTS + N_GROUPS)
    g_logit = jnp.where(g_mask, logits, -jnp.inf)
    g_exp = jnp.exp(g_logit - jnp.max(g_logit, axis=1, keepdims=True))
    g_prob = g_exp / jnp.sum(g_exp, axis=1, keepdims=True)
    g_p = jnp.max(g_prob, axis=1, keepdims=True)
    g_sel = jnp.min(jnp.where(g_mask & (g_prob == g_p), lane, float(LANES)), axis=1, keepdims=True) - N_EXPERTS
    e_mask = (lane >= g_sel * EXPERTS_PER_GROUP) & (lane < (g_sel + 1.0) * EXPERTS_PER_GROUP)
    e_logit = jnp.where(e_mask, logits, -jnp.inf)
    e_exp = jnp.exp(e_logit - jnp.max(e_logit, axis=1, keepdims=True))
    e_prob = e_exp / jnp.sum(e_exp, axis=1, keepdims=True)
    p1 = jnp.max(jnp.where(e_mask, e_prob, -1.0), axis=1, keepdims=True)
    i1 = jnp.min(jnp.where(e_mask & (e_prob == p1), lane, float(LANES)), axis=1, keepdims=True)
    rest = e_mask & (lane != i1)
    p2 = jnp.max(jnp.where(rest, e_prob, -1.0), axis=1, keepdims=True)
    i2 = jnp.min(jnp.where(rest & (e_prob == p2), lane, float(LANES)), axis=1, keepdims=True)
    total = p1 + p2
    comb = (jnp.where(lane == i1, p1 / total, 0.0) + jnp.where(lane == i2, p2 / total, 0.0)) * g_p
    return comb + jnp.where(lane == COMB_GROUP_LANE, g_sel, 0.0)


def _mix_body(ya_ref, ym_ref, x_ref, wo_ref, g_ref, b_ref, wr_ref, br_ref, x1_ref, comb_ref, *, alpha):
    mix = (lax.dot_general(ya_ref[...], wo_ref[0:D_A, :], TN_DIMS, preferred_element_type=F32)
           + jnp.dot(ym_ref[...], wo_ref[D_A:D_A + D_M, :], preferred_element_type=F32))
    x1 = _layer_norm(alpha * x_ref[...] + mix, g_ref[...], b_ref[...])
    x1_ref[...] = x1
    x_hi = x1.astype(BF16)
    x_lo = (x1 - x_hi.astype(F32)).astype(BF16)
    w_hi = wr_ref[0]
    w_lo = wr_ref[1]
    logits = (jnp.dot(x_hi, w_hi, preferred_element_type=F32) + jnp.dot(x_lo, w_hi, preferred_element_type=F32)
              + jnp.dot(x_hi, w_lo, preferred_element_type=F32)) + br_ref[...]
    comb_ref[...] = _route(logits)


def _mix_ln_router(y_a_t, y_m, x2d, w_out, ln_g, ln_b, w_router, b_router, alpha, tm):
    n, d = x2d.shape
    s = y_a_t.shape[2]
    assert s % tm == 0
    per_seq = s // tm
    const = lambda i: (0, 0)
    return pl.pallas_call(
        functools.partial(_mix_body, alpha=alpha),
        grid=(n // tm,),
        in_specs=[pl.BlockSpec((None, D_A, tm), lambda i: (i // per_seq, 0, i % per_seq)),
                  pl.BlockSpec((tm, D_M), lambda i: (i, 0)),
                  pl.BlockSpec((tm, d), lambda i: (i, 0)),
                  pl.BlockSpec((D_A + D_M, d), const),
                  pl.BlockSpec((1, d), const),
                  pl.BlockSpec((1, d), const),
                  pl.BlockSpec((2, d, LANES), lambda i: (0, 0, 0)),
                  pl.BlockSpec((1, LANES), const)],
        out_specs=[pl.BlockSpec((tm, d), lambda i: (i, 0)),
                   pl.BlockSpec((tm, LANES), lambda i: (i, 0))],
        out_shape=[jax.ShapeDtypeStruct((n, d), F32), jax.ShapeDtypeStruct((n, LANES), F32)],
        compiler_params=_cparams(("arbitrary",), 32),
        name="mix_ln_router",
    )(y_a_t, y_m, x2d, w_out, ln_g, ln_b, w_router, b_router)


MOE_ROWS = TM_MOE // N_GROUPS + 32


def _moe_body(x1_ref, comb_ref, wg_ref, wu_ref, wd_ref, g_ref, b_ref, p_ref, wpg_ref, wpp_ref, y_ref,
              xb_ref, before_ref, row_ref, *, alpha, tm):
    g = pl.program_id(1)
    g_f = g.astype(F32)

    @pl.when((pl.program_id(0) == 0) & (g == 0))
    def _():
        r = lax.broadcasted_iota(I32, (tm, tm), 0)
        c = lax.broadcasted_iota(I32, (tm, tm), 1)
        before_ref[...] = (c < r).astype(BF16)

    @pl.when(g == 0)
    def _():
        d = x1_ref.shape[1]
        xb_ref[:, 0:d] = x1_ref[...].astype(BF16)
        y_ref[...] = jnp.zeros(y_ref.shape, F32)
        comb = comb_ref[...]
        for k, term in enumerate(_split3(comb)):
            xb_ref[:, d + k * LANES:d + (k + 1) * LANES] = term
        group_row = comb.T[COMB_GROUP_LANE:COMB_GROUP_LANE + 1, :]
        sub = lax.broadcasted_iota(I32, (SUBLANES, tm), 0)
        member_t = (sub.astype(F32) == group_row) & (sub < N_GROUPS)
        ahead_t = lax.dot_general(jnp.where(member_t, 1.0, 0.0).astype(BF16), before_ref[...], NT_DIMS,
                                  preferred_element_type=F32)
        rank_row = jnp.sum(jnp.where(member_t, ahead_t, 0.0), axis=0, keepdims=True)
        row_ref[0] = jnp.broadcast_to(group_row, (SUBLANES, tm))
        row_ref[1] = jnp.broadcast_to(rank_row, (SUBLANES, tm))

    in_group_row = row_ref[0, 0:1, :] == g_f
    n_tokens = jnp.sum(jnp.where(in_group_row, 1.0, 0.0))
    n_blocks = (n_tokens.astype(I32) + MOE_ROWS - 1) // MOE_ROWS
    slot = lax.broadcasted_iota(I32, (MOE_ROWS, tm), 0).astype(F32)
    comb_lane = lax.broadcasted_iota(I32, (MOE_ROWS, LANES), 1)

    def block(b, carry):
        base = (b * MOE_ROWS).astype(F32)
        gather = jnp.where(in_group_row & (row_ref[1, 0:1, :] - base == slot), 1.0, 0.0).astype(BF16)
        d = x1_ref.shape[1]
        picked = jnp.dot(gather, xb_ref[...], preferred_element_type=F32)
        xc = picked[:, 0:d].astype(BF16)
        comb_c = picked[:, d:d + LANES] + picked[:, d + LANES:d + 2 * LANES] + picked[:, d + 2 * LANES:d + 3 * LANES]
        yc = jnp.zeros((MOE_ROWS, d), F32)
        for e in range(EXPERTS_PER_GROUP):
            gate = jnp.dot(xc, wg_ref[e], preferred_element_type=F32)
            up = jnp.dot(xc, wu_ref[e], preferred_element_type=F32)
            weight = jnp.sum(jnp.where(comb_lane == g * EXPERTS_PER_GROUP + e, comb_c, 0.0), axis=1, keepdims=True)
            hidden = gate * jax.nn.sigmoid(gate) * up * weight
            yc = yc + jnp.dot(hidden.astype(BF16), wd_ref[e], preferred_element_type=F32)
        yc_hi = yc.astype(BF16)
        yc_lo = (yc - yc_hi.astype(F32)).astype(BF16)
        y_ref[...] += lax.dot_general(jnp.concatenate([gather, gather], axis=0),
                                      jnp.concatenate([yc_hi, yc_lo], axis=0), TN_DIMS, preferred_element_type=F32)
        return carry

    lax.fori_loop(0, n_blocks, block, 0)

    @pl.when(g == N_GROUPS - 1)
    def _():
        x2 = _layer_norm(alpha * x1_ref[...] + y_ref[...], g_ref[...], b_ref[...])
        gate = jax.nn.sigmoid(jnp.dot(x2.astype(BF16), wpg_ref[...], preferred_element_type=F32))
        proj = jnp.dot(p_ref[...].astype(BF16), wpp_ref[...], preferred_element_type=F32)
        y_ref[...] = x2 + gate * proj


def _moe_ln_ple(x1, comb, w_gate, w_up, w_down, ln_g, ln_b, p2d, w_ple_gate, w_ple_proj, alpha, tm):
    n, d = x1.shape
    dp = p2d.shape[1]
    const = lambda i, g: (0, 0)
    once = pl.Buffered(1)
    grouped = lambda w: w.reshape((N_GROUPS, EXPERTS_PER_GROUP) + w.shape[1:])
    return pl.pallas_call(
        functools.partial(_moe_body, alpha=alpha, tm=tm),
        grid=(n // tm, N_GROUPS),
        in_specs=[pl.BlockSpec((tm, d), lambda i, g: (i, 0), pipeline_mode=once),
                  pl.BlockSpec((tm, LANES), lambda i, g: (i, 0)),
                  pl.BlockSpec((None, EXPERTS_PER_GROUP, d, D_EXPERT), lambda i, g: (g, 0, 0, 0)),
                  pl.BlockSpec((None, EXPERTS_PER_GROUP, d, D_EXPERT), lambda i, g: (g, 0, 0, 0)),
                  pl.BlockSpec((None, EXPERTS_PER_GROUP, D_EXPERT, d), lambda i, g: (g, 0, 0, 0)),
                  pl.BlockSpec((1, d), const),
                  pl.BlockSpec((1, d), const),
                  pl.BlockSpec((tm, dp), lambda i, g: (i, 0), pipeline_mode=once),
                  pl.BlockSpec((d, d), const, pipeline_mode=once),
                  pl.BlockSpec((dp, d), const, pipeline_mode=once)],
        out_specs=pl.BlockSpec((tm, d), lambda i, g: (i, 0)),
        out_shape=jax.ShapeDtypeStruct((n, d), F32),
        scratch_shapes=[pltpu.VMEM((tm, d + 3 * LANES), BF16),
                        pltpu.VMEM((tm, tm), BF16),
                        pltpu.VMEM((2, SUBLANES, tm), F32)],
        compiler_params=_cparams(("arbitrary", "arbitrary"), 58),
        name="moe_ln_ple",
    )(x1, comb, grouped(w_gate), grouped(w_up), grouped(w_down), ln_g, ln_b, p2d, w_ple_gate, w_ple_proj)


def _tile(n, want):
    t = min(want, n)
    assert n % t == 0
    return t


def _layer(x, p_i, w_in, b_igate, b_fgate, conv_w, conv_b, gn_w, w_out, ln1_g, ln1_b,
           rg_w, rg_b, re_w, re_b, e_gate, e_up, e_down, ln2_g, ln2_b, ple_gate_w, ple_proj_w, alpha):
    b, s, d = x.shape
    n = b * s
    x2d = x.reshape(n, d)

    tk = _tile(s, TK_DSA)
    w_packed, w_vt = _pack_w_in(w_in)
    qk, v_t, q_idx, k4, gates, qk_m, v_m, o_m = _in_proj(x2d, w_packed, w_vt, tk)

    gates3 = gates.reshape(b, s, LANES)
    y_a = _dsa_attention(qk.reshape(2 * SLABS_PER_TENSOR, b, s, LANES),
                         v_t.reshape(SLABS_PER_TENSOR, b, s // tk, LANES, tk),
                         q_idx.reshape(b, s, N_IDX_HEADS * IDX_DIM), k4.reshape(b, s, LANES), gates3,
                         _tile(s, TQ_DSA), tk)

    gate_bias = jnp.concatenate([b_igate, b_fgate]).astype(F32)
    bias_slab = jnp.zeros((1, LANES), F32).at[0, G_I:G_I + 2 * N_HEADS_M].set(gate_bias)
    gates_t = jnp.swapaxes(gates3[:, :, G_I:G_I + 2 * N_HEADS_M], 1, 2)
    y_m = _mlstm(qk_m.reshape(b, s, 2 * D_M), v_m.reshape(b, s, D_M), o_m.reshape(b, s, D_M), gates3, gates_t,
                 conv_w, conv_b.reshape(1, -1), bias_slab, gate_bias.reshape(-1, 1), gn_w.reshape(1, -1),
                 _tile(s, CHUNK_MLSTM))

    w_router = jnp.concatenate([re_w, rg_w, jnp.zeros((d, LANES - N_EXPERTS - N_GROUPS), F32)], axis=1)
    w_router_hi = w_router.astype(BF16)
    w_router = jnp.stack([w_router_hi, (w_router - w_router_hi.astype(F32)).astype(BF16)])
    b_router = jnp.concatenate([re_b, rg_b, jnp.zeros((LANES - N_EXPERTS - N_GROUPS,), F32)]).reshape(1, LANES)
    x1, comb = _mix_ln_router(y_a, y_m.reshape(n, D_M), x2d, w_out.astype(BF16),
                              ln1_g.reshape(1, d), ln1_b.reshape(1, d), w_router, b_router, alpha, _tile(s, TM_MIX))

    out = _moe_ln_ple(x1, comb, e_gate.astype(BF16), e_up.astype(BF16), e_down.astype(BF16),
                      ln2_g.reshape(1, d), ln2_b.reshape(1, d), p_i.reshape(n, -1), ple_gate_w.astype(BF16),
                      ple_proj_w.astype(BF16), alpha, _tile(n, TM_MOE))
    return out.reshape(b, s, d)


def kernel(x, p, w_in, b_igate, b_fgate, conv_w, conv_b, gn_w, w_out, ln1_g, ln1_b, router_group_w, router_group_b,
           router_expert_w, router_expert_b, expert_w_gate, expert_w_up, expert_w_down, ln2_g, ln2_b, ple_gate_w,
           ple_proj_w):
    depth = w_in.shape[0]
    alpha = (2.0 * depth) ** 0.25
    for i in range(depth):
        x = _layer(x, p[i], w_in[i], b_igate[i], b_fgate[i], conv_w[i], conv_b[i], gn_w[i], w_out[i], ln1_g[i],
                   ln1_b[i], router_group_w[i], router_group_b[i], router_expert_w[i], router_expert_b[i],
                   expert_w_gate[i], expert_w_up[i], expert_w_down[i], ln2_g[i], ln2_b[i], ple_gate_w[i],
                   ple_proj_w[i], alpha)
    return x
```

```python
import functools

import numpy as np
import jax
import jax.numpy as jnp
from jax import lax
from jax.experimental import pallas as pl
from jax.experimental.pallas import tpu as pltpu

F32 = jnp.float32
BF16 = jnp.bfloat16
I32 = jnp.int32

N_HEADS_A = 8
HEAD_DIM_A = 64
D_A = N_HEADS_A * HEAD_DIM_A
N_IDX_HEADS = 8
IDX_DIM = 32
TOPK_MAX = 256
N_HEADS_M = 4
HEAD_DIM_M = 128
D_M = N_HEADS_M * HEAD_DIM_M
CONV_K = 4
N_GROUPS = 4
EXPERTS_PER_GROUP = 8
N_EXPERTS = N_GROUPS * EXPERTS_PER_GROUP
D_EXPERT = 256
LN_EPS = 1e-5
NEG_BIG = -1e30
IN_SPLITS = (D_A, D_A, D_A, N_IDX_HEADS * IDX_DIM, IDX_DIM, N_IDX_HEADS, D_M, D_M, D_M, D_M, N_HEADS_M, N_HEADS_M)

LANES = 128
SUBLANES = 8
MIB = 1024 * 1024

C_QK = 0
C_QIDX = C_QK + 2 * D_A
C_K4 = C_QIDX + N_IDX_HEADS * IDX_DIM
C_GATE = C_K4 + LANES
C_QKM = C_GATE + LANES
C_VM = C_QKM + 2 * D_M
C_OM = C_VM + D_M
C_END = C_OM + D_M
SLABS_PER_TENSOR = D_A // LANES
HEADS_PER_SLAB = LANES // HEAD_DIM_A
G_W = 0
G_I = N_IDX_HEADS
G_F = G_I + N_HEADS_M

INT_MIN = -2 ** 31

TQ_DSA = LANES
TK_DSA = 1024
TM_IN_PROJ = TK_DSA
CHUNK_MLSTM = 256
TM_MIX = 1024
TM_MOE = 1024

NT_DIMS = (((1,), (1,)), ((), ()))
TN_DIMS = (((0,), (0,)), ((), ()))


def _cparams(semantics, vmem_mib):
    return pltpu.CompilerParams(dimension_semantics=semantics, vmem_limit_bytes=int(vmem_mib * MIB))


def _layer_norm(r, g, b):
    mu = jnp.mean(r, axis=-1, keepdims=True)
    d = r - mu
    var = jnp.mean(d * d, axis=-1, keepdims=True)
    return d * lax.rsqrt(var + LN_EPS) * g + b


def _pack_w_in(w_in):
    d = w_in.shape[0]
    cuts = [int(c) for c in np.cumsum(IN_SPLITS)[:-1]]
    q_a, k_a, v_a, q_idx, k_idx, w_idx, q_m, k_m, v_m, o_m, i_m, f_m = jnp.split(w_in, cuts, axis=1)
    k4 = jnp.tile(k_idx, (1, LANES // IDX_DIM))
    gate = jnp.concatenate([w_idx, i_m, f_m, jnp.zeros((d, LANES - G_F - N_HEADS_M), w_in.dtype)], axis=1)
    packed = jnp.concatenate([q_a, k_a, q_idx, k4, gate, q_m, k_m, v_m, o_m], axis=1).astype(BF16)
    return packed, v_a.T.astype(BF16)


def _in_proj_body(x_ref, w_ref, wvt_ref, qk_ref, vt_ref, qi_ref, k4_ref, g_ref, qkm_ref, vm_ref, om_ref):
    xb = x_ref[...].astype(BF16)

    def proj(lo, hi):
        return jnp.dot(xb, w_ref[:, lo:hi], preferred_element_type=F32)

    qk = proj(C_QK, C_QIDX).astype(BF16)
    for c in range(2 * SLABS_PER_TENSOR):
        qk_ref[c] = qk[:, c * LANES:(c + 1) * LANES]
    v_t = lax.dot_general(wvt_ref[...], xb, NT_DIMS, preferred_element_type=F32).astype(BF16)
    for c in range(SLABS_PER_TENSOR):
        vt_ref[c, 0] = v_t[c * LANES:(c + 1) * LANES, :]
    qi_ref[...] = proj(C_QIDX, C_K4).astype(BF16)
    k4_ref[...] = proj(C_K4, C_GATE).astype(BF16)
    g_ref[...] = proj(C_GATE, C_QKM)
    qkm_ref[...] = proj(C_QKM, C_VM)
    vm_ref[...] = proj(C_VM, C_OM).astype(BF16)
    om_ref[...] = proj(C_OM, C_END)


def _in_proj(x2d, w_packed, w_vt, tm):
    n, d = x2d.shape
    widths = [(C_K4 - C_QIDX, BF16), (C_GATE - C_K4, BF16), (C_QKM - C_GATE, F32),
              (C_VM - C_QKM, F32), (C_OM - C_VM, BF16), (C_END - C_OM, F32)]
    return pl.pallas_call(
        _in_proj_body,
        grid=(n // tm,),
        in_specs=[pl.BlockSpec((tm, d), lambda i: (i, 0)),
                  pl.BlockSpec((d, C_END), lambda i: (0, 0)),
                  pl.BlockSpec((D_A, d), lambda i: (0, 0))],
        out_specs=([pl.BlockSpec((2 * SLABS_PER_TENSOR, tm, LANES), lambda i: (0, i, 0)),
                    pl.BlockSpec((SLABS_PER_TENSOR, 1, LANES, tm), lambda i: (0, i, 0, 0))]
                   + [pl.BlockSpec((tm, w), lambda i: (i, 0)) for w, _ in widths]),
        out_shape=([jax.ShapeDtypeStruct((2 * SLABS_PER_TENSOR, n, LANES), BF16),
                    jax.ShapeDtypeStruct((SLABS_PER_TENSOR, n // tm, LANES, tm), BF16)]
                   + [jax.ShapeDtypeStruct((n, w), dt) for w, dt in widths]),
        compiler_params=_cparams(("arbitrary",), 48),
        name="in_proj",
    )(x2d, w_packed, w_vt)


def _float_to_ordered_int(value):
    bits = lax.bitcast_convert_type(value, I32)
    return bits ^ ((bits >> 31) & jnp.int32(0x7FFFFFFF))


def _ordered_int_to_float(key):
    bits = key ^ ((key >> 31) & jnp.int32(0x7FFFFFFF))
    return lax.bitcast_convert_type(bits, F32)


SEARCH_FIXED_PASSES = 12
SEARCH_FREE_PASSES = 16
SEARCH_MAX_PASSES = SEARCH_FREE_PASSES + 33
ZERO_TIE_MARGIN = 128
CHAINS = 4
ATTEND_BLOCKS = 2
POS_RADIX = 64
POS_TERMS = 3
LOG2_E = float(np.log2(np.e))


def _dsa_body(qa_ref, ka_ref, vt_ref, qi_ref, k4_ref, g_ref, o_ref,
              sc_ref, qm_ref, qh_ref, pos_ref, mask_ref, acc_ref, m_ref, l_ref, *, seq, topk, tq, tk):
    t0 = pl.program_id(1) * tq
    n_kb = (t0 + tq + tk - 1) // tk
    groups = tk // (CHAINS * SUBLANES)
    shape4 = (groups, CHAINS, SUBLANES, tq)

    def chunks(x):
        return x.reshape(shape4)

    def fold(x):
        out = x[0]
        for c in range(1, CHAINS):
            out = out + x[c]
        return out

    def rows8(row):
        return jnp.broadcast_to(row, (SUBLANES, tq))

    key_off = ((lax.broadcasted_iota(I32, shape4, 0) * CHAINS + lax.broadcasted_iota(I32, shape4, 1)) * SUBLANES
               + lax.broadcasted_iota(I32, shape4, 2))
    q_pos = t0 + lax.broadcasted_iota(I32, shape4, 3)
    lane = lax.broadcasted_iota(I32, (tq, LANES), 1)

    for h in range(N_IDX_HEADS):
        per_slab = LANES // IDX_DIM
        slab = qi_ref[:, (h // per_slab) * LANES:(h // per_slab + 1) * LANES]
        lo = (h % per_slab) * IDX_DIM
        qm_ref[h * tq:(h + 1) * tq, :] = jnp.where((lane >= lo) & (lane < lo + IDX_DIM), slab, jnp.zeros_like(slab))
    for h in range(N_HEADS_A):
        p, hh = divmod(h, HEADS_PER_SLAB)
        slab = qa_ref[p].astype(F32) * (HEAD_DIM_A ** -0.5 * LOG2_E)
        in_head = (lane >= hh * HEAD_DIM_A) & (lane < (hh + 1) * HEAD_DIM_A)
        qh_ref[p, hh * tq:(hh + 1) * tq, 0:LANES] = jnp.where(in_head, slab, 0.0).astype(BF16)
    w_rows = g_ref[...].T[G_W:G_W + N_IDX_HEADS, :] * ((IDX_DIM * N_IDX_HEADS) ** -0.5)

    @pl.when((pl.program_id(0) == 0) & (pl.program_id(1) == 0))
    def _():
        for h in range(N_HEADS_A):
            p, hh = divmod(h, HEADS_PER_SLAB)
            slope = 2.0 ** (-8.0 * (h + 1) / N_HEADS_A)
            weight = jnp.where(lane < POS_TERMS, slope * POS_RADIX * LOG2_E,
                               jnp.where(lane < 2 * POS_TERMS, slope * LOG2_E, 0.0))
            terms = _split3(weight)
            feat = jnp.where(lane % POS_TERMS == 0, terms[0], jnp.where(lane % POS_TERMS == 1, terms[1], terms[2]))
            qh_ref[p, hh * tq:(hh + 1) * tq, LANES:2 * LANES] = feat
        pos_lane = lax.broadcasted_iota(I32, (tk, LANES), 1)
        pos_row = lax.broadcasted_iota(I32, (tk, LANES), 0)
        for j in range(seq // tk):
            pos = pos_row + j * tk
            pos_ref[j * tk:(j + 1) * tk, :] = jnp.where(
                pos_lane < POS_TERMS, pos // POS_RADIX,
                jnp.where(pos_lane < 2 * POS_TERMS, pos % POS_RADIX, 0)).astype(F32).astype(BF16)

    def score_blocks(n_blocks, first, step, carry):
        hi_part, lo_part, zero_part = carry
        blocks = [first + step * n_blocks + u for u in range(n_blocks)]
        starts = [pl.multiple_of(j * tk, tk) for j in blocks]
        logits = [lax.dot_general(k4_ref[pl.ds(k0, tk), :], qm_ref[...], NT_DIMS,
                                  preferred_element_type=F32) for k0 in starts]
        for j, k0, lg in zip(blocks, starts, logits):
            score = jnp.zeros(shape4, F32)
            for h in range(N_IDX_HEADS):
                score = score + rows8(w_rows[h:h + 1, :]) * jnp.maximum(chunks(lg[:, h * tq:(h + 1) * tq]), 0.0)
            causal = k0 + key_off <= q_pos
            kept = jnp.where(causal, score, NEG_BIG)
            sc_ref[j] = kept
            hi_part = jnp.maximum(hi_part, jnp.max(kept, axis=0))
            lo_part = jnp.minimum(lo_part, jnp.min(score, axis=0))
            zero_part = zero_part + jnp.sum(jnp.where(kept >= 0.0, 1.0, 0.0), axis=0)
        return hi_part, lo_part, zero_part

    part_shape = (CHAINS, SUBLANES, tq)
    n_pairs = n_kb // ATTEND_BLOCKS
    parts = (jnp.full(part_shape, -jnp.inf, F32), jnp.full(part_shape, jnp.inf, F32), jnp.zeros(part_shape, F32))
    parts = lax.fori_loop(0, n_pairs, functools.partial(score_blocks, ATTEND_BLOCKS, 0), parts)
    hi_part, lo_part, zero_part = lax.fori_loop(
        0, n_kb - ATTEND_BLOCKS * n_pairs, functools.partial(score_blocks, 1, ATTEND_BLOCKS * n_pairs), parts)
    row_max = jnp.max(jnp.max(hi_part, axis=0), axis=0, keepdims=True)
    row_min = jnp.min(jnp.min(lo_part, axis=0), axis=0, keepdims=True)

    n_tail = (seq - n_kb * tk).astype(F32)

    def count(thr, strict):
        thr8 = rows8(thr)

        def body(j, acc):
            s = sc_ref[j]
            return acc + jnp.sum(jnp.where((s > thr8) if strict else (s >= thr8), 1.0, 0.0), axis=0)

        acc = lax.fori_loop(0, n_kb, body, jnp.zeros(part_shape, F32))
        tail_hit = (NEG_BIG > thr) if strict else (NEG_BIG >= thr)
        return jnp.sum(fold(acc), axis=0, keepdims=True) + jnp.where(tail_hit, n_tail, 0.0)

    n_causal = (t0 + 1 + lax.broadcasted_iota(I32, (1, tq), 1)).astype(F32)
    few = n_causal < topk
    lo0 = jnp.where(few, _float_to_ordered_int(jnp.full((1, tq), NEG_BIG, F32)), _float_to_ordered_int(row_min))
    cnt_lo0 = jnp.where(few, float(seq), n_causal)
    hi0 = _float_to_ordered_int(row_max) + 1
    log_k = float(np.log(topk))

    def log_excess(cnt):
        return jnp.log(jnp.maximum(cnt, 0.5)) - log_k

    def probe(it, bracket):
        lo, hi, cnt_lo, f_lo, f_hi, last, done_i = bracket
        done = done_i != 0
        it_v = jnp.zeros((1, tq), I32) + it
        lo_v = _ordered_int_to_float(lo)
        hi_v = _ordered_int_to_float(hi)
        guess = _float_to_ordered_int(lo_v + (f_lo / (f_lo - f_hi)) * (hi_v - lo_v))
        guess = jnp.where((it_v == 1) & (lo == 0) & (cnt_lo < topk + ZERO_TIE_MARGIN), 1, guess)
        middle = lo + lax.shift_right_logical(hi - lo, jnp.ones((1, tq), I32))
        usable = (guess > lo) & (guess < hi) & (it_v < SEARCH_FREE_PASSES)
        cand = jnp.where(usable, guess, middle)
        cnt = count(_ordered_int_to_float(cand), strict=False)
        f_c = log_excess(cnt)
        take = (cnt >= topk) & ~done
        drop = (cnt < topk) & ~done
        f_hi = jnp.where(take, jnp.where(last == 1, 0.5 * f_hi, f_hi), jnp.where(drop, f_c, f_hi))
        f_lo = jnp.where(drop, jnp.where(last == -1, 0.5 * f_lo, f_lo), jnp.where(take, f_c, f_lo))
        last = jnp.where(take, 1, jnp.where(drop, -1, last))
        lo = jnp.where(take, cand, lo)
        cnt_lo = jnp.where(take, cnt, cnt_lo)
        hi = jnp.where(drop, cand, hi)
        done = done | (cnt_lo == topk) | ((hi - lo) == 1)
        return lo, hi, cnt_lo, f_lo, f_hi, last, done.astype(I32)

    def search_step(state):
        bracket = probe(state[0], state[3:])
        n_active = jnp.sum(jnp.where(bracket[6] != 0, 0.0, 1.0))
        return (state[0] + 1, n_active, jnp.max(bracket[2])) + bracket

    def search_on(state):
        return (state[0] < SEARCH_MAX_PASSES) & (state[1] > 0.0)

    cnt_zero = jnp.sum(fold(zero_part), axis=0, keepdims=True)
    inside = (lo0 < 0) & (hi0 > 0) & ~few
    take0 = inside & (cnt_zero >= topk)
    drop0 = inside & (cnt_zero < topk)
    lo1 = jnp.where(take0, 0, lo0)
    hi1 = jnp.where(drop0, 0, hi0)
    cnt_lo1 = jnp.where(take0, cnt_zero, cnt_lo0)
    done1 = few | (cnt_lo1 == topk) | ((hi1 - lo1) == 1)
    bracket = (lo1, hi1, cnt_lo1, log_excess(cnt_lo1), log_excess(jnp.where(drop0, cnt_zero, 0.0)),
               jnp.where(take0, 1, jnp.where(drop0, -1, 0)), done1.astype(I32))
    bracket = lax.fori_loop(1, 1 + SEARCH_FIXED_PASSES, probe, bracket)
    state = search_step((jnp.int32(1 + SEARCH_FIXED_PASSES), jnp.float32(0.0), jnp.float32(0.0)) + bracket)
    state = lax.while_loop(search_on, search_step, state)
    max_cnt_thr = state[2]
    thr = _ordered_int_to_float(state[3])
    thr8 = rows8(thr)

    @pl.when(max_cnt_thr > topk)
    def _():
        need = topk - count(thr, strict=True)
        tri_r = lax.broadcasted_iota(I32, (tk, tk), 0)
        tri_c = lax.broadcasted_iota(I32, (tk, tk), 1)
        earlier = (tri_c < tri_r).astype(BF16)

        def drop_late_ties(j, seen):
            s = sc_ref[j]
            eq = s == thr8
            eq_f = jnp.where(eq, 1.0, 0.0)
            rank = seen + jnp.dot(earlier, eq_f.reshape(tk, tq).astype(BF16), preferred_element_type=F32)
            sc_ref[j] = jnp.where(eq & (chunks(rank) >= rows8(need)), -jnp.inf, s)
            return seen + jnp.sum(fold(jnp.sum(eq_f, axis=0)), axis=0, keepdims=True)

        lax.fori_loop(0, n_kb, drop_late_ties, jnp.zeros((1, tq), F32))

    m_ref[...] = jnp.full(m_ref.shape, NEG_BIG, F32)
    l_ref[...] = jnp.zeros(l_ref.shape, F32)
    acc_ref[...] = jnp.zeros(acc_ref.shape, F32)

    def fold_max(x):
        out = x[0]
        for c in range(1, CHAINS):
            out = jnp.maximum(out, x[c])
        return out

    def attend(n_blocks, first, step, carry):
        blocks = [first + step * n_blocks + u for u in range(n_blocks)]
        starts = [pl.multiple_of(j * tk, tk) for j in blocks]
        for u, (j, k0) in enumerate(zip(blocks, starts)):
            mask_ref[u] = jnp.where((sc_ref[j] >= thr8) & (k0 + key_off <= q_pos), 0.0, -jnp.inf)

        scores = {}
        for p in range(SLABS_PER_TENSOR):
            for u, k0 in enumerate(starts):
                keys = jnp.concatenate([ka_ref[p, pl.ds(k0, tk), :], pos_ref[pl.ds(k0, tk), :]], axis=1)
                scores[p, u] = lax.dot_general(keys, qh_ref[p], NT_DIMS, preferred_element_type=F32)
        weights = {}
        for p in range(SLABS_PER_TENSOR):
            for hh in range(HEADS_PER_SLAB):
                s = [chunks(scores[p, u][:, hh * tq:(hh + 1) * tq]) + mask_ref[u] for u in range(n_blocks)]
                m_old = m_ref[p, hh]
                part = jnp.max(s[0], axis=0)
                for u in range(1, n_blocks):
                    part = jnp.maximum(part, jnp.max(s[u], axis=0))
                m_new = jnp.maximum(m_old, rows8(jnp.max(fold_max(part), axis=0, keepdims=True)))
                alpha = jnp.exp2(m_old - m_new)
                m_ref[p, hh] = m_new
                weights[p, hh] = (alpha, [jnp.exp2(su - m_new).reshape(tk, tq).astype(BF16) for su in s])
        ones_rows = jnp.ones((SUBLANES, tk), BF16)
        for p in range(SLABS_PER_TENSOR):
            for hh in range(HEADS_PER_SLAB):
                rows = slice(hh * HEAD_DIM_A, (hh + 1) * HEAD_DIM_A)
                alpha, probs = weights[p, hh]
                pv = jnp.zeros((HEAD_DIM_A + SUBLANES, tq), F32)
                for u, j in enumerate(blocks):
                    values = jnp.concatenate([vt_ref[p, j, rows, :], ones_rows], axis=0)
                    pv = pv + jnp.dot(values, probs[u], preferred_element_type=F32)
                old = acc_ref[p, rows, :].reshape(HEAD_DIM_A // SUBLANES, SUBLANES, tq)
                acc_ref[p, rows, :] = (alpha * old).reshape(HEAD_DIM_A, tq) + pv[0:HEAD_DIM_A]
                l_ref[p, hh] = alpha * l_ref[p, hh] + pv[HEAD_DIM_A:HEAD_DIM_A + SUBLANES]
        return carry

    n_pairs = n_kb // ATTEND_BLOCKS
    lax.fori_loop(0, n_pairs, functools.partial(attend, ATTEND_BLOCKS, 0), 0)
    lax.fori_loop(0, n_kb - ATTEND_BLOCKS * n_pairs, functools.partial(attend, 1, ATTEND_BLOCKS * n_pairs), 0)

    for p in range(SLABS_PER_TENSOR):
        for hh in range(HEADS_PER_SLAB):
            rows = slice(hh * HEAD_DIM_A, (hh + 1) * HEAD_DIM_A)
            acc = acc_ref[p, rows, :].reshape(HEAD_DIM_A // SUBLANES, SUBLANES, tq)
            out_rows = slice(p * LANES + hh * HEAD_DIM_A, p * LANES + (hh + 1) * HEAD_DIM_A)
            o_ref[out_rows, :] = (acc / l_ref[p, hh]).reshape(HEAD_DIM_A, tq).astype(BF16)


def _dsa_attention(qk, v_t, q_idx, k4, gates, tq, tk):
    _, b, s, _ = qk.shape
    assert tq == LANES and s // POS_RADIX <= 256
    topk = min(TOPK_MAX, s // 4)
    once = pl.Buffered(1)
    groups = tk // (CHAINS * SUBLANES)
    body = functools.partial(_dsa_body, seq=s, topk=topk, tq=tq, tk=tk)
    return pl.pallas_call(
        body,
        grid=(b, s // tq),
        in_specs=[pl.BlockSpec((SLABS_PER_TENSOR, None, tq, LANES), lambda bi, qi: (0, bi, qi, 0)),
                  pl.BlockSpec((SLABS_PER_TENSOR, None, s, LANES), lambda bi, qi: (1, bi, 0, 0), pipeline_mode=once),
                  pl.BlockSpec((SLABS_PER_TENSOR, None, s // tk, LANES, tk), lambda bi, qi: (0, bi, 0, 0, 0),
                               pipeline_mode=once),
                  pl.BlockSpec((None, tq, N_IDX_HEADS * IDX_DIM), lambda bi, qi: (bi, qi, 0)),
                  pl.BlockSpec((None, s, LANES), lambda bi, qi: (bi, 0, 0), pipeline_mode=once),
                  pl.BlockSpec((None, tq, LANES), lambda bi, qi: (bi, qi, 0))],
        out_specs=pl.BlockSpec((None, D_A, tq), lambda bi, qi: (bi, 0, qi)),
        out_shape=jax.ShapeDtypeStruct((b, D_A, s), BF16),
        scratch_shapes=[pltpu.VMEM((s // tk, groups, CHAINS, SUBLANES, tq), F32),
                        pltpu.VMEM((N_IDX_HEADS * tq, LANES), BF16),
                        pltpu.VMEM((SLABS_PER_TENSOR, HEADS_PER_SLAB * tq, 2 * LANES), BF16),
                        pltpu.VMEM((s, LANES), BF16),
                        pltpu.VMEM((ATTEND_BLOCKS, groups, CHAINS, SUBLANES, tq), F32),
                        pltpu.VMEM((SLABS_PER_TENSOR, LANES, tq), F32),
                        pltpu.VMEM((SLABS_PER_TENSOR, HEADS_PER_SLAB, SUBLANES, tq), F32),
                        pltpu.VMEM((SLABS_PER_TENSOR, HEADS_PER_SLAB, SUBLANES, tq), F32)],
        compiler_params=_cparams(("arbitrary", "arbitrary"), 48),
        name="dsa_attention",
    )(qk, qk, v_t, q_idx, k4, gates)


def _log_sigmoid(x):
    return jnp.minimum(x, 0.0) - jnp.log(1.0 + jnp.exp(-jnp.abs(x)))


def _split3(a):
    hi = a.astype(BF16)
    r1 = a - hi.astype(F32)
    mid = r1.astype(BF16)
    lo = (r1 - mid.astype(F32)).astype(BF16)
    return hi, mid, lo


def _mlstm_body(qk_ref, v_ref, o_ref, g_ref, gt_ref, cw_ref, cb_ref, bg_ref, bgt_ref, gn_ref, y_ref,
                xe_ref, c_ref, m_ref, *, chunk, batch):
    @pl.when(pl.program_id(0) == 0)
    def _():
        xe_ref[:, 0:SUBLANES, :] = jnp.zeros((batch, SUBLANES, 2 * D_M), F32)
        c_ref[...] = jnp.zeros(c_ref.shape, F32)
        m_ref[...] = jnp.zeros(m_ref.shape, F32)

    gates = [_mlstm_gates(qk_ref.at[bi], g_ref.at[bi], gt_ref.at[bi], cw_ref, cb_ref, bg_ref, bgt_ref,
                          xe_ref.at[bi], chunk) for bi in range(batch)]
    chains = [(h, bi) for h in range(N_HEADS_M) for bi in range(batch)]
    scores = [_mlstm_head_scores(h, gates[bi], v_ref.at[bi], c_ref.at[bi], m_ref.at[bi], chunk) for h, bi in chains]
    for (h, bi), sc in zip(chains, scores):
        _mlstm_head(h, gates[bi], sc, o_ref.at[bi], gn_ref, y_ref.at[bi], chunk)
    for (h, bi), sc in zip(chains, scores):
        _mlstm_head_state(h, gates[bi], sc, c_ref.at[bi], m_ref.at[bi], chunk)


def _mlstm_gates(qk_ref, g_ref, gt_ref, cw_ref, cb_ref, bg_ref, bgt_ref, xe_ref, chunk):
    xe_ref[SUBLANES:SUBLANES + chunk, :] = qk_ref[...]
    conv = cb_ref[...]
    for j in range(CONV_K):
        start = SUBLANES - (CONV_K - 1) + j
        conv = conv + cw_ref[j:j + 1, :] * xe_ref[start:start + chunk, :]
    qk = conv * jax.nn.sigmoid(conv)
    xe_ref[0:SUBLANES, :] = xe_ref[chunk:chunk + SUBLANES, :]

    gb = g_ref[...] + bg_ref[...]
    gbt = gt_ref[...] + bgt_ref[...]
    r = lax.broadcasted_iota(I32, (chunk, chunk), 0)
    c = lax.broadcasted_iota(I32, (chunk, chunk), 1)
    causal = r >= c
    incl = causal.astype(BF16)
    incl_t = (r <= c).astype(BF16)
    cum = jnp.zeros((chunk, LANES), F32)
    for term in _split3(_log_sigmoid(gb)):
        cum = cum + jnp.dot(incl, term, preferred_element_type=F32)
    cum_t = jnp.zeros((2 * N_HEADS_M, chunk), F32)
    for term in _split3(_log_sigmoid(gbt)):
        cum_t = cum_t + jnp.dot(term, incl_t, preferred_element_type=F32)

    return qk, gb, gbt, cum, cum_t, causal


def _mlstm_head_scores(h, gates, v_ref, c_ref, m_ref, chunk):
    qk = gates[0]
    lane = lax.broadcasted_iota(I32, (chunk, LANES), 1)
    ones_col = jnp.where(lane == 0, 1.0, 0.0).astype(BF16)
    sl = slice(h * HEAD_DIM_M, (h + 1) * HEAD_DIM_M)
    q_h = (qk[:, sl] * (HEAD_DIM_M ** -0.5)).astype(BF16)
    k_f = qk[:, D_M + h * HEAD_DIM_M:D_M + (h + 1) * HEAD_DIM_M]
    v_aug = jnp.concatenate([v_ref[:, sl], ones_col], axis=1)
    m_prev = m_ref[h:h + 1, 0:1]
    c_aug = c_ref[h]
    qk_t = lax.dot_general(q_h, k_f.astype(BF16), NT_DIMS, preferred_element_type=F32)
    carried = jnp.dot(q_h, c_aug.astype(BF16), preferred_element_type=F32)
    return k_f, v_aug, m_prev, c_aug, qk_t, carried


def _mlstm_head(h, gates, scores, o_ref, gn_ref, y_ref, chunk):
    _, _, gbt, cum, cum_t, causal = gates
    _, v_aug, m_prev, _, qk_t, carried = scores
    sl = slice(h * HEAD_DIM_M, (h + 1) * HEAD_DIM_M)
    b_col = cum[:, G_F + h:G_F + h + 1]
    b_row = cum_t[N_HEADS_M + h:N_HEADS_M + h + 1, :]
    i_row = gbt[h:h + 1, :]

    d_log = jnp.where(causal, b_col - b_row + i_row, -jnp.inf)
    inter = b_col + m_prev
    m_t = jnp.maximum(inter, jnp.max(d_log, axis=1, keepdims=True))
    w_intra = qk_t * jnp.exp(d_log - m_t)
    w_inter = jnp.exp(inter - m_t)
    intra = jnp.dot(w_intra.astype(BF16), v_aug[:, 0:HEAD_DIM_M], preferred_element_type=F32)
    num = intra + w_inter * carried[:, 0:HEAD_DIM_M]
    den = jnp.sum(w_intra, axis=1, keepdims=True) + w_inter * carried[:, HEAD_DIM_M:HEAD_DIM_M + 1]
    hid = num / jnp.maximum(jnp.abs(den), jnp.exp(-m_t))

    mu = jnp.mean(hid, axis=1, keepdims=True)
    dev = hid - mu
    var = jnp.mean(dev * dev, axis=1, keepdims=True)
    normed = dev * lax.rsqrt(var + LN_EPS) * gn_ref[:, sl]
    y_ref[:, sl] = (jax.nn.sigmoid(o_ref[:, sl]) * normed).astype(BF16)


def _mlstm_head_state(h, gates, scores, c_ref, m_ref, chunk):
    _, gb, _, cum, _, _ = gates
    k_f, v_aug, m_prev, c_aug, _, _ = scores
    b_col = cum[:, G_F + h:G_F + h + 1]
    i_col = gb[:, G_I + h:G_I + h + 1]
    b_last = b_col[chunk - 1:chunk, :]
    w_log = b_last - b_col + i_col
    m_new = jnp.maximum(b_last + m_prev, jnp.max(w_log, axis=0, keepdims=True))
    decay = jnp.exp(b_last + m_prev - m_new)
    kw_t = (k_f * jnp.exp(w_log - m_new)).T.astype(BF16)
    c_ref[h] = decay * c_aug + jnp.dot(kw_t, v_aug, preferred_element_type=F32)
    m_ref[h:h + 1, :] = jnp.broadcast_to(m_new, (1, LANES))


def _mlstm(qk_m, v_m, o_m, gates, gates_t, conv_w, conv_b, bias_slab, bias_col, gn_w, chunk):
    b, s, _ = qk_m.shape
    body = functools.partial(_mlstm_body, chunk=chunk, batch=b)
    const = lambda ci: (0, 0)
    return pl.pallas_call(
        body,
        grid=(s // chunk,),
        in_specs=[pl.BlockSpec((b, chunk, 2 * D_M), lambda ci: (0, ci, 0)),
                  pl.BlockSpec((b, chunk, D_M), lambda ci: (0, ci, 0)),
                  pl.BlockSpec((b, chunk, D_M), lambda ci: (0, ci, 0)),
                  pl.BlockSpec((b, chunk, LANES), lambda ci: (0, ci, 0)),
                  pl.BlockSpec((b, 2 * N_HEADS_M, chunk), lambda ci: (0, 0, ci)),
                  pl.BlockSpec((CONV_K, 2 * D_M), const),
                  pl.BlockSpec((1, 2 * D_M), const),
                  pl.BlockSpec((1, LANES), const),
                  pl.BlockSpec((2 * N_HEADS_M, 1), const),
                  pl.BlockSpec((1, D_M), const)],
        out_specs=pl.BlockSpec((b, chunk, D_M), lambda ci: (0, ci, 0)),
        out_shape=jax.ShapeDtypeStruct((b, s, D_M), BF16),
        scratch_shapes=[pltpu.VMEM((b, chunk + SUBLANES, 2 * D_M), F32),
                        pltpu.VMEM((b, N_HEADS_M, HEAD_DIM_M, 2 * LANES), F32),
                        pltpu.VMEM((b, SUBLANES, LANES), F32)],
        compiler_params=_cparams(("arbitrary",), 32),
        name="mlstm",
    )(qk_m, v_m, o_m, gates, gates_t, conv_w, conv_b, bias_slab, bias_col, gn_w)


COMB_GROUP_LANE = N_EXPERTS


def _route(logits):
    lane = lax.broadcasted_iota(I32, logits.shape, 1).astype(F32)
    g_mask = (lane >= N_EXPERTS) & (lane < N_EXPERTS + N_GROUPS)
    g_logit = jnp.where(g_mask, logits, -jnp.inf)
    g_exp = jnp.exp(g_logit - jnp.max(g_logit, axis=1, keepdims=True))
    g_prob = g_exp / jnp.sum(g_exp, axis=1, keepdims=True)
    g_p = jnp.max(g_prob, axis=1, keepdims=True)
    g_sel = jnp.min(jnp.where(g_mask & (g_prob == g_p), lane, float(LANES)), axis=1, keepdims=True) - N_EXPERTS
    e_mask = (lane >= g_sel * EXPERTS_PER_GROUP) & (lane < (g_sel + 1.0) * EXPERTS_PER_GROUP)
    e_logit = jnp.where(e_mask, logits, -jnp.inf)
    e_exp = jnp.exp(e_logit - jnp.max(e_logit, axis=1, keepdims=True))
    e_prob = e_exp / jnp.sum(e_exp, axis=1, keepdims=True)
    p1 = jnp.max(jnp.where(e_mask, e_prob, -1.0), axis=1, keepdims=True)
    i1 = jnp.min(jnp.where(e_mask & (e_prob == p1), lane, float(LANES)), axis=1, keepdims=True)
    rest = e_mask & (lane != i1)
    p2 = jnp.max(jnp.where(rest, e_prob, -1.0), axis=1, keepdims=True)
    i2 = jnp.min(jnp.where(rest & (e_prob == p2), lane, float(LANES)), axis=1, keepdims=True)
    total = p1 + p2
    comb = (jnp.where(lane == i1, p1 / total, 0.0) + jnp.where(lane == i2, p2 / total, 0.0)) * g_p
    return comb + jnp.where(lane == COMB_GROUP_LANE, g_sel, 0.0)


def _mix_body(ya_ref, ym_ref, x_ref, wo_ref, g_ref, b_ref, wr_ref, br_ref, x1_ref, comb_ref, *, alpha):
    mix = (lax.dot_general(ya_ref[...], wo_ref[0:D_A, :], TN_DIMS, preferred_element_type=F32)
           + jnp.dot(ym_ref[...], wo_ref[D_A:D_A + D_M, :], preferred_element_type=F32))
    x1 = _layer_norm(alpha * x_ref[...] + mix, g_ref[...], b_ref[...])
    x1_ref[...] = x1
    x_hi = x1.astype(BF16)
    x_lo = (x1 - x_hi.astype(F32)).astype(BF16)
    w_hi = wr_ref[0]
    w_lo = wr_ref[1]
    logits = (jnp.dot(x_hi, w_hi, preferred_element_type=F32) + jnp.dot(x_lo, w_hi, preferred_element_type=F32)
              + jnp.dot(x_hi, w_lo, preferred_element_type=F32)) + br_ref[...]
    comb_ref[...] = _route(logits)


def _mix_ln_router(y_a_t, y_m, x2d, w_out, ln_g, ln_b, w_router, b_router, alpha, tm):
    n, d = x2d.shape
    s = y_a_t.shape[2]
    assert s % tm == 0
    per_seq = s // tm
    const = lambda i: (0, 0)
    return pl.pallas_call(
        functools.partial(_mix_body, alpha=alpha),
        grid=(n // tm,),
        in_specs=[pl.BlockSpec((None, D_A, tm), lambda i: (i // per_seq, 0, i % per_seq)),
                  pl.BlockSpec((tm, D_M), lambda i: (i, 0)),
                  pl.BlockSpec((tm, d), lambda i: (i, 0)),
                  pl.BlockSpec((D_A + D_M, d), const),
                  pl.BlockSpec((1, d), const),
                  pl.BlockSpec((1, d), const),
                  pl.BlockSpec((2, d, LANES), lambda i: (0, 0, 0)),
                  pl.BlockSpec((1, LANES), const)],
        out_specs=[pl.BlockSpec((tm, d), lambda i: (i, 0)),
                   pl.BlockSpec((tm, LANES), lambda i: (i, 0))],
        out_shape=[jax.ShapeDtypeStruct((n, d), F32), jax.ShapeDtypeStruct((n, LANES), F32)],
        compiler_params=_cparams(("arbitrary",), 32),
        name="mix_ln_router",
    )(y_a_t, y_m, x2d, w_out, ln_g, ln_b, w_router, b_router)


MOE_ROWS = TM_MOE // N_GROUPS + 32


def _moe_body(x1_ref, comb_ref, wg_ref, wu_ref, wd_ref, g_ref, b_ref, p_ref, wpg_ref, wpp_ref, y_ref,
              xb_ref, before_ref, row_ref, *, alpha, tm):
    g = pl.program_id(1)
    g_f = g.astype(F32)

    @pl.when((pl.program_id(0) == 0) & (g == 0))
    def _():
        r = lax.broadcasted_iota(I32, (tm, tm), 0)
        c = lax.broadcasted_iota(I32, (tm, tm), 1)
        before_ref[...] = (c < r).astype(BF16)

    @pl.when(g == 0)
    def _():
        d = x1_ref.shape[1]
        xb_ref[:, 0:d] = x1_ref[...].astype(BF16)
        y_ref[...] = jnp.zeros(y_ref.shape, F32)
        comb = comb_ref[...]
        for k, term in enumerate(_split3(comb)):
            xb_ref[:, d + k * LANES:d + (k + 1) * LANES] = term
        group_row = comb.T[COMB_GROUP_LANE:COMB_GROUP_LANE + 1, :]
        sub = lax.broadcasted_iota(I32, (SUBLANES, tm), 0)
        member_t = (sub.astype(F32) == group_row) & (sub < N_GROUPS)
        ahead_t = lax.dot_general(jnp.where(member_t, 1.0, 0.0).astype(BF16), before_ref[...], NT_DIMS,
                                  preferred_element_type=F32)
        rank_row = jnp.sum(jnp.where(member_t, ahead_t, 0.0), axis=0, keepdims=True)
        row_ref[0] = jnp.broadcast_to(group_row, (SUBLANES, tm))
        row_ref[1] = jnp.broadcast_to(rank_row, (SUBLANES, tm))

    in_group_row = row_ref[0, 0:1, :] == g_f
    n_tokens = jnp.sum(jnp.where(in_group_row, 1.0, 0.0))
    n_blocks = (n_tokens.astype(I32) + MOE_ROWS - 1) // MOE_ROWS
    slot = lax.broadcasted_iota(I32, (MOE_ROWS, tm), 0).astype(F32)
    comb_lane = lax.broadcasted_iota(I32, (MOE_ROWS, LANES), 1)

    def block(b, carry):
        base = (b * MOE_ROWS).astype(F32)
        gather = jnp.where(in_group_row & (row_ref[1, 0:1, :] - base == slot), 1.0, 0.0).astype(BF16)
        d = x1_ref.shape[1]
        picked = jnp.dot(gather, xb_ref[...], preferred_element_type=F32)
        xc = picked[:, 0:d].astype(BF16)
        comb_c = picked[:, d:d + LANES] + picked[:, d + LANES:d + 2 * LANES] + picked[:, d + 2 * LANES:d + 3 * LANES]
        yc = jnp.zeros((MOE_ROWS, d), F32)
        for e in range(EXPERTS_PER_GROUP):
            gate = jnp.dot(xc, wg_ref[e], preferred_element_type=F32)
            up = jnp.dot(xc, wu_ref[e], preferred_element_type=F32)
            weight = jnp.sum(jnp.where(comb_lane == g * EXPERTS_PER_GROUP + e, comb_c, 0.0), axis=1, keepdims=True)
            hidden = gate * jax.nn.sigmoid(gate) * up * weight
            yc = yc + jnp.dot(hidden.astype(BF16), wd_ref[e], preferred_element_type=F32)
        yc_hi = yc.astype(BF16)
        yc_lo = (yc - yc_hi.astype(F32)).astype(BF16)
        y_ref[...] += lax.dot_general(jnp.concatenate([gather, gather], axis=0),
                                      jnp.concatenate([yc_hi, yc_lo], axis=0), TN_DIMS, preferred_element_type=F32)
        return carry

    lax.fori_loop(0, n_blocks, block, 0)

    @pl.when(g == N_GROUPS - 1)
    def _():
        x2 = _layer_norm(alpha * x1_ref[...] + y_ref[...], g_ref[...], b_ref[...])
        gate = jax.nn.sigmoid(jnp.dot(x2.astype(BF16), wpg_ref[...], preferred_element_type=F32))
        proj = jnp.dot(p_ref[...].astype(BF16), wpp_ref[...], preferred_element_type=F32)
        y_ref[...] = x2 + gate * proj


def _moe_ln_ple(x1, comb, w_gate, w_up, w_down, ln_g, ln_b, p2d, w_ple_gate, w_ple_proj, alpha, tm):
    n, d = x1.shape
    dp = p2d.shape[1]
    const = lambda i, g: (0, 0)
    once = pl.Buffered(1)
    grouped = lambda w: w.reshape((N_GROUPS, EXPERTS_PER_GROUP) + w.shape[1:])
    return pl.pallas_call(
        functools.partial(_moe_body, alpha=alpha, tm=tm),
        grid=(n // tm, N_GROUPS),
        in_specs=[pl.BlockSpec((tm, d), lambda i, g: (i, 0), pipeline_mode=once),
                  pl.BlockSpec((tm, LANES), lambda i, g: (i, 0)),
                  pl.BlockSpec((None, EXPERTS_PER_GROUP, d, D_EXPERT), lambda i, g: (g, 0, 0, 0)),
                  pl.BlockSpec((None, EXPERTS_PER_GROUP, d, D_EXPERT), lambda i, g: (g, 0, 0, 0)),
                  pl.BlockSpec((None, EXPERTS_PER_GROUP, D_EXPERT, d), lambda i, g: (g, 0, 0, 0)),
                  pl.BlockSpec((1, d), const),
                  pl.BlockSpec((1, d), const),
                  pl.BlockSpec((tm, dp), lambda i, g: (i, 0), pipeline_mode=once),
                  pl.BlockSpec((d, d), const, pipeline_mode=once),
                  pl.BlockSpec((dp, d), const, pipeline_mode=once)],
        out_specs=pl.BlockSpec((tm, d), lambda i, g: (i, 0)),
        out_shape=jax.ShapeDtypeStruct((n, d), F32),
        scratch_shapes=[pltpu.VMEM((tm, d + 3 * LANES), BF16),
                        pltpu.VMEM((tm, tm), BF16),
                        pltpu.VMEM((2, SUBLANES, tm), F32)],
        compiler_params=_cparams(("arbitrary", "arbitrary"), 58),
        name="moe_ln_ple",
    )(x1, comb, grouped(w_gate), grouped(w_up), grouped(w_down), ln_g, ln_b, p2d, w_ple_gate, w_ple_proj)


def _tile(n, want):
    t = min(want, n)
    assert n % t == 0
    return t


def _layer(x, p_i, w_in, b_igate, b_fgate, conv_w, conv_b, gn_w, w_out, ln1_g, ln1_b,
           rg_w, rg_b, re_w, re_b, e_gate, e_up, e_down, ln2_g, ln2_b, ple_gate_w, ple_proj_w, alpha):
    b, s, d = x.shape
    n = b * s
    x2d = x.reshape(n, d)

    tk = _tile(s, TK_DSA)
    w_packed, w_vt = _pack_w_in(w_in)
    qk, v_t, q_idx, k4, gates, qk_m, v_m, o_m = _in_proj(x2d, w_packed, w_vt, tk)

    gates3 = gates.reshape(b, s, LANES)
    y_a = _dsa_attention(qk.reshape(2 * SLABS_PER_TENSOR, b, s, LANES),
                         v_t.reshape(SLABS_PER_TENSOR, b, s // tk, LANES, tk),
                         q_idx.reshape(b, s, N_IDX_HEADS * IDX_DIM), k4.reshape(b, s, LANES), gates3,
                         _tile(s, TQ_DSA), tk)

    gate_bias = jnp.concatenate([b_igate, b_fgate]).astype(F32)
    bias_slab = jnp.zeros((1, LANES), F32).at[0, G_I:G_I + 2 * N_HEADS_M].set(gate_bias)
    gates_t = jnp.swapaxes(gates3[:, :, G_I:G_I + 2 * N_HEADS_M], 1, 2)
    y_m = _mlstm(qk_m.reshape(b, s, 2 * D_M), v_m.reshape(b, s, D_M), o_m.reshape(b, s, D_M), gates3, gates_t,
                 conv_w, conv_b.reshape(1, -1), bias_slab, gate_bias.reshape(-1, 1), gn_w.reshape(1, -1),
                 _tile(s, CHUNK_MLSTM))

    w_router = jnp.concatenate([re_w, rg_w, jnp.zeros((d, LANES - N_EXPERTS - N_GROUPS), F32)], axis=1)
    w_router_hi = w_router.astype(BF16)
    w_router = jnp.stack([w_router_hi, (w_router - w_router_hi.astype(F32)).astype(BF16)])
    b_router = jnp.concatenate([re_b, rg_b, jnp.zeros((LANES - N_EXPERTS - N_GROUPS,), F32)]).reshape(1, LANES)
    x1, comb = _mix_ln_router(y_a, y_m.reshape(n, D_M), x2d, w_out.astype(BF16),
                              ln1_g.reshape(1, d), ln1_b.reshape(1, d), w_router, b_router, alpha, _tile(s, TM_MIX))

    out = _moe_ln_ple(x1, comb, e_gate.astype(BF16), e_up.astype(BF16), e_down.astype(BF16),
                      ln2_g.reshape(1, d), ln2_b.reshape(1, d), p_i.reshape(n, -1), ple_gate_w.astype(BF16),
                      ple_proj_w.astype(BF16), alpha, _tile(n, TM_MOE))
    return out.reshape(b, s, d)


def kernel(x, p, w_in, b_igate, b_fgate, conv_w, conv_b, gn_w, w_out, ln1_g, ln1_b, router_group_w, router_group_b,
           router_expert_w, router_expert_b, expert_w_gate, expert_w_up, expert_w_down, ln2_g, ln2_b, ple_gate_w,
           ple_proj_w):
    depth = w_in.shape[0]
    alpha = (2.0 * depth) ** 0.25
    for i in range(depth):
        x = _layer(x, p[i], w_in[i], b_igate[i], b_fgate[i], conv_w[i], conv_b[i], gn_w[i], w_out[i], ln1_g[i],
                   ln1_b[i], router_group_w[i], router_group_b[i], router_expert_w[i], router_expert_b[i],
                   expert_w_gate[i], expert_w_up[i], expert_w_down[i], ln2_g[i], ln2_b[i], ple_gate_w[i],
                   ple_proj_w[i], alpha)
    return x
```
